```python
import jax, jax.numpy as jnp
from jax import lax
import numpy as np

D_MODEL = 1024
BATCH = 2
SEQ = 8192
DEPTH = 1

D_MIX = D_MODEL
GDN_HEADS = 4
GDN_DK = 128
GDN_DV = 128
GDN_QK = GDN_HEADS * GDN_DK
GDN_V = GDN_HEADS * GDN_DV
GDN_CONV = 4
GDN_CHUNK = 64
CF_CH = D_MIX - GDN_V
CF_KERNEL = 31
N_EXPERTS = 32
TOP_K = 4
D_FF = D_MODEL
SWIGLU_LIMIT = 7.0
SWIGLU_ALPHA = 1.702
MOE_BLOCK = 128
NORM_EPS = 1e-6

OFF_Q = 0
OFF_K = OFF_Q + GDN_QK
OFF_V = OFF_K + GDN_QK
OFF_Z = OFF_V + GDN_V
OFF_B = OFF_Z + GDN_V
OFF_A = OFF_B + GDN_HEADS
OFF_CF = OFF_A + GDN_HEADS
IN_COLS = OFF_CF + 2 * CF_CH

kernel_name = "hybrid_gdn_conformer_moe_block"


def _rmsnorm(x, w):
    xf = x.astype(jnp.float32)
    y = xf * lax.rsqrt(jnp.mean(xf * xf, axis=-1, keepdims=True) + NORM_EPS)
    return (y * w.astype(jnp.float32)).astype(x.dtype)


def _layernorm(x, w, b):
    xf = x.astype(jnp.float32)
    mu = jnp.mean(xf, axis=-1, keepdims=True)
    xc = xf - mu
    y = xc * lax.rsqrt(jnp.mean(xc * xc, axis=-1, keepdims=True) + NORM_EPS)
    return (y * w.astype(jnp.float32) + b.astype(jnp.float32)).astype(x.dtype)


def _gated_rmsnorm(o, w, z):
    of = o.astype(jnp.float32)
    y = of * lax.rsqrt(jnp.mean(of * of, axis=-1, keepdims=True) + NORM_EPS) * w.astype(jnp.float32)
    return (y * jax.nn.silu(z.astype(jnp.float32))).astype(z.dtype)


def _l2norm(t):
    return t * lax.rsqrt(jnp.sum(t * t, axis=-1, keepdims=True) + NORM_EPS)


def _causal_dwconv(x, w):
    width, ch = w.shape
    return lax.conv_general_dilated(
        x, w[:, None, :].astype(x.dtype), window_strides=(1,), padding=[(width - 1, 0)],
        dimension_numbers=("NWC", "WIO", "NWC"), feature_group_count=ch)


def _chunked_gated_delta_rule(q, k, v, g, beta):
    f32 = jnp.float32
    b, s, h, dk = q.shape
    dv = v.shape[-1]
    c = GDN_CHUNK
    n = s // c
    q = q.astype(f32) * (dk ** -0.5)
    k, v, g, beta = k.astype(f32), v.astype(f32), g.astype(f32), beta.astype(f32)

    def chunks(t):
        return t.reshape(b, n, c, h, -1).transpose(0, 3, 1, 2, 4)

    q, k, v = chunks(q), chunks(k), chunks(v)
    g = g.reshape(b, n, c, h).transpose(0, 3, 1, 2)
    beta = beta.reshape(b, n, c, h).transpose(0, 3, 1, 2)
    g = jnp.cumsum(g, axis=-1)

    causal = jnp.tril(jnp.ones((c, c), dtype=bool))
    strict = jnp.tril(jnp.ones((c, c), dtype=bool), k=-1)
    diff = g[..., :, None] - g[..., None, :]
    decay = jnp.exp(jnp.where(causal, diff, -jnp.inf))

    kb = k * beta[..., None]
    vb = v * beta[..., None]
    a_strict = jnp.where(strict, jnp.einsum("bhncd,bhnmd->bhncm", kb, k) * decay, 0.0)
    lower = jnp.eye(c, dtype=f32) + a_strict
    rhs = jnp.concatenate([vb, kb * jnp.exp(g)[..., None]], axis=-1)
    sol = lax.linalg.triangular_solve(lower, rhs, left_side=True, lower=True, unit_diagonal=True)
    u_val = sol[..., :dv]
    w_key = sol[..., dv:]

    intra = jnp.einsum("bhncd,bhnmd->bhncm", q, k) * decay
    q_dec = q * jnp.exp(g)[..., None]
    k_tail = k * jnp.exp(g[..., -1:] - g)[..., None]
    g_tot = jnp.exp(g[..., -1])

    xs = tuple(jnp.moveaxis(t, 2, 0) for t in (u_val, w_key, q_dec, k_tail, intra, g_tot))

    def step(state, inp):
        u_c, w_c, qd_c, kt_c, in_c, gt_c = inp
        v_new = u_c - jnp.einsum("bhcd,bhde->bhce", w_c, state)
        o_c = jnp.einsum("bhcd,bhde->bhce", qd_c, state) + jnp.einsum("bhcm,bhme->bhce", in_c, v_new)
        state = state * gt_c[..., None, None] + jnp.einsum("bhcd,bhce->bhde", kt_c, v_new)
        return state, o_c

    state0 = jnp.zeros((b, h, dk, dv), f32)
    _, o = lax.scan(step, state0, xs)
    return o.transpose(1, 0, 3, 2, 4).reshape(b, s, h, dv)


def _moe(xn, w_router, b_router, w_gate_up, b_gate_up, w_down, b_down):
    b, s, d = xn.shape
    t = b * s
    xt = xn.reshape(t, d)
    logits = (xt @ w_router + b_router).astype(jnp.float32)
    top_val, top_idx = lax.top_k(logits, TOP_K)
    gates = jax.nn.softmax(top_val, axis=-1)

    p = t * TOP_K
    e_flat = top_idx.reshape(p).astype(jnp.int32)
    tok_flat = jnp.arange(p, dtype=jnp.int32) // TOP_K
    g_flat = gates.reshape(p)
    order = jnp.argsort(e_flat)
    e_s, tok_s, g_s = e_flat[order], tok_flat[order], g_flat[order]

    counts = jnp.bincount(e_s, length=N_EXPERTS).astype(jnp.int32)
    padded = ((counts + MOE_BLOCK - 1) // MOE_BLOCK) * MOE_BLOCK
    start = jnp.cumsum(counts) - counts
    pend = jnp.cumsum(padded)
    pstart = pend - padded
    dest = pstart[e_s] + (jnp.arange(p, dtype=jnp.int32) - start[e_s])

    n_blocks = -(-p // MOE_BLOCK) + N_EXPERTS
    rows = n_blocks * MOE_BLOCK
    row_tok = jnp.full((rows,), t, jnp.int32).at[dest].set(tok_s)
    row_gate = jnp.zeros((rows,), jnp.float32).at[dest].set(g_s)
    blk_start = jnp.arange(n_blocks, dtype=jnp.int32) * MOE_BLOCK
    blk_exp = jnp.minimum(jnp.searchsorted(pend, blk_start, side="right"), N_EXPERTS - 1)

    x_pad = jnp.concatenate([xt, jnp.zeros((1, d), xt.dtype)], axis=0)
    xb = x_pad[row_tok].reshape(n_blocks, MOE_BLOCK, d)

    def expert_block(args):
        x_blk, e = args
        hid = x_blk @ w_gate_up[e] + b_gate_up[e]
        gate, up = hid[:, :D_FF], hid[:, D_FF:]
        gate = jnp.minimum(gate, SWIGLU_LIMIT)
        up = jnp.clip(up, -SWIGLU_LIMIT, SWIGLU_LIMIT)
        glu = gate * jax.nn.sigmoid(SWIGLU_ALPHA * gate)
        return ((up + 1.0) * glu) @ w_down[e] + b_down[e]

    yb = lax.map(expert_block, (xb, blk_exp))
    weighted = yb.reshape(rows, d).astype(jnp.float32) * row_gate[:, None]
    y = jax.ops.segment_sum(weighted, row_tok, num_segments=t + 1)[:t]
    return y.astype(xn.dtype).reshape(b, s, d)


def setup_inputs(seed: int = 0) -> dict:
    key = jax.random.key(seed)
    ks = jax.random.split(key, 20)
    f32 = jnp.float32
    nrm = lambda k, shape, sc: jax.random.normal(k, shape, f32) * sc
    return {
        "x": jax.random.normal(ks[0], (BATCH, SEQ, D_MODEL), f32),
        "attn_norm_w": 1.0 + nrm(ks[1], (DEPTH, D_MODEL), 0.02),
        "w_in": nrm(ks[2], (DEPTH, D_MODEL, IN_COLS), D_MODEL ** -0.5),
        "gdn_conv_w": nrm(ks[3], (DEPTH, GDN_CONV, 2 * GDN_QK + GDN_V), GDN_CONV ** -0.5),
        "gdn_a_log": jnp.log(jax.random.uniform(ks[4], (DEPTH, GDN_HEADS), f32, 1.0, 16.0)),
        "gdn_dt_bias": nrm(ks[5], (DEPTH, GDN_HEADS), 0.1),
        "gdn_norm_w": 1.0 + nrm(ks[6], (DEPTH, GDN_DV), 0.02),
        "cf_dw_w": nrm(ks[7], (DEPTH, CF_KERNEL, CF_CH), CF_KERNEL ** -0.5),
        "cf_dw_b": nrm(ks[8], (DEPTH, CF_CH), 0.02),
        "cf_ln_w": 1.0 + nrm(ks[9], (DEPTH, CF_CH), 0.02),
        "cf_ln_b": nrm(ks[10], (DEPTH, CF_CH), 0.02),
        "w_out": nrm(ks[11], (DEPTH, D_MIX, D_MODEL), D_MIX ** -0.5),
        "ffn_norm_w": 1.0 + nrm(ks[12], (DEPTH, D_MODEL), 0.02),
        "w_router": nrm(ks[13], (DEPTH, D_MODEL, N_EXPERTS), D_MODEL ** -0.5),
        "b_router": nrm(ks[14], (DEPTH, N_EXPERTS), 0.01),
        "w_gate_up": nrm(ks[15], (DEPTH, N_EXPERTS, D_MODEL, 2 * D_FF), D_MODEL ** -0.5),
        "b_gate_up": nrm(ks[16], (DEPTH, N_EXPERTS, 2 * D_FF), 0.02),
        "w_down": nrm(ks[17], (DEPTH, N_EXPERTS, D_FF, D_MODEL), D_FF ** -0.5),
        "b_down": nrm(ks[18], (DEPTH, N_EXPERTS, D_MODEL), 0.02),
        "final_norm_w": 1.0 + nrm(ks[19], (D_MODEL,), 0.02),
    }


def reference(x, attn_norm_w, w_in, gdn_conv_w, gdn_a_log, gdn_dt_bias, gdn_norm_w,
              cf_dw_w, cf_dw_b, cf_ln_w, cf_ln_b, w_out, ffn_norm_w, w_router, b_router,
              w_gate_up, b_gate_up, w_down, b_down, final_norm_w):
    b, s, _ = x.shape
    for l in range(DEPTH):
        h = _rmsnorm(x, attn_norm_w[l])
        proj = h @ w_in[l]

        qkv = jax.nn.silu(_causal_dwconv(proj[..., OFF_Q:OFF_Z], gdn_conv_w[l]))
        q = _l2norm(qkv[..., :GDN_QK].astype(jnp.float32).reshape(b, s, GDN_HEADS, GDN_DK))
        k = _l2norm(qkv[..., GDN_QK:2 * GDN_QK].astype(jnp.float32).reshape(b, s, GDN_HEADS, GDN_DK))
        v = qkv[..., 2 * GDN_QK:].reshape(b, s, GDN_HEADS, GDN_DV)
        z = proj[..., OFF_Z:OFF_B].reshape(b, s, GDN_HEADS, GDN_DV)
        beta = jax.nn.sigmoid(proj[..., OFF_B:OFF_A].astype(jnp.float32))
        g = -jnp.exp(gdn_a_log[l].astype(jnp.float32)) * jax.nn.softplus(
            proj[..., OFF_A:OFF_CF].astype(jnp.float32) + gdn_dt_bias[l].astype(jnp.float32))
        o = _chunked_gated_delta_rule(q, k, v, g, beta)
        out_a = _gated_rmsnorm(o, gdn_norm_w[l], z).reshape(b, s, GDN_V)

        cf = proj[..., OFF_CF:]
        u = cf[..., :CF_CH] * jax.nn.sigmoid(cf[..., CF_CH:])
        u = _causal_dwconv(u, cf_dw_w[l]) + cf_dw_b[l]
        u = jax.nn.silu(_layernorm(u, cf_ln_w[l], cf_ln_b[l]))

        mix = jnp.concatenate([out_a.astype(x.dtype), u.astype(x.dtype)], axis=-1)
        x = x + mix @ w_out[l]

        x = x + _moe(_rmsnorm(x, ffn_norm_w[l]), w_router[l], b_router[l],
                     w_gate_up[l], b_gate_up[l], w_down[l], b_down[l])
    return _rmsnorm(x, final_norm_w)
```

```python
import functools

import jax
import jax.numpy as jnp
from jax import lax
from jax.experimental import pallas as pl
from jax.experimental.pallas import tpu as pltpu

F32 = jnp.float32
BF16 = jnp.bfloat16
I32 = jnp.int32

NORM_EPS = 1e-6
LANES = 128
SUBLANES = 8
GDN_HEADS = 4
HEAD_DIM = 128
GDN_CHUNK = 64
GDN_CONV = 4
CF_KERNEL = 31
N_EXPERTS = 32
TOP_K = 4
SWIGLU_LIMIT = 7.0
SWIGLU_ALPHA = 1.702

QKV_HALO = 8
CF_HALO = 32
EXPERT_BLOCK = 256
VMEM_LIMIT = 56 * 1024 * 1024


def _silu(x):
    return x * jax.nn.sigmoid(x)


def _dot(a, b):
    return jnp.dot(a, b, preferred_element_type=F32)


def _dot_nt(a, b):
    return lax.dot_general(a, b, (((1,), (1,)), ((), ())), preferred_element_type=F32)


def _dot_tn(a, b):
    return lax.dot_general(a, b, (((0,), (0,)), ((), ())), preferred_element_type=F32)


def _inproj_kernel(x_ref, nw_ref, wm_ref, wba_ref, wcf_ref, cw_ref, alog_ref, dtb_ref,
                   dww_ref, dwb_ref, lnw_ref, lnb_ref,
                   q_ref, k_ref, v_ref, z_ref, bg_ref, bgt_ref, u_ref,
                   qkv_buf, cf_buf):
    tt = x_ref.shape[1]
    qk = GDN_HEADS * HEAD_DIM
    cfc = u_ref.shape[2]

    @pl.when(pl.program_id(1) == 0)
    def _():
        qkv_buf[0:QKV_HALO, :] = jnp.zeros((QKV_HALO, qkv_buf.shape[1]), F32)
        cf_buf[0:CF_HALO, :] = jnp.zeros((CF_HALO, cf_buf.shape[1]), F32)

    x = x_ref[0]
    h = x * lax.rsqrt(jnp.mean(x * x, axis=-1, keepdims=True) + NORM_EPS) * nw_ref[...]
    h = h.astype(BF16)
    pm = _dot(h, wm_ref[...])
    pba = _dot(h, wba_ref[...])
    pcf = _dot(h, wcf_ref[...])

    z_ref[0] = pm[:, 3 * qk:]

    qkv_buf[QKV_HALO:QKV_HALO + tt, :] = pm[:, :3 * qk]
    acc = None
    for j in range(GDN_CONV):
        term = cw_ref[j:j + 1, :] * qkv_buf[pl.ds(QKV_HALO - (GDN_CONV - 1) + j, tt), :]
        acc = term if acc is None else acc + term
    qkv_buf[0:QKV_HALO, :] = qkv_buf[tt:tt + QKV_HALO, :]
    qkv = _silu(acc)
    for hd in range(GDN_HEADS):
        for base, ref in ((0, q_ref), (qk, k_ref)):
            t = qkv[:, base + hd * HEAD_DIM: base + (hd + 1) * HEAD_DIM]
            t = t * lax.rsqrt(jnp.sum(t * t, axis=-1, keepdims=True) + NORM_EPS)
            ref[0, :, hd * HEAD_DIM:(hd + 1) * HEAD_DIM] = t
    v_ref[0] = qkv[:, 2 * qk:]

    lane = lax.broadcasted_iota(I32, (tt, LANES), 1)
    row = lax.broadcasted_iota(I32, (tt, LANES), 0)
    beta = jax.nn.sigmoid(pba)
    sp_in = pba + dtb_ref[...]
    softplus = jnp.maximum(sp_in, 0.0) + jnp.log(1.0 + jnp.exp(-jnp.abs(sp_in)))
    g = -jnp.exp(alog_ref[...]) * softplus
    g = jnp.where((lane >= GDN_HEADS) & (lane < 2 * GDN_HEADS), g, 0.0)
    pos = row % GDN_CHUNK
    shift = 1
    while shift < GDN_CHUNK:
        g = g + jnp.where(pos >= shift, pltpu.roll(g, shift, 0), 0.0)
        shift *= 2
    bg = jnp.where(lane < GDN_HEADS, beta, g)
    bg_ref[0] = bg
    bgt_ref[0] = jnp.transpose(bg)[0:SUBLANES, :]

    glu = pcf[:, :cfc] * jax.nn.sigmoid(pcf[:, cfc:])
    cf_buf[CF_HALO:CF_HALO + tt, :] = glu
    rows = 64
    for r0 in range(0, tt, rows):
        acc = None
        for j in range(CF_KERNEL):
            term = dww_ref[j:j + 1, :] * cf_buf[pl.ds(CF_HALO - (CF_KERNEL - 1) + j + r0, rows), :]
            acc = term if acc is None else acc + term
        c = acc + dwb_ref[...]
        mu = jnp.mean(c, axis=-1, keepdims=True)
        cc = c - mu
        y = cc * lax.rsqrt(jnp.mean(cc * cc, axis=-1, keepdims=True) + NORM_EPS)
        u_ref[0, r0:r0 + rows, :] = _silu(y * lnw_ref[...] + lnb_ref[...])
    cf_buf[0:CF_HALO, :] = cf_buf[tt:tt + CF_HALO, :]


def _inproj(x, nw, wm, wba, wcf, cw, alog, dtb, dww, dwb, lnw, lnb, *, tile):
    b, s, d = x.shape
    qk = GDN_HEADS * HEAD_DIM
    cfc = dww.shape[1]
    grid = (b, s // tile)
    full = lambda a: pl.BlockSpec(a.shape, lambda i, j: (0,) * a.ndim)
    tok = lambda w: pl.BlockSpec((1, tile, w), lambda i, j: (i, j, 0))
    out_shape = (
        jax.ShapeDtypeStruct((b, s, qk), F32), jax.ShapeDtypeStruct((b, s, qk), F32),
        jax.ShapeDtypeStruct((b, s, qk), F32), jax.ShapeDtypeStruct((b, s, qk), F32),
        jax.ShapeDtypeStruct((b, s, LANES), F32), jax.ShapeDtypeStruct((b, SUBLANES, s), F32),
        jax.ShapeDtypeStruct((b, s, cfc), F32))
    return pl.pallas_call(
        _inproj_kernel,
        grid=grid,
        in_specs=[tok(d)] + [full(a) for a in (nw, wm, wba, wcf, cw, alog, dtb, dww, dwb, lnw, lnb)],
        out_specs=(tok(qk), tok(qk), tok(qk), tok(qk), tok(LANES),
                   pl.BlockSpec((1, SUBLANES, tile), lambda i, j: (i, 0, j)), tok(cfc)),
        out_shape=out_shape,
        scratch_shapes=[pltpu.VMEM((QKV_HALO + tile, 3 * qk), F32),
                        pltpu.VMEM((CF_HALO + tile, cfc), F32)],
        compiler_params=pltpu.CompilerParams(
            dimension_semantics=("arbitrary", "arbitrary"), vmem_limit_bytes=VMEM_LIMIT),
        name="inproj",
    )(x, nw, wm, wba, wcf, cw, alog, dtb, dww, dwb, lnw, lnb)


def _unit_lower_inverse(a):
    c = a.shape[0]
    ii = lax.broadcasted_iota(I32, (c, c), 0)
    jj = lax.broadcasted_iota(I32, (c, c), 1)
    eye = (ii == jj).astype(F32)
    same16 = (ii // 16) == (jj // 16)
    same32 = (ii // 32) == (jj // 32)
    x = jnp.where(same16, -a, 0.0)
    t = eye + x
    xp = x
    for _ in range(3):
        xp_b = xp.astype(BF16)
        xp = _dot(xp_b, xp_b)
        t = t + _dot(t.astype(BF16), xp.astype(BF16))
    for off in (jnp.where(same32 & ~same16, a, 0.0), jnp.where(~same32, a, 0.0)):
        tb = t.astype(BF16)
        t = t - _dot(tb, _dot(off.astype(BF16), tb).astype(BF16))
    return t


def _gdn_kernel(q_ref, k_ref, v_ref, bg_ref, bgt_ref, o_ref, state):
    lt = q_ref.shape[1]
    c = GDN_CHUNK
    hd = pl.program_id(1)

    @pl.when(pl.program_id(2) == 0)
    def _():
        state[...] = jnp.zeros(state.shape, F32)

    bg = bg_ref[0]
    lane = lax.broadcasted_iota(I32, bg.shape, 1)
    beta_col = jnp.sum(jnp.where(lane == hd, bg, 0.0), axis=1, keepdims=True)
    gc_col = jnp.sum(jnp.where(lane == hd + GDN_HEADS, bg, 0.0), axis=1, keepdims=True)
    gc_row = bgt_ref[0, pl.ds(hd + GDN_HEADS, 1), :]

    ii = lax.broadcasted_iota(I32, (c, c), 0)
    jj = lax.broadcasted_iota(I32, (c, c), 1)
    scale = HEAD_DIM ** -0.5

    s = state[...]
    for n in range(lt // c):
        sl = slice(n * c, (n + 1) * c)
        qn = q_ref[0, sl, :] * scale
        kn = k_ref[0, sl, :]
        vn = v_ref[0, sl, :]
        bcol = beta_col[sl]
        gcol = gc_col[sl]
        grow = gc_row[:, sl]
        glast = gcol[c - 1:c, :]
        eg = jnp.exp(gcol)
        decay = jnp.where(ii >= jj, jnp.exp(jnp.minimum(gcol - grow, 0.0)), 0.0)
        kb = kn * bcol
        kn_b = kn.astype(BF16)
        a = jnp.where(ii > jj, _dot_nt(kb.astype(BF16), kn_b) * decay, 0.0)
        t = _unit_lower_inverse(a)
        rhs = jnp.concatenate([vn * bcol, kb * eg], axis=-1).astype(BF16)
        sol = _dot(t.astype(BF16), rhs)
        u_val = sol[:, :HEAD_DIM]
        w_key = sol[:, HEAD_DIM:]
        intra = _dot_nt(qn.astype(BF16), kn_b) * decay
        q_dec = qn * eg
        k_tail = kn * jnp.exp(glast - gcol)

        s_b = s.astype(BF16)
        ws_qs = _dot(jnp.concatenate([w_key, q_dec], axis=0).astype(BF16), s_b)
        v_new = u_val - ws_qs[:c]
        v_new_b = v_new.astype(BF16)
        o_ref[0, sl, :] = ws_qs[c:] + _dot(intra.astype(BF16), v_new_b)
        s = s * jnp.exp(glast) + _dot_tn(k_tail.astype(BF16), v_new_b)
    state[...] = s


def _gdn(q, k, v, bg, bgt, *, tile):
    b, s, qk = q.shape
    grid = (b, GDN_HEADS, s // tile)
    head = pl.BlockSpec((1, tile, HEAD_DIM), lambda i, h, j: (i, j, h))
    return pl.pallas_call(
        _gdn_kernel,
        grid=grid,
        in_specs=[head, head, head,
                  pl.BlockSpec((1, tile, LANES), lambda i, h, j: (i, j, 0)),
                  pl.BlockSpec((1, SUBLANES, tile), lambda i, h, j: (i, 0, j))],
        out_specs=head,
        out_shape=jax.ShapeDtypeStruct((b, s, qk), F32),
        scratch_shapes=[pltpu.VMEM((HEAD_DIM, HEAD_DIM), F32)],
        compiler_params=pltpu.CompilerParams(
            dimension_semantics=("arbitrary", "arbitrary", "arbitrary"), vmem_limit_bytes=VMEM_LIMIT),
        name="gdn",
    )(q, k, v, bg, bgt)


def _split_bf16(x):
    hi = x.astype(BF16)
    lo = (x - hi.astype(F32)).astype(BF16)
    return hi, lo


def _outproj_router_kernel(o_ref, z_ref, u_ref, x_ref, gnw_ref, wa_ref, wb_ref, fnw_ref,
                           wr_ref, br_ref,
                           x2_ref, xn_ref, route_ref, gate_ref, counts_ref, carry):
    tt = x_ref.shape[0]

    @pl.when(pl.program_id(0) == 0)
    def _():
        carry[...] = jnp.zeros(carry.shape, F32)

    parts = []
    for hd in range(GDN_HEADS):
        sl = slice(hd * HEAD_DIM, (hd + 1) * HEAD_DIM)
        oh = o_ref[:, sl]
        y = oh * lax.rsqrt(jnp.mean(oh * oh, axis=-1, keepdims=True) + NORM_EPS) * gnw_ref[...]
        parts.append((y * _silu(z_ref[:, sl])).astype(BF16))
    out_a = jnp.concatenate(parts, axis=-1)
    x2 = x_ref[...] + _dot(out_a, wa_ref[...]) + _dot(u_ref[...].astype(BF16), wb_ref[...])
    x2_ref[...] = x2

    xn = x2 * lax.rsqrt(jnp.mean(x2 * x2, axis=-1, keepdims=True) + NORM_EPS) * fnw_ref[...]
    xn_ref[...] = xn

    xh, xl = _split_bf16(xn)
    wh, wl = _split_bf16(wr_ref[...])
    logits = _dot(xh, wh) + _dot(xh, wl) + _dot(xl, wh) + br_ref[...]

    lane = lax.broadcasted_iota(I32, (tt, LANES), 1)
    lane_f = lane.astype(F32)
    neg = jnp.float32(-jnp.inf)
    work = jnp.where(lane < N_EXPERTS, logits, neg)
    vals, idxs = [], []
    onehot = jnp.zeros((tt, LANES), F32)
    for _ in range(TOP_K):
        m = jnp.max(work, axis=-1, keepdims=True)
        idx = jnp.min(jnp.where(work == m, lane_f, float(LANES)), axis=-1, keepdims=True).astype(I32)
        sel = lane == idx
        vals.append(m)
        idxs.append(idx)
        onehot = onehot + sel.astype(F32)
        work = jnp.where(sel, neg, work)
    exps = [jnp.exp(v - vals[0]) for v in vals]
    denom = exps[0] + exps[1] + exps[2] + exps[3]

    ri = lax.broadcasted_iota(I32, (tt, tt), 0)
    ci = lax.broadcasted_iota(I32, (tt, tt), 1)
    strict = (ri > ci).astype(BF16)
    base = carry[...] + _dot(strict, onehot.astype(BF16))
    route = jnp.zeros((tt, LANES), I32)
    gates = jnp.zeros((tt, LANES), F32)
    for kk in range(TOP_K):
        rank = jnp.sum(jnp.where(lane == idxs[kk], base, 0.0), axis=-1, keepdims=True)
        route = jnp.where(lane == kk, idxs[kk], route)
        route = jnp.where(lane == kk + TOP_K, rank.astype(I32), route)
        gates = jnp.where(lane == kk, exps[kk] / denom, gates)
    route_ref[...] = route
    gate_ref[...] = gates
    new_carry = carry[...] + jnp.sum(onehot, axis=0, keepdims=True)
    carry[...] = new_carry
    counts_ref[...] = new_carry.astype(I32)


def _outproj_router(o, z, u, x, gnw, wa, wb, fnw, wr, br, *, tile):
    t, d = x.shape
    grid = (t // tile,)
    full = lambda a: pl.BlockSpec(a.shape, lambda i: (0,) * a.ndim)
    tok = lambda w: pl.BlockSpec((tile, w), lambda i: (i, 0))
    return pl.pallas_call(
        _outproj_router_kernel,
        grid=grid,
        in_specs=[tok(o.shape[1]), tok(z.shape[1]), tok(u.shape[1]), tok(d)]
                 + [full(a) for a in (gnw, wa, wb, fnw, wr, br)],
        out_specs=(tok(d), tok(d), tok(LANES), tok(LANES), pl.BlockSpec((1, LANES), lambda i: (0, 0))),
        out_shape=(jax.ShapeDtypeStruct((t, d), F32), jax.ShapeDtypeStruct((t, d), F32),
                   jax.ShapeDtypeStruct((t, LANES), I32), jax.ShapeDtypeStruct((t, LANES), F32),
                   jax.ShapeDtypeStruct((1, LANES), I32)),
        scratch_shapes=[pltpu.VMEM((1, LANES), F32)],
        compiler_params=pltpu.CompilerParams(
            dimension_semantics=("arbitrary",), vmem_limit_bytes=VMEM_LIMIT),
        name="outproj_router",
    )(o, z, u, x, gnw, wa, wb, fnw, wr, br)


def _dest_kernel(route_ref, pstart_ref, dest_ref):
    route = route_ref[...].astype(F32)
    tt = route.shape[0]
    lane = lax.broadcasted_iota(I32, (tt, LANES), 1)
    pstart = pstart_ref[...].astype(F32)
    dest = jnp.zeros((tt, LANES), F32)
    for kk in range(TOP_K):
        idx = jnp.sum(jnp.where(lane == kk, route, 0.0), axis=-1, keepdims=True)
        rank = jnp.sum(jnp.where(lane == kk + TOP_K, route, 0.0), axis=-1, keepdims=True)
        start = jnp.sum(jnp.where(lane == idx.astype(I32), pstart, 0.0), axis=-1, keepdims=True)
        dest = jnp.where(lane == kk, start + rank, dest)
    dest_ref[...] = dest[:, :TOP_K].astype(I32)


def _dest(route, pstart, *, tile):
    t = route.shape[0]
    return pl.pallas_call(
        _dest_kernel,
        grid=(t // tile,),
        in_specs=[pl.BlockSpec((tile, LANES), lambda i: (i, 0)),
                  pl.BlockSpec((1, LANES), lambda i: (0, 0))],
        out_specs=pl.BlockSpec((tile, TOP_K), lambda i: (i, 0)),
        out_shape=jax.ShapeDtypeStruct((t, TOP_K), I32),
        compiler_params=pltpu.CompilerParams(dimension_semantics=("arbitrary",)),
        name="dest_rows",
    )(route, pstart)


def _scatter_kernel(dest_ref, xn_ref, xb_ref, sem):
    tt = xn_ref.shape[0]

    def issue(i, carry):
        for kk in range(TOP_K):
            row = dest_ref[i * TOP_K + kk]
            pltpu.make_async_copy(xn_ref.at[pl.ds(i, 1), :], xb_ref.at[pl.ds(row, 1), :], sem).start()
        return carry

    lax.fori_loop(0, tt, issue, 0)
    for kk in range(TOP_K):
        pltpu.make_async_copy(xn_ref, xb_ref.at[pl.ds(0, tt), :], sem).wait()


def _scatter_rows(dest_flat, xn, n_rows, *, tile):
    t, d = xn.shape
    return pl.pallas_call(
        _scatter_kernel,
        grid=(t // tile,),
        in_specs=[pl.BlockSpec((tile * TOP_K,), lambda i: (i,), memory_space=pltpu.SMEM),
                  pl.BlockSpec((tile, d), lambda i: (i, 0))],
        out_specs=pl.BlockSpec(memory_space=pl.ANY),
        out_shape=jax.ShapeDtypeStruct((n_rows, d), xn.dtype),
        scratch_shapes=[pltpu.SemaphoreType.DMA(())],
        compiler_params=pltpu.CompilerParams(dimension_semantics=("arbitrary",)),
        name="scatter_rows",
    )(dest_flat, xn)


def _expert_kernel(blk_exp_ref, n_used_ref, blk_valid_ref, xb_ref, wgu_ref, bgu_ref, wd_ref, bd_ref,
                   yb_ref, wgu_b, wd_b):
    i = pl.program_id(0)
    dff = wd_ref.shape[1]
    prev = blk_exp_ref[jnp.maximum(i - 1, 0)]
    changed = (i == 0) | (blk_exp_ref[i] != prev)

    @pl.when(changed & (i < n_used_ref[0]))
    def _():
        wgu_b[...] = wgu_ref[0].astype(BF16)
        wd_b[...] = wd_ref[0].astype(BF16)

    @pl.when(i < n_used_ref[0])
    def _():
        rid = lax.broadcasted_iota(I32, (xb_ref.shape[0], 1), 0)
        xb = jnp.where(rid < blk_valid_ref[i], xb_ref[...], 0.0).astype(BF16)
        hid = _dot(xb, wgu_b[...]) + bgu_ref[0]
        gate = jnp.minimum(hid[:, :dff], SWIGLU_LIMIT)
        up = jnp.clip(hid[:, dff:], -SWIGLU_LIMIT, SWIGLU_LIMIT)
        glu = gate * jax.nn.sigmoid(SWIGLU_ALPHA * gate)
        act = ((up + 1.0) * glu).astype(BF16)
        yb_ref[...] = _dot(act, wd_b[...]) + bd_ref[0]


def _experts(blk_exp, n_used, blk_valid, xb, wgu, bgu, wd, bd):
    rows, d = xb.shape
    n_blocks = rows // EXPERT_BLOCK
    two_f = wgu.shape[2]
    dff = wd.shape[1]
    blk = lambda i, be, nu, bv: (jnp.minimum(i, nu[0] - 1), 0)
    exp3 = lambda i, be, nu, bv: (be[jnp.minimum(i, nu[0] - 1)], 0, 0)
    grid_spec = pltpu.PrefetchScalarGridSpec(
        num_scalar_prefetch=3,
        grid=(n_blocks,),
        in_specs=[pl.BlockSpec((EXPERT_BLOCK, d), blk),
                  pl.BlockSpec((1, d, two_f), exp3),
                  pl.BlockSpec((1, 1, two_f), exp3),
                  pl.BlockSpec((1, dff, d), exp3),
                  pl.BlockSpec((1, 1, d), exp3)],
        out_specs=pl.BlockSpec((EXPERT_BLOCK, d), blk),
        scratch_shapes=[pltpu.VMEM((d, two_f), BF16), pltpu.VMEM((dff, d), BF16)])
    return pl.pallas_call(
        _expert_kernel,
        grid_spec=grid_spec,
        out_shape=jax.ShapeDtypeStruct((rows, d), F32),
        compiler_params=pltpu.CompilerParams(
            dimension_semantics=("arbitrary",), vmem_limit_bytes=VMEM_LIMIT),
        name="experts",
    )(blk_exp, n_used, blk_valid, xb, wgu, bgu, wd, bd)


def _combine_kernel(dest_ref, yb_ref, gate_ref, x2_ref, fw_ref, out_ref, buf, sem):
    tt = x2_ref.shape[0]

    def issue(i, carry):
        for kk in range(TOP_K):
            row = dest_ref[i * TOP_K + kk]
            pltpu.make_async_copy(yb_ref.at[pl.ds(row, 1), :], buf.at[kk, pl.ds(i, 1), :], sem).start()
        return carry

    lax.fori_loop(0, tt, issue, 0)
    for kk in range(TOP_K):
        pltpu.make_async_copy(yb_ref.at[pl.ds(0, tt), :], buf.at[kk], sem).wait()

    gates = gate_ref[...]
    lane = lax.broadcasted_iota(I32, gates.shape, 1)
    x3 = x2_ref[...]
    for kk in range(TOP_K):
        gk = jnp.sum(jnp.where(lane == kk, gates, 0.0), axis=-1, keepdims=True)
        x3 = x3 + gk * buf[kk]
    out_ref[...] = x3 * lax.rsqrt(jnp.mean(x3 * x3, axis=-1, keepdims=True) + NORM_EPS) * fw_ref[...]


def _combine(dest_flat, yb, gates, x2, fw, *, tile):
    t, d = x2.shape
    return pl.pallas_call(
        _combine_kernel,
        grid=(t // tile,),
        in_specs=[pl.BlockSpec((tile * TOP_K,), lambda i: (i,), memory_space=pltpu.SMEM),
                  pl.BlockSpec(memory_space=pl.ANY),
                  pl.BlockSpec((tile, LANES), lambda i: (i, 0)),
                  pl.BlockSpec((tile, d), lambda i: (i, 0)),
                  pl.BlockSpec((1, d), lambda i: (0, 0))],
        out_specs=pl.BlockSpec((tile, d), lambda i: (i, 0)),
        out_shape=jax.ShapeDtypeStruct((t, d), F32),
        scratch_shapes=[pltpu.VMEM((TOP_K, tile, d), F32), pltpu.SemaphoreType.DMA(())],
        compiler_params=pltpu.CompilerParams(
            dimension_semantics=("arbitrary",), vmem_limit_bytes=VMEM_LIMIT),
        name="combine",
    )(dest_flat, yb, gates, x2, fw)


def _pad_lanes(a, offset=0, fill=0.0):
    out = jnp.full((1, LANES), fill, a.dtype)
    return out.at[0, offset:offset + a.shape[0]].set(a)


def _layer(x, attn_norm_w, w_in, gdn_conv_w, gdn_a_log, gdn_dt_bias, gdn_norm_w,
           cf_dw_w, cf_dw_b, cf_ln_w, cf_ln_b, w_out, ffn_norm_w, w_router, b_router,
           w_gate_up, b_gate_up, w_down, b_down, final_norm_w, apply_final):
    b, s, d = x.shape
    t = b * s
    qk = GDN_HEADS * HEAD_DIM
    cfc = cf_dw_w.shape[1]
    off_b = 4 * qk
    off_cf = off_b + 2 * GDN_HEADS

    wm = w_in[:, :off_b].astype(BF16)
    wba = jnp.zeros((d, LANES), F32).at[:, :2 * GDN_HEADS].set(w_in[:, off_b:off_cf]).astype(BF16)
    wcf = w_in[:, off_cf:].astype(BF16)
    alog = _pad_lanes(gdn_a_log, GDN_HEADS)
    dtb = _pad_lanes(gdn_dt_bias, GDN_HEADS)

    q, k, v, z, bg, bgt, u = _inproj(
        x, attn_norm_w[None, :], wm, wba, wcf, gdn_conv_w, alog, dtb,
        cf_dw_w, cf_dw_b[None, :], cf_ln_w[None, :], cf_ln_b[None, :], tile=min(512, s))
    o = _gdn(q, k, v, bg, bgt, tile=min(512, s))

    wr = jnp.zeros((d, LANES), F32).at[:, :N_EXPERTS].set(w_router)
    br = _pad_lanes(b_router)
    x2, xn, route, gates, counts = _outproj_router(
        o.reshape(t, qk), z.reshape(t, qk), u.reshape(t, cfc), x.reshape(t, d),
        gdn_norm_w[None, :], w_out[:qk].astype(BF16), w_out[qk:].astype(BF16),
        ffn_norm_w[None, :], wr, br, tile=min(512, t))

    cnt = counts[0, :N_EXPERTS]
    nblk = (cnt + EXPERT_BLOCK - 1) // EXPERT_BLOCK
    blk_end = jnp.cumsum(nblk)
    pstart = (blk_end - nblk) * EXPERT_BLOCK
    n_blocks = (t * TOP_K) // EXPERT_BLOCK + N_EXPERTS
    blk_exp = jnp.minimum(
        jnp.searchsorted(blk_end, jnp.arange(n_blocks, dtype=I32), side="right"), N_EXPERTS - 1
    ).astype(I32)
    n_used = blk_end[-1:].astype(I32)
    row_end = (pstart + cnt)[blk_exp]
    blk_valid = jnp.clip(row_end - jnp.arange(n_blocks, dtype=I32) * EXPERT_BLOCK, 0, EXPERT_BLOCK)
    blk_valid = jnp.where(jnp.arange(n_blocks) < n_used[0], blk_valid, 0).astype(I32)

    dest = _dest(route, _pad_lanes(pstart.astype(I32)), tile=min(2048, t))
    dest_flat = dest.reshape(t * TOP_K)
    xb = _scatter_rows(dest_flat, xn, n_blocks * EXPERT_BLOCK, tile=min(256, t))
    yb = _experts(blk_exp, n_used, blk_valid, xb, w_gate_up, b_gate_up[:, None, :], w_down, b_down[:, None, :])
    out = _combine(dest_flat, yb, gates, x2, final_norm_w[None, :], tile=min(256, t))
    return out.reshape(b, s, d)


def kernel(x, attn_norm_w, w_in, gdn_conv_w, gdn_a_log, gdn_dt_bias, gdn_norm_w, cf_dw_w, cf_dw_b,
           cf_ln_w, cf_ln_b, w_out, ffn_norm_w, w_router, b_router, w_gate_up, b_gate_up, w_down,
           b_down, final_norm_w):
    depth = w_in.shape[0]
    assert depth == 1, "the fused final norm assumes a single trunk layer"
    return _layer(x, attn_norm_w[0], w_in[0], gdn_conv_w[0], gdn_a_log[0], gdn_dt_bias[0],
                  gdn_norm_w[0], cf_dw_w[0], cf_dw_b[0], cf_ln_w[0], cf_ln_b[0], w_out[0],
                  ffn_norm_w[0], w_router[0], b_router[0], w_gate_up[0], b_gate_up[0], w_down[0],
                  b_down[0], final_norm_w, True)
```

```python
import functools

import jax
import jax.numpy as jnp
from jax import lax
from jax.experimental import pallas as pl
from jax.experimental.pallas import tpu as pltpu

F32 = jnp.float32
BF16 = jnp.bfloat16
I32 = jnp.int32

NORM_EPS = 1e-6
LANES = 128
SUBLANES = 8
GDN_HEADS = 4
HEAD_DIM = 128
GDN_CHUNK = 64
GDN_CONV = 4
CF_KERNEL = 31
N_EXPERTS = 32
TOP_K = 4
SWIGLU_LIMIT = 7.0
SWIGLU_ALPHA = 1.702

QKV_HALO = 8
CF_HALO = 32
EXPERT_BLOCK = 256
VMEM_LIMIT = 56 * 1024 * 1024


def _silu(x):
    return x * jax.nn.sigmoid(x)


def _dot(a, b):
    return jnp.dot(a, b, preferred_element_type=F32)


def _dot_nt(a, b):
    return lax.dot_general(a, b, (((1,), (1,)), ((), ())), preferred_element_type=F32)


def _dot_tn(a, b):
    return lax.dot_general(a, b, (((0,), (0,)), ((), ())), preferred_element_type=F32)


def _inproj_kernel(x_ref, nw_ref, wm_ref, wba_ref, wcf_ref, cw_ref, alog_ref, dtb_ref,
                   dww_ref, dwb_ref, lnw_ref, lnb_ref,
                   q_ref, k_ref, v_ref, z_ref, bg_ref, bgt_ref, u_ref,
                   qkv_buf, cf_buf):
    tt = x_ref.shape[1]
    qk = GDN_HEADS * HEAD_DIM
    cfc = u_ref.shape[2]

    @pl.when(pl.program_id(1) == 0)
    def _():
        qkv_buf[0:QKV_HALO, :] = jnp.zeros((QKV_HALO, qkv_buf.shape[1]), F32)
        cf_buf[0:CF_HALO, :] = jnp.zeros((CF_HALO, cf_buf.shape[1]), F32)

    x = x_ref[0]
    h = x * lax.rsqrt(jnp.mean(x * x, axis=-1, keepdims=True) + NORM_EPS) * nw_ref[...]
    h = h.astype(BF16)
    pm = _dot(h, wm_ref[...])
    pba = _dot(h, wba_ref[...])
    pcf = _dot(h, wcf_ref[...])

    z_ref[0] = pm[:, 3 * qk:]

    qkv_buf[QKV_HALO:QKV_HALO + tt, :] = pm[:, :3 * qk]
    acc = None
    for j in range(GDN_CONV):
        term = cw_ref[j:j + 1, :] * qkv_buf[pl.ds(QKV_HALO - (GDN_CONV - 1) + j, tt), :]
        acc = term if acc is None else acc + term
    qkv_buf[0:QKV_HALO, :] = qkv_buf[tt:tt + QKV_HALO, :]
    qkv = _silu(acc)
    for hd in range(GDN_HEADS):
        for base, ref in ((0, q_ref), (qk, k_ref)):
            t = qkv[:, base + hd * HEAD_DIM: base + (hd + 1) * HEAD_DIM]
            t = t * lax.rsqrt(jnp.sum(t * t, axis=-1, keepdims=True) + NORM_EPS)
            ref[0, :, hd * HEAD_DIM:(hd + 1) * HEAD_DIM] = t
    v_ref[0] = qkv[:, 2 * qk:]

    lane = lax.broadcasted_iota(I32, (tt, LANES), 1)
    row = lax.broadcasted_iota(I32, (tt, LANES), 0)
    beta = jax.nn.sigmoid(pba)
    sp_in = pba + dtb_ref[...]
    softplus = jnp.maximum(sp_in, 0.0) + jnp.log(1.0 + jnp.exp(-jnp.abs(sp_in)))
    g = -jnp.exp(alog_ref[...]) * softplus
    g = jnp.where((lane >= GDN_HEADS) & (lane < 2 * GDN_HEADS), g, 0.0)
    pos = row % GDN_CHUNK
    shift = 1
    while shift < GDN_CHUNK:
        g = g + jnp.where(pos >= shift, pltpu.roll(g, shift, 0), 0.0)
        shift *= 2
    bg = jnp.where(lane < GDN_HEADS, beta, g)
    bg_ref[0] = bg
    bgt_ref[0] = jnp.transpose(bg)[0:SUBLANES, :]

    glu = pcf[:, :cfc] * jax.nn.sigmoid(pcf[:, cfc:])
    cf_buf[CF_HALO:CF_HALO + tt, :] = glu
    rows = 64
    for r0 in range(0, tt, rows):
        acc = None
        for j in range(CF_KERNEL):
            term = dww_ref[j:j + 1, :] * cf_buf[pl.ds(CF_HALO - (CF_KERNEL - 1) + j + r0, rows), :]
            acc = term if acc is None else acc + term
        c = acc + dwb_ref[...]
        mu = jnp.mean(c, axis=-1, keepdims=True)
        cc = c - mu
        y = cc * lax.rsqrt(jnp.mean(cc * cc, axis=-1, keepdims=True) + NORM_EPS)
        u_ref[0, r0:r0 + rows, :] = _silu(y * lnw_ref[...] + lnb_ref[...])
    cf_buf[0:CF_HALO, :] = cf_buf[tt:tt + CF_HALO, :]


def _inproj(x, nw, wm, wba, wcf, cw, alog, dtb, dww, dwb, lnw, lnb, *, tile):
    b, s, d = x.shape
    qk = GDN_HEADS * HEAD_DIM
    cfc = dww.shape[1]
    grid = (b, s // tile)
    full = lambda a: pl.BlockSpec(a.shape, lambda i, j: (0,) * a.ndim)
    tok = lambda w: pl.BlockSpec((1, tile, w), lambda i, j: (i, j, 0))
    out_shape = (
        jax.ShapeDtypeStruct((b, s, qk), F32), jax.ShapeDtypeStruct((b, s, qk), F32),
        jax.ShapeDtypeStruct((b, s, qk), F32), jax.ShapeDtypeStruct((b, s, qk), F32),
        jax.ShapeDtypeStruct((b, s, LANES), F32), jax.ShapeDtypeStruct((b, SUBLANES, s), F32),
        jax.ShapeDtypeStruct((b, s, cfc), F32))
    return pl.pallas_call(
        _inproj_kernel,
        grid=grid,
        in_specs=[tok(d)] + [full(a) for a in (nw, wm, wba, wcf, cw, alog, dtb, dww, dwb, lnw, lnb)],
        out_specs=(tok(qk), tok(qk), tok(qk), tok(qk), tok(LANES),
                   pl.BlockSpec((1, SUBLANES, tile), lambda i, j: (i, 0, j)), tok(cfc)),
        out_shape=out_shape,
        scratch_shapes=[pltpu.VMEM((QKV_HALO + tile, 3 * qk), F32),
                        pltpu.VMEM((CF_HALO + tile, cfc), F32)],
        compiler_params=pltpu.CompilerParams(
            dimension_semantics=("arbitrary", "arbitrary"), vmem_limit_bytes=VMEM_LIMIT),
        name="inproj",
    )(x, nw, wm, wba, wcf, cw, alog, dtb, dww, dwb, lnw, lnb)


def _bmm(a, b):
    return jnp.einsum("bmk,bkn->bmn", a, b, preferred_element_type=F32)


def _bmm_nt(a, b):
    return jnp.einsum("bmk,bnk->bmn", a, b, preferred_element_type=F32)


def _bmm_tn(a, b):
    return jnp.einsum("bkm,bkn->bmn", a, b, preferred_element_type=F32)


def _unit_lower_inverse(a):
    c = a.shape[-1]
    ii = lax.broadcasted_iota(I32, (c, c), 0)
    jj = lax.broadcasted_iota(I32, (c, c), 1)
    eye = (ii == jj).astype(F32)
    same16 = (ii // 16) == (jj // 16)
    same32 = (ii // 32) == (jj // 32)
    x = jnp.where(same16, -a, 0.0)
    t = eye + x
    xp = x
    for _ in range(3):
        xp_b = xp.astype(BF16)
        xp = _bmm(xp_b, xp_b)
        t = t + _bmm(t.astype(BF16), xp.astype(BF16))
    for off in (jnp.where(same32 & ~same16, a, 0.0), jnp.where(~same32, a, 0.0)):
        tb = t.astype(BF16)
        t = t - _bmm(tb, _bmm(off.astype(BF16), tb).astype(BF16))
    return t


def _gdn_kernel(q_ref, k_ref, v_ref, bg_ref, bgt_ref, o_ref, state, s_all):
    lt = q_ref.shape[1]
    c = GDN_CHUNK
    nh = GDN_HEADS
    nc = lt // c

    @pl.when(pl.program_id(1) == 0)
    def _():
        state[...] = jnp.zeros(state.shape, F32)

    def stack(fn):
        return jnp.stack([fn(slice(n * c, (n + 1) * c), h) for n in range(nc) for h in range(nh)])

    head = lambda h: slice(h * HEAD_DIM, (h + 1) * HEAD_DIM)
    q = stack(lambda r, h: q_ref[0, r, head(h)]) * (HEAD_DIM ** -0.5)
    k = stack(lambda r, h: k_ref[0, r, head(h)])
    v = stack(lambda r, h: v_ref[0, r, head(h)])
    beta = stack(lambda r, h: bg_ref[0, r, h:h + 1])
    gcol = stack(lambda r, h: bg_ref[0, r, nh + h:nh + h + 1])
    grow = stack(lambda r, h: bgt_ref[0, nh + h:nh + h + 1, r])

    ii = lax.broadcasted_iota(I32, (c, c), 0)
    jj = lax.broadcasted_iota(I32, (c, c), 1)
    glast = gcol[:, c - 1:c, :]
    eg = jnp.exp(gcol)
    decay = jnp.where(ii >= jj, jnp.exp(jnp.minimum(gcol - grow, 0.0)), 0.0)
    kb = k * beta
    k_b = k.astype(BF16)
    a = jnp.where(ii > jj, _bmm_nt(kb.astype(BF16), k_b) * decay, 0.0)
    t = _unit_lower_inverse(a)
    rhs = jnp.concatenate([v * beta, kb * eg], axis=-1).astype(BF16)
    sol = _bmm(t.astype(BF16), rhs)
    u_val = sol[..., :HEAD_DIM]
    w_key = sol[..., HEAD_DIM:]
    intra = _bmm_nt(q.astype(BF16), k_b) * decay
    k_tail = (k * jnp.exp(glast - gcol)).astype(BF16)
    upd = _bmm_tn(k_tail, sol.astype(BF16))
    b_mat = upd[..., :HEAD_DIM]
    p_mat = upd[..., HEAD_DIM:].astype(BF16)
    g_tot = jnp.exp(glast)

    s = state[...]
    for n in range(nc):
        grp = slice(n * nh, (n + 1) * nh)
        s_b = s.astype(BF16)
        s_all[grp] = s_b
        s = s * g_tot[grp] - _bmm(p_mat[grp], s_b) + b_mat[grp]
    state[...] = s

    wq = jnp.concatenate([w_key, q * eg], axis=1).astype(BF16)
    ws_qs = _bmm(wq, s_all[...])
    v_new = u_val - ws_qs[:, :c]
    o = ws_qs[:, c:] + _bmm(intra.astype(BF16), v_new.astype(BF16))
    for n in range(nc):
        for h in range(nh):
            o_ref[0, n * c:(n + 1) * c, head(h)] = o[n * nh + h]


def _gdn(q, k, v, bg, bgt, *, tile):
    b, s, qk = q.shape
    grid = (b, s // tile)
    tok = lambda w: pl.BlockSpec((1, tile, w), lambda i, j: (i, j, 0))
    n_prob = (tile // GDN_CHUNK) * GDN_HEADS
    return pl.pallas_call(
        _gdn_kernel,
        grid=grid,
        in_specs=[tok(qk), tok(qk), tok(qk), tok(LANES),
                  pl.BlockSpec((1, SUBLANES, tile), lambda i, j: (i, 0, j))],
        out_specs=tok(qk),
        out_shape=jax.ShapeDtypeStruct((b, s, qk), F32),
        scratch_shapes=[pltpu.VMEM((GDN_HEADS, HEAD_DIM, HEAD_DIM), F32),
                        pltpu.VMEM((n_prob, HEAD_DIM, HEAD_DIM), BF16)],
        compiler_params=pltpu.CompilerParams(
            dimension_semantics=("arbitrary", "arbitrary"), vmem_limit_bytes=VMEM_LIMIT),
        name="gdn",
    )(q, k, v, bg, bgt)


def _split_bf16(x):
    hi = x.astype(BF16)
    lo = (x - hi.astype(F32)).astype(BF16)
    return hi, lo


def _outproj_router_kernel(o_ref, z_ref, u_ref, x_ref, gnw_ref, wa_ref, wb_ref, fnw_ref,
                           wr_ref, br_ref,
                           x2_ref, xn_ref, route_ref, gate_ref, counts_ref, carry):
    tt = x_ref.shape[0]

    @pl.when(pl.program_id(0) == 0)
    def _():
        carry[...] = jnp.zeros(carry.shape, F32)

    parts = []
    for hd in range(GDN_HEADS):
        sl = slice(hd * HEAD_DIM, (hd + 1) * HEAD_DIM)
        oh = o_ref[:, sl]
        y = oh * lax.rsqrt(jnp.mean(oh * oh, axis=-1, keepdims=True) + NORM_EPS) * gnw_ref[...]
        parts.append((y * _silu(z_ref[:, sl])).astype(BF16))
    out_a = jnp.concatenate(parts, axis=-1)
    x2 = x_ref[...] + _dot(out_a, wa_ref[...]) + _dot(u_ref[...].astype(BF16), wb_ref[...])
    x2_ref[...] = x2

    xn = x2 * lax.rsqrt(jnp.mean(x2 * x2, axis=-1, keepdims=True) + NORM_EPS) * fnw_ref[...]
    xn_ref[...] = xn

    xh, xl = _split_bf16(xn)
    wh, wl = _split_bf16(wr_ref[...])
    logits = _dot(xh, wh) + _dot(xh, wl) + _dot(xl, wh) + br_ref[...]

    lane = lax.broadcasted_iota(I32, (tt, LANES), 1)
    lane_f = lane.astype(F32)
    neg = jnp.float32(-jnp.inf)
    work = jnp.where(lane < N_EXPERTS, logits, neg)
    vals, idxs = [], []
    onehot = jnp.zeros((tt, LANES), F32)
    for _ in range(TOP_K):
        m = jnp.max(work, axis=-1, keepdims=True)
        idx = jnp.min(jnp.where(work == m, lane_f, float(LANES)), axis=-1, keepdims=True).astype(I32)
        sel = lane == idx
        vals.append(m)
        idxs.append(idx)
        onehot = onehot + sel.astype(F32)
        work = jnp.where(sel, neg, work)
    exps = [jnp.exp(v - vals[0]) for v in vals]
    denom = exps[0] + exps[1] + exps[2] + exps[3]

    ri = lax.broadcasted_iota(I32, (tt, tt), 0)
    ci = lax.broadcasted_iota(I32, (tt, tt), 1)
    strict = (ri > ci).astype(BF16)
    base = carry[...] + _dot(strict, onehot.astype(BF16))
    route = jnp.zeros((tt, LANES), I32)
    gates = jnp.zeros((tt, LANES), F32)
    for kk in range(TOP_K):
        rank = jnp.sum(jnp.where(lane == idxs[kk], base, 0.0), axis=-1, keepdims=True)
        route = jnp.where(lane == kk, idxs[kk], route)
        route = jnp.where(lane == kk + TOP_K, rank.astype(I32), route)
        gates = jnp.where(lane == kk, exps[kk] / denom, gates)
    route_ref[...] = route
    gate_ref[...] = gates
    new_carry = carry[...] + jnp.sum(onehot, axis=0, keepdims=True)
    carry[...] = new_carry
    counts_ref[...] = new_carry.astype(I32)


def _outproj_router(o, z, u, x, gnw, wa, wb, fnw, wr, br, *, tile):
    t, d = x.shape
    grid = (t // tile,)
    full = lambda a: pl.BlockSpec(a.shape, lambda i: (0,) * a.ndim)
    tok = lambda w: pl.BlockSpec((tile, w), lambda i: (i, 0))
    return pl.pallas_call(
        _outproj_router_kernel,
        grid=grid,
        in_specs=[tok(o.shape[1]), tok(z.shape[1]), tok(u.shape[1]), tok(d)]
                 + [full(a) for a in (gnw, wa, wb, fnw, wr, br)],
        out_specs=(tok(d), tok(d), tok(LANES), tok(LANES), pl.BlockSpec((1, LANES), lambda i: (0, 0))),
        out_shape=(jax.ShapeDtypeStruct((t, d), F32), jax.ShapeDtypeStruct((t, d), F32),
                   jax.ShapeDtypeStruct((t, LANES), I32), jax.ShapeDtypeStruct((t, LANES), F32),
                   jax.ShapeDtypeStruct((1, LANES), I32)),
        scratch_shapes=[pltpu.VMEM((1, LANES), F32)],
        compiler_params=pltpu.CompilerParams(
            dimension_semantics=("arbitrary",), vmem_limit_bytes=VMEM_LIMIT),
        name="outproj_router",
    )(o, z, u, x, gnw, wa, wb, fnw, wr, br)


def _dest_kernel(route_ref, pstart_ref, dest_ref):
    route = route_ref[...].astype(F32)
    tt = route.shape[0]
    lane = lax.broadcasted_iota(I32, (tt, LANES), 1)
    pstart = pstart_ref[...].astype(F32)
    dest = jnp.zeros((tt, LANES), F32)
    for kk in range(TOP_K):
        idx = jnp.sum(jnp.where(lane == kk, route, 0.0), axis=-1, keepdims=True)
        rank = jnp.sum(jnp.where(lane == kk + TOP_K, route, 0.0), axis=-1, keepdims=True)
        start = jnp.sum(jnp.where(lane == idx.astype(I32), pstart, 0.0), axis=-1, keepdims=True)
        dest = jnp.where(lane == kk, start + rank, dest)
    dest_ref[...] = dest[:, :TOP_K].astype(I32)


def _dest(route, pstart, *, tile):
    t = route.shape[0]
    return pl.pallas_call(
        _dest_kernel,
        grid=(t // tile,),
        in_specs=[pl.BlockSpec((tile, LANES), lambda i: (i, 0)),
                  pl.BlockSpec((1, LANES), lambda i: (0, 0))],
        out_specs=pl.BlockSpec((tile, TOP_K), lambda i: (i, 0)),
        out_shape=jax.ShapeDtypeStruct((t, TOP_K), I32),
        compiler_params=pltpu.CompilerParams(dimension_semantics=("arbitrary",)),
        name="dest_rows",
    )(route, pstart)


def _scatter_kernel(dest_ref, xn_ref, xb_ref, sem):
    tt = xn_ref.shape[0]

    def issue(i, carry):
        for kk in range(TOP_K):
            row = dest_ref[i * TOP_K + kk]
            pltpu.make_async_copy(xn_ref.at[pl.ds(i, 1), :], xb_ref.at[pl.ds(row, 1), :], sem).start()
        return carry

    lax.fori_loop(0, tt, issue, 0)
    for kk in range(TOP_K):
        pltpu.make_async_copy(xn_ref, xb_ref.at[pl.ds(0, tt), :], sem).wait()


def _scatter_rows(dest_flat, xn, n_rows, *, tile):
    t, d = xn.shape
    return pl.pallas_call(
        _scatter_kernel,
        grid=(t // tile,),
        in_specs=[pl.BlockSpec((tile * TOP_K,), lambda i: (i,), memory_space=pltpu.SMEM),
                  pl.BlockSpec((tile, d), lambda i: (i, 0))],
        out_specs=pl.BlockSpec(memory_space=pl.ANY),
        out_shape=jax.ShapeDtypeStruct((n_rows, d), xn.dtype),
        scratch_shapes=[pltpu.SemaphoreType.DMA(())],
        compiler_params=pltpu.CompilerParams(dimension_semantics=("arbitrary",)),
        name="scatter_rows",
    )(dest_flat, xn)


def _expert_kernel(blk_exp_ref, n_used_ref, blk_valid_ref, xb_ref, wgu_ref, bgu_ref, wd_ref, bd_ref,
                   yb_ref, wgu_b, wd_b):
    i = pl.program_id(0)
    dff = wd_ref.shape[1]
    prev = blk_exp_ref[jnp.maximum(i - 1, 0)]
    changed = (i == 0) | (blk_exp_ref[i] != prev)

    @pl.when(changed & (i < n_used_ref[0]))
    def _():
        wgu_b[...] = wgu_ref[0].astype(BF16)
        wd_b[...] = wd_ref[0].astype(BF16)

    @pl.when(i < n_used_ref[0])
    def _():
        rid = lax.broadcasted_iota(I32, (xb_ref.shape[0], 1), 0)
        xb = jnp.where(rid < blk_valid_ref[i], xb_ref[...], 0.0).astype(BF16)
        hid = _dot(xb, wgu_b[...]) + bgu_ref[0]
        gate = jnp.minimum(hid[:, :dff], SWIGLU_LIMIT)
        up = jnp.clip(hid[:, dff:], -SWIGLU_LIMIT, SWIGLU_LIMIT)
        glu = gate * jax.nn.sigmoid(SWIGLU_ALPHA * gate)
        act = ((up + 1.0) * glu).astype(BF16)
        yb_ref[...] = _dot(act, wd_b[...]) + bd_ref[0]


def _experts(blk_exp, n_used, blk_valid, xb, wgu, bgu, wd, bd):
    rows, d = xb.shape
    n_blocks = rows // EXPERT_BLOCK
    two_f = wgu.shape[2]
    dff = wd.shape[1]
    blk = lambda i, be, nu, bv: (jnp.minimum(i, nu[0] - 1), 0)
    exp3 = lambda i, be, nu, bv: (be[jnp.minimum(i, nu[0] - 1)], 0, 0)
    grid_spec = pltpu.PrefetchScalarGridSpec(
        num_scalar_prefetch=3,
        grid=(n_blocks,),
        in_specs=[pl.BlockSpec((EXPERT_BLOCK, d), blk),
                  pl.BlockSpec((1, d, two_f), exp3),
                  pl.BlockSpec((1, 1, two_f), exp3),
                  pl.BlockSpec((1, dff, d), exp3),
                  pl.BlockSpec((1, 1, d), exp3)],
        out_specs=pl.BlockSpec((EXPERT_BLOCK, d), blk),
        scratch_shapes=[pltpu.VMEM((d, two_f), BF16), pltpu.VMEM((dff, d), BF16)])
    return pl.pallas_call(
        _expert_kernel,
        grid_spec=grid_spec,
        out_shape=jax.ShapeDtypeStruct((rows, d), F32),
        compiler_params=pltpu.CompilerParams(
            dimension_semantics=("arbitrary",), vmem_limit_bytes=VMEM_LIMIT),
        name="experts",
    )(blk_exp, n_used, blk_valid, xb, wgu, bgu, wd, bd)


def _combine_kernel(dest_ref, yb_ref, gate_ref, x2_ref, fw_ref, out_ref, buf, sem):
    tt = x2_ref.shape[0]

    def issue(i, carry):
        for kk in range(TOP_K):
            row = dest_ref[i * TOP_K + kk]
            pltpu.make_async_copy(yb_ref.at[pl.ds(row, 1), :], buf.at[kk, pl.ds(i, 1), :], sem).start()
        return carry

    lax.fori_loop(0, tt, issue, 0)
    for kk in range(TOP_K):
        pltpu.make_async_copy(yb_ref.at[pl.ds(0, tt), :], buf.at[kk], sem).wait()

    gates = gate_ref[...]
    lane = lax.broadcasted_iota(I32, gates.shape, 1)
    x3 = x2_ref[...]
    for kk in range(TOP_K):
        gk = jnp.sum(jnp.where(lane == kk, gates, 0.0), axis=-1, keepdims=True)
        x3 = x3 + gk * buf[kk]
    out_ref[...] = x3 * lax.rsqrt(jnp.mean(x3 * x3, axis=-1, keepdims=True) + NORM_EPS) * fw_ref[...]


def _combine(dest_flat, yb, gates, x2, fw, *, tile):
    t, d = x2.shape
    return pl.pallas_call(
        _combine_kernel,
        grid=(t // tile,),
        in_specs=[pl.BlockSpec((tile * TOP_K,), lambda i: (i,), memory_space=pltpu.SMEM),
                  pl.BlockSpec(memory_space=pl.ANY),
                  pl.BlockSpec((tile, LANES), lambda i: (i, 0)),
                  pl.BlockSpec((tile, d), lambda i: (i, 0)),
                  pl.BlockSpec((1, d), lambda i: (0, 0))],
        out_specs=pl.BlockSpec((tile, d), lambda i: (i, 0)),
        out_shape=jax.ShapeDtypeStruct((t, d), F32),
        scratch_shapes=[pltpu.VMEM((TOP_K, tile, d), F32), pltpu.SemaphoreType.DMA(())],
        compiler_params=pltpu.CompilerParams(
            dimension_semantics=("arbitrary",), vmem_limit_bytes=VMEM_LIMIT),
        name="combine",
    )(dest_flat, yb, gates, x2, fw)


def _pad_lanes(a, offset=0, fill=0.0):
    out = jnp.full((1, LANES), fill, a.dtype)
    return out.at[0, offset:offset + a.shape[0]].set(a)


def _layer(x, attn_norm_w, w_in, gdn_conv_w, gdn_a_log, gdn_dt_bias, gdn_norm_w,
           cf_dw_w, cf_dw_b, cf_ln_w, cf_ln_b, w_out, ffn_norm_w, w_router, b_router,
           w_gate_up, b_gate_up, w_down, b_down, final_norm_w, apply_final):
    b, s, d = x.shape
    t = b * s
    qk = GDN_HEADS * HEAD_DIM
    cfc = cf_dw_w.shape[1]
    off_b = 4 * qk
    off_cf = off_b + 2 * GDN_HEADS

    wm = w_in[:, :off_b].astype(BF16)
    wba = jnp.zeros((d, LANES), F32).at[:, :2 * GDN_HEADS].set(w_in[:, off_b:off_cf]).astype(BF16)
    wcf = w_in[:, off_cf:].astype(BF16)
    alog = _pad_lanes(gdn_a_log, GDN_HEADS)
    dtb = _pad_lanes(gdn_dt_bias, GDN_HEADS)

    q, k, v, z, bg, bgt, u = _inproj(
        x, attn_norm_w[None, :], wm, wba, wcf, gdn_conv_w, alog, dtb,
        cf_dw_w, cf_dw_b[None, :], cf_ln_w[None, :], cf_ln_b[None, :], tile=min(512, s))
    o = _gdn(q, k, v, bg, bgt, tile=min(256, s))

    wr = jnp.zeros((d, LANES), F32).at[:, :N_EXPERTS].set(w_router)
    br = _pad_lanes(b_router)
    x2, xn, route, gates, counts = _outproj_router(
        o.reshape(t, qk), z.reshape(t, qk), u.reshape(t, cfc), x.reshape(t, d),
        gdn_norm_w[None, :], w_out[:qk].astype(BF16), w_out[qk:].astype(BF16),
        ffn_norm_w[None, :], wr, br, tile=min(512, t))

    cnt = counts[0, :N_EXPERTS]
    nblk = (cnt + EXPERT_BLOCK - 1) // EXPERT_BLOCK
    blk_end = jnp.cumsum(nblk)
    pstart = (blk_end - nblk) * EXPERT_BLOCK
    n_blocks = (t * TOP_K) // EXPERT_BLOCK + N_EXPERTS
    blk_ids = jnp.arange(n_blocks, dtype=I32)
    blk_exp = jnp.minimum(
        jnp.sum((blk_end[None, :] <= blk_ids[:, None]).astype(I32), axis=1), N_EXPERTS - 1)
    n_used = blk_end[-1:].astype(I32)
    row_end = (pstart + cnt)[blk_exp]
    blk_valid = jnp.clip(row_end - jnp.arange(n_blocks, dtype=I32) * EXPERT_BLOCK, 0, EXPERT_BLOCK)
    blk_valid = jnp.where(jnp.arange(n_blocks) < n_used[0], blk_valid, 0).astype(I32)

    dest = _dest(route, _pad_lanes(pstart.astype(I32)), tile=min(2048, t))
    dest_flat = dest.reshape(t * TOP_K)
    xb = _scatter_rows(dest_flat, xn, n_blocks * EXPERT_BLOCK, tile=min(256, t))
    yb = _experts(blk_exp, n_used, blk_valid, xb, w_gate_up, b_gate_up[:, None, :], w_down, b_down[:, None, :])
    out = _combine(dest_flat, yb, gates, x2, final_norm_w[None, :], tile=min(256, t))
    return out.reshape(b, s, d)


def kernel(x, attn_norm_w, w_in, gdn_conv_w, gdn_a_log, gdn_dt_bias, gdn_norm_w, cf_dw_w, cf_dw_b,
           cf_ln_w, cf_ln_b, w_out, ffn_norm_w, w_router, b_router, w_gate_up, b_gate_up, w_down,
           b_down, final_norm_w):
    depth = w_in.shape[0]
    assert depth == 1, "the fused final norm assumes a single trunk layer"
    return _layer(x, attn_norm_w[0], w_in[0], gdn_conv_w[0], gdn_a_log[0], gdn_dt_bias[0],
                  gdn_norm_w[0], cf_dw_w[0], cf_dw_b[0], cf_ln_w[0], cf_ln_b[0], w_out[0],
                  ffn_norm_w[0], w_router[0], b_router[0], w_gate_up[0], b_gate_up[0], w_down[0],
                  b_down[0], final_norm_w, True)
```

```python
import functools

import jax
import jax.numpy as jnp
from jax import lax
from jax.experimental import pallas as pl
from jax.experimental.pallas import tpu as pltpu

F32 = jnp.float32
BF16 = jnp.bfloat16
I32 = jnp.int32

NORM_EPS = 1e-6
LANES = 128
SUBLANES = 8
GDN_HEADS = 4
HEAD_DIM = 128
GDN_CHUNK = 64
GDN_CONV = 4
CF_KERNEL = 31
N_EXPERTS = 32
TOP_K = 4
SWIGLU_LIMIT = 7.0
SWIGLU_ALPHA = 1.702

QKV_HALO = 8
CF_HALO = 32
EXPERT_BLOCK = 256
VMEM_LIMIT = 56 * 1024 * 1024


def _silu(x):
    return x * jax.nn.sigmoid(x)


def _dot(a, b):
    return jnp.dot(a, b, preferred_element_type=F32)


def _dot_nt(a, b):
    return lax.dot_general(a, b, (((1,), (1,)), ((), ())), preferred_element_type=F32)


def _dot_tn(a, b):
    return lax.dot_general(a, b, (((0,), (0,)), ((), ())), preferred_element_type=F32)


def _inproj_kernel(x_ref, nw_ref, wm_ref, wba_ref, wcf_ref, cw_ref, alog_ref, dtb_ref,
                   dww_ref, dwb_ref, lnw_ref, lnb_ref,
                   q_ref, k_ref, v_ref, z_ref, bg_ref, bgt_ref, u_ref,
                   qkv_buf, cf_buf, cf_shift):
    tt = x_ref.shape[1]
    qk = GDN_HEADS * HEAD_DIM
    cfc = u_ref.shape[2]

    @pl.when(pl.program_id(1) == 0)
    def _():
        qkv_buf[0:QKV_HALO, :] = jnp.zeros((QKV_HALO, qkv_buf.shape[1]), F32)
        cf_buf[0:CF_HALO, :] = jnp.zeros((CF_HALO, cf_buf.shape[1]), F32)

    x = x_ref[0]
    h = x * lax.rsqrt(jnp.mean(x * x, axis=-1, keepdims=True) + NORM_EPS) * nw_ref[...]
    h = h.astype(BF16)
    pm = _dot(h, wm_ref[...])
    pba = _dot(h, wba_ref[...])
    pcf = _dot(h, wcf_ref[...])

    z_ref[0] = pm[:, 3 * qk:]

    qkv_buf[QKV_HALO:QKV_HALO + tt, :] = pm[:, :3 * qk]
    acc = None
    for j in range(GDN_CONV):
        term = cw_ref[j:j + 1, :] * qkv_buf[pl.ds(QKV_HALO - (GDN_CONV - 1) + j, tt), :]
        acc = term if acc is None else acc + term
    qkv_buf[0:QKV_HALO, :] = qkv_buf[tt:tt + QKV_HALO, :]
    qkv = _silu(acc)
    for hd in range(GDN_HEADS):
        for base, ref in ((0, q_ref), (qk, k_ref)):
            t = qkv[:, base + hd * HEAD_DIM: base + (hd + 1) * HEAD_DIM]
            t = t * lax.rsqrt(jnp.sum(t * t, axis=-1, keepdims=True) + NORM_EPS)
            ref[0, :, hd * HEAD_DIM:(hd + 1) * HEAD_DIM] = t
    v_ref[0] = qkv[:, 2 * qk:]

    lane = lax.broadcasted_iota(I32, (tt, LANES), 1)
    row = lax.broadcasted_iota(I32, (tt, LANES), 0)
    beta = jax.nn.sigmoid(pba)
    sp_in = pba + dtb_ref[...]
    softplus = jnp.maximum(sp_in, 0.0) + jnp.log(1.0 + jnp.exp(-jnp.abs(sp_in)))
    g = -jnp.exp(alog_ref[...]) * softplus
    g = jnp.where((lane >= GDN_HEADS) & (lane < 2 * GDN_HEADS), g, 0.0)
    pos = row % GDN_CHUNK
    shift = 1
    while shift < GDN_CHUNK:
        g = g + jnp.where(pos >= shift, pltpu.roll(g, shift, 0), 0.0)
        shift *= 2
    bg = jnp.where(lane < GDN_HEADS, beta, g)
    bg_ref[0] = bg
    bgt_ref[0] = jnp.transpose(bg)[0:SUBLANES, :]

    glu = pcf[:, :cfc] * jax.nn.sigmoid(pcf[:, cfc:])
    cf_buf[CF_HALO:CF_HALO + tt, :] = glu
    lo = SUBLANES
    span = tt + CF_HALO - lo
    for r in range(1, SUBLANES):
        cf_shift[r - 1, lo:lo + span, :] = cf_buf[pl.ds(lo - r, span), :]
    rows = 64
    for r0 in range(0, tt, rows):
        acc = None
        for j in range(CF_KERNEL):
            a, r = divmod(CF_KERNEL - 1 - j, SUBLANES)
            start = CF_HALO + r0 - a * SUBLANES
            src = cf_buf[start:start + rows, :] if r == 0 else cf_shift[r - 1, start:start + rows, :]
            term = dww_ref[j:j + 1, :] * src
            acc = term if acc is None else acc + term
        c = acc + dwb_ref[...]
        mu = jnp.mean(c, axis=-1, keepdims=True)
        cc = c - mu
        y = cc * lax.rsqrt(jnp.mean(cc * cc, axis=-1, keepdims=True) + NORM_EPS)
        u_ref[0, r0:r0 + rows, :] = _silu(y * lnw_ref[...] + lnb_ref[...])
    cf_buf[0:CF_HALO, :] = cf_buf[tt:tt + CF_HALO, :]


def _inproj(x, nw, wm, wba, wcf, cw, alog, dtb, dww, dwb, lnw, lnb, *, tile):
    b, s, d = x.shape
    qk = GDN_HEADS * HEAD_DIM
    cfc = dww.shape[1]
    grid = (b, s // tile)
    full = lambda a: pl.BlockSpec(a.shape, lambda i, j: (0,) * a.ndim)
    tok = lambda w: pl.BlockSpec((1, tile, w), lambda i, j: (i, j, 0))
    out_shape = (
        jax.ShapeDtypeStruct((b, s, qk), F32), jax.ShapeDtypeStruct((b, s, qk), F32),
        jax.ShapeDtypeStruct((b, s, qk), F32), jax.ShapeDtypeStruct((b, s, qk), F32),
        jax.ShapeDtypeStruct((b, s, LANES), F32), jax.ShapeDtypeStruct((b, SUBLANES, s), F32),
        jax.ShapeDtypeStruct((b, s, cfc), F32))
    return pl.pallas_call(
        _inproj_kernel,
        grid=grid,
        in_specs=[tok(d)] + [full(a) for a in (nw, wm, wba, wcf, cw, alog, dtb, dww, dwb, lnw, lnb)],
        out_specs=(tok(qk), tok(qk), tok(qk), tok(qk), tok(LANES),
                   pl.BlockSpec((1, SUBLANES, tile), lambda i, j: (i, 0, j)), tok(cfc)),
        out_shape=out_shape,
        scratch_shapes=[pltpu.VMEM((QKV_HALO + tile, 3 * qk), F32),
                        pltpu.VMEM((CF_HALO + tile, cfc), F32),
                        pltpu.VMEM((SUBLANES - 1, CF_HALO + tile, cfc), F32)],
        compiler_params=pltpu.CompilerParams(
            dimension_semantics=("arbitrary", "arbitrary"), vmem_limit_bytes=VMEM_LIMIT),
        name="inproj",
    )(x, nw, wm, wba, wcf, cw, alog, dtb, dww, dwb, lnw, lnb)


def _bmm(a, b):
    return jnp.einsum("bmk,bkn->bmn", a, b, preferred_element_type=F32)


def _bmm_nt(a, b):
    return jnp.einsum("bmk,bnk->bmn", a, b, preferred_element_type=F32)


def _bmm_tn(a, b):
    return jnp.einsum("bkm,bkn->bmn", a, b, preferred_element_type=F32)


def _unit_lower_inverse(a):
    c = a.shape[-1]
    ii = lax.broadcasted_iota(I32, (c, c), 0)
    jj = lax.broadcasted_iota(I32, (c, c), 1)
    eye = (ii == jj).astype(F32)
    same16 = (ii // 16) == (jj // 16)
    same32 = (ii // 32) == (jj // 32)
    x = jnp.where(same16, -a, 0.0)
    t = eye + x
    xp = x
    for _ in range(3):
        xp_b = xp.astype(BF16)
        xp = _bmm(xp_b, xp_b)
        t = t + _bmm(t.astype(BF16), xp.astype(BF16))
    for off in (jnp.where(same32 & ~same16, a, 0.0), jnp.where(~same32, a, 0.0)):
        tb = t.astype(BF16)
        t = t - _bmm(tb, _bmm(off.astype(BF16), tb).astype(BF16))
    return t


def _gdn_kernel(q_ref, k_ref, v_ref, bg_ref, bgt_ref, o_ref, state, s_all):
    lt = q_ref.shape[1]
    c = GDN_CHUNK
    nh = GDN_HEADS
    nc = lt // c

    @pl.when(pl.program_id(1) == 0)
    def _():
        state[...] = jnp.zeros(state.shape, F32)

    def stack(fn):
        return jnp.stack([fn(slice(n * c, (n + 1) * c), h) for n in range(nc) for h in range(nh)])

    head = lambda h: slice(h * HEAD_DIM, (h + 1) * HEAD_DIM)
    q = stack(lambda r, h: q_ref[0, r, head(h)]) * (HEAD_DIM ** -0.5)
    k = stack(lambda r, h: k_ref[0, r, head(h)])
    v = stack(lambda r, h: v_ref[0, r, head(h)])
    beta = stack(lambda r, h: bg_ref[0, r, h:h + 1])
    gcol = stack(lambda r, h: bg_ref[0, r, nh + h:nh + h + 1])
    grow = stack(lambda r, h: bgt_ref[0, nh + h:nh + h + 1, r])

    ii = lax.broadcasted_iota(I32, (c, c), 0)
    jj = lax.broadcasted_iota(I32, (c, c), 1)
    glast = gcol[:, c - 1:c, :]
    eg = jnp.exp(gcol)
    decay = jnp.where(ii >= jj, jnp.exp(jnp.minimum(gcol - grow, 0.0)), 0.0)
    kb = k * beta
    k_b = k.astype(BF16)
    a = jnp.where(ii > jj, _bmm_nt(kb.astype(BF16), k_b) * decay, 0.0)
    t = _unit_lower_inverse(a)
    rhs = jnp.concatenate([v * beta, kb * eg], axis=-1).astype(BF16)
    sol = _bmm(t.astype(BF16), rhs)
    u_val = sol[..., :HEAD_DIM]
    w_key = sol[..., HEAD_DIM:]
    intra = _bmm_nt(q.astype(BF16), k_b) * decay
    k_tail = (k * jnp.exp(glast - gcol)).astype(BF16)
    upd = _bmm_tn(k_tail, sol.astype(BF16))
    b_mat = upd[..., :HEAD_DIM]
    p_mat = upd[..., HEAD_DIM:].astype(BF16)
    g_tot = jnp.exp(glast)

    s = state[...]
    for n in range(nc):
        grp = slice(n * nh, (n + 1) * nh)
        s_b = s.astype(BF16)
        s_all[grp] = s_b
        s = s * g_tot[grp] - _bmm(p_mat[grp], s_b) + b_mat[grp]
    state[...] = s

    wq = jnp.concatenate([w_key, q * eg], axis=1).astype(BF16)
    ws_qs = _bmm(wq, s_all[...])
    v_new = u_val - ws_qs[:, :c]
    o = ws_qs[:, c:] + _bmm(intra.astype(BF16), v_new.astype(BF16))
    for n in range(nc):
        for h in range(nh):
            o_ref[0, n * c:(n + 1) * c, head(h)] = o[n * nh + h]


def _gdn(q, k, v, bg, bgt, *, tile):
    b, s, qk = q.shape
    grid = (b, s // tile)
    tok = lambda w: pl.BlockSpec((1, tile, w), lambda i, j: (i, j, 0))
    n_prob = (tile // GDN_CHUNK) * GDN_HEADS
    return pl.pallas_call(
        _gdn_kernel,
        grid=grid,
        in_specs=[tok(qk), tok(qk), tok(qk), tok(LANES),
                  pl.BlockSpec((1, SUBLANES, tile), lambda i, j: (i, 0, j))],
        out_specs=tok(qk),
        out_shape=jax.ShapeDtypeStruct((b, s, qk), F32),
        scratch_shapes=[pltpu.VMEM((GDN_HEADS, HEAD_DIM, HEAD_DIM), F32),
                        pltpu.VMEM((n_prob, HEAD_DIM, HEAD_DIM), BF16)],
        compiler_params=pltpu.CompilerParams(
            dimension_semantics=("arbitrary", "arbitrary"), vmem_limit_bytes=VMEM_LIMIT),
        name="gdn",
    )(q, k, v, bg, bgt)


def _split_bf16(x):
    hi = x.astype(BF16)
    lo = (x - hi.astype(F32)).astype(BF16)
    return hi, lo


def _outproj_router_kernel(o_ref, z_ref, u_ref, x_ref, gnw_ref, wa_ref, wb_ref, fnw_ref,
                           wr_ref, br_ref,
                           x2_ref, xn_ref, route_ref, gate_ref, counts_ref, carry):
    tt = x_ref.shape[0]

    @pl.when(pl.program_id(0) == 0)
    def _():
        carry[...] = jnp.zeros(carry.shape, F32)

    parts = []
    for hd in range(GDN_HEADS):
        sl = slice(hd * HEAD_DIM, (hd + 1) * HEAD_DIM)
        oh = o_ref[:, sl]
        y = oh * lax.rsqrt(jnp.mean(oh * oh, axis=-1, keepdims=True) + NORM_EPS) * gnw_ref[...]
        parts.append((y * _silu(z_ref[:, sl])).astype(BF16))
    out_a = jnp.concatenate(parts, axis=-1)
    x2 = x_ref[...] + _dot(out_a, wa_ref[...]) + _dot(u_ref[...].astype(BF16), wb_ref[...])
    x2_ref[...] = x2

    xn = x2 * lax.rsqrt(jnp.mean(x2 * x2, axis=-1, keepdims=True) + NORM_EPS) * fnw_ref[...]
    xn_ref[...] = xn

    xh, xl = _split_bf16(xn)
    wh, wl = _split_bf16(wr_ref[...])
    logits = _dot(xh, wh) + _dot(xh, wl) + _dot(xl, wh) + br_ref[...]

    lane = lax.broadcasted_iota(I32, (tt, LANES), 1)
    lane_f = lane.astype(F32)
    neg = jnp.float32(-jnp.inf)
    work = jnp.where(lane < N_EXPERTS, logits, neg)
    vals, idxs = [], []
    onehot = jnp.zeros((tt, LANES), F32)
    for _ in range(TOP_K):
        m = jnp.max(work, axis=-1, keepdims=True)
        idx = jnp.min(jnp.where(work == m, lane_f, float(LANES)), axis=-1, keepdims=True).astype(I32)
        sel = lane == idx
        vals.append(m)
        idxs.append(idx)
        onehot = onehot + sel.astype(F32)
        work = jnp.where(sel, neg, work)
    exps = [jnp.exp(v - vals[0]) for v in vals]
    denom = exps[0] + exps[1] + exps[2] + exps[3]

    ri = lax.broadcasted_iota(I32, (tt, tt), 0)
    ci = lax.broadcasted_iota(I32, (tt, tt), 1)
    strict = (ri > ci).astype(BF16)
    base = carry[...] + _dot(strict, onehot.astype(BF16))
    route = jnp.zeros((tt, LANES), I32)
    gates = jnp.zeros((tt, LANES), F32)
    for kk in range(TOP_K):
        rank = jnp.sum(jnp.where(lane == idxs[kk], base, 0.0), axis=-1, keepdims=True)
        route = jnp.where(lane == kk, idxs[kk], route)
        route = jnp.where(lane == kk + TOP_K, rank.astype(I32), route)
        gates = jnp.where(lane == kk, exps[kk] / denom, gates)
    route_ref[...] = route
    gate_ref[...] = gates
    new_carry = carry[...] + jnp.sum(onehot, axis=0, keepdims=True)
    carry[...] = new_carry
    counts_ref[...] = new_carry.astype(I32)


def _outproj_router(o, z, u, x, gnw, wa, wb, fnw, wr, br, *, tile):
    t, d = x.shape
    grid = (t // tile,)
    full = lambda a: pl.BlockSpec(a.shape, lambda i: (0,) * a.ndim)
    tok = lambda w: pl.BlockSpec((tile, w), lambda i: (i, 0))
    return pl.pallas_call(
        _outproj_router_kernel,
        grid=grid,
        in_specs=[tok(o.shape[1]), tok(z.shape[1]), tok(u.shape[1]), tok(d)]
                 + [full(a) for a in (gnw, wa, wb, fnw, wr, br)],
        out_specs=(tok(d), tok(d), tok(LANES), tok(LANES), pl.BlockSpec((1, LANES), lambda i: (0, 0))),
        out_shape=(jax.ShapeDtypeStruct((t, d), F32), jax.ShapeDtypeStruct((t, d), F32),
                   jax.ShapeDtypeStruct((t, LANES), I32), jax.ShapeDtypeStruct((t, LANES), F32),
                   jax.ShapeDtypeStruct((1, LANES), I32)),
        scratch_shapes=[pltpu.VMEM((1, LANES), F32)],
        compiler_params=pltpu.CompilerParams(
            dimension_semantics=("arbitrary",), vmem_limit_bytes=VMEM_LIMIT),
        name="outproj_router",
    )(o, z, u, x, gnw, wa, wb, fnw, wr, br)


def _dest_kernel(route_ref, pstart_ref, dest_ref):
    route = route_ref[...].astype(F32)
    tt = route.shape[0]
    lane = lax.broadcasted_iota(I32, (tt, LANES), 1)
    pstart = pstart_ref[...].astype(F32)
    dest = jnp.zeros((tt, LANES), F32)
    for kk in range(TOP_K):
        idx = jnp.sum(jnp.where(lane == kk, route, 0.0), axis=-1, keepdims=True)
        rank = jnp.sum(jnp.where(lane == kk + TOP_K, route, 0.0), axis=-1, keepdims=True)
        start = jnp.sum(jnp.where(lane == idx.astype(I32), pstart, 0.0), axis=-1, keepdims=True)
        dest = jnp.where(lane == kk, start + rank, dest)
    dest_ref[...] = dest[:, :TOP_K].astype(I32)


def _dest(route, pstart, *, tile):
    t = route.shape[0]
    return pl.pallas_call(
        _dest_kernel,
        grid=(t // tile,),
        in_specs=[pl.BlockSpec((tile, LANES), lambda i: (i, 0)),
                  pl.BlockSpec((1, LANES), lambda i: (0, 0))],
        out_specs=pl.BlockSpec((tile, TOP_K), lambda i: (i, 0)),
        out_shape=jax.ShapeDtypeStruct((t, TOP_K), I32),
        compiler_params=pltpu.CompilerParams(dimension_semantics=("arbitrary",)),
        name="dest_rows",
    )(route, pstart)


def _scatter_kernel(dest_ref, xn_ref, xb_ref, sem):
    tt = xn_ref.shape[0]

    def issue(i, carry):
        for kk in range(TOP_K):
            row = dest_ref[i * TOP_K + kk]
            pltpu.make_async_copy(xn_ref.at[pl.ds(i, 1), :], xb_ref.at[pl.ds(row, 1), :], sem).start()
        return carry

    lax.fori_loop(0, tt, issue, 0)
    for kk in range(TOP_K):
        pltpu.make_async_copy(xn_ref, xb_ref.at[pl.ds(0, tt), :], sem).wait()


def _scatter_rows(dest_flat, xn, n_rows, *, tile):
    t, d = xn.shape
    return pl.pallas_call(
        _scatter_kernel,
        grid=(t // tile,),
        in_specs=[pl.BlockSpec((tile * TOP_K,), lambda i: (i,), memory_space=pltpu.SMEM),
                  pl.BlockSpec((tile, d), lambda i: (i, 0))],
        out_specs=pl.BlockSpec(memory_space=pl.ANY),
        out_shape=jax.ShapeDtypeStruct((n_rows, d), xn.dtype),
        scratch_shapes=[pltpu.SemaphoreType.DMA(())],
        compiler_params=pltpu.CompilerParams(dimension_semantics=("arbitrary",)),
        name="scatter_rows",
    )(dest_flat, xn)


def _expert_kernel(blk_exp_ref, n_used_ref, blk_valid_ref, xb_ref, wgu_ref, bgu_ref, wd_ref, bd_ref,
                   yb_ref, wgu_b, wd_b):
    i = pl.program_id(0)
    dff = wd_ref.shape[1]
    prev = blk_exp_ref[jnp.maximum(i - 1, 0)]
    changed = (i == 0) | (blk_exp_ref[i] != prev)

    @pl.when(changed & (i < n_used_ref[0]))
    def _():
        wgu_b[...] = wgu_ref[0].astype(BF16)
        wd_b[...] = wd_ref[0].astype(BF16)

    @pl.when(i < n_used_ref[0])
    def _():
        rid = lax.broadcasted_iota(I32, (xb_ref.shape[0], 1), 0)
        xb = jnp.where(rid < blk_valid_ref[i], xb_ref[...], 0.0).astype(BF16)
        hid = _dot(xb, wgu_b[...]) + bgu_ref[0]
        gate = jnp.minimum(hid[:, :dff], SWIGLU_LIMIT)
        up = jnp.clip(hid[:, dff:], -SWIGLU_LIMIT, SWIGLU_LIMIT)
        glu = gate * jax.nn.sigmoid(SWIGLU_ALPHA * gate)
        act = ((up + 1.0) * glu).astype(BF16)
        yb_ref[...] = _dot(act, wd_b[...]) + bd_ref[0]


def _experts(blk_exp, n_used, blk_valid, xb, wgu, bgu, wd, bd):
    rows, d = xb.shape
    n_blocks = rows // EXPERT_BLOCK
    two_f = wgu.shape[2]
    dff = wd.shape[1]
    blk = lambda i, be, nu, bv: (jnp.minimum(i, nu[0] - 1), 0)
    exp3 = lambda i, be, nu, bv: (be[jnp.minimum(i, nu[0] - 1)], 0, 0)
    grid_spec = pltpu.PrefetchScalarGridSpec(
        num_scalar_prefetch=3,
        grid=(n_blocks,),
        in_specs=[pl.BlockSpec((EXPERT_BLOCK, d), blk),
                  pl.BlockSpec((1, d, two_f), exp3),
                  pl.BlockSpec((1, 1, two_f), exp3),
                  pl.BlockSpec((1, dff, d), exp3),
                  pl.BlockSpec((1, 1, d), exp3)],
        out_specs=pl.BlockSpec((EXPERT_BLOCK, d), blk),
        scratch_shapes=[pltpu.VMEM((d, two_f), BF16), pltpu.VMEM((dff, d), BF16)])
    return pl.pallas_call(
        _expert_kernel,
        grid_spec=grid_spec,
        out_shape=jax.ShapeDtypeStruct((rows, d), F32),
        compiler_params=pltpu.CompilerParams(
            dimension_semantics=("arbitrary",), vmem_limit_bytes=VMEM_LIMIT),
        name="experts",
    )(blk_exp, n_used, blk_valid, xb, wgu, bgu, wd, bd)


def _combine_kernel(dest_ref, yb_ref, gate_ref, x2_ref, fw_ref, out_ref, buf, sem):
    tt = x2_ref.shape[0]

    def issue(i, carry):
        for kk in range(TOP_K):
            row = dest_ref[i * TOP_K + kk]
            pltpu.make_async_copy(yb_ref.at[pl.ds(row, 1), :], buf.at[kk, pl.ds(i, 1), :], sem).start()
        return carry

    lax.fori_loop(0, tt, issue, 0)
    for kk in range(TOP_K):
        pltpu.make_async_copy(yb_ref.at[pl.ds(0, tt), :], buf.at[kk], sem).wait()

    gates = gate_ref[...]
    lane = lax.broadcasted_iota(I32, gates.shape, 1)
    x3 = x2_ref[...]
    for kk in range(TOP_K):
        gk = jnp.sum(jnp.where(lane == kk, gates, 0.0), axis=-1, keepdims=True)
        x3 = x3 + gk * buf[kk]
    out_ref[...] = x3 * lax.rsqrt(jnp.mean(x3 * x3, axis=-1, keepdims=True) + NORM_EPS) * fw_ref[...]


def _combine(dest_flat, yb, gates, x2, fw, *, tile):
    t, d = x2.shape
    return pl.pallas_call(
        _combine_kernel,
        grid=(t // tile,),
        in_specs=[pl.BlockSpec((tile * TOP_K,), lambda i: (i,), memory_space=pltpu.SMEM),
                  pl.BlockSpec(memory_space=pl.ANY),
                  pl.BlockSpec((tile, LANES), lambda i: (i, 0)),
                  pl.BlockSpec((tile, d), lambda i: (i, 0)),
                  pl.BlockSpec((1, d), lambda i: (0, 0))],
        out_specs=pl.BlockSpec((tile, d), lambda i: (i, 0)),
        out_shape=jax.ShapeDtypeStruct((t, d), F32),
        scratch_shapes=[pltpu.VMEM((TOP_K, tile, d), F32), pltpu.SemaphoreType.DMA(())],
        compiler_params=pltpu.CompilerParams(
            dimension_semantics=("arbitrary",), vmem_limit_bytes=VMEM_LIMIT),
        name="combine",
    )(dest_flat, yb, gates, x2, fw)


def _pad_lanes(a, offset=0, fill=0.0):
    out = jnp.full((1, LANES), fill, a.dtype)
    return out.at[0, offset:offset + a.shape[0]].set(a)


def _layer(x, attn_norm_w, w_in, gdn_conv_w, gdn_a_log, gdn_dt_bias, gdn_norm_w,
           cf_dw_w, cf_dw_b, cf_ln_w, cf_ln_b, w_out, ffn_norm_w, w_router, b_router,
           w_gate_up, b_gate_up, w_down, b_down, final_norm_w, apply_final):
    b, s, d = x.shape
    t = b * s
    qk = GDN_HEADS * HEAD_DIM
    cfc = cf_dw_w.shape[1]
    off_b = 4 * qk
    off_cf = off_b + 2 * GDN_HEADS

    wm = w_in[:, :off_b].astype(BF16)
    wba = jnp.zeros((d, LANES), F32).at[:, :2 * GDN_HEADS].set(w_in[:, off_b:off_cf]).astype(BF16)
    wcf = w_in[:, off_cf:].astype(BF16)
    alog = _pad_lanes(gdn_a_log, GDN_HEADS)
    dtb = _pad_lanes(gdn_dt_bias, GDN_HEADS)

    q, k, v, z, bg, bgt, u = _inproj(
        x, attn_norm_w[None, :], wm, wba, wcf, gdn_conv_w, alog, dtb,
        cf_dw_w, cf_dw_b[None, :], cf_ln_w[None, :], cf_ln_b[None, :], tile=min(512, s))
    o = _gdn(q, k, v, bg, bgt, tile=min(256, s))

    wr = jnp.zeros((d, LANES), F32).at[:, :N_EXPERTS].set(w_router)
    br = _pad_lanes(b_router)
    x2, xn, route, gates, counts = _outproj_router(
        o.reshape(t, qk), z.reshape(t, qk), u.reshape(t, cfc), x.reshape(t, d),
        gdn_norm_w[None, :], w_out[:qk].astype(BF16), w_out[qk:].astype(BF16),
        ffn_norm_w[None, :], wr, br, tile=min(512, t))

    cnt = counts[0, :N_EXPERTS]
    nblk = (cnt + EXPERT_BLOCK - 1) // EXPERT_BLOCK
    blk_end = jnp.cumsum(nblk)
    pstart = (blk_end - nblk) * EXPERT_BLOCK
    n_blocks = (t * TOP_K) // EXPERT_BLOCK + N_EXPERTS
    blk_ids = jnp.arange(n_blocks, dtype=I32)
    blk_exp = jnp.minimum(
        jnp.sum((blk_end[None, :] <= blk_ids[:, None]).astype(I32), axis=1), N_EXPERTS - 1)
    n_used = blk_end[-1:].astype(I32)
    row_end = (pstart + cnt)[blk_exp]
    blk_valid = jnp.clip(row_end - jnp.arange(n_blocks, dtype=I32) * EXPERT_BLOCK, 0, EXPERT_BLOCK)
    blk_valid = jnp.where(jnp.arange(n_blocks) < n_used[0], blk_valid, 0).astype(I32)

    dest = _dest(route, _pad_lanes(pstart.astype(I32)), tile=min(2048, t))
    dest_flat = dest.reshape(t * TOP_K)
    xb = _scatter_rows(dest_flat, xn, n_blocks * EXPERT_BLOCK, tile=min(256, t))
    yb = _experts(blk_exp, n_used, blk_valid, xb, w_gate_up, b_gate_up[:, None, :], w_down, b_down[:, None, :])
    out = _combine(dest_flat, yb, gates, x2, final_norm_w[None, :], tile=min(256, t))
    return out.reshape(b, s, d)


def kernel(x, attn_norm_w, w_in, gdn_conv_w, gdn_a_log, gdn_dt_bias, gdn_norm_w, cf_dw_w, cf_dw_b,
           cf_ln_w, cf_ln_b, w_out, ffn_norm_w, w_router, b_router, w_gate_up, b_gate_up, w_down,
           b_down, final_norm_w):
    depth = w_in.shape[0]
    assert depth == 1, "the fused final norm assumes a single trunk layer"
    return _layer(x, attn_norm_w[0], w_in[0], gdn_conv_w[0], gdn_a_log[0], gdn_dt_bias[0],
                  gdn_norm_w[0], cf_dw_w[0], cf_dw_b[0], cf_ln_w[0], cf_ln_b[0], w_out[0],
                  ffn_norm_w[0], w_router[0], b_router[0], w_gate_up[0], b_gate_up[0], w_down[0],
                  b_down[0], final_norm_w, True)
```

```python
import functools

import jax
import jax.numpy as jnp
from jax import lax
from jax.experimental import pallas as pl
from jax.experimental.pallas import tpu as pltpu

F32 = jnp.float32
BF16 = jnp.bfloat16
I32 = jnp.int32

NORM_EPS = 1e-6
LANES = 128
SUBLANES = 8
GDN_HEADS = 4
HEAD_DIM = 128
GDN_CHUNK = 64
GDN_CONV = 4
CF_KERNEL = 31
N_EXPERTS = 32
TOP_K = 4
TOP_K_SHIFT = 2
DMA_UNROLL = 8
SWIGLU_LIMIT = 7.0
SWIGLU_ALPHA = 1.702

QKV_HALO = 8
CF_HALO = 32
EXPERT_BLOCK = 256
VMEM_LIMIT = 56 * 1024 * 1024


def _silu(x):
    return x * jax.nn.sigmoid(x)


def _dot(a, b):
    return jnp.dot(a, b, preferred_element_type=F32)


def _dot_nt(a, b):
    return lax.dot_general(a, b, (((1,), (1,)), ((), ())), preferred_element_type=F32)


def _dot_tn(a, b):
    return lax.dot_general(a, b, (((0,), (0,)), ((), ())), preferred_element_type=F32)


def _store_token_tiles(ref, x, lead=()):
    rows, d = x.shape
    chunks = d // LANES
    for c in range(chunks):
        ref[lead + (pl.ds(c, rows, stride=chunks), slice(None))] = x[:, c * LANES:(c + 1) * LANES]


def _load_token_tiles(ref, rows, lead=()):
    chunks = ref.shape[-2] // rows
    return jnp.concatenate(
        [ref[lead + (pl.ds(c, rows, stride=chunks), slice(None))] for c in range(chunks)], axis=1)


def _inproj_kernel(x_ref, nw_ref, wm_ref, wba_ref, wcf_ref, cw_ref, alog_ref, dtb_ref,
                   dww_ref, dwb_ref, lnw_ref, lnb_ref,
                   q_ref, k_ref, v_ref, z_ref, bg_ref, bgt_ref, u_ref,
                   qkv_buf, cf_buf, cf_shift):
    tt = x_ref.shape[1]
    qk = GDN_HEADS * HEAD_DIM
    cfc = u_ref.shape[2]

    @pl.when(pl.program_id(1) == 0)
    def _():
        qkv_buf[0:QKV_HALO, :] = jnp.zeros((QKV_HALO, qkv_buf.shape[1]), F32)
        cf_buf[0:CF_HALO, :] = jnp.zeros((CF_HALO, cf_buf.shape[1]), F32)

    x = x_ref[0]
    h = x * lax.rsqrt(jnp.mean(x * x, axis=-1, keepdims=True) + NORM_EPS) * nw_ref[...]
    h = h.astype(BF16)
    pm = _dot(h, wm_ref[...])
    pba = _dot(h, wba_ref[...])
    pcf = _dot(h, wcf_ref[...])

    z_ref[0] = pm[:, 3 * qk:]

    qkv_buf[QKV_HALO:QKV_HALO + tt, :] = pm[:, :3 * qk]
    acc = None
    for j in range(GDN_CONV):
        term = cw_ref[j:j + 1, :] * qkv_buf[pl.ds(QKV_HALO - (GDN_CONV - 1) + j, tt), :]
        acc = term if acc is None else acc + term
    qkv_buf[0:QKV_HALO, :] = qkv_buf[tt:tt + QKV_HALO, :]
    qkv = _silu(acc)
    for hd in range(GDN_HEADS):
        for base, ref in ((0, q_ref), (qk, k_ref)):
            t = qkv[:, base + hd * HEAD_DIM: base + (hd + 1) * HEAD_DIM]
            t = t * lax.rsqrt(jnp.sum(t * t, axis=-1, keepdims=True) + NORM_EPS)
            ref[0, :, hd * HEAD_DIM:(hd + 1) * HEAD_DIM] = t
    v_ref[0] = qkv[:, 2 * qk:]

    lane = lax.broadcasted_iota(I32, (tt, LANES), 1)
    row = lax.broadcasted_iota(I32, (tt, LANES), 0)
    beta = jax.nn.sigmoid(pba)
    sp_in = pba + dtb_ref[...]
    softplus = jnp.maximum(sp_in, 0.0) + jnp.log(1.0 + jnp.exp(-jnp.abs(sp_in)))
    g = -jnp.exp(alog_ref[...]) * softplus
    g = jnp.where((lane >= GDN_HEADS) & (lane < 2 * GDN_HEADS), g, 0.0)
    pos = row % GDN_CHUNK
    shift = 1
    while shift < GDN_CHUNK:
        g = g + jnp.where(pos >= shift, pltpu.roll(g, shift, 0), 0.0)
        shift *= 2
    bg = jnp.where(lane < GDN_HEADS, beta, g)
    bg_ref[0] = bg
    bgt_ref[0] = jnp.transpose(bg)[0:SUBLANES, :]

    glu = pcf[:, :cfc] * jax.nn.sigmoid(pcf[:, cfc:])
    cf_buf[CF_HALO:CF_HALO + tt, :] = glu
    lo = SUBLANES
    span = tt + CF_HALO - lo
    for r in range(1, SUBLANES):
        cf_shift[r - 1, lo:lo + span, :] = cf_buf[pl.ds(lo - r, span), :]
    rows = 64
    for r0 in range(0, tt, rows):
        acc = None
        for j in range(CF_KERNEL):
            a, r = divmod(CF_KERNEL - 1 - j, SUBLANES)
            start = CF_HALO + r0 - a * SUBLANES
            src = cf_buf[start:start + rows, :] if r == 0 else cf_shift[r - 1, start:start + rows, :]
            term = dww_ref[j:j + 1, :] * src
            acc = term if acc is None else acc + term
        c = acc + dwb_ref[...]
        mu = jnp.mean(c, axis=-1, keepdims=True)
        cc = c - mu
        y = cc * lax.rsqrt(jnp.mean(cc * cc, axis=-1, keepdims=True) + NORM_EPS)
        u_ref[0, r0:r0 + rows, :] = _silu(y * lnw_ref[...] + lnb_ref[...])
    cf_buf[0:CF_HALO, :] = cf_buf[tt:tt + CF_HALO, :]


def _inproj(x, nw, wm, wba, wcf, cw, alog, dtb, dww, dwb, lnw, lnb, *, tile):
    b, s, d = x.shape
    qk = GDN_HEADS * HEAD_DIM
    cfc = dww.shape[1]
    grid = (b, s // tile)
    full = lambda a: pl.BlockSpec(a.shape, lambda i, j: (0,) * a.ndim)
    tok = lambda w: pl.BlockSpec((1, tile, w), lambda i, j: (i, j, 0))
    out_shape = (
        jax.ShapeDtypeStruct((b, s, qk), F32), jax.ShapeDtypeStruct((b, s, qk), F32),
        jax.ShapeDtypeStruct((b, s, qk), F32), jax.ShapeDtypeStruct((b, s, qk), F32),
        jax.ShapeDtypeStruct((b, s, LANES), F32), jax.ShapeDtypeStruct((b, SUBLANES, s), F32),
        jax.ShapeDtypeStruct((b, s, cfc), F32))
    return pl.pallas_call(
        _inproj_kernel,
        grid=grid,
        in_specs=[tok(d)] + [full(a) for a in (nw, wm, wba, wcf, cw, alog, dtb, dww, dwb, lnw, lnb)],
        out_specs=(tok(qk), tok(qk), tok(qk), tok(qk), tok(LANES),
                   pl.BlockSpec((1, SUBLANES, tile), lambda i, j: (i, 0, j)), tok(cfc)),
        out_shape=out_shape,
        scratch_shapes=[pltpu.VMEM((QKV_HALO + tile, 3 * qk), F32),
                        pltpu.VMEM((CF_HALO + tile, cfc), F32),
                        pltpu.VMEM((SUBLANES - 1, CF_HALO + tile, cfc), F32)],
        compiler_params=pltpu.CompilerParams(
            dimension_semantics=("arbitrary", "arbitrary"), vmem_limit_bytes=VMEM_LIMIT),
        name="inproj",
    )(x, nw, wm, wba, wcf, cw, alog, dtb, dww, dwb, lnw, lnb)


def _bmm(a, b):
    return jnp.einsum("bmk,bkn->bmn", a, b, preferred_element_type=F32)


def _bmm_nt(a, b):
    return jnp.einsum("bmk,bnk->bmn", a, b, preferred_element_type=F32)


def _bmm_tn(a, b):
    return jnp.einsum("bkm,bkn->bmn", a, b, preferred_element_type=F32)


def _unit_lower_inverse(a):
    c = a.shape[-1]
    ii = lax.broadcasted_iota(I32, (c, c), 0)
    jj = lax.broadcasted_iota(I32, (c, c), 1)
    eye = (ii == jj).astype(F32)
    same16 = (ii // 16) == (jj // 16)
    same32 = (ii // 32) == (jj // 32)
    x = jnp.where(same16, -a, 0.0)
    t = eye + x
    xp = x
    for _ in range(3):
        xp_b = xp.astype(BF16)
        xp = _bmm(xp_b, xp_b)
        t = t + _bmm(t.astype(BF16), xp.astype(BF16))
    for off in (jnp.where(same32 & ~same16, a, 0.0), jnp.where(~same32, a, 0.0)):
        tb = t.astype(BF16)
        t = t - _bmm(tb, _bmm(off.astype(BF16), tb).astype(BF16))
    return t


def _gdn_kernel(q_ref, k_ref, v_ref, bg_ref, bgt_ref, o_ref, state, s_all):
    lt = q_ref.shape[1]
    c = GDN_CHUNK
    nh = GDN_HEADS
    nc = lt // c

    @pl.when(pl.program_id(1) == 0)
    def _():
        state[...] = jnp.zeros(state.shape, F32)

    def stack(fn):
        return jnp.stack([fn(slice(n * c, (n + 1) * c), h) for n in range(nc) for h in range(nh)])

    head = lambda h: slice(h * HEAD_DIM, (h + 1) * HEAD_DIM)
    q = stack(lambda r, h: q_ref[0, r, head(h)]) * (HEAD_DIM ** -0.5)
    k = stack(lambda r, h: k_ref[0, r, head(h)])
    v = stack(lambda r, h: v_ref[0, r, head(h)])
    beta = stack(lambda r, h: bg_ref[0, r, h:h + 1])
    gcol = stack(lambda r, h: bg_ref[0, r, nh + h:nh + h + 1])
    grow = stack(lambda r, h: bgt_ref[0, nh + h:nh + h + 1, r])

    ii = lax.broadcasted_iota(I32, (c, c), 0)
    jj = lax.broadcasted_iota(I32, (c, c), 1)
    glast = gcol[:, c - 1:c, :]
    eg = jnp.exp(gcol)
    decay = jnp.where(ii >= jj, jnp.exp(jnp.minimum(gcol - grow, 0.0)), 0.0)
    kb = k * beta
    k_b = k.astype(BF16)
    a = jnp.where(ii > jj, _bmm_nt(kb.astype(BF16), k_b) * decay, 0.0)
    t = _unit_lower_inverse(a)
    rhs = jnp.concatenate([v * beta, kb * eg], axis=-1).astype(BF16)
    sol = _bmm(t.astype(BF16), rhs)
    u_val = sol[..., :HEAD_DIM]
    w_key = sol[..., HEAD_DIM:]
    intra = _bmm_nt(q.astype(BF16), k_b) * decay
    k_tail = (k * jnp.exp(glast - gcol)).astype(BF16)
    upd = _bmm_tn(k_tail, sol.astype(BF16))
    b_mat = upd[..., :HEAD_DIM]
    p_mat = upd[..., HEAD_DIM:].astype(BF16)
    g_tot = jnp.exp(glast)

    s = state[...]
    for n in range(nc):
        grp = slice(n * nh, (n + 1) * nh)
        s_b = s.astype(BF16)
        s_all[grp] = s_b
        s = s * g_tot[grp] - _bmm(p_mat[grp], s_b) + b_mat[grp]
    state[...] = s

    wq = jnp.concatenate([w_key, q * eg], axis=1).astype(BF16)
    ws_qs = _bmm(wq, s_all[...])
    v_new = u_val - ws_qs[:, :c]
    o = ws_qs[:, c:] + _bmm(intra.astype(BF16), v_new.astype(BF16))
    for n in range(nc):
        for h in range(nh):
            o_ref[0, n * c:(n + 1) * c, head(h)] = o[n * nh + h]


def _gdn(q, k, v, bg, bgt, *, tile):
    b, s, qk = q.shape
    grid = (b, s // tile)
    tok = lambda w: pl.BlockSpec((1, tile, w), lambda i, j: (i, j, 0))
    n_prob = (tile // GDN_CHUNK) * GDN_HEADS
    return pl.pallas_call(
        _gdn_kernel,
        grid=grid,
        in_specs=[tok(qk), tok(qk), tok(qk), tok(LANES),
                  pl.BlockSpec((1, SUBLANES, tile), lambda i, j: (i, 0, j))],
        out_specs=tok(qk),
        out_shape=jax.ShapeDtypeStruct((b, s, qk), F32),
        scratch_shapes=[pltpu.VMEM((GDN_HEADS, HEAD_DIM, HEAD_DIM), F32),
                        pltpu.VMEM((n_prob, HEAD_DIM, HEAD_DIM), BF16)],
        compiler_params=pltpu.CompilerParams(
            dimension_semantics=("arbitrary", "arbitrary"), vmem_limit_bytes=VMEM_LIMIT),
        name="gdn",
    )(q, k, v, bg, bgt)


def _split_bf16(x):
    hi = x.astype(BF16)
    lo = (x - hi.astype(F32)).astype(BF16)
    return hi, lo


def _outproj_router_kernel(o_ref, z_ref, u_ref, x_ref, gnw_ref, wa_ref, wb_ref, fnw_ref,
                           wr_ref, br_ref,
                           x2_ref, xn_ref, route_ref, gate_ref, counts_ref, carry):
    tt = x_ref.shape[0]

    @pl.when(pl.program_id(0) == 0)
    def _():
        carry[...] = jnp.zeros(carry.shape, F32)

    parts = []
    for hd in range(GDN_HEADS):
        sl = slice(hd * HEAD_DIM, (hd + 1) * HEAD_DIM)
        oh = o_ref[:, sl]
        y = oh * lax.rsqrt(jnp.mean(oh * oh, axis=-1, keepdims=True) + NORM_EPS) * gnw_ref[...]
        parts.append((y * _silu(z_ref[:, sl])).astype(BF16))
    out_a = jnp.concatenate(parts, axis=-1)
    x2 = x_ref[...] + _dot(out_a, wa_ref[...]) + _dot(u_ref[...].astype(BF16), wb_ref[...])
    x2_ref[...] = x2

    xn = x2 * lax.rsqrt(jnp.mean(x2 * x2, axis=-1, keepdims=True) + NORM_EPS) * fnw_ref[...]
    _store_token_tiles(xn_ref, xn)

    xh, xl = _split_bf16(xn)
    wh, wl = _split_bf16(wr_ref[...])
    logits = _dot(xh, wh) + _dot(xh, wl) + _dot(xl, wh) + br_ref[...]

    lane = lax.broadcasted_iota(I32, (tt, LANES), 1)
    lane_f = lane.astype(F32)
    neg = jnp.float32(-jnp.inf)
    work = jnp.where(lane < N_EXPERTS, logits, neg)
    vals, idxs = [], []
    onehot = jnp.zeros((tt, LANES), F32)
    for _ in range(TOP_K):
        m = jnp.max(work, axis=-1, keepdims=True)
        idx = jnp.min(jnp.where(work == m, lane_f, float(LANES)), axis=-1, keepdims=True).astype(I32)
        sel = lane == idx
        vals.append(m)
        idxs.append(idx)
        onehot = onehot + sel.astype(F32)
        work = jnp.where(sel, neg, work)
    exps = [jnp.exp(v - vals[0]) for v in vals]
    denom = exps[0] + exps[1] + exps[2] + exps[3]

    ri = lax.broadcasted_iota(I32, (tt, tt), 0)
    ci = lax.broadcasted_iota(I32, (tt, tt), 1)
    strict = (ri > ci).astype(BF16)
    base = carry[...] + _dot(strict, onehot.astype(BF16))
    route = jnp.zeros((tt, LANES), I32)
    gates = jnp.zeros((tt, LANES), F32)
    for kk in range(TOP_K):
        rank = jnp.sum(jnp.where(lane == idxs[kk], base, 0.0), axis=-1, keepdims=True)
        route = jnp.where(lane == kk, idxs[kk], route)
        route = jnp.where(lane == kk + TOP_K, rank.astype(I32), route)
        gates = jnp.where(lane == kk, exps[kk] / denom, gates)
    route_ref[...] = route
    gate_ref[...] = gates
    new_carry = carry[...] + jnp.sum(onehot, axis=0, keepdims=True)
    carry[...] = new_carry
    counts_ref[...] = new_carry.astype(I32)


def _outproj_router(o, z, u, x, gnw, wa, wb, fnw, wr, br, *, tile):
    t, d = x.shape
    grid = (t // tile,)
    full = lambda a: pl.BlockSpec(a.shape, lambda i: (0,) * a.ndim)
    tok = lambda w: pl.BlockSpec((tile, w), lambda i: (i, 0))
    return pl.pallas_call(
        _outproj_router_kernel,
        grid=grid,
        in_specs=[tok(o.shape[1]), tok(z.shape[1]), tok(u.shape[1]), tok(d)]
                 + [full(a) for a in (gnw, wa, wb, fnw, wr, br)],
        out_specs=(tok(d), pl.BlockSpec((tile * (d // LANES), LANES), lambda i: (i, 0)),
                   tok(LANES), tok(LANES), pl.BlockSpec((1, LANES), lambda i: (0, 0))),
        out_shape=(jax.ShapeDtypeStruct((t, d), F32), jax.ShapeDtypeStruct((t * (d // LANES), LANES), F32),
                   jax.ShapeDtypeStruct((t, LANES), I32), jax.ShapeDtypeStruct((t, LANES), F32),
                   jax.ShapeDtypeStruct((1, LANES), I32)),
        scratch_shapes=[pltpu.VMEM((1, LANES), F32)],
        compiler_params=pltpu.CompilerParams(
            dimension_semantics=("arbitrary",), vmem_limit_bytes=VMEM_LIMIT),
        name="outproj_router",
    )(o, z, u, x, gnw, wa, wb, fnw, wr, br)


def _dest_kernel(route_ref, pstart_ref, dest_ref):
    route = route_ref[...].astype(F32)
    tt = route.shape[0]
    lane = lax.broadcasted_iota(I32, (tt, LANES), 1)
    pstart = pstart_ref[...].astype(F32)
    dest = jnp.zeros((tt, LANES), F32)
    for kk in range(TOP_K):
        idx = jnp.sum(jnp.where(lane == kk, route, 0.0), axis=-1, keepdims=True)
        rank = jnp.sum(jnp.where(lane == kk + TOP_K, route, 0.0), axis=-1, keepdims=True)
        start = jnp.sum(jnp.where(lane == idx.astype(I32), pstart, 0.0), axis=-1, keepdims=True)
        dest = jnp.where(lane == kk, start + rank, dest)
    dest_ref[...] = dest[:, :TOP_K].astype(I32)


def _dest(route, pstart, *, tile):
    t = route.shape[0]
    return pl.pallas_call(
        _dest_kernel,
        grid=(t // tile,),
        in_specs=[pl.BlockSpec((tile, LANES), lambda i: (i, 0)),
                  pl.BlockSpec((1, LANES), lambda i: (0, 0))],
        out_specs=pl.BlockSpec((tile, TOP_K), lambda i: (i, 0)),
        out_shape=jax.ShapeDtypeStruct((t, TOP_K), I32),
        compiler_params=pltpu.CompilerParams(dimension_semantics=("arbitrary",)),
        name="dest_rows",
    )(route, pstart)


ROW_MAP_CHUNK = 8192
ROW_MAP_UNROLL = 8


def _row_map_kernel(pad_lo_ref, pad_hi_ref, dest_ref, map_ref, *, n_tokens):
    g = pl.program_id(0)
    n = dest_ref.shape[0]

    @pl.when(g == 0)
    def _():
        def fill_range(e, carry):
            def fill(r, c):
                map_ref[r] = (n_tokens + (r & (EXPERT_BLOCK - 1))) * TOP_K
                return c
            return lax.fori_loop(pad_lo_ref[e], pad_hi_ref[e], fill, carry)
        lax.fori_loop(0, pad_lo_ref.shape[0], fill_range, 0)

    def body(j, carry):
        for un in range(ROW_MAP_UNROLL):
            i = j * ROW_MAP_UNROLL + un
            map_ref[dest_ref[i]] = g * n + i
        return carry

    lax.fori_loop(0, n // ROW_MAP_UNROLL, body, 0)


def _row_map(pad_lo, pad_hi, dest_flat, n_rows):
    n = dest_flat.shape[0]
    chunk = min(ROW_MAP_CHUNK, n)
    grid_spec = pltpu.PrefetchScalarGridSpec(
        num_scalar_prefetch=2,
        grid=(n // chunk,),
        in_specs=[pl.BlockSpec((chunk,), lambda g, lo, hi: (g,), memory_space=pltpu.SMEM)],
        out_specs=pl.BlockSpec(memory_space=pltpu.SMEM))
    return pl.pallas_call(
        functools.partial(_row_map_kernel, n_tokens=n // TOP_K),
        grid_spec=grid_spec,
        out_shape=jax.ShapeDtypeStruct((n_rows,), I32),
        compiler_params=pltpu.CompilerParams(dimension_semantics=("arbitrary",)),
        name="row_map",
    )(pad_lo, pad_hi, dest_flat)


def _expert_kernel(blk_exp_ref, n_used_ref, map_ref, map_next_ref,
                   xn_hbm, wgu_ref, bgu_ref, wd_ref, bd_ref, y4_hbm,
                   xbuf, ybuf, wgu_b, wd_b, gsem, ssem):
    i = pl.program_id(0)
    n_used = n_used_ref[0]
    chunks = SUBLANES
    bm = xbuf.shape[1] // chunks
    t = xn_hbm.shape[0] // chunks
    dff = wd_ref.shape[1]
    slot = i % 2

    def tile_rows(r):
        return pl.ds(pl.multiple_of(r * chunks, chunks), chunks)

    def start_gather(mref, s):
        def body(j, carry):
            for un in range(DMA_UNROLL):
                r = j * DMA_UNROLL + un
                tok = jnp.minimum(lax.shift_right_logical(mref[r], TOP_K_SHIFT), t - 1)
                pltpu.make_async_copy(xn_hbm.at[tile_rows(tok)], xbuf.at[s, tile_rows(r)], gsem.at[s]).start()
            return carry
        lax.fori_loop(0, bm // DMA_UNROLL, body, 0)

    def start_scatter(s):
        def body(j, carry):
            for un in range(DMA_UNROLL):
                r = j * DMA_UNROLL + un
                pair = map_ref[r]
                flat = (pair & (TOP_K - 1)) * (t + bm) + lax.shift_right_logical(pair, TOP_K_SHIFT)
                pltpu.make_async_copy(ybuf.at[s, tile_rows(r)], y4_hbm.at[tile_rows(flat)], ssem.at[s]).start()
            return carry
        lax.fori_loop(0, bm // DMA_UNROLL, body, 0)

    def wait_gather(s):
        pltpu.make_async_copy(xn_hbm.at[pl.ds(0, bm * chunks)], xbuf.at[s], gsem.at[s]).wait()

    def wait_scatter(s):
        pltpu.make_async_copy(ybuf.at[s], y4_hbm.at[pl.ds(0, bm * chunks)], ssem.at[s]).wait()

    def per_slot(fn):
        for s in range(2):
            pl.when(slot == s)(functools.partial(fn, s))

    @pl.when(i == 0)
    def _():
        start_gather(map_ref, 0)

    @pl.when(i + 1 < n_used)
    def _():
        per_slot(lambda s: start_gather(map_next_ref, 1 - s))

    prev = blk_exp_ref[jnp.maximum(i - 1, 0)]
    changed = (i == 0) | (blk_exp_ref[i] != prev)

    @pl.when(changed & (i < n_used))
    def _():
        wgu_b[...] = wgu_ref[0].astype(BF16)
        wd_b[...] = wd_ref[0].astype(BF16)

    @pl.when(i < n_used)
    def _():
        per_slot(wait_gather)

        @pl.when(i >= 2)
        def _():
            per_slot(wait_scatter)

        xb = _load_token_tiles(xbuf, bm, (slot,)).astype(BF16)
        hid = _dot(xb, wgu_b[...]) + bgu_ref[0]
        gate = jnp.minimum(hid[:, :dff], SWIGLU_LIMIT)
        up = jnp.clip(hid[:, dff:], -SWIGLU_LIMIT, SWIGLU_LIMIT)
        glu = gate * jax.nn.sigmoid(SWIGLU_ALPHA * gate)
        act = ((up + 1.0) * glu).astype(BF16)
        _store_token_tiles(ybuf, _dot(act, wd_b[...]) + bd_ref[0], (slot,))

        per_slot(start_scatter)

        @pl.when(i == n_used - 1)
        def _():
            per_slot(wait_scatter)

            @pl.when(i >= 1)
            def _():
                per_slot(lambda s: wait_scatter(1 - s))


def _experts(blk_exp, n_used, row_map, xn_tiles, wgu, bgu, wd, bd):
    d = wgu.shape[1]
    chunks = d // LANES
    t = xn_tiles.shape[0] // chunks
    n_blocks = row_map.shape[0] // EXPERT_BLOCK
    buf = pltpu.VMEM((2, EXPERT_BLOCK * chunks, LANES), F32)
    two_f = wgu.shape[2]
    dff = wd.shape[1]
    exp3 = lambda i, be, nu: (be[jnp.minimum(i, nu[0] - 1)], 0, 0)
    smem_blk = lambda off: pl.BlockSpec(
        (EXPERT_BLOCK,), lambda i, be, nu: (jnp.minimum(i + off, nu[0] - 1),), memory_space=pltpu.SMEM)
    grid_spec = pltpu.PrefetchScalarGridSpec(
        num_scalar_prefetch=2,
        grid=(n_blocks,),
        in_specs=[smem_blk(0), smem_blk(1),
                  pl.BlockSpec(memory_space=pl.ANY),
                  pl.BlockSpec((1, d, two_f), exp3),
                  pl.BlockSpec((1, 1, two_f), exp3),
                  pl.BlockSpec((1, dff, d), exp3),
                  pl.BlockSpec((1, 1, d), exp3)],
        out_specs=pl.BlockSpec(memory_space=pl.ANY),
        scratch_shapes=[buf, buf, pltpu.VMEM((d, two_f), BF16), pltpu.VMEM((dff, d), BF16),
                        pltpu.SemaphoreType.DMA((2,)), pltpu.SemaphoreType.DMA((2,))])
    return pl.pallas_call(
        _expert_kernel,
        grid_spec=grid_spec,
        out_shape=jax.ShapeDtypeStruct((TOP_K * (t + EXPERT_BLOCK) * chunks, LANES), F32),
        compiler_params=pltpu.CompilerParams(
            dimension_semantics=("arbitrary",), vmem_limit_bytes=VMEM_LIMIT),
        name="experts",
    )(blk_exp, n_used, row_map, row_map, xn_tiles, wgu, bgu, wd, bd)


def _combine_kernel(y0_ref, y1_ref, y2_ref, y3_ref, gate_ref, x2_ref, fw_ref, out_ref):
    gates = gate_ref[...]
    lane = lax.broadcasted_iota(I32, gates.shape, 1)
    x3 = x2_ref[...]
    for kk, y_ref in enumerate((y0_ref, y1_ref, y2_ref, y3_ref)):
        gk = jnp.sum(jnp.where(lane == kk, gates, 0.0), axis=-1, keepdims=True)
        x3 = x3 + gk * _load_token_tiles(y_ref, x3.shape[0])
    out_ref[...] = x3 * lax.rsqrt(jnp.mean(x3 * x3, axis=-1, keepdims=True) + NORM_EPS) * fw_ref[...]


def _combine(y4, gates, x2, fw, *, tile):
    t, d = x2.shape
    steps = t // tile
    seg = (t + EXPERT_BLOCK) // tile
    choice = lambda kk: pl.BlockSpec((tile * (d // LANES), LANES), lambda i: (kk * seg + i, 0))
    return pl.pallas_call(
        _combine_kernel,
        grid=(steps,),
        in_specs=[choice(kk) for kk in range(TOP_K)]
                 + [pl.BlockSpec((tile, LANES), lambda i: (i, 0)),
                    pl.BlockSpec((tile, d), lambda i: (i, 0)),
                    pl.BlockSpec((1, d), lambda i: (0, 0))],
        out_specs=pl.BlockSpec((tile, d), lambda i: (i, 0)),
        out_shape=jax.ShapeDtypeStruct((t, d), F32),
        compiler_params=pltpu.CompilerParams(
            dimension_semantics=("arbitrary",), vmem_limit_bytes=VMEM_LIMIT),
        name="combine",
    )(y4, y4, y4, y4, gates, x2, fw)


def _pad_lanes(a, offset=0, fill=0.0):
    out = jnp.full((1, LANES), fill, a.dtype)
    return out.at[0, offset:offset + a.shape[0]].set(a)


def _layer(x, attn_norm_w, w_in, gdn_conv_w, gdn_a_log, gdn_dt_bias, gdn_norm_w,
           cf_dw_w, cf_dw_b, cf_ln_w, cf_ln_b, w_out, ffn_norm_w, w_router, b_router,
           w_gate_up, b_gate_up, w_down, b_down, final_norm_w, apply_final):
    b, s, d = x.shape
    t = b * s
    assert d == SUBLANES * LANES, "the token-tile layout needs one (8, 128) tile per token row"
    qk = GDN_HEADS * HEAD_DIM
    cfc = cf_dw_w.shape[1]
    off_b = 4 * qk
    off_cf = off_b + 2 * GDN_HEADS

    wm = w_in[:, :off_b].astype(BF16)
    wba = jnp.zeros((d, LANES), F32).at[:, :2 * GDN_HEADS].set(w_in[:, off_b:off_cf]).astype(BF16)
    wcf = w_in[:, off_cf:].astype(BF16)
    alog = _pad_lanes(gdn_a_log, GDN_HEADS)
    dtb = _pad_lanes(gdn_dt_bias, GDN_HEADS)

    q, k, v, z, bg, bgt, u = _inproj(
        x, attn_norm_w[None, :], wm, wba, wcf, gdn_conv_w, alog, dtb,
        cf_dw_w, cf_dw_b[None, :], cf_ln_w[None, :], cf_ln_b[None, :], tile=min(512, s))
    o = _gdn(q, k, v, bg, bgt, tile=min(256, s))

    wr = jnp.zeros((d, LANES), F32).at[:, :N_EXPERTS].set(w_router)
    br = _pad_lanes(b_router)
    x2, xn, route, gates, counts = _outproj_router(
        o.reshape(t, qk), z.reshape(t, qk), u.reshape(t, cfc), x.reshape(t, d),
        gdn_norm_w[None, :], w_out[:qk].astype(BF16), w_out[qk:].astype(BF16),
        ffn_norm_w[None, :], wr, br, tile=min(512, t))

    cnt = counts[0, :N_EXPERTS]
    nblk = (cnt + EXPERT_BLOCK - 1) // EXPERT_BLOCK
    blk_end = jnp.cumsum(nblk)
    pstart = (blk_end - nblk) * EXPERT_BLOCK
    n_blocks = (t * TOP_K) // EXPERT_BLOCK + N_EXPERTS
    blk_ids = jnp.arange(n_blocks, dtype=I32)
    blk_exp = jnp.minimum(
        jnp.sum((blk_end[None, :] <= blk_ids[:, None]).astype(I32), axis=1), N_EXPERTS - 1)
    n_used = blk_end[-1:].astype(I32)

    dest = _dest(route, _pad_lanes(pstart.astype(I32)), tile=min(2048, t))
    n_rows = n_blocks * EXPERT_BLOCK
    pad_lo = jnp.concatenate([pstart + cnt, n_used * EXPERT_BLOCK]).astype(I32)
    pad_hi = jnp.concatenate([blk_end * EXPERT_BLOCK, jnp.full((1,), n_rows, I32)]).astype(I32)
    row_map = _row_map(pad_lo, pad_hi, dest.reshape(t * TOP_K), n_rows)
    y4 = _experts(blk_exp, n_used, row_map, xn, w_gate_up, b_gate_up[:, None, :],
                  w_down, b_down[:, None, :])
    out = _combine(y4, gates, x2, final_norm_w[None, :], tile=min(EXPERT_BLOCK, t))
    return out.reshape(b, s, d)


def kernel(x, attn_norm_w, w_in, gdn_conv_w, gdn_a_log, gdn_dt_bias, gdn_norm_w, cf_dw_w, cf_dw_b,
           cf_ln_w, cf_ln_b, w_out, ffn_norm_w, w_router, b_router, w_gate_up, b_gate_up, w_down,
           b_down, final_norm_w):
    depth = w_in.shape[0]
    assert depth == 1, "the fused final norm assumes a single trunk layer"
    return _layer(x, attn_norm_w[0], w_in[0], gdn_conv_w[0], gdn_a_log[0], gdn_dt_bias[0],
                  gdn_norm_w[0], cf_dw_w[0], cf_dw_b[0], cf_ln_w[0], cf_ln_b[0], w_out[0],
                  ffn_norm_w[0], w_router[0], b_router[0], w_gate_up[0], b_gate_up[0], w_down[0],
                  b_down[0], final_norm_w, True)
```

```python
import functools

import jax
import jax.numpy as jnp
from jax import lax
from jax.experimental import pallas as pl
from jax.experimental.pallas import tpu as pltpu

F32 = jnp.float32
BF16 = jnp.bfloat16
I32 = jnp.int32

NORM_EPS = 1e-6
LANES = 128
SUBLANES = 8
GDN_HEADS = 4
HEAD_DIM = 128
GDN_CHUNK = 64
GDN_CONV = 4
CF_KERNEL = 31
N_EXPERTS = 32
TOP_K = 4
TOP_K_SHIFT = 2
DMA_UNROLL = 8
SWIGLU_LIMIT = 7.0
SWIGLU_ALPHA = 1.702

QKV_HALO = 8
CF_HALO = 32
EXPERT_BLOCK = 256
VMEM_LIMIT = 56 * 1024 * 1024


def _silu(x):
    return x * jax.nn.sigmoid(x)


def _dot(a, b):
    return jnp.dot(a, b, preferred_element_type=F32)


def _dot_nt(a, b):
    return lax.dot_general(a, b, (((1,), (1,)), ((), ())), preferred_element_type=F32)


def _dot_tn(a, b):
    return lax.dot_general(a, b, (((0,), (0,)), ((), ())), preferred_element_type=F32)


def _store_token_tiles(ref, x, lead=()):
    rows, d = x.shape
    chunks = d // LANES
    for c in range(chunks):
        ref[lead + (pl.ds(c, rows, stride=chunks), slice(None))] = x[:, c * LANES:(c + 1) * LANES]


def _load_token_tiles(ref, rows, lead=()):
    chunks = ref.shape[-2] // rows
    return jnp.concatenate(
        [ref[lead + (pl.ds(c, rows, stride=chunks), slice(None))] for c in range(chunks)], axis=1)


def _inproj_kernel(x_ref, nw_ref, wm_ref, wba_ref, wcf_ref, cw_ref, alog_ref, dtb_ref,
                   dww_ref, dwb_ref, lnw_ref, lnb_ref,
                   q_ref, k_ref, v_ref, z_ref, bg_ref, bgt_ref, u_ref,
                   qkv_buf, cf_buf, cf_shift):
    tt = x_ref.shape[1]
    qk = GDN_HEADS * HEAD_DIM
    cfc = u_ref.shape[2]

    @pl.when(pl.program_id(1) == 0)
    def _():
        qkv_buf[0:QKV_HALO, :] = jnp.zeros((QKV_HALO, qkv_buf.shape[1]), F32)
        cf_buf[0:CF_HALO, :] = jnp.zeros((CF_HALO, cf_buf.shape[1]), F32)

    x = x_ref[0]
    h = x * lax.rsqrt(jnp.mean(x * x, axis=-1, keepdims=True) + NORM_EPS) * nw_ref[...]
    h = h.astype(BF16)
    pm = _dot(h, wm_ref[...])
    pba = _dot(h, wba_ref[...])
    pcf = _dot(h, wcf_ref[...])

    z_ref[0] = pm[:, 3 * qk:]

    qkv_buf[QKV_HALO:QKV_HALO + tt, :] = pm[:, :3 * qk]
    acc = None
    for j in range(GDN_CONV):
        term = cw_ref[j:j + 1, :] * qkv_buf[pl.ds(QKV_HALO - (GDN_CONV - 1) + j, tt), :]
        acc = term if acc is None else acc + term
    qkv_buf[0:QKV_HALO, :] = qkv_buf[tt:tt + QKV_HALO, :]
    qkv = _silu(acc)
    for hd in range(GDN_HEADS):
        for base, ref in ((0, q_ref), (qk, k_ref)):
            t = qkv[:, base + hd * HEAD_DIM: base + (hd + 1) * HEAD_DIM]
            t = t * lax.rsqrt(jnp.sum(t * t, axis=-1, keepdims=True) + NORM_EPS)
            ref[0, :, hd * HEAD_DIM:(hd + 1) * HEAD_DIM] = t
    v_ref[0] = qkv[:, 2 * qk:]

    lane = lax.broadcasted_iota(I32, (tt, LANES), 1)
    row = lax.broadcasted_iota(I32, (tt, LANES), 0)
    beta = jax.nn.sigmoid(pba)
    sp_in = pba + dtb_ref[...]
    softplus = jnp.maximum(sp_in, 0.0) + jnp.log(1.0 + jnp.exp(-jnp.abs(sp_in)))
    g = -jnp.exp(alog_ref[...]) * softplus
    g = jnp.where((lane >= GDN_HEADS) & (lane < 2 * GDN_HEADS), g, 0.0)
    pos = row % GDN_CHUNK
    shift = 1
    while shift < GDN_CHUNK:
        g = g + jnp.where(pos >= shift, pltpu.roll(g, shift, 0), 0.0)
        shift *= 2
    bg = jnp.where(lane < GDN_HEADS, beta, g)
    bg_ref[0] = bg
    bgt_ref[0] = jnp.transpose(bg)[0:SUBLANES, :]

    glu = pcf[:, :cfc] * jax.nn.sigmoid(pcf[:, cfc:])
    cf_buf[CF_HALO:CF_HALO + tt, :] = glu
    lo = SUBLANES
    span = tt + CF_HALO - lo
    for r in range(1, SUBLANES):
        cf_shift[r - 1, lo:lo + span, :] = cf_buf[pl.ds(lo - r, span), :]
    rows = 64
    for r0 in range(0, tt, rows):
        acc = None
        for j in range(CF_KERNEL):
            a, r = divmod(CF_KERNEL - 1 - j, SUBLANES)
            start = CF_HALO + r0 - a * SUBLANES
            src = cf_buf[start:start + rows, :] if r == 0 else cf_shift[r - 1, start:start + rows, :]
            term = dww_ref[j:j + 1, :] * src
            acc = term if acc is None else acc + term
        c = acc + dwb_ref[...]
        mu = jnp.mean(c, axis=-1, keepdims=True)
        cc = c - mu
        y = cc * lax.rsqrt(jnp.mean(cc * cc, axis=-1, keepdims=True) + NORM_EPS)
        u_ref[0, r0:r0 + rows, :] = _silu(y * lnw_ref[...] + lnb_ref[...])
    cf_buf[0:CF_HALO, :] = cf_buf[tt:tt + CF_HALO, :]


def _inproj(x, nw, wm, wba, wcf, cw, alog, dtb, dww, dwb, lnw, lnb, *, tile):
    b, s, d = x.shape
    qk = GDN_HEADS * HEAD_DIM
    cfc = dww.shape[1]
    grid = (b, s // tile)
    full = lambda a: pl.BlockSpec(a.shape, lambda i, j: (0,) * a.ndim)
    tok = lambda w: pl.BlockSpec((1, tile, w), lambda i, j: (i, j, 0))
    out_shape = (
        jax.ShapeDtypeStruct((b, s, qk), F32), jax.ShapeDtypeStruct((b, s, qk), F32),
        jax.ShapeDtypeStruct((b, s, qk), F32), jax.ShapeDtypeStruct((b, s, qk), F32),
        jax.ShapeDtypeStruct((b, s, LANES), F32), jax.ShapeDtypeStruct((b, SUBLANES, s), F32),
        jax.ShapeDtypeStruct((b, s, cfc), F32))
    return pl.pallas_call(
        _inproj_kernel,
        grid=grid,
        in_specs=[tok(d)] + [full(a) for a in (nw, wm, wba, wcf, cw, alog, dtb, dww, dwb, lnw, lnb)],
        out_specs=(tok(qk), tok(qk), tok(qk), tok(qk), tok(LANES),
                   pl.BlockSpec((1, SUBLANES, tile), lambda i, j: (i, 0, j)), tok(cfc)),
        out_shape=out_shape,
        scratch_shapes=[pltpu.VMEM((QKV_HALO + tile, 3 * qk), F32),
                        pltpu.VMEM((CF_HALO + tile, cfc), F32),
                        pltpu.VMEM((SUBLANES - 1, CF_HALO + tile, cfc), F32)],
        compiler_params=pltpu.CompilerParams(
            dimension_semantics=("arbitrary", "arbitrary"), vmem_limit_bytes=VMEM_LIMIT),
        name="inproj",
    )(x, nw, wm, wba, wcf, cw, alog, dtb, dww, dwb, lnw, lnb)


def _bmm(a, b):
    return jnp.einsum("bmk,bkn->bmn", a, b, preferred_element_type=F32)


def _bmm_nt(a, b):
    return jnp.einsum("bmk,bnk->bmn", a, b, preferred_element_type=F32)


def _bmm_tn(a, b):
    return jnp.einsum("bkm,bkn->bmn", a, b, preferred_element_type=F32)


def _unit_lower_inverse(a):
    c = a.shape[-1]
    ii = lax.broadcasted_iota(I32, (c, c), 0)
    jj = lax.broadcasted_iota(I32, (c, c), 1)
    eye = (ii == jj).astype(F32)
    same16 = (ii // 16) == (jj // 16)
    same32 = (ii // 32) == (jj // 32)
    x = jnp.where(same16, -a, 0.0)
    t = eye + x
    xp = x
    for _ in range(3):
        xp_b = xp.astype(BF16)
        xp = _bmm(xp_b, xp_b)
        t = t + _bmm(t.astype(BF16), xp.astype(BF16))
    for off in (jnp.where(same32 & ~same16, a, 0.0), jnp.where(~same32, a, 0.0)):
        tb = t.astype(BF16)
        t = t - _bmm(tb, _bmm(off.astype(BF16), tb).astype(BF16))
    return t


def _gdn_kernel(q_ref, k_ref, v_ref, bg_ref, bgt_ref, o_ref, state, s_all):
    lt = q_ref.shape[1]
    c = GDN_CHUNK
    nh = GDN_HEADS
    nc = lt // c

    @pl.when(pl.program_id(1) == 0)
    def _():
        state[...] = jnp.zeros(state.shape, F32)

    def stack(fn):
        return jnp.stack([fn(slice(n * c, (n + 1) * c), h) for n in range(nc) for h in range(nh)])

    head = lambda h: slice(h * HEAD_DIM, (h + 1) * HEAD_DIM)
    q = stack(lambda r, h: q_ref[0, r, head(h)]) * (HEAD_DIM ** -0.5)
    k = stack(lambda r, h: k_ref[0, r, head(h)])
    v = stack(lambda r, h: v_ref[0, r, head(h)])
    beta = stack(lambda r, h: bg_ref[0, r, h:h + 1])
    gcol = stack(lambda r, h: bg_ref[0, r, nh + h:nh + h + 1])
    grow = stack(lambda r, h: bgt_ref[0, nh + h:nh + h + 1, r])

    ii = lax.broadcasted_iota(I32, (c, c), 0)
    jj = lax.broadcasted_iota(I32, (c, c), 1)
    glast = gcol[:, c - 1:c, :]
    eg = jnp.exp(gcol)
    decay = jnp.where(ii >= jj, jnp.exp(jnp.minimum(gcol - grow, 0.0)), 0.0)
    kb = k * beta
    k_b = k.astype(BF16)
    a = jnp.where(ii > jj, _bmm_nt(kb.astype(BF16), k_b) * decay, 0.0)
    t = _unit_lower_inverse(a)
    rhs = jnp.concatenate([v * beta, kb * eg], axis=-1).astype(BF16)
    sol = _bmm(t.astype(BF16), rhs)
    u_val = sol[..., :HEAD_DIM]
    w_key = sol[..., HEAD_DIM:]
    intra = _bmm_nt(q.astype(BF16), k_b) * decay
    k_tail = (k * jnp.exp(glast - gcol)).astype(BF16)
    upd = _bmm_tn(k_tail, sol.astype(BF16))
    b_mat = upd[..., :HEAD_DIM]
    p_mat = upd[..., HEAD_DIM:].astype(BF16)
    g_tot = jnp.exp(glast)

    s = state[...]
    for n in range(nc):
        grp = slice(n * nh, (n + 1) * nh)
        s_b = s.astype(BF16)
        s_all[grp] = s_b
        s = s * g_tot[grp] - _bmm(p_mat[grp], s_b) + b_mat[grp]
    state[...] = s

    wq = jnp.concatenate([w_key, q * eg], axis=1).astype(BF16)
    ws_qs = _bmm(wq, s_all[...])
    v_new = u_val - ws_qs[:, :c]
    o = ws_qs[:, c:] + _bmm(intra.astype(BF16), v_new.astype(BF16))
    for n in range(nc):
        for h in range(nh):
            o_ref[0, n * c:(n + 1) * c, head(h)] = o[n * nh + h]


def _gdn(q, k, v, bg, bgt, *, tile):
    b, s, qk = q.shape
    grid = (b, s // tile)
    tok = lambda w: pl.BlockSpec((1, tile, w), lambda i, j: (i, j, 0))
    n_prob = (tile // GDN_CHUNK) * GDN_HEADS
    return pl.pallas_call(
        _gdn_kernel,
        grid=grid,
        in_specs=[tok(qk), tok(qk), tok(qk), tok(LANES),
                  pl.BlockSpec((1, SUBLANES, tile), lambda i, j: (i, 0, j))],
        out_specs=tok(qk),
        out_shape=jax.ShapeDtypeStruct((b, s, qk), F32),
        scratch_shapes=[pltpu.VMEM((GDN_HEADS, HEAD_DIM, HEAD_DIM), F32),
                        pltpu.VMEM((n_prob, HEAD_DIM, HEAD_DIM), BF16)],
        compiler_params=pltpu.CompilerParams(
            dimension_semantics=("arbitrary", "arbitrary"), vmem_limit_bytes=VMEM_LIMIT),
        name="gdn",
    )(q, k, v, bg, bgt)


def _split_bf16(x):
    hi = x.astype(BF16)
    lo = (x - hi.astype(F32)).astype(BF16)
    return hi, lo


def _outproj_router_kernel(o_ref, z_ref, u_ref, x_ref, gnw_ref, wa_ref, wb_ref, fnw_ref,
                           wr_ref, br_ref,
                           x2_ref, xn_ref, route_ref, gate_ref, counts_ref, carry):
    tt = x_ref.shape[0]

    @pl.when(pl.program_id(0) == 0)
    def _():
        carry[...] = jnp.zeros(carry.shape, F32)

    parts = []
    for hd in range(GDN_HEADS):
        sl = slice(hd * HEAD_DIM, (hd + 1) * HEAD_DIM)
        oh = o_ref[:, sl]
        y = oh * lax.rsqrt(jnp.mean(oh * oh, axis=-1, keepdims=True) + NORM_EPS) * gnw_ref[...]
        parts.append((y * _silu(z_ref[:, sl])).astype(BF16))
    out_a = jnp.concatenate(parts, axis=-1)
    x2 = x_ref[...] + _dot(out_a, wa_ref[...]) + _dot(u_ref[...].astype(BF16), wb_ref[...])
    x2_ref[...] = x2

    xn = x2 * lax.rsqrt(jnp.mean(x2 * x2, axis=-1, keepdims=True) + NORM_EPS) * fnw_ref[...]
    _store_token_tiles(xn_ref, xn)

    xh, xl = _split_bf16(xn)
    wh, wl = _split_bf16(wr_ref[...])
    logits = _dot(xh, wh) + _dot(xh, wl) + _dot(xl, wh) + br_ref[...]

    lane = lax.broadcasted_iota(I32, (tt, LANES), 1)
    lane_f = lane.astype(F32)
    neg = jnp.float32(-jnp.inf)
    work = jnp.where(lane < N_EXPERTS, logits, neg)
    vals, idxs = [], []
    onehot = jnp.zeros((tt, LANES), F32)
    for _ in range(TOP_K):
        m = jnp.max(work, axis=-1, keepdims=True)
        idx = jnp.min(jnp.where(work == m, lane_f, float(LANES)), axis=-1, keepdims=True).astype(I32)
        sel = lane == idx
        vals.append(m)
        idxs.append(idx)
        onehot = onehot + sel.astype(F32)
        work = jnp.where(sel, neg, work)
    exps = [jnp.exp(v - vals[0]) for v in vals]
    denom = exps[0] + exps[1] + exps[2] + exps[3]

    ri = lax.broadcasted_iota(I32, (tt, tt), 0)
    ci = lax.broadcasted_iota(I32, (tt, tt), 1)
    strict = (ri > ci).astype(BF16)
    base = carry[...] + _dot(strict, onehot.astype(BF16))
    route = jnp.zeros((tt, LANES), I32)
    gates = jnp.zeros((tt, LANES), F32)
    for kk in range(TOP_K):
        rank = jnp.sum(jnp.where(lane == idxs[kk], base, 0.0), axis=-1, keepdims=True)
        route = jnp.where(lane == kk, idxs[kk], route)
        route = jnp.where(lane == kk + TOP_K, rank.astype(I32), route)
        gates = jnp.where(lane == kk, exps[kk] / denom, gates)
    route_ref[...] = route
    gate_ref[...] = gates
    new_carry = carry[...] + jnp.sum(onehot, axis=0, keepdims=True)
    carry[...] = new_carry
    counts_ref[...] = new_carry.astype(I32)


def _outproj_router(o, z, u, x, gnw, wa, wb, fnw, wr, br, *, tile):
    t, d = x.shape
    grid = (t // tile,)
    full = lambda a: pl.BlockSpec(a.shape, lambda i: (0,) * a.ndim)
    tok = lambda w: pl.BlockSpec((tile, w), lambda i: (i, 0))
    return pl.pallas_call(
        _outproj_router_kernel,
        grid=grid,
        in_specs=[tok(o.shape[1]), tok(z.shape[1]), tok(u.shape[1]), tok(d)]
                 + [full(a) for a in (gnw, wa, wb, fnw, wr, br)],
        out_specs=(tok(d), pl.BlockSpec((tile * (d // LANES), LANES), lambda i: (i, 0)),
                   tok(LANES), tok(LANES), pl.BlockSpec((1, LANES), lambda i: (0, 0))),
        out_shape=(jax.ShapeDtypeStruct((t, d), F32), jax.ShapeDtypeStruct((t * (d // LANES), LANES), F32),
                   jax.ShapeDtypeStruct((t, LANES), I32), jax.ShapeDtypeStruct((t, LANES), F32),
                   jax.ShapeDtypeStruct((1, LANES), I32)),
        scratch_shapes=[pltpu.VMEM((1, LANES), F32)],
        compiler_params=pltpu.CompilerParams(
            dimension_semantics=("arbitrary",), vmem_limit_bytes=VMEM_LIMIT),
        name="outproj_router",
    )(o, z, u, x, gnw, wa, wb, fnw, wr, br)


def _dest_kernel(route_ref, pstart_ref, dest_ref):
    route = route_ref[...].astype(F32)
    tt = route.shape[0]
    lane = lax.broadcasted_iota(I32, (tt, LANES), 1)
    pstart = pstart_ref[...].astype(F32)
    dest = jnp.zeros((tt, LANES), F32)
    for kk in range(TOP_K):
        idx = jnp.sum(jnp.where(lane == kk, route, 0.0), axis=-1, keepdims=True)
        rank = jnp.sum(jnp.where(lane == kk + TOP_K, route, 0.0), axis=-1, keepdims=True)
        start = jnp.sum(jnp.where(lane == idx.astype(I32), pstart, 0.0), axis=-1, keepdims=True)
        dest = jnp.where(lane == kk, start + rank, dest)
    dest_ref[...] = dest[:, :TOP_K].astype(I32)


def _dest(route, pstart, *, tile):
    t = route.shape[0]
    return pl.pallas_call(
        _dest_kernel,
        grid=(t // tile,),
        in_specs=[pl.BlockSpec((tile, LANES), lambda i: (i, 0)),
                  pl.BlockSpec((1, LANES), lambda i: (0, 0))],
        out_specs=pl.BlockSpec((tile, TOP_K), lambda i: (i, 0)),
        out_shape=jax.ShapeDtypeStruct((t, TOP_K), I32),
        compiler_params=pltpu.CompilerParams(dimension_semantics=("arbitrary",)),
        name="dest_rows",
    )(route, pstart)


ROW_MAP_CHUNK = 8192
ROW_MAP_UNROLL = 8


def _row_map_kernel(pad_lo_ref, pad_hi_ref, dest_ref, map_ref, *, n_tokens):
    g = pl.program_id(0)
    n = dest_ref.shape[0]

    @pl.when(g == 0)
    def _():
        def fill_range(e, carry):
            def fill(r, c):
                map_ref[r] = (n_tokens + (r & (EXPERT_BLOCK - 1))) * TOP_K
                return c
            return lax.fori_loop(pad_lo_ref[e], pad_hi_ref[e], fill, carry)
        lax.fori_loop(0, pad_lo_ref.shape[0], fill_range, 0)

    def body(j, carry):
        for un in range(ROW_MAP_UNROLL):
            i = j * ROW_MAP_UNROLL + un
            map_ref[dest_ref[i]] = g * n + i
        return carry

    lax.fori_loop(0, n // ROW_MAP_UNROLL, body, 0)


def _row_map(pad_lo, pad_hi, dest_flat, n_rows):
    n = dest_flat.shape[0]
    chunk = min(ROW_MAP_CHUNK, n)
    grid_spec = pltpu.PrefetchScalarGridSpec(
        num_scalar_prefetch=2,
        grid=(n // chunk,),
        in_specs=[pl.BlockSpec((chunk,), lambda g, lo, hi: (g,), memory_space=pltpu.SMEM)],
        out_specs=pl.BlockSpec(memory_space=pltpu.SMEM))
    return pl.pallas_call(
        functools.partial(_row_map_kernel, n_tokens=n // TOP_K),
        grid_spec=grid_spec,
        out_shape=jax.ShapeDtypeStruct((n_rows,), I32),
        compiler_params=pltpu.CompilerParams(dimension_semantics=("arbitrary",)),
        name="row_map",
    )(pad_lo, pad_hi, dest_flat)


def _expert_kernel(blk_exp_ref, n_used_ref, map_ref, map_next_ref,
                   xn_hbm, wgu_ref, bgu_ref, wd_ref, bd_ref, y4_hbm,
                   xbuf, ybuf, wgu_b, wd_b, gsem, ssem):
    i = pl.program_id(0)
    n_used = n_used_ref[0]
    chunks = SUBLANES
    bm = xbuf.shape[1] // chunks
    t = xn_hbm.shape[0] // chunks
    dff = wd_ref.shape[1]
    slot = i % 2

    def tile_rows(r):
        return pl.ds(pl.multiple_of(r * chunks, chunks), chunks)

    def start_gather(mref, s):
        def body(j, carry):
            for un in range(DMA_UNROLL):
                r = j * DMA_UNROLL + un
                tok = jnp.minimum(lax.shift_right_logical(mref[r], TOP_K_SHIFT), t - 1)
                pltpu.make_async_copy(xn_hbm.at[tile_rows(tok)], xbuf.at[s, tile_rows(r)], gsem.at[s]).start()
            return carry
        lax.fori_loop(0, bm // DMA_UNROLL, body, 0)

    def start_scatter(s):
        def body(j, carry):
            for un in range(DMA_UNROLL):
                r = j * DMA_UNROLL + un
                pair = map_ref[r]
                flat = (pair & (TOP_K - 1)) * (t + bm) + lax.shift_right_logical(pair, TOP_K_SHIFT)
                pltpu.async_copy(ybuf.at[s, tile_rows(r)], y4_hbm.at[tile_rows(flat)], ssem.at[s], priority=1)
            return carry
        lax.fori_loop(0, bm // DMA_UNROLL, body, 0)

    def wait_gather(s):
        pltpu.make_async_copy(xn_hbm.at[pl.ds(0, bm * chunks)], xbuf.at[s], gsem.at[s]).wait()

    def wait_scatter(s):
        pltpu.make_async_copy(ybuf.at[s], y4_hbm.at[pl.ds(0, bm * chunks)], ssem.at[s]).wait()

    def per_slot(fn):
        for s in range(2):
            pl.when(slot == s)(functools.partial(fn, s))

    @pl.when(i == 0)
    def _():
        start_gather(map_ref, 0)

    @pl.when(i + 1 < n_used)
    def _():
        per_slot(lambda s: start_gather(map_next_ref, 1 - s))

    prev = blk_exp_ref[jnp.maximum(i - 1, 0)]
    changed = (i == 0) | (blk_exp_ref[i] != prev)

    @pl.when(changed & (i < n_used))
    def _():
        wgu_b[...] = wgu_ref[0].astype(BF16)
        wd_b[...] = wd_ref[0].astype(BF16)

    @pl.when(i < n_used)
    def _():
        per_slot(wait_gather)

        @pl.when(i >= 2)
        def _():
            per_slot(wait_scatter)

        xb = _load_token_tiles(xbuf, bm, (slot,)).astype(BF16)
        hid = _dot(xb, wgu_b[...]) + bgu_ref[0]
        gate = jnp.minimum(hid[:, :dff], SWIGLU_LIMIT)
        up = jnp.clip(hid[:, dff:], -SWIGLU_LIMIT, SWIGLU_LIMIT)
        glu = gate * jax.nn.sigmoid(SWIGLU_ALPHA * gate)
        act = ((up + 1.0) * glu).astype(BF16)
        _store_token_tiles(ybuf, _dot(act, wd_b[...]) + bd_ref[0], (slot,))

        per_slot(start_scatter)

        @pl.when(i == n_used - 1)
        def _():
            per_slot(wait_scatter)

            @pl.when(i >= 1)
            def _():
                per_slot(lambda s: wait_scatter(1 - s))


def _experts(blk_exp, n_used, row_map, xn_tiles, wgu, bgu, wd, bd):
    d = wgu.shape[1]
    chunks = d // LANES
    t = xn_tiles.shape[0] // chunks
    n_blocks = row_map.shape[0] // EXPERT_BLOCK
    buf = pltpu.VMEM((2, EXPERT_BLOCK * chunks, LANES), F32)
    two_f = wgu.shape[2]
    dff = wd.shape[1]
    exp3 = lambda i, be, nu: (be[jnp.minimum(i, nu[0] - 1)], 0, 0)
    smem_blk = lambda off: pl.BlockSpec(
        (EXPERT_BLOCK,), lambda i, be, nu: (jnp.minimum(i + off, nu[0] - 1),), memory_space=pltpu.SMEM)
    grid_spec = pltpu.PrefetchScalarGridSpec(
        num_scalar_prefetch=2,
        grid=(n_blocks,),
        in_specs=[smem_blk(0), smem_blk(1),
                  pl.BlockSpec(memory_space=pl.ANY),
                  pl.BlockSpec((1, d, two_f), exp3),
                  pl.BlockSpec((1, 1, two_f), exp3),
                  pl.BlockSpec((1, dff, d), exp3),
                  pl.BlockSpec((1, 1, d), exp3)],
        out_specs=pl.BlockSpec(memory_space=pl.ANY),
        scratch_shapes=[buf, buf, pltpu.VMEM((d, two_f), BF16), pltpu.VMEM((dff, d), BF16),
                        pltpu.SemaphoreType.DMA((2,)), pltpu.SemaphoreType.DMA((2,))])
    return pl.pallas_call(
        _expert_kernel,
        grid_spec=grid_spec,
        out_shape=jax.ShapeDtypeStruct((TOP_K * (t + EXPERT_BLOCK) * chunks, LANES), F32),
        compiler_params=pltpu.CompilerParams(
            dimension_semantics=("arbitrary",), vmem_limit_bytes=VMEM_LIMIT),
        name="experts",
    )(blk_exp, n_used, row_map, row_map, xn_tiles, wgu, bgu, wd, bd)


def _combine_kernel(y0_ref, y1_ref, y2_ref, y3_ref, gate_ref, x2_ref, fw_ref, out_ref):
    gates = gate_ref[...]
    lane = lax.broadcasted_iota(I32, gates.shape, 1)
    x3 = x2_ref[...]
    for kk, y_ref in enumerate((y0_ref, y1_ref, y2_ref, y3_ref)):
        gk = jnp.sum(jnp.where(lane == kk, gates, 0.0), axis=-1, keepdims=True)
        x3 = x3 + gk * _load_token_tiles(y_ref, x3.shape[0])
    out_ref[...] = x3 * lax.rsqrt(jnp.mean(x3 * x3, axis=-1, keepdims=True) + NORM_EPS) * fw_ref[...]


def _combine(y4, gates, x2, fw, *, tile):
    t, d = x2.shape
    steps = t // tile
    seg = (t + EXPERT_BLOCK) // tile
    choice = lambda kk: pl.BlockSpec((tile * (d // LANES), LANES), lambda i: (kk * seg + i, 0))
    return pl.pallas_call(
        _combine_kernel,
        grid=(steps,),
        in_specs=[choice(kk) for kk in range(TOP_K)]
                 + [pl.BlockSpec((tile, LANES), lambda i: (i, 0)),
                    pl.BlockSpec((tile, d), lambda i: (i, 0)),
                    pl.BlockSpec((1, d), lambda i: (0, 0))],
        out_specs=pl.BlockSpec((tile, d), lambda i: (i, 0)),
        out_shape=jax.ShapeDtypeStruct((t, d), F32),
        compiler_params=pltpu.CompilerParams(
            dimension_semantics=("arbitrary",), vmem_limit_bytes=VMEM_LIMIT),
        name="combine",
    )(y4, y4, y4, y4, gates, x2, fw)


def _pad_lanes(a, offset=0, fill=0.0):
    out = jnp.full((1, LANES), fill, a.dtype)
    return out.at[0, offset:offset + a.shape[0]].set(a)


def _layer(x, attn_norm_w, w_in, gdn_conv_w, gdn_a_log, gdn_dt_bias, gdn_norm_w,
           cf_dw_w, cf_dw_b, cf_ln_w, cf_ln_b, w_out, ffn_norm_w, w_router, b_router,
           w_gate_up, b_gate_up, w_down, b_down, final_norm_w, apply_final):
    b, s, d = x.shape
    t = b * s
    assert d == SUBLANES * LANES, "the token-tile layout needs one (8, 128) tile per token row"
    qk = GDN_HEADS * HEAD_DIM
    cfc = cf_dw_w.shape[1]
    off_b = 4 * qk
    off_cf = off_b + 2 * GDN_HEADS

    wm = w_in[:, :off_b].astype(BF16)
    wba = jnp.zeros((d, LANES), F32).at[:, :2 * GDN_HEADS].set(w_in[:, off_b:off_cf]).astype(BF16)
    wcf = w_in[:, off_cf:].astype(BF16)
    alog = _pad_lanes(gdn_a_log, GDN_HEADS)
    dtb = _pad_lanes(gdn_dt_bias, GDN_HEADS)

    q, k, v, z, bg, bgt, u = _inproj(
        x, attn_norm_w[None, :], wm, wba, wcf, gdn_conv_w, alog, dtb,
        cf_dw_w, cf_dw_b[None, :], cf_ln_w[None, :], cf_ln_b[None, :], tile=min(512, s))
    o = _gdn(q, k, v, bg, bgt, tile=min(256, s))

    wr = jnp.zeros((d, LANES), F32).at[:, :N_EXPERTS].set(w_router)
    br = _pad_lanes(b_router)
    x2, xn, route, gates, counts = _outproj_router(
        o.reshape(t, qk), z.reshape(t, qk), u.reshape(t, cfc), x.reshape(t, d),
        gdn_norm_w[None, :], w_out[:qk].astype(BF16), w_out[qk:].astype(BF16),
        ffn_norm_w[None, :], wr, br, tile=min(512, t))

    cnt = counts[0, :N_EXPERTS]
    nblk = (cnt + EXPERT_BLOCK - 1) // EXPERT_BLOCK
    blk_end = jnp.cumsum(nblk)
    pstart = (blk_end - nblk) * EXPERT_BLOCK
    n_blocks = (t * TOP_K) // EXPERT_BLOCK + N_EXPERTS
    blk_ids = jnp.arange(n_blocks, dtype=I32)
    blk_exp = jnp.minimum(
        jnp.sum((blk_end[None, :] <= blk_ids[:, None]).astype(I32), axis=1), N_EXPERTS - 1)
    n_used = blk_end[-1:].astype(I32)

    dest = _dest(route, _pad_lanes(pstart.astype(I32)), tile=min(2048, t))
    n_rows = n_blocks * EXPERT_BLOCK
    pad_lo = jnp.concatenate([pstart + cnt, n_used * EXPERT_BLOCK]).astype(I32)
    pad_hi = jnp.concatenate([blk_end * EXPERT_BLOCK, jnp.full((1,), n_rows, I32)]).astype(I32)
    row_map = _row_map(pad_lo, pad_hi, dest.reshape(t * TOP_K), n_rows)
    y4 = _experts(blk_exp, n_used, row_map, xn, w_gate_up, b_gate_up[:, None, :],
                  w_down, b_down[:, None, :])
    out = _combine(y4, gates, x2, final_norm_w[None, :], tile=min(EXPERT_BLOCK, t))
    return out.reshape(b, s, d)


def kernel(x, attn_norm_w, w_in, gdn_conv_w, gdn_a_log, gdn_dt_bias, gdn_norm_w, cf_dw_w, cf_dw_b,
           cf_ln_w, cf_ln_b, w_out, ffn_norm_w, w_router, b_router, w_gate_up, b_gate_up, w_down,
           b_down, final_norm_w):
    depth = w_in.shape[0]
    assert depth == 1, "the fused final norm assumes a single trunk layer"
    return _layer(x, attn_norm_w[0], w_in[0], gdn_conv_w[0], gdn_a_log[0], gdn_dt_bias[0],
                  gdn_norm_w[0], cf_dw_w[0], cf_dw_b[0], cf_ln_w[0], cf_ln_b[0], w_out[0],
                  ffn_norm_w[0], w_router[0], b_router[0], w_gate_up[0], b_gate_up[0], w_down[0],
                  b_down[0], final_norm_w, True)
```

```python
import functools

import jax
import jax.numpy as jnp
from jax import lax
from jax.experimental import pallas as pl
from jax.experimental.pallas import tpu as pltpu

F32 = jnp.float32
BF16 = jnp.bfloat16
I32 = jnp.int32

NORM_EPS = 1e-6
LANES = 128
SUBLANES = 8
GDN_HEADS = 4
HEAD_DIM = 128
GDN_CHUNK = 64
GDN_CONV = 4
CF_KERNEL = 31
N_EXPERTS = 32
TOP_K = 4
TOP_K_SHIFT = 2
DMA_UNROLL = 32
SWIGLU_LIMIT = 7.0
SWIGLU_ALPHA = 1.702

QKV_HALO = 8
CF_HALO = 32
EXPERT_BLOCK = 256
VMEM_LIMIT = 56 * 1024 * 1024


def _silu(x):
    return x * jax.nn.sigmoid(x)


def _dot(a, b):
    return jnp.dot(a, b, preferred_element_type=F32)


def _dot_nt(a, b):
    return lax.dot_general(a, b, (((1,), (1,)), ((), ())), preferred_element_type=F32)


def _dot_tn(a, b):
    return lax.dot_general(a, b, (((0,), (0,)), ((), ())), preferred_element_type=F32)


def _store_token_tiles(ref, x, lead=()):
    rows, d = x.shape
    chunks = d // LANES
    for c in range(chunks):
        ref[lead + (pl.ds(c, rows, stride=chunks), slice(None))] = x[:, c * LANES:(c + 1) * LANES]


def _load_token_tiles(ref, rows, lead=()):
    chunks = ref.shape[-2] // rows
    return jnp.concatenate(
        [ref[lead + (pl.ds(c, rows, stride=chunks), slice(None))] for c in range(chunks)], axis=1)


def _inproj_kernel(x_ref, nw_ref, wm_ref, wba_ref, wcf_ref, cw_ref, alog_ref, dtb_ref,
                   dww_ref, dwb_ref, lnw_ref, lnb_ref,
                   q_ref, k_ref, v_ref, z_ref, bg_ref, bgt_ref, u_ref,
                   qkv_buf, cf_buf, cf_shift):
    tt = x_ref.shape[1]
    qk = GDN_HEADS * HEAD_DIM
    cfc = u_ref.shape[2]

    @pl.when(pl.program_id(1) == 0)
    def _():
        qkv_buf[0:QKV_HALO, :] = jnp.zeros((QKV_HALO, qkv_buf.shape[1]), F32)
        cf_buf[0:CF_HALO, :] = jnp.zeros((CF_HALO, cf_buf.shape[1]), F32)

    x = x_ref[0]
    h = x * lax.rsqrt(jnp.mean(x * x, axis=-1, keepdims=True) + NORM_EPS) * nw_ref[...]
    h = h.astype(BF16)
    pm = _dot(h, wm_ref[...])
    pba = _dot(h, wba_ref[...])
    pcf = _dot(h, wcf_ref[...])

    z_ref[0] = pm[:, 3 * qk:]

    qkv_buf[QKV_HALO:QKV_HALO + tt, :] = pm[:, :3 * qk]
    acc = None
    for j in range(GDN_CONV):
        term = cw_ref[j:j + 1, :] * qkv_buf[pl.ds(QKV_HALO - (GDN_CONV - 1) + j, tt), :]
        acc = term if acc is None else acc + term
    qkv_buf[0:QKV_HALO, :] = qkv_buf[tt:tt + QKV_HALO, :]
    qkv = _silu(acc)
    for hd in range(GDN_HEADS):
        for base, ref in ((0, q_ref), (qk, k_ref)):
            t = qkv[:, base + hd * HEAD_DIM: base + (hd + 1) * HEAD_DIM]
            t = t * lax.rsqrt(jnp.sum(t * t, axis=-1, keepdims=True) + NORM_EPS)
            ref[0, :, hd * HEAD_DIM:(hd + 1) * HEAD_DIM] = t
    v_ref[0] = qkv[:, 2 * qk:]

    lane = lax.broadcasted_iota(I32, (tt, LANES), 1)
    row = lax.broadcasted_iota(I32, (tt, LANES), 0)
    beta = jax.nn.sigmoid(pba)
    sp_in = pba + dtb_ref[...]
    softplus = jnp.maximum(sp_in, 0.0) + jnp.log(1.0 + jnp.exp(-jnp.abs(sp_in)))
    g = -jnp.exp(alog_ref[...]) * softplus
    g = jnp.where((lane >= GDN_HEADS) & (lane < 2 * GDN_HEADS), g, 0.0)
    pos = row % GDN_CHUNK
    shift = 1
    while shift < GDN_CHUNK:
        g = g + jnp.where(pos >= shift, pltpu.roll(g, shift, 0), 0.0)
        shift *= 2
    bg = jnp.where(lane < GDN_HEADS, beta, g)
    bg_ref[0] = bg
    bgt_ref[0] = jnp.transpose(bg)[0:SUBLANES, :]

    glu = pcf[:, :cfc] * jax.nn.sigmoid(pcf[:, cfc:])
    cf_buf[CF_HALO:CF_HALO + tt, :] = glu
    lo = SUBLANES
    span = tt + CF_HALO - lo
    for r in range(1, SUBLANES):
        cf_shift[r - 1, lo:lo + span, :] = cf_buf[pl.ds(lo - r, span), :]
    rows = 64
    for r0 in range(0, tt, rows):
        acc = None
        for j in range(CF_KERNEL):
            a, r = divmod(CF_KERNEL - 1 - j, SUBLANES)
            start = CF_HALO + r0 - a * SUBLANES
            src = cf_buf[start:start + rows, :] if r == 0 else cf_shift[r - 1, start:start + rows, :]
            term = dww_ref[j:j + 1, :] * src
            acc = term if acc is None else acc + term
        c = acc + dwb_ref[...]
        mu = jnp.mean(c, axis=-1, keepdims=True)
        cc = c - mu
        y = cc * lax.rsqrt(jnp.mean(cc * cc, axis=-1, keepdims=True) + NORM_EPS)
        u_ref[0, r0:r0 + rows, :] = _silu(y * lnw_ref[...] + lnb_ref[...])
    cf_buf[0:CF_HALO, :] = cf_buf[tt:tt + CF_HALO, :]


def _inproj(x, nw, wm, wba, wcf, cw, alog, dtb, dww, dwb, lnw, lnb, *, tile):
    b, s, d = x.shape
    qk = GDN_HEADS * HEAD_DIM
    cfc = dww.shape[1]
    grid = (b, s // tile)
    full = lambda a: pl.BlockSpec(a.shape, lambda i, j: (0,) * a.ndim)
    tok = lambda w: pl.BlockSpec((1, tile, w), lambda i, j: (i, j, 0))
    out_shape = (
        jax.ShapeDtypeStruct((b, s, qk), F32), jax.ShapeDtypeStruct((b, s, qk), F32),
        jax.ShapeDtypeStruct((b, s, qk), F32), jax.ShapeDtypeStruct((b, s, qk), F32),
        jax.ShapeDtypeStruct((b, s, LANES), F32), jax.ShapeDtypeStruct((b, SUBLANES, s), F32),
        jax.ShapeDtypeStruct((b, s, cfc), F32))
    return pl.pallas_call(
        _inproj_kernel,
        grid=grid,
        in_specs=[tok(d)] + [full(a) for a in (nw, wm, wba, wcf, cw, alog, dtb, dww, dwb, lnw, lnb)],
        out_specs=(tok(qk), tok(qk), tok(qk), tok(qk), tok(LANES),
                   pl.BlockSpec((1, SUBLANES, tile), lambda i, j: (i, 0, j)), tok(cfc)),
        out_shape=out_shape,
        scratch_shapes=[pltpu.VMEM((QKV_HALO + tile, 3 * qk), F32),
                        pltpu.VMEM((CF_HALO + tile, cfc), F32),
                        pltpu.VMEM((SUBLANES - 1, CF_HALO + tile, cfc), F32)],
        compiler_params=pltpu.CompilerParams(
            dimension_semantics=("arbitrary", "arbitrary"), vmem_limit_bytes=VMEM_LIMIT),
        name="inproj",
    )(x, nw, wm, wba, wcf, cw, alog, dtb, dww, dwb, lnw, lnb)


def _bmm(a, b):
    return jnp.einsum("bmk,bkn->bmn", a, b, preferred_element_type=F32)


def _bmm_nt(a, b):
    return jnp.einsum("bmk,bnk->bmn", a, b, preferred_element_type=F32)


def _bmm_tn(a, b):
    return jnp.einsum("bkm,bkn->bmn", a, b, preferred_element_type=F32)


def _unit_lower_inverse(a):
    c = a.shape[-1]
    ii = lax.broadcasted_iota(I32, (c, c), 0)
    jj = lax.broadcasted_iota(I32, (c, c), 1)
    eye = (ii == jj).astype(F32)
    same16 = (ii // 16) == (jj // 16)
    same32 = (ii // 32) == (jj // 32)
    x = jnp.where(same16, -a, 0.0)
    t = eye + x
    xp = x
    for _ in range(3):
        xp_b = xp.astype(BF16)
        xp = _bmm(xp_b, xp_b)
        t = t + _bmm(t.astype(BF16), xp.astype(BF16))
    for off in (jnp.where(same32 & ~same16, a, 0.0), jnp.where(~same32, a, 0.0)):
        tb = t.astype(BF16)
        t = t - _bmm(tb, _bmm(off.astype(BF16), tb).astype(BF16))
    return t


def _gdn_kernel(q_ref, k_ref, v_ref, bg_ref, bgt_ref, o_ref, state, s_all):
    lt = q_ref.shape[1]
    c = GDN_CHUNK
    nh = GDN_HEADS
    nc = lt // c

    @pl.when(pl.program_id(1) == 0)
    def _():
        state[...] = jnp.zeros(state.shape, F32)

    def stack(fn):
        return jnp.stack([fn(slice(n * c, (n + 1) * c), h) for n in range(nc) for h in range(nh)])

    head = lambda h: slice(h * HEAD_DIM, (h + 1) * HEAD_DIM)
    q = stack(lambda r, h: q_ref[0, r, head(h)]) * (HEAD_DIM ** -0.5)
    k = stack(lambda r, h: k_ref[0, r, head(h)])
    v = stack(lambda r, h: v_ref[0, r, head(h)])
    beta = stack(lambda r, h: bg_ref[0, r, h:h + 1])
    gcol = stack(lambda r, h: bg_ref[0, r, nh + h:nh + h + 1])
    grow = stack(lambda r, h: bgt_ref[0, nh + h:nh + h + 1, r])

    ii = lax.broadcasted_iota(I32, (c, c), 0)
    jj = lax.broadcasted_iota(I32, (c, c), 1)
    glast = gcol[:, c - 1:c, :]
    eg = jnp.exp(gcol)
    decay = jnp.where(ii >= jj, jnp.exp(jnp.minimum(gcol - grow, 0.0)), 0.0)
    kb = k * beta
    k_b = k.astype(BF16)
    a = jnp.where(ii > jj, _bmm_nt(kb.astype(BF16), k_b) * decay, 0.0)
    t = _unit_lower_inverse(a)
    rhs = jnp.concatenate([v * beta, kb * eg], axis=-1).astype(BF16)
    sol = _bmm(t.astype(BF16), rhs)
    u_val = sol[..., :HEAD_DIM]
    w_key = sol[..., HEAD_DIM:]
    intra = _bmm_nt(q.astype(BF16), k_b) * decay
    k_tail = (k * jnp.exp(glast - gcol)).astype(BF16)
    upd = _bmm_tn(k_tail, sol.astype(BF16))
    b_mat = upd[..., :HEAD_DIM]
    p_mat = upd[..., HEAD_DIM:].astype(BF16)
    g_tot = jnp.exp(glast)

    s = state[...]
    for n in range(nc):
        grp = slice(n * nh, (n + 1) * nh)
        s_b = s.astype(BF16)
        s_all[grp] = s_b
        s = s * g_tot[grp] - _bmm(p_mat[grp], s_b) + b_mat[grp]
    state[...] = s

    wq = jnp.concatenate([w_key, q * eg], axis=1).astype(BF16)
    ws_qs = _bmm(wq, s_all[...])
    v_new = u_val - ws_qs[:, :c]
    o = ws_qs[:, c:] + _bmm(intra.astype(BF16), v_new.astype(BF16))
    for n in range(nc):
        for h in range(nh):
            o_ref[0, n * c:(n + 1) * c, head(h)] = o[n * nh + h]


def _gdn(q, k, v, bg, bgt, *, tile):
    b, s, qk = q.shape
    grid = (b, s // tile)
    tok = lambda w: pl.BlockSpec((1, tile, w), lambda i, j: (i, j, 0))
    n_prob = (tile // GDN_CHUNK) * GDN_HEADS
    return pl.pallas_call(
        _gdn_kernel,
        grid=grid,
        in_specs=[tok(qk), tok(qk), tok(qk), tok(LANES),
                  pl.BlockSpec((1, SUBLANES, tile), lambda i, j: (i, 0, j))],
        out_specs=tok(qk),
        out_shape=jax.ShapeDtypeStruct((b, s, qk), F32),
        scratch_shapes=[pltpu.VMEM((GDN_HEADS, HEAD_DIM, HEAD_DIM), F32),
                        pltpu.VMEM((n_prob, HEAD_DIM, HEAD_DIM), BF16)],
        compiler_params=pltpu.CompilerParams(
            dimension_semantics=("arbitrary", "arbitrary"), vmem_limit_bytes=VMEM_LIMIT),
        name="gdn",
    )(q, k, v, bg, bgt)


def _split_bf16(x):
    hi = x.astype(BF16)
    lo = (x - hi.astype(F32)).astype(BF16)
    return hi, lo


def _outproj_router_kernel(o_ref, z_ref, u_ref, x_ref, gnw_ref, wa_ref, wb_ref, fnw_ref,
                           wr_ref, br_ref,
                           x2_ref, xn_ref, route_ref, gate_ref, counts_ref, carry):
    tt = x_ref.shape[0]

    @pl.when(pl.program_id(0) == 0)
    def _():
        carry[...] = jnp.zeros(carry.shape, F32)

    parts = []
    for hd in range(GDN_HEADS):
        sl = slice(hd * HEAD_DIM, (hd + 1) * HEAD_DIM)
        oh = o_ref[:, sl]
        y = oh * lax.rsqrt(jnp.mean(oh * oh, axis=-1, keepdims=True) + NORM_EPS) * gnw_ref[...]
        parts.append((y * _silu(z_ref[:, sl])).astype(BF16))
    out_a = jnp.concatenate(parts, axis=-1)
    x2 = x_ref[...] + _dot(out_a, wa_ref[...]) + _dot(u_ref[...].astype(BF16), wb_ref[...])
    x2_ref[...] = x2

    xn = x2 * lax.rsqrt(jnp.mean(x2 * x2, axis=-1, keepdims=True) + NORM_EPS) * fnw_ref[...]
    _store_token_tiles(xn_ref, xn)

    xh, xl = _split_bf16(xn)
    wh, wl = _split_bf16(wr_ref[...])
    logits = _dot(xh, wh) + _dot(xh, wl) + _dot(xl, wh) + br_ref[...]

    lane = lax.broadcasted_iota(I32, (tt, LANES), 1)
    lane_f = lane.astype(F32)
    neg = jnp.float32(-jnp.inf)
    work = jnp.where(lane < N_EXPERTS, logits, neg)
    vals, idxs = [], []
    onehot = jnp.zeros((tt, LANES), F32)
    for _ in range(TOP_K):
        m = jnp.max(work, axis=-1, keepdims=True)
        idx = jnp.min(jnp.where(work == m, lane_f, float(LANES)), axis=-1, keepdims=True).astype(I32)
        sel = lane == idx
        vals.append(m)
        idxs.append(idx)
        onehot = onehot + sel.astype(F32)
        work = jnp.where(sel, neg, work)
    exps = [jnp.exp(v - vals[0]) for v in vals]
    denom = exps[0] + exps[1] + exps[2] + exps[3]

    ri = lax.broadcasted_iota(I32, (tt, tt), 0)
    ci = lax.broadcasted_iota(I32, (tt, tt), 1)
    strict = (ri > ci).astype(BF16)
    base = carry[...] + _dot(strict, onehot.astype(BF16))
    route = jnp.zeros((tt, LANES), I32)
    gates = jnp.zeros((tt, LANES), F32)
    for kk in range(TOP_K):
        rank = jnp.sum(jnp.where(lane == idxs[kk], base, 0.0), axis=-1, keepdims=True)
        route = jnp.where(lane == kk, idxs[kk], route)
        route = jnp.where(lane == kk + TOP_K, rank.astype(I32), route)
        gates = jnp.where(lane == kk, exps[kk] / denom, gates)
    route_ref[...] = route
    gate_ref[...] = gates
    new_carry = carry[...] + jnp.sum(onehot, axis=0, keepdims=True)
    carry[...] = new_carry
    counts_ref[...] = new_carry.astype(I32)


def _outproj_router(o, z, u, x, gnw, wa, wb, fnw, wr, br, *, tile):
    t, d = x.shape
    grid = (t // tile,)
    full = lambda a: pl.BlockSpec(a.shape, lambda i: (0,) * a.ndim)
    tok = lambda w: pl.BlockSpec((tile, w), lambda i: (i, 0))
    return pl.pallas_call(
        _outproj_router_kernel,
        grid=grid,
        in_specs=[tok(o.shape[1]), tok(z.shape[1]), tok(u.shape[1]), tok(d)]
                 + [full(a) for a in (gnw, wa, wb, fnw, wr, br)],
        out_specs=(tok(d), pl.BlockSpec((tile * (d // LANES), LANES), lambda i: (i, 0)),
                   tok(LANES), tok(LANES), pl.BlockSpec((1, LANES), lambda i: (0, 0))),
        out_shape=(jax.ShapeDtypeStruct((t, d), F32), jax.ShapeDtypeStruct((t * (d // LANES), LANES), F32),
                   jax.ShapeDtypeStruct((t, LANES), I32), jax.ShapeDtypeStruct((t, LANES), F32),
                   jax.ShapeDtypeStruct((1, LANES), I32)),
        scratch_shapes=[pltpu.VMEM((1, LANES), F32)],
        compiler_params=pltpu.CompilerParams(
            dimension_semantics=("arbitrary",), vmem_limit_bytes=VMEM_LIMIT),
        name="outproj_router",
    )(o, z, u, x, gnw, wa, wb, fnw, wr, br)


def _dest_kernel(route_ref, pstart_ref, dest_ref):
    route = route_ref[...].astype(F32)
    tt = route.shape[0]
    lane = lax.broadcasted_iota(I32, (tt, LANES), 1)
    pstart = pstart_ref[...].astype(F32)
    dest = jnp.zeros((tt, LANES), F32)
    for kk in range(TOP_K):
        idx = jnp.sum(jnp.where(lane == kk, route, 0.0), axis=-1, keepdims=True)
        rank = jnp.sum(jnp.where(lane == kk + TOP_K, route, 0.0), axis=-1, keepdims=True)
        start = jnp.sum(jnp.where(lane == idx.astype(I32), pstart, 0.0), axis=-1, keepdims=True)
        dest = jnp.where(lane == kk, start + rank, dest)
    dest_ref[...] = dest[:, :TOP_K].astype(I32)


def _dest(route, pstart, *, tile):
    t = route.shape[0]
    return pl.pallas_call(
        _dest_kernel,
        grid=(t // tile,),
        in_specs=[pl.BlockSpec((tile, LANES), lambda i: (i, 0)),
                  pl.BlockSpec((1, LANES), lambda i: (0, 0))],
        out_specs=pl.BlockSpec((tile, TOP_K), lambda i: (i, 0)),
        out_shape=jax.ShapeDtypeStruct((t, TOP_K), I32),
        compiler_params=pltpu.CompilerParams(dimension_semantics=("arbitrary",)),
        name="dest_rows",
    )(route, pstart)


ROW_MAP_CHUNK = 8192
ROW_MAP_UNROLL = 32


def _row_map_kernel(pad_lo_ref, pad_hi_ref, dest_ref, map_ref, *, n_tokens):
    g = pl.program_id(0)
    n = dest_ref.shape[0]

    @pl.when(g == 0)
    def _():
        def fill_range(e, carry):
            def fill(r, c):
                map_ref[r] = (n_tokens + (r & (EXPERT_BLOCK - 1))) * TOP_K
                return c
            return lax.fori_loop(pad_lo_ref[e], pad_hi_ref[e], fill, carry)
        lax.fori_loop(0, pad_lo_ref.shape[0], fill_range, 0)

    def body(j, carry):
        for un in range(ROW_MAP_UNROLL):
            i = j * ROW_MAP_UNROLL + un
            map_ref[dest_ref[i]] = g * n + i
        return carry

    lax.fori_loop(0, n // ROW_MAP_UNROLL, body, 0)


def _row_map(pad_lo, pad_hi, dest_flat, n_rows):
    n = dest_flat.shape[0]
    chunk = min(ROW_MAP_CHUNK, n)
    grid_spec = pltpu.PrefetchScalarGridSpec(
        num_scalar_prefetch=2,
        grid=(n // chunk,),
        in_specs=[pl.BlockSpec((chunk,), lambda g, lo, hi: (g,), memory_space=pltpu.SMEM)],
        out_specs=pl.BlockSpec(memory_space=pltpu.SMEM))
    return pl.pallas_call(
        functools.partial(_row_map_kernel, n_tokens=n // TOP_K),
        grid_spec=grid_spec,
        out_shape=jax.ShapeDtypeStruct((n_rows,), I32),
        compiler_params=pltpu.CompilerParams(dimension_semantics=("arbitrary",)),
        name="row_map",
    )(pad_lo, pad_hi, dest_flat)


def _expert_kernel(blk_exp_ref, n_used_ref, map_ref, map_next_ref,
                   xn_hbm, wgu_ref, bgu_ref, wd_ref, bd_ref, y4_hbm,
                   xbuf, ybuf, wgu_b, wd_b, gsem, ssem):
    i = pl.program_id(0)
    n_used = n_used_ref[0]
    chunks = SUBLANES
    bm = xbuf.shape[1] // chunks
    t = xn_hbm.shape[0] // chunks
    dff = wd_ref.shape[1]
    slot = i % 2

    def tile_rows(r):
        return pl.ds(pl.multiple_of(r * chunks, chunks), chunks)

    def start_gather(mref, s):
        def body(j, carry):
            for un in range(DMA_UNROLL):
                r = j * DMA_UNROLL + un
                tok = jnp.minimum(lax.shift_right_logical(mref[r], TOP_K_SHIFT), t - 1)
                pltpu.make_async_copy(xn_hbm.at[tile_rows(tok)], xbuf.at[s, tile_rows(r)], gsem.at[s]).start()
            return carry
        lax.fori_loop(0, bm // DMA_UNROLL, body, 0)

    def start_scatter(s):
        def body(j, carry):
            for un in range(DMA_UNROLL):
                r = j * DMA_UNROLL + un
                pair = map_ref[r]
                flat = (pair & (TOP_K - 1)) * (t + bm) + lax.shift_right_logical(pair, TOP_K_SHIFT)
                pltpu.make_async_copy(ybuf.at[s, tile_rows(r)], y4_hbm.at[tile_rows(flat)], ssem.at[s]).start()
            return carry
        lax.fori_loop(0, bm // DMA_UNROLL, body, 0)

    def wait_gather(s):
        pltpu.make_async_copy(xn_hbm.at[pl.ds(0, bm * chunks)], xbuf.at[s], gsem.at[s]).wait()

    def wait_scatter(s):
        pltpu.make_async_copy(ybuf.at[s], y4_hbm.at[pl.ds(0, bm * chunks)], ssem.at[s]).wait()

    def per_slot(fn):
        for s in range(2):
            pl.when(slot == s)(functools.partial(fn, s))

    @pl.when(i == 0)
    def _():
        start_gather(map_ref, 0)

    @pl.when(i + 1 < n_used)
    def _():
        per_slot(lambda s: start_gather(map_next_ref, 1 - s))

    prev = blk_exp_ref[jnp.maximum(i - 1, 0)]
    changed = (i == 0) | (blk_exp_ref[i] != prev)

    @pl.when(changed & (i < n_used))
    def _():
        wgu_b[...] = wgu_ref[0].astype(BF16)
        wd_b[...] = wd_ref[0].astype(BF16)

    @pl.when(i < n_used)
    def _():
        per_slot(wait_gather)

        @pl.when(i >= 2)
        def _():
            per_slot(wait_scatter)

        xb = _load_token_tiles(xbuf, bm, (slot,)).astype(BF16)
        hid = _dot(xb, wgu_b[...]) + bgu_ref[0]
        gate = jnp.minimum(hid[:, :dff], SWIGLU_LIMIT)
        up = jnp.clip(hid[:, dff:], -SWIGLU_LIMIT, SWIGLU_LIMIT)
        glu = gate * jax.nn.sigmoid(SWIGLU_ALPHA * gate)
        act = ((up + 1.0) * glu).astype(BF16)
        _store_token_tiles(ybuf, _dot(act, wd_b[...]) + bd_ref[0], (slot,))

        per_slot(start_scatter)

        @pl.when(i == n_used - 1)
        def _():
            per_slot(wait_scatter)

            @pl.when(i >= 1)
            def _():
                per_slot(lambda s: wait_scatter(1 - s))


def _experts(blk_exp, n_used, row_map, xn_tiles, wgu, bgu, wd, bd):
    d = wgu.shape[1]
    chunks = d // LANES
    t = xn_tiles.shape[0] // chunks
    n_blocks = row_map.shape[0] // EXPERT_BLOCK
    buf = pltpu.VMEM((2, EXPERT_BLOCK * chunks, LANES), F32)
    two_f = wgu.shape[2]
    dff = wd.shape[1]
    exp3 = lambda i, be, nu: (be[jnp.minimum(i, nu[0] - 1)], 0, 0)
    smem_blk = lambda off: pl.BlockSpec(
        (EXPERT_BLOCK,), lambda i, be, nu: (jnp.minimum(i + off, nu[0] - 1),), memory_space=pltpu.SMEM)
    grid_spec = pltpu.PrefetchScalarGridSpec(
        num_scalar_prefetch=2,
        grid=(n_blocks,),
        in_specs=[smem_blk(0), smem_blk(1),
                  pl.BlockSpec(memory_space=pl.ANY),
                  pl.BlockSpec((1, d, two_f), exp3),
                  pl.BlockSpec((1, 1, two_f), exp3),
                  pl.BlockSpec((1, dff, d), exp3),
                  pl.BlockSpec((1, 1, d), exp3)],
        out_specs=pl.BlockSpec(memory_space=pl.ANY),
        scratch_shapes=[buf, buf, pltpu.VMEM((d, two_f), BF16), pltpu.VMEM((dff, d), BF16),
                        pltpu.SemaphoreType.DMA((2,)), pltpu.SemaphoreType.DMA((2,))])
    return pl.pallas_call(
        _expert_kernel,
        grid_spec=grid_spec,
        out_shape=jax.ShapeDtypeStruct((TOP_K * (t + EXPERT_BLOCK) * chunks, LANES), F32),
        compiler_params=pltpu.CompilerParams(
            dimension_semantics=("arbitrary",), vmem_limit_bytes=VMEM_LIMIT),
        name="experts",
    )(blk_exp, n_used, row_map, row_map, xn_tiles, wgu, bgu, wd, bd)


def _combine_kernel(y0_ref, y1_ref, y2_ref, y3_ref, gate_ref, x2_ref, fw_ref, out_ref):
    gates = gate_ref[...]
    lane = lax.broadcasted_iota(I32, gates.shape, 1)
    x3 = x2_ref[...]
    for kk, y_ref in enumerate((y0_ref, y1_ref, y2_ref, y3_ref)):
        gk = jnp.sum(jnp.where(lane == kk, gates, 0.0), axis=-1, keepdims=True)
        x3 = x3 + gk * _load_token_tiles(y_ref, x3.shape[0])
    out_ref[...] = x3 * lax.rsqrt(jnp.mean(x3 * x3, axis=-1, keepdims=True) + NORM_EPS) * fw_ref[...]


def _combine(y4, gates, x2, fw, *, tile):
    t, d = x2.shape
    steps = t // tile
    seg = (t + EXPERT_BLOCK) // tile
    choice = lambda kk: pl.BlockSpec((tile * (d // LANES), LANES), lambda i: (kk * seg + i, 0))
    return pl.pallas_call(
        _combine_kernel,
        grid=(steps,),
        in_specs=[choice(kk) for kk in range(TOP_K)]
                 + [pl.BlockSpec((tile, LANES), lambda i: (i, 0)),
                    pl.BlockSpec((tile, d), lambda i: (i, 0)),
                    pl.BlockSpec((1, d), lambda i: (0, 0))],
        out_specs=pl.BlockSpec((tile, d), lambda i: (i, 0)),
        out_shape=jax.ShapeDtypeStruct((t, d), F32),
        compiler_params=pltpu.CompilerParams(
            dimension_semantics=("arbitrary",), vmem_limit_bytes=VMEM_LIMIT),
        name="combine",
    )(y4, y4, y4, y4, gates, x2, fw)


def _pad_lanes(a, offset=0, fill=0.0):
    out = jnp.full((1, LANES), fill, a.dtype)
    return out.at[0, offset:offset + a.shape[0]].set(a)


def _layer(x, attn_norm_w, w_in, gdn_conv_w, gdn_a_log, gdn_dt_bias, gdn_norm_w,
           cf_dw_w, cf_dw_b, cf_ln_w, cf_ln_b, w_out, ffn_norm_w, w_router, b_router,
           w_gate_up, b_gate_up, w_down, b_down, final_norm_w, apply_final):
    b, s, d = x.shape
    t = b * s
    assert d == SUBLANES * LANES, "the token-tile layout needs one (8, 128) tile per token row"
    qk = GDN_HEADS * HEAD_DIM
    cfc = cf_dw_w.shape[1]
    off_b = 4 * qk
    off_cf = off_b + 2 * GDN_HEADS

    wm = w_in[:, :off_b].astype(BF16)
    wba = jnp.zeros((d, LANES), F32).at[:, :2 * GDN_HEADS].set(w_in[:, off_b:off_cf]).astype(BF16)
    wcf = w_in[:, off_cf:].astype(BF16)
    alog = _pad_lanes(gdn_a_log, GDN_HEADS)
    dtb = _pad_lanes(gdn_dt_bias, GDN_HEADS)

    q, k, v, z, bg, bgt, u = _inproj(
        x, attn_norm_w[None, :], wm, wba, wcf, gdn_conv_w, alog, dtb,
        cf_dw_w, cf_dw_b[None, :], cf_ln_w[None, :], cf_ln_b[None, :], tile=min(512, s))
    o = _gdn(q, k, v, bg, bgt, tile=min(256, s))

    wr = jnp.zeros((d, LANES), F32).at[:, :N_EXPERTS].set(w_router)
    br = _pad_lanes(b_router)
    x2, xn, route, gates, counts = _outproj_router(
        o.reshape(t, qk), z.reshape(t, qk), u.reshape(t, cfc), x.reshape(t, d),
        gdn_norm_w[None, :], w_out[:qk].astype(BF16), w_out[qk:].astype(BF16),
        ffn_norm_w[None, :], wr, br, tile=min(512, t))

    cnt = counts[0, :N_EXPERTS]
    nblk = (cnt + EXPERT_BLOCK - 1) // EXPERT_BLOCK
    blk_end = jnp.cumsum(nblk)
    pstart = (blk_end - nblk) * EXPERT_BLOCK
    n_blocks = (t * TOP_K) // EXPERT_BLOCK + N_EXPERTS
    blk_ids = jnp.arange(n_blocks, dtype=I32)
    blk_exp = jnp.minimum(
        jnp.sum((blk_end[None, :] <= blk_ids[:, None]).astype(I32), axis=1), N_EXPERTS - 1)
    n_used = blk_end[-1:].astype(I32)

    dest = _dest(route, _pad_lanes(pstart.astype(I32)), tile=min(2048, t))
    n_rows = n_blocks * EXPERT_BLOCK
    pad_lo = jnp.concatenate([pstart + cnt, n_used * EXPERT_BLOCK]).astype(I32)
    pad_hi = jnp.concatenate([blk_end * EXPERT_BLOCK, jnp.full((1,), n_rows, I32)]).astype(I32)
    row_map = _row_map(pad_lo, pad_hi, dest.reshape(t * TOP_K), n_rows)
    y4 = _experts(blk_exp, n_used, row_map, xn, w_gate_up, b_gate_up[:, None, :],
                  w_down, b_down[:, None, :])
    out = _combine(y4, gates, x2, final_norm_w[None, :], tile=min(EXPERT_BLOCK, t))
    return out.reshape(b, s, d)


def kernel(x, attn_norm_w, w_in, gdn_conv_w, gdn_a_log, gdn_dt_bias, gdn_norm_w, cf_dw_w, cf_dw_b,
           cf_ln_w, cf_ln_b, w_out, ffn_norm_w, w_router, b_router, w_gate_up, b_gate_up, w_down,
           b_down, final_norm_w):
    depth = w_in.shape[0]
    assert depth == 1, "the fused final norm assumes a single trunk layer"
    return _layer(x, attn_norm_w[0], w_in[0], gdn_conv_w[0], gdn_a_log[0], gdn_dt_bias[0],
                  gdn_norm_w[0], cf_dw_w[0], cf_dw_b[0], cf_ln_w[0], cf_ln_b[0], w_out[0],
                  ffn_norm_w[0], w_router[0], b_router[0], w_gate_up[0], b_gate_up[0], w_down[0],
                  b_down[0], final_norm_w, True)
```

```python
import functools

import jax
import jax.numpy as jnp
from jax import lax
from jax.experimental import pallas as pl
from jax.experimental.pallas import tpu as pltpu
from jax.experimental.pallas import tpu_sc as plsc

F32 = jnp.float32
BF16 = jnp.bfloat16
I32 = jnp.int32

NORM_EPS = 1e-6
LANES = 128
SUBLANES = 8
GDN_HEADS = 4
HEAD_DIM = 128
GDN_CHUNK = 64
GDN_CONV = 4
CF_KERNEL = 31
N_EXPERTS = 32
TOP_K = 4
SWIGLU_LIMIT = 7.0
SWIGLU_ALPHA = 1.702

QKV_HALO = 8
CF_HALO = 32
EXPERT_BLOCK = 256
VMEM_LIMIT = 56 * 1024 * 1024


def _silu(x):
    return x * jax.nn.sigmoid(x)


def _dot(a, b):
    return jnp.dot(a, b, preferred_element_type=F32)


def _dot_nt(a, b):
    return lax.dot_general(a, b, (((1,), (1,)), ((), ())), preferred_element_type=F32)


def _dot_tn(a, b):
    return lax.dot_general(a, b, (((0,), (0,)), ((), ())), preferred_element_type=F32)


def _store_token_tiles(ref, x, lead=()):
    rows, d = x.shape
    chunks = d // LANES
    for c in range(chunks):
        ref[lead + (pl.ds(c, rows, stride=chunks), slice(None))] = x[:, c * LANES:(c + 1) * LANES]


def _load_token_tiles(ref, rows, lead=()):
    chunks = ref.shape[-2] // rows
    return jnp.concatenate(
        [ref[lead + (pl.ds(c, rows, stride=chunks), slice(None))] for c in range(chunks)], axis=1)


def _inproj_kernel(x_ref, nw_ref, wm_ref, wba_ref, wcf_ref, cw_ref, alog_ref, dtb_ref,
                   dww_ref, dwb_ref, lnw_ref, lnb_ref,
                   q_ref, k_ref, v_ref, z_ref, bg_ref, bgt_ref, u_ref,
                   qkv_buf, cf_buf, cf_shift):
    tt = x_ref.shape[1]
    qk = GDN_HEADS * HEAD_DIM
    cfc = u_ref.shape[2]

    @pl.when(pl.program_id(1) == 0)
    def _():
        qkv_buf[0:QKV_HALO, :] = jnp.zeros((QKV_HALO, qkv_buf.shape[1]), F32)
        cf_buf[0:CF_HALO, :] = jnp.zeros((CF_HALO, cf_buf.shape[1]), F32)

    x = x_ref[0]
    h = x * lax.rsqrt(jnp.mean(x * x, axis=-1, keepdims=True) + NORM_EPS) * nw_ref[...]
    h = h.astype(BF16)
    pm = _dot(h, wm_ref[...])
    pba = _dot(h, wba_ref[...])
    pcf = _dot(h, wcf_ref[...])

    z_ref[0] = pm[:, 3 * qk:]

    qkv_buf[QKV_HALO:QKV_HALO + tt, :] = pm[:, :3 * qk]
    acc = None
    for j in range(GDN_CONV):
        term = cw_ref[j:j + 1, :] * qkv_buf[pl.ds(QKV_HALO - (GDN_CONV - 1) + j, tt), :]
        acc = term if acc is None else acc + term
    qkv_buf[0:QKV_HALO, :] = qkv_buf[tt:tt + QKV_HALO, :]
    qkv = _silu(acc)
    for hd in range(GDN_HEADS):
        for base, ref in ((0, q_ref), (qk, k_ref)):
            t = qkv[:, base + hd * HEAD_DIM: base + (hd + 1) * HEAD_DIM]
            t = t * lax.rsqrt(jnp.sum(t * t, axis=-1, keepdims=True) + NORM_EPS)
            ref[0, :, hd * HEAD_DIM:(hd + 1) * HEAD_DIM] = t
    v_ref[0] = qkv[:, 2 * qk:]

    lane = lax.broadcasted_iota(I32, (tt, LANES), 1)
    row = lax.broadcasted_iota(I32, (tt, LANES), 0)
    beta = jax.nn.sigmoid(pba)
    sp_in = pba + dtb_ref[...]
    softplus = jnp.maximum(sp_in, 0.0) + jnp.log(1.0 + jnp.exp(-jnp.abs(sp_in)))
    g = -jnp.exp(alog_ref[...]) * softplus
    g = jnp.where((lane >= GDN_HEADS) & (lane < 2 * GDN_HEADS), g, 0.0)
    pos = row % GDN_CHUNK
    shift = 1
    while shift < GDN_CHUNK:
        g = g + jnp.where(pos >= shift, pltpu.roll(g, shift, 0), 0.0)
        shift *= 2
    bg = jnp.where(lane < GDN_HEADS, beta, g)
    bg_ref[0] = bg
    bgt_ref[0] = jnp.transpose(bg)[0:SUBLANES, :]

    glu = pcf[:, :cfc] * jax.nn.sigmoid(pcf[:, cfc:])
    cf_buf[CF_HALO:CF_HALO + tt, :] = glu
    lo = SUBLANES
    span = tt + CF_HALO - lo
    for r in range(1, SUBLANES):
        cf_shift[r - 1, lo:lo + span, :] = cf_buf[pl.ds(lo - r, span), :]
    rows = 64
    for r0 in range(0, tt, rows):
        acc = None
        for j in range(CF_KERNEL):
            a, r = divmod(CF_KERNEL - 1 - j, SUBLANES)
            start = CF_HALO + r0 - a * SUBLANES
            src = cf_buf[start:start + rows, :] if r == 0 else cf_shift[r - 1, start:start + rows, :]
            term = dww_ref[j:j + 1, :] * src
            acc = term if acc is None else acc + term
        c = acc + dwb_ref[...]
        mu = jnp.mean(c, axis=-1, keepdims=True)
        cc = c - mu
        y = cc * lax.rsqrt(jnp.mean(cc * cc, axis=-1, keepdims=True) + NORM_EPS)
        u_ref[0, r0:r0 + rows, :] = _silu(y * lnw_ref[...] + lnb_ref[...])
    cf_buf[0:CF_HALO, :] = cf_buf[tt:tt + CF_HALO, :]


def _inproj(x, nw, wm, wba, wcf, cw, alog, dtb, dww, dwb, lnw, lnb, *, tile):
    b, s, d = x.shape
    qk = GDN_HEADS * HEAD_DIM
    cfc = dww.shape[1]
    grid = (b, s // tile)
    full = lambda a: pl.BlockSpec(a.shape, lambda i, j: (0,) * a.ndim)
    tok = lambda w: pl.BlockSpec((1, tile, w), lambda i, j: (i, j, 0))
    out_shape = (
        jax.ShapeDtypeStruct((b, s, qk), F32), jax.ShapeDtypeStruct((b, s, qk), F32),
        jax.ShapeDtypeStruct((b, s, qk), F32), jax.ShapeDtypeStruct((b, s, qk), F32),
        jax.ShapeDtypeStruct((b, s, LANES), F32), jax.ShapeDtypeStruct((b, SUBLANES, s), F32),
        jax.ShapeDtypeStruct((b, s, cfc), F32))
    return pl.pallas_call(
        _inproj_kernel,
        grid=grid,
        in_specs=[tok(d)] + [full(a) for a in (nw, wm, wba, wcf, cw, alog, dtb, dww, dwb, lnw, lnb)],
        out_specs=(tok(qk), tok(qk), tok(qk), tok(qk), tok(LANES),
                   pl.BlockSpec((1, SUBLANES, tile), lambda i, j: (i, 0, j)), tok(cfc)),
        out_shape=out_shape,
        scratch_shapes=[pltpu.VMEM((QKV_HALO + tile, 3 * qk), F32),
                        pltpu.VMEM((CF_HALO + tile, cfc), F32),
                        pltpu.VMEM((SUBLANES - 1, CF_HALO + tile, cfc), F32)],
        compiler_params=pltpu.CompilerParams(
            dimension_semantics=("arbitrary", "arbitrary"), vmem_limit_bytes=VMEM_LIMIT),
        name="inproj",
    )(x, nw, wm, wba, wcf, cw, alog, dtb, dww, dwb, lnw, lnb)


def _bmm(a, b):
    return jnp.einsum("bmk,bkn->bmn", a, b, preferred_element_type=F32)


def _bmm_nt(a, b):
    return jnp.einsum("bmk,bnk->bmn", a, b, preferred_element_type=F32)


def _bmm_tn(a, b):
    return jnp.einsum("bkm,bkn->bmn", a, b, preferred_element_type=F32)


def _unit_lower_inverse(a):
    c = a.shape[-1]
    ii = lax.broadcasted_iota(I32, (c, c), 0)
    jj = lax.broadcasted_iota(I32, (c, c), 1)
    eye = (ii == jj).astype(F32)
    same16 = (ii // 16) == (jj // 16)
    same32 = (ii // 32) == (jj // 32)
    x = jnp.where(same16, -a, 0.0)
    t = eye + x
    xp = x
    for _ in range(3):
        xp_b = xp.astype(BF16)
        xp = _bmm(xp_b, xp_b)
        t = t + _bmm(t.astype(BF16), xp.astype(BF16))
    for off in (jnp.where(same32 & ~same16, a, 0.0), jnp.where(~same32, a, 0.0)):
        tb = t.astype(BF16)
        t = t - _bmm(tb, _bmm(off.astype(BF16), tb).astype(BF16))
    return t


def _gdn_kernel(q_ref, k_ref, v_ref, bg_ref, bgt_ref, o_ref, state, s_all):
    lt = q_ref.shape[1]
    c = GDN_CHUNK
    nh = GDN_HEADS
    nc = lt // c

    @pl.when(pl.program_id(1) == 0)
    def _():
        state[...] = jnp.zeros(state.shape, F32)

    def stack(fn):
        return jnp.stack([fn(slice(n * c, (n + 1) * c), h) for n in range(nc) for h in range(nh)])

    head = lambda h: slice(h * HEAD_DIM, (h + 1) * HEAD_DIM)
    q = stack(lambda r, h: q_ref[0, r, head(h)]) * (HEAD_DIM ** -0.5)
    k = stack(lambda r, h: k_ref[0, r, head(h)])
    v = stack(lambda r, h: v_ref[0, r, head(h)])
    beta = stack(lambda r, h: bg_ref[0, r, h:h + 1])
    gcol = stack(lambda r, h: bg_ref[0, r, nh + h:nh + h + 1])
    grow = stack(lambda r, h: bgt_ref[0, nh + h:nh + h + 1, r])

    ii = lax.broadcasted_iota(I32, (c, c), 0)
    jj = lax.broadcasted_iota(I32, (c, c), 1)
    glast = gcol[:, c - 1:c, :]
    eg = jnp.exp(gcol)
    decay = jnp.where(ii >= jj, jnp.exp(jnp.minimum(gcol - grow, 0.0)), 0.0)
    kb = k * beta
    k_b = k.astype(BF16)
    a = jnp.where(ii > jj, _bmm_nt(kb.astype(BF16), k_b) * decay, 0.0)
    t = _unit_lower_inverse(a)
    rhs = jnp.concatenate([v * beta, kb * eg], axis=-1).astype(BF16)
    sol = _bmm(t.astype(BF16), rhs)
    u_val = sol[..., :HEAD_DIM]
    w_key = sol[..., HEAD_DIM:]
    intra = _bmm_nt(q.astype(BF16), k_b) * decay
    k_tail = (k * jnp.exp(glast - gcol)).astype(BF16)
    upd = _bmm_tn(k_tail, sol.astype(BF16))
    b_mat = upd[..., :HEAD_DIM]
    p_mat = upd[..., HEAD_DIM:].astype(BF16)
    g_tot = jnp.exp(glast)

    s = state[...]
    for n in range(nc):
        grp = slice(n * nh, (n + 1) * nh)
        s_b = s.astype(BF16)
        s_all[grp] = s_b
        s = s * g_tot[grp] - _bmm(p_mat[grp], s_b) + b_mat[grp]
    state[...] = s

    wq = jnp.concatenate([w_key, q * eg], axis=1).astype(BF16)
    ws_qs = _bmm(wq, s_all[...])
    v_new = u_val - ws_qs[:, :c]
    o = ws_qs[:, c:] + _bmm(intra.astype(BF16), v_new.astype(BF16))
    for n in range(nc):
        for h in range(nh):
            o_ref[0, n * c:(n + 1) * c, head(h)] = o[n * nh + h]


def _gdn(q, k, v, bg, bgt, *, tile):
    b, s, qk = q.shape
    grid = (b, s // tile)
    tok = lambda w: pl.BlockSpec((1, tile, w), lambda i, j: (i, j, 0))
    n_prob = (tile // GDN_CHUNK) * GDN_HEADS
    return pl.pallas_call(
        _gdn_kernel,
        grid=grid,
        in_specs=[tok(qk), tok(qk), tok(qk), tok(LANES),
                  pl.BlockSpec((1, SUBLANES, tile), lambda i, j: (i, 0, j))],
        out_specs=tok(qk),
        out_shape=jax.ShapeDtypeStruct((b, s, qk), F32),
        scratch_shapes=[pltpu.VMEM((GDN_HEADS, HEAD_DIM, HEAD_DIM), F32),
                        pltpu.VMEM((n_prob, HEAD_DIM, HEAD_DIM), BF16)],
        compiler_params=pltpu.CompilerParams(
            dimension_semantics=("arbitrary", "arbitrary"), vmem_limit_bytes=VMEM_LIMIT),
        name="gdn",
    )(q, k, v, bg, bgt)


def _split_bf16(x):
    hi = x.astype(BF16)
    lo = (x - hi.astype(F32)).astype(BF16)
    return hi, lo


def _outproj_router_kernel(o_ref, z_ref, u_ref, x_ref, gnw_ref, wa_ref, wb_ref, fnw_ref,
                           wr_ref, br_ref,
                           x2_ref, xn_ref, route_ref, gate_ref, counts_ref, carry):
    tt = x_ref.shape[0]

    @pl.when(pl.program_id(0) == 0)
    def _():
        carry[...] = jnp.zeros(carry.shape, F32)

    parts = []
    for hd in range(GDN_HEADS):
        sl = slice(hd * HEAD_DIM, (hd + 1) * HEAD_DIM)
        oh = o_ref[:, sl]
        y = oh * lax.rsqrt(jnp.mean(oh * oh, axis=-1, keepdims=True) + NORM_EPS) * gnw_ref[...]
        parts.append((y * _silu(z_ref[:, sl])).astype(BF16))
    out_a = jnp.concatenate(parts, axis=-1)
    x2 = x_ref[...] + _dot(out_a, wa_ref[...]) + _dot(u_ref[...].astype(BF16), wb_ref[...])
    x2_ref[...] = x2

    xn = x2 * lax.rsqrt(jnp.mean(x2 * x2, axis=-1, keepdims=True) + NORM_EPS) * fnw_ref[...]
    _store_token_tiles(xn_ref, xn)

    xh, xl = _split_bf16(xn)
    wh, wl = _split_bf16(wr_ref[...])
    logits = _dot(xh, wh) + _dot(xh, wl) + _dot(xl, wh) + br_ref[...]

    lane = lax.broadcasted_iota(I32, (tt, LANES), 1)
    lane_f = lane.astype(F32)
    neg = jnp.float32(-jnp.inf)
    work = jnp.where(lane < N_EXPERTS, logits, neg)
    vals, idxs = [], []
    onehot = jnp.zeros((tt, LANES), F32)
    for _ in range(TOP_K):
        m = jnp.max(work, axis=-1, keepdims=True)
        idx = jnp.min(jnp.where(work == m, lane_f, float(LANES)), axis=-1, keepdims=True).astype(I32)
        sel = lane == idx
        vals.append(m)
        idxs.append(idx)
        onehot = onehot + sel.astype(F32)
        work = jnp.where(sel, neg, work)
    exps = [jnp.exp(v - vals[0]) for v in vals]
    denom = exps[0] + exps[1] + exps[2] + exps[3]

    ri = lax.broadcasted_iota(I32, (tt, tt), 0)
    ci = lax.broadcasted_iota(I32, (tt, tt), 1)
    strict = (ri > ci).astype(BF16)
    base = carry[...] + _dot(strict, onehot.astype(BF16))
    route = jnp.zeros((tt, LANES), I32)
    gates = jnp.zeros((tt, LANES), F32)
    for kk in range(TOP_K):
        rank = jnp.sum(jnp.where(lane == idxs[kk], base, 0.0), axis=-1, keepdims=True)
        route = jnp.where(lane == kk, idxs[kk], route)
        route = jnp.where(lane == kk + TOP_K, rank.astype(I32), route)
        gates = jnp.where(lane == kk, exps[kk] / denom, gates)
    route_ref[...] = route
    gate_ref[...] = gates
    new_carry = carry[...] + jnp.sum(onehot, axis=0, keepdims=True)
    carry[...] = new_carry
    counts_ref[...] = new_carry.astype(I32)


def _outproj_router(o, z, u, x, gnw, wa, wb, fnw, wr, br, *, tile):
    t, d = x.shape
    grid = (t // tile,)
    full = lambda a: pl.BlockSpec(a.shape, lambda i: (0,) * a.ndim)
    tok = lambda w: pl.BlockSpec((tile, w), lambda i: (i, 0))
    return pl.pallas_call(
        _outproj_router_kernel,
        grid=grid,
        in_specs=[tok(o.shape[1]), tok(z.shape[1]), tok(u.shape[1]), tok(d)]
                 + [full(a) for a in (gnw, wa, wb, fnw, wr, br)],
        out_specs=(tok(d), pl.BlockSpec((tile * (d // LANES), LANES), lambda i: (i, 0)),
                   tok(LANES), tok(LANES), pl.BlockSpec((1, LANES), lambda i: (0, 0))),
        out_shape=(jax.ShapeDtypeStruct((t, d), F32), jax.ShapeDtypeStruct((t * (d // LANES), LANES), F32),
                   jax.ShapeDtypeStruct((t, LANES), I32), jax.ShapeDtypeStruct((t, LANES), F32),
                   jax.ShapeDtypeStruct((1, LANES), I32)),
        scratch_shapes=[pltpu.VMEM((1, LANES), F32)],
        compiler_params=pltpu.CompilerParams(
            dimension_semantics=("arbitrary",), vmem_limit_bytes=VMEM_LIMIT),
        name="outproj_router",
    )(o, z, u, x, gnw, wa, wb, fnw, wr, br)


def _dest_kernel(route_ref, pstart_ref, dest_ref):
    route = route_ref[...].astype(F32)
    tt = route.shape[0]
    lane = lax.broadcasted_iota(I32, (tt, LANES), 1)
    pstart = pstart_ref[...].astype(F32)
    dest = jnp.zeros((tt, LANES), F32)
    for kk in range(TOP_K):
        idx = jnp.sum(jnp.where(lane == kk, route, 0.0), axis=-1, keepdims=True)
        rank = jnp.sum(jnp.where(lane == kk + TOP_K, route, 0.0), axis=-1, keepdims=True)
        start = jnp.sum(jnp.where(lane == idx.astype(I32), pstart, 0.0), axis=-1, keepdims=True)
        dest = jnp.where(lane == kk, start + rank, dest)
    dest_ref[...] = dest[:, :TOP_K].astype(I32)


def _dest(route, pstart, *, tile):
    t = route.shape[0]
    return pl.pallas_call(
        _dest_kernel,
        grid=(t // tile,),
        in_specs=[pl.BlockSpec((tile, LANES), lambda i: (i, 0)),
                  pl.BlockSpec((1, LANES), lambda i: (0, 0))],
        out_specs=pl.BlockSpec((tile, TOP_K), lambda i: (i, 0)),
        out_shape=jax.ShapeDtypeStruct((t, TOP_K), I32),
        compiler_params=pltpu.CompilerParams(dimension_semantics=("arbitrary",)),
        name="dest_rows",
    )(route, pstart)


ROW_WINDOW = 32


def _sc_mesh():
    return plsc.VectorSubcoreMesh(core_axis_name="core", subcore_axis_name="subcore")


def _dispatch_rows(xn_tiles, dest_win, n_rows):
    t = xn_tiles.shape[0]

    @functools.partial(
        pl.kernel, mesh=_sc_mesh(), scratch_types=[],
        out_type=jax.ShapeDtypeStruct((n_rows,) + xn_tiles.shape[1:], xn_tiles.dtype))
    def dispatch(x_hbm, idx_hbm, o_hbm):
        def body(x_vmem, idx_vmem):
            for kk in range(TOP_K):
                pltpu.sync_copy(x_vmem, o_hbm.at[idx_vmem.at[0, pl.ds(kk * ROW_WINDOW, ROW_WINDOW)]])

        pltpu.emit_pipeline(
            body,
            grid=(t // ROW_WINDOW,),
            in_specs=[pl.BlockSpec((ROW_WINDOW,) + xn_tiles.shape[1:], lambda i: (i, 0, 0)),
                      pl.BlockSpec((1, TOP_K * ROW_WINDOW), lambda i: (i, 0))],
            out_specs=[],
            core_axis_name=("core", "subcore"),
            dimension_semantics=(pltpu.PARALLEL,),
        )(x_hbm, idx_hbm)

    return dispatch(xn_tiles, dest_win)


def _collect_rows(yb_tiles, src_win):
    n_pairs = src_win.shape[0] * ROW_WINDOW

    @functools.partial(
        pl.kernel, mesh=_sc_mesh(), scratch_types=[],
        out_type=jax.ShapeDtypeStruct((n_pairs,) + yb_tiles.shape[1:], yb_tiles.dtype))
    def collect(y_hbm, idx_hbm, o_hbm):
        def body(idx_vmem, o_vmem):
            pltpu.sync_copy(y_hbm.at[idx_vmem.at[0, pl.ds(0, ROW_WINDOW)]], o_vmem)

        pltpu.emit_pipeline(
            body,
            grid=(n_pairs // ROW_WINDOW,),
            in_specs=[pl.BlockSpec((1, LANES), lambda i: (i, 0))],
            out_specs=[pl.BlockSpec((ROW_WINDOW,) + yb_tiles.shape[1:], lambda i: (i, 0, 0))],
            core_axis_name=("core", "subcore"),
            dimension_semantics=(pltpu.PARALLEL,),
        )(idx_hbm, o_hbm)

    return collect(yb_tiles, src_win)


def _expert_kernel(blk_exp_ref, n_used_ref, xb_ref, wgu_ref, bgu_ref, wd_ref, bd_ref, yb_ref, wgu_b, wd_b):
    i = pl.program_id(0)
    n_used = n_used_ref[0]
    bm = xb_ref.shape[0] // SUBLANES
    dff = wd_ref.shape[1]

    prev = blk_exp_ref[jnp.maximum(i - 1, 0)]
    changed = (i == 0) | (blk_exp_ref[i] != prev)

    @pl.when(changed & (i < n_used))
    def _():
        wgu_b[...] = wgu_ref[0].astype(BF16)
        wd_b[...] = wd_ref[0].astype(BF16)

    @pl.when(i < n_used)
    def _():
        xb = _load_token_tiles(xb_ref, bm).astype(BF16)
        hid = _dot(xb, wgu_b[...]) + bgu_ref[0]
        gate = jnp.minimum(hid[:, :dff], SWIGLU_LIMIT)
        up = jnp.clip(hid[:, dff:], -SWIGLU_LIMIT, SWIGLU_LIMIT)
        glu = gate * jax.nn.sigmoid(SWIGLU_ALPHA * gate)
        act = ((up + 1.0) * glu).astype(BF16)
        _store_token_tiles(yb_ref, _dot(act, wd_b[...]) + bd_ref[0])


def _experts(blk_exp, n_used, xb_tiles, wgu, bgu, wd, bd):
    d = wgu.shape[1]
    chunks = d // LANES
    n_blocks = xb_tiles.shape[0] // (EXPERT_BLOCK * chunks)
    two_f = wgu.shape[2]
    dff = wd.shape[1]
    blk = lambda i, be, nu: (jnp.minimum(i, nu[0] - 1), 0)
    exp3 = lambda i, be, nu: (be[jnp.minimum(i, nu[0] - 1)], 0, 0)
    grid_spec = pltpu.PrefetchScalarGridSpec(
        num_scalar_prefetch=2,
        grid=(n_blocks,),
        in_specs=[pl.BlockSpec((EXPERT_BLOCK * chunks, LANES), blk),
                  pl.BlockSpec((1, d, two_f), exp3),
                  pl.BlockSpec((1, 1, two_f), exp3),
                  pl.BlockSpec((1, dff, d), exp3),
                  pl.BlockSpec((1, 1, d), exp3)],
        out_specs=pl.BlockSpec((EXPERT_BLOCK * chunks, LANES), blk),
        scratch_shapes=[pltpu.VMEM((d, two_f), BF16), pltpu.VMEM((dff, d), BF16)])
    return pl.pallas_call(
        _expert_kernel,
        grid_spec=grid_spec,
        out_shape=jax.ShapeDtypeStruct(xb_tiles.shape, F32),
        compiler_params=pltpu.CompilerParams(
            dimension_semantics=("arbitrary",), vmem_limit_bytes=VMEM_LIMIT),
        name="experts",
    )(blk_exp, n_used, xb_tiles, wgu, bgu, wd, bd)


def _combine_kernel(y0_ref, y1_ref, y2_ref, y3_ref, gate_ref, x2_ref, fw_ref, out_ref):
    gates = gate_ref[...]
    lane = lax.broadcasted_iota(I32, gates.shape, 1)
    x3 = x2_ref[...]
    for kk, y_ref in enumerate((y0_ref, y1_ref, y2_ref, y3_ref)):
        gk = jnp.sum(jnp.where(lane == kk, gates, 0.0), axis=-1, keepdims=True)
        x3 = x3 + gk * _load_token_tiles(y_ref, x3.shape[0])
    out_ref[...] = x3 * lax.rsqrt(jnp.mean(x3 * x3, axis=-1, keepdims=True) + NORM_EPS) * fw_ref[...]


def _combine(y4, gates, x2, fw, *, tile):
    t, d = x2.shape
    steps = t // tile
    choice = lambda kk: pl.BlockSpec((tile * (d // LANES), LANES), lambda i: (kk * steps + i, 0))
    return pl.pallas_call(
        _combine_kernel,
        grid=(steps,),
        in_specs=[choice(kk) for kk in range(TOP_K)]
                 + [pl.BlockSpec((tile, LANES), lambda i: (i, 0)),
                    pl.BlockSpec((tile, d), lambda i: (i, 0)),
                    pl.BlockSpec((1, d), lambda i: (0, 0))],
        out_specs=pl.BlockSpec((tile, d), lambda i: (i, 0)),
        out_shape=jax.ShapeDtypeStruct((t, d), F32),
        compiler_params=pltpu.CompilerParams(
            dimension_semantics=("arbitrary",), vmem_limit_bytes=VMEM_LIMIT),
        name="combine",
    )(y4, y4, y4, y4, gates, x2, fw)


def _pad_lanes(a, offset=0, fill=0.0):
    out = jnp.full((1, LANES), fill, a.dtype)
    return out.at[0, offset:offset + a.shape[0]].set(a)


def _layer(x, attn_norm_w, w_in, gdn_conv_w, gdn_a_log, gdn_dt_bias, gdn_norm_w,
           cf_dw_w, cf_dw_b, cf_ln_w, cf_ln_b, w_out, ffn_norm_w, w_router, b_router,
           w_gate_up, b_gate_up, w_down, b_down, final_norm_w, apply_final):
    b, s, d = x.shape
    t = b * s
    assert d == SUBLANES * LANES, "the token-tile layout needs one (8, 128) tile per token row"
    qk = GDN_HEADS * HEAD_DIM
    cfc = cf_dw_w.shape[1]
    off_b = 4 * qk
    off_cf = off_b + 2 * GDN_HEADS

    wm = w_in[:, :off_b].astype(BF16)
    wba = jnp.zeros((d, LANES), F32).at[:, :2 * GDN_HEADS].set(w_in[:, off_b:off_cf]).astype(BF16)
    wcf = w_in[:, off_cf:].astype(BF16)
    alog = _pad_lanes(gdn_a_log, GDN_HEADS)
    dtb = _pad_lanes(gdn_dt_bias, GDN_HEADS)

    q, k, v, z, bg, bgt, u = _inproj(
        x, attn_norm_w[None, :], wm, wba, wcf, gdn_conv_w, alog, dtb,
        cf_dw_w, cf_dw_b[None, :], cf_ln_w[None, :], cf_ln_b[None, :], tile=min(512, s))
    o = _gdn(q, k, v, bg, bgt, tile=min(256, s))

    wr = jnp.zeros((d, LANES), F32).at[:, :N_EXPERTS].set(w_router)
    br = _pad_lanes(b_router)
    x2, xn, route, gates, counts = _outproj_router(
        o.reshape(t, qk), z.reshape(t, qk), u.reshape(t, cfc), x.reshape(t, d),
        gdn_norm_w[None, :], w_out[:qk].astype(BF16), w_out[qk:].astype(BF16),
        ffn_norm_w[None, :], wr, br, tile=min(512, t))

    cnt = counts[0, :N_EXPERTS]
    nblk = (cnt + EXPERT_BLOCK - 1) // EXPERT_BLOCK
    blk_end = jnp.cumsum(nblk)
    pstart = (blk_end - nblk) * EXPERT_BLOCK
    n_blocks = (t * TOP_K) // EXPERT_BLOCK + N_EXPERTS
    blk_ids = jnp.arange(n_blocks, dtype=I32)
    blk_exp = jnp.minimum(
        jnp.sum((blk_end[None, :] <= blk_ids[:, None]).astype(I32), axis=1), N_EXPERTS - 1)
    n_used = blk_end[-1:].astype(I32)

    dest = _dest(route, _pad_lanes(pstart.astype(I32)), tile=min(2048, t))
    n_rows = n_blocks * EXPERT_BLOCK
    chunks = d // LANES
    windows = t // ROW_WINDOW
    dest_win = dest.reshape(windows, ROW_WINDOW, TOP_K).transpose(0, 2, 1).reshape(windows, TOP_K * ROW_WINDOW)
    src_win = jnp.pad(dest.T.reshape(TOP_K * windows, ROW_WINDOW), ((0, 0), (0, LANES - ROW_WINDOW)))
    xb = _dispatch_rows(xn.reshape(t, chunks, LANES), dest_win, n_rows)
    yb = _experts(blk_exp, n_used, xb.reshape(n_rows * chunks, LANES), w_gate_up, b_gate_up[:, None, :],
                  w_down, b_down[:, None, :])
    y4 = _collect_rows(yb.reshape(n_rows, chunks, LANES), src_win).reshape(TOP_K * t * chunks, LANES)
    out = _combine(y4, gates, x2, final_norm_w[None, :], tile=min(EXPERT_BLOCK, t))
    return out.reshape(b, s, d)


def kernel(x, attn_norm_w, w_in, gdn_conv_w, gdn_a_log, gdn_dt_bias, gdn_norm_w, cf_dw_w, cf_dw_b,
           cf_ln_w, cf_ln_b, w_out, ffn_norm_w, w_router, b_router, w_gate_up, b_gate_up, w_down,
           b_down, final_norm_w):
    depth = w_in.shape[0]
    assert depth == 1, "the fused final norm assumes a single trunk layer"
    return _layer(x, attn_norm_w[0], w_in[0], gdn_conv_w[0], gdn_a_log[0], gdn_dt_bias[0],
                  gdn_norm_w[0], cf_dw_w[0], cf_dw_b[0], cf_ln_w[0], cf_ln_b[0], w_out[0],
                  ffn_norm_w[0], w_router[0], b_router[0], w_gate_up[0], b_gate_up[0], w_down[0],
                  b_down[0], final_norm_w, True)
```

```python
import functools

import jax
import jax.numpy as jnp
from jax import lax
from jax.experimental import pallas as pl
from jax.experimental.pallas import tpu as pltpu
from jax.experimental.pallas import tpu_sc as plsc

F32 = jnp.float32
BF16 = jnp.bfloat16
I32 = jnp.int32

NORM_EPS = 1e-6
LANES = 128
SUBLANES = 8
GDN_HEADS = 4
HEAD_DIM = 128
GDN_CHUNK = 64
GDN_CONV = 4
CF_KERNEL = 31
N_EXPERTS = 32
TOP_K = 4
SWIGLU_LIMIT = 7.0
SWIGLU_ALPHA = 1.702

QKV_HALO = 8
CF_HALO = 32
EXPERT_BLOCK = 256
VMEM_LIMIT = 56 * 1024 * 1024


def _silu(x):
    return x * jax.nn.sigmoid(x)


def _dot(a, b):
    return jnp.dot(a, b, preferred_element_type=F32)


def _dot_nt(a, b):
    return lax.dot_general(a, b, (((1,), (1,)), ((), ())), preferred_element_type=F32)


def _dot_tn(a, b):
    return lax.dot_general(a, b, (((0,), (0,)), ((), ())), preferred_element_type=F32)


def _store_token_tiles(ref, x, lead=()):
    rows, d = x.shape
    chunks = d // LANES
    for c in range(chunks):
        ref[lead + (pl.ds(c, rows, stride=chunks), slice(None))] = x[:, c * LANES:(c + 1) * LANES]


def _load_token_tiles(ref, rows, lead=()):
    chunks = ref.shape[-2] // rows
    return jnp.concatenate(
        [ref[lead + (pl.ds(c, rows, stride=chunks), slice(None))] for c in range(chunks)], axis=1)


def _inproj_kernel(x_ref, nw_ref, wm_ref, wba_ref, wcf_ref, cw_ref, alog_ref, dtb_ref,
                   dww_ref, dwb_ref, lnw_ref, lnb_ref,
                   q_ref, k_ref, v_ref, z_ref, bg_ref, bgt_ref, u_ref,
                   qkv_buf, cf_buf, cf_shift):
    tt = x_ref.shape[1]
    qk = GDN_HEADS * HEAD_DIM
    cfc = u_ref.shape[2]

    @pl.when(pl.program_id(1) == 0)
    def _():
        qkv_buf[0:QKV_HALO, :] = jnp.zeros((QKV_HALO, qkv_buf.shape[1]), F32)
        cf_buf[0:CF_HALO, :] = jnp.zeros((CF_HALO, cf_buf.shape[1]), F32)

    x = x_ref[0]
    h = x * lax.rsqrt(jnp.mean(x * x, axis=-1, keepdims=True) + NORM_EPS) * nw_ref[...]
    h = h.astype(BF16)
    pm = _dot(h, wm_ref[...])
    pba = _dot(h, wba_ref[...])
    pcf = _dot(h, wcf_ref[...])

    z_ref[0] = pm[:, 3 * qk:]

    qkv_buf[QKV_HALO:QKV_HALO + tt, :] = pm[:, :3 * qk]
    acc = None
    for j in range(GDN_CONV):
        term = cw_ref[j:j + 1, :] * qkv_buf[pl.ds(QKV_HALO - (GDN_CONV - 1) + j, tt), :]
        acc = term if acc is None else acc + term
    qkv_buf[0:QKV_HALO, :] = qkv_buf[tt:tt + QKV_HALO, :]
    qkv = _silu(acc)
    for hd in range(GDN_HEADS):
        for base, ref in ((0, q_ref), (qk, k_ref)):
            t = qkv[:, base + hd * HEAD_DIM: base + (hd + 1) * HEAD_DIM]
            t = t * lax.rsqrt(jnp.sum(t * t, axis=-1, keepdims=True) + NORM_EPS)
            ref[0, :, hd * HEAD_DIM:(hd + 1) * HEAD_DIM] = t
    v_ref[0] = qkv[:, 2 * qk:]

    lane = lax.broadcasted_iota(I32, (tt, LANES), 1)
    row = lax.broadcasted_iota(I32, (tt, LANES), 0)
    beta = jax.nn.sigmoid(pba)
    sp_in = pba + dtb_ref[...]
    softplus = jnp.maximum(sp_in, 0.0) + jnp.log(1.0 + jnp.exp(-jnp.abs(sp_in)))
    g = -jnp.exp(alog_ref[...]) * softplus
    g = jnp.where((lane >= GDN_HEADS) & (lane < 2 * GDN_HEADS), g, 0.0)
    pos = row % GDN_CHUNK
    shift = 1
    while shift < GDN_CHUNK:
        g = g + jnp.where(pos >= shift, pltpu.roll(g, shift, 0), 0.0)
        shift *= 2
    bg = jnp.where(lane < GDN_HEADS, beta, g)
    bg_ref[0] = bg
    bgt_ref[0] = jnp.transpose(bg)[0:SUBLANES, :]

    glu = pcf[:, :cfc] * jax.nn.sigmoid(pcf[:, cfc:])
    cf_buf[CF_HALO:CF_HALO + tt, :] = glu
    lo = SUBLANES
    span = tt + CF_HALO - lo
    for r in range(1, SUBLANES):
        cf_shift[r - 1, lo:lo + span, :] = cf_buf[pl.ds(lo - r, span), :]
    rows = 64
    for r0 in range(0, tt, rows):
        acc = None
        for j in range(CF_KERNEL):
            a, r = divmod(CF_KERNEL - 1 - j, SUBLANES)
            start = CF_HALO + r0 - a * SUBLANES
            src = cf_buf[start:start + rows, :] if r == 0 else cf_shift[r - 1, start:start + rows, :]
            term = dww_ref[j:j + 1, :] * src
            acc = term if acc is None else acc + term
        c = acc + dwb_ref[...]
        mu = jnp.mean(c, axis=-1, keepdims=True)
        cc = c - mu
        y = cc * lax.rsqrt(jnp.mean(cc * cc, axis=-1, keepdims=True) + NORM_EPS)
        u_ref[0, r0:r0 + rows, :] = _silu(y * lnw_ref[...] + lnb_ref[...])
    cf_buf[0:CF_HALO, :] = cf_buf[tt:tt + CF_HALO, :]


def _inproj(x, nw, wm, wba, wcf, cw, alog, dtb, dww, dwb, lnw, lnb, *, tile):
    b, s, d = x.shape
    qk = GDN_HEADS * HEAD_DIM
    cfc = dww.shape[1]
    grid = (b, s // tile)
    full = lambda a: pl.BlockSpec(a.shape, lambda i, j: (0,) * a.ndim)
    tok = lambda w: pl.BlockSpec((1, tile, w), lambda i, j: (i, j, 0))
    out_shape = (
        jax.ShapeDtypeStruct((b, s, qk), F32), jax.ShapeDtypeStruct((b, s, qk), F32),
        jax.ShapeDtypeStruct((b, s, qk), F32), jax.ShapeDtypeStruct((b, s, qk), F32),
        jax.ShapeDtypeStruct((b, s, LANES), F32), jax.ShapeDtypeStruct((b, SUBLANES, s), F32),
        jax.ShapeDtypeStruct((b, s, cfc), F32))
    return pl.pallas_call(
        _inproj_kernel,
        grid=grid,
        in_specs=[tok(d)] + [full(a) for a in (nw, wm, wba, wcf, cw, alog, dtb, dww, dwb, lnw, lnb)],
        out_specs=(tok(qk), tok(qk), tok(qk), tok(qk), tok(LANES),
                   pl.BlockSpec((1, SUBLANES, tile), lambda i, j: (i, 0, j)), tok(cfc)),
        out_shape=out_shape,
        scratch_shapes=[pltpu.VMEM((QKV_HALO + tile, 3 * qk), F32),
                        pltpu.VMEM((CF_HALO + tile, cfc), F32),
                        pltpu.VMEM((SUBLANES - 1, CF_HALO + tile, cfc), F32)],
        compiler_params=pltpu.CompilerParams(
            dimension_semantics=("arbitrary", "arbitrary"), vmem_limit_bytes=VMEM_LIMIT),
        name="inproj",
    )(x, nw, wm, wba, wcf, cw, alog, dtb, dww, dwb, lnw, lnb)


def _bmm(a, b):
    return jnp.einsum("bmk,bkn->bmn", a, b, preferred_element_type=F32)


def _bmm_nt(a, b):
    return jnp.einsum("bmk,bnk->bmn", a, b, preferred_element_type=F32)


def _bmm_tn(a, b):
    return jnp.einsum("bkm,bkn->bmn", a, b, preferred_element_type=F32)


def _unit_lower_inverse(a):
    c = a.shape[-1]
    ii = lax.broadcasted_iota(I32, (c, c), 0)
    jj = lax.broadcasted_iota(I32, (c, c), 1)
    eye = (ii == jj).astype(F32)
    same16 = (ii // 16) == (jj // 16)
    same32 = (ii // 32) == (jj // 32)
    x = jnp.where(same16, -a, 0.0)
    t = eye + x
    xp = x
    for _ in range(3):
        xp_b = xp.astype(BF16)
        xp = _bmm(xp_b, xp_b)
        t = t + _bmm(t.astype(BF16), xp.astype(BF16))
    for off in (jnp.where(same32 & ~same16, a, 0.0), jnp.where(~same32, a, 0.0)):
        tb = t.astype(BF16)
        t = t - _bmm(tb, _bmm(off.astype(BF16), tb).astype(BF16))
    return t


def _gdn_kernel(q_ref, k_ref, v_ref, bg_ref, bgt_ref, o_ref, state, s_all):
    lt = q_ref.shape[1]
    c = GDN_CHUNK
    nh = GDN_HEADS
    nc = lt // c

    @pl.when(pl.program_id(1) == 0)
    def _():
        state[...] = jnp.zeros(state.shape, F32)

    def stack(fn):
        return jnp.stack([fn(slice(n * c, (n + 1) * c), h) for n in range(nc) for h in range(nh)])

    head = lambda h: slice(h * HEAD_DIM, (h + 1) * HEAD_DIM)
    q = stack(lambda r, h: q_ref[0, r, head(h)]) * (HEAD_DIM ** -0.5)
    k = stack(lambda r, h: k_ref[0, r, head(h)])
    v = stack(lambda r, h: v_ref[0, r, head(h)])
    beta = stack(lambda r, h: bg_ref[0, r, h:h + 1])
    gcol = stack(lambda r, h: bg_ref[0, r, nh + h:nh + h + 1])
    grow = stack(lambda r, h: bgt_ref[0, nh + h:nh + h + 1, r])

    ii = lax.broadcasted_iota(I32, (c, c), 0)
    jj = lax.broadcasted_iota(I32, (c, c), 1)
    glast = gcol[:, c - 1:c, :]
    eg = jnp.exp(gcol)
    decay = jnp.where(ii >= jj, jnp.exp(jnp.minimum(gcol - grow, 0.0)), 0.0)
    kb = k * beta
    k_b = k.astype(BF16)
    a = jnp.where(ii > jj, _bmm_nt(kb.astype(BF16), k_b) * decay, 0.0)
    t = _unit_lower_inverse(a)
    rhs = jnp.concatenate([v * beta, kb * eg], axis=-1).astype(BF16)
    sol = _bmm(t.astype(BF16), rhs)
    u_val = sol[..., :HEAD_DIM]
    w_key = sol[..., HEAD_DIM:]
    intra = _bmm_nt(q.astype(BF16), k_b) * decay
    k_tail = (k * jnp.exp(glast - gcol)).astype(BF16)
    upd = _bmm_tn(k_tail, sol.astype(BF16))
    b_mat = upd[..., :HEAD_DIM]
    p_mat = upd[..., HEAD_DIM:].astype(BF16)
    g_tot = jnp.exp(glast)

    s = state[...]
    for n in range(nc):
        grp = slice(n * nh, (n + 1) * nh)
        s_b = s.astype(BF16)
        s_all[grp] = s_b
        s = s * g_tot[grp] - _bmm(p_mat[grp], s_b) + b_mat[grp]
    state[...] = s

    wq = jnp.concatenate([w_key, q * eg], axis=1).astype(BF16)
    ws_qs = _bmm(wq, s_all[...])
    v_new = u_val - ws_qs[:, :c]
    o = ws_qs[:, c:] + _bmm(intra.astype(BF16), v_new.astype(BF16))
    for n in range(nc):
        for h in range(nh):
            o_ref[0, n * c:(n + 1) * c, head(h)] = o[n * nh + h]


def _gdn(q, k, v, bg, bgt, *, tile):
    b, s, qk = q.shape
    grid = (b, s // tile)
    tok = lambda w: pl.BlockSpec((1, tile, w), lambda i, j: (i, j, 0))
    n_prob = (tile // GDN_CHUNK) * GDN_HEADS
    return pl.pallas_call(
        _gdn_kernel,
        grid=grid,
        in_specs=[tok(qk), tok(qk), tok(qk), tok(LANES),
                  pl.BlockSpec((1, SUBLANES, tile), lambda i, j: (i, 0, j))],
        out_specs=tok(qk),
        out_shape=jax.ShapeDtypeStruct((b, s, qk), F32),
        scratch_shapes=[pltpu.VMEM((GDN_HEADS, HEAD_DIM, HEAD_DIM), F32),
                        pltpu.VMEM((n_prob, HEAD_DIM, HEAD_DIM), BF16)],
        compiler_params=pltpu.CompilerParams(
            dimension_semantics=("arbitrary", "arbitrary"), vmem_limit_bytes=VMEM_LIMIT),
        name="gdn",
    )(q, k, v, bg, bgt)


def _split_bf16(x):
    hi = x.astype(BF16)
    lo = (x - hi.astype(F32)).astype(BF16)
    return hi, lo


def _outproj_router_kernel(o_ref, z_ref, u_ref, x_ref, gnw_ref, wa_ref, wb_ref, fnw_ref,
                           wr_ref, br_ref,
                           x2_ref, xn_ref, route_ref, gate_ref, counts_ref, carry):
    tt = x_ref.shape[0]

    @pl.when(pl.program_id(0) == 0)
    def _():
        carry[...] = jnp.zeros(carry.shape, F32)

    parts = []
    for hd in range(GDN_HEADS):
        sl = slice(hd * HEAD_DIM, (hd + 1) * HEAD_DIM)
        oh = o_ref[:, sl]
        y = oh * lax.rsqrt(jnp.mean(oh * oh, axis=-1, keepdims=True) + NORM_EPS) * gnw_ref[...]
        parts.append((y * _silu(z_ref[:, sl])).astype(BF16))
    out_a = jnp.concatenate(parts, axis=-1)
    x2 = x_ref[...] + _dot(out_a, wa_ref[...]) + _dot(u_ref[...].astype(BF16), wb_ref[...])
    x2_ref[...] = x2

    xn = x2 * lax.rsqrt(jnp.mean(x2 * x2, axis=-1, keepdims=True) + NORM_EPS) * fnw_ref[...]
    _store_token_tiles(xn_ref, xn)

    xh, xl = _split_bf16(xn)
    wh, wl = _split_bf16(wr_ref[...])
    logits = _dot(xh, wh) + _dot(xh, wl) + _dot(xl, wh) + br_ref[...]

    lane = lax.broadcasted_iota(I32, (tt, LANES), 1)
    lane_f = lane.astype(F32)
    neg = jnp.float32(-jnp.inf)
    work = jnp.where(lane < N_EXPERTS, logits, neg)
    vals, idxs = [], []
    onehot = jnp.zeros((tt, LANES), F32)
    for _ in range(TOP_K):
        m = jnp.max(work, axis=-1, keepdims=True)
        idx = jnp.min(jnp.where(work == m, lane_f, float(LANES)), axis=-1, keepdims=True).astype(I32)
        sel = lane == idx
        vals.append(m)
        idxs.append(idx)
        onehot = onehot + sel.astype(F32)
        work = jnp.where(sel, neg, work)
    exps = [jnp.exp(v - vals[0]) for v in vals]
    denom = exps[0] + exps[1] + exps[2] + exps[3]

    ri = lax.broadcasted_iota(I32, (tt, tt), 0)
    ci = lax.broadcasted_iota(I32, (tt, tt), 1)
    strict = (ri > ci).astype(BF16)
    base = carry[...] + _dot(strict, onehot.astype(BF16))
    route = jnp.zeros((tt, LANES), I32)
    gates = jnp.zeros((tt, LANES), F32)
    for kk in range(TOP_K):
        rank = jnp.sum(jnp.where(lane == idxs[kk], base, 0.0), axis=-1, keepdims=True)
        route = jnp.where(lane == kk, idxs[kk], route)
        route = jnp.where(lane == kk + TOP_K, rank.astype(I32), route)
        gates = jnp.where(lane == kk, exps[kk] / denom, gates)
    route_ref[...] = route
    gate_ref[...] = gates
    new_carry = carry[...] + jnp.sum(onehot, axis=0, keepdims=True)
    carry[...] = new_carry
    counts_ref[...] = new_carry.astype(I32)


def _outproj_router(o, z, u, x, gnw, wa, wb, fnw, wr, br, *, tile):
    t, d = x.shape
    grid = (t // tile,)
    full = lambda a: pl.BlockSpec(a.shape, lambda i: (0,) * a.ndim)
    tok = lambda w: pl.BlockSpec((tile, w), lambda i: (i, 0))
    return pl.pallas_call(
        _outproj_router_kernel,
        grid=grid,
        in_specs=[tok(o.shape[1]), tok(z.shape[1]), tok(u.shape[1]), tok(d)]
                 + [full(a) for a in (gnw, wa, wb, fnw, wr, br)],
        out_specs=(tok(d), pl.BlockSpec((tile * (d // LANES), LANES), lambda i: (i, 0)),
                   tok(LANES), tok(LANES), pl.BlockSpec((1, LANES), lambda i: (0, 0))),
        out_shape=(jax.ShapeDtypeStruct((t, d), F32), jax.ShapeDtypeStruct((t * (d // LANES), LANES), F32),
                   jax.ShapeDtypeStruct((t, LANES), I32), jax.ShapeDtypeStruct((t, LANES), F32),
                   jax.ShapeDtypeStruct((1, LANES), I32)),
        scratch_shapes=[pltpu.VMEM((1, LANES), F32)],
        compiler_params=pltpu.CompilerParams(
            dimension_semantics=("arbitrary",), vmem_limit_bytes=VMEM_LIMIT),
        name="outproj_router",
    )(o, z, u, x, gnw, wa, wb, fnw, wr, br)


def _dest_kernel(route_ref, pstart_ref, dest_ref):
    route = route_ref[...].astype(F32)
    tt = route.shape[0]
    lane = lax.broadcasted_iota(I32, (tt, LANES), 1)
    pstart = pstart_ref[...].astype(F32)
    dest = jnp.zeros((tt, LANES), F32)
    for kk in range(TOP_K):
        idx = jnp.sum(jnp.where(lane == kk, route, 0.0), axis=-1, keepdims=True)
        rank = jnp.sum(jnp.where(lane == kk + TOP_K, route, 0.0), axis=-1, keepdims=True)
        start = jnp.sum(jnp.where(lane == idx.astype(I32), pstart, 0.0), axis=-1, keepdims=True)
        dest = jnp.where(lane == kk, start + rank, dest)
    dest_ref[...] = dest[:, :TOP_K].astype(I32)


def _dest(route, pstart, *, tile):
    t = route.shape[0]
    return pl.pallas_call(
        _dest_kernel,
        grid=(t // tile,),
        in_specs=[pl.BlockSpec((tile, LANES), lambda i: (i, 0)),
                  pl.BlockSpec((1, LANES), lambda i: (0, 0))],
        out_specs=pl.BlockSpec((tile, TOP_K), lambda i: (i, 0)),
        out_shape=jax.ShapeDtypeStruct((t, TOP_K), I32),
        compiler_params=pltpu.CompilerParams(dimension_semantics=("arbitrary",)),
        name="dest_rows",
    )(route, pstart)


ROW_WINDOW = 32
COMBINE_CHUNKS = 4


def _sc_mesh():
    return plsc.VectorSubcoreMesh(core_axis_name="core", subcore_axis_name="subcore")


def _dispatch_rows(xn_tiles, dest_win, n_rows):
    t = xn_tiles.shape[0]

    @functools.partial(
        pl.kernel, mesh=_sc_mesh(), scratch_types=[],
        out_type=jax.ShapeDtypeStruct((n_rows,) + xn_tiles.shape[1:], xn_tiles.dtype))
    def dispatch(x_hbm, idx_hbm, o_hbm):
        def body(x_vmem, idx_vmem):
            for kk in range(TOP_K):
                pltpu.sync_copy(x_vmem, o_hbm.at[idx_vmem.at[0, pl.ds(kk * ROW_WINDOW, ROW_WINDOW)]])

        pltpu.emit_pipeline(
            body,
            grid=(t // ROW_WINDOW,),
            in_specs=[pl.BlockSpec((ROW_WINDOW,) + xn_tiles.shape[1:], lambda i: (i, 0, 0)),
                      pl.BlockSpec((1, TOP_K * ROW_WINDOW), lambda i: (i, 0))],
            out_specs=[],
            core_axis_name=("core", "subcore"),
            dimension_semantics=(pltpu.PARALLEL,),
        )(x_hbm, idx_hbm)

    return dispatch(xn_tiles, dest_win)


def _collect_rows(yb_tiles, src_win):
    n_pairs = src_win.shape[0] * ROW_WINDOW

    @functools.partial(
        pl.kernel, mesh=_sc_mesh(), scratch_types=[],
        out_type=jax.ShapeDtypeStruct((n_pairs,) + yb_tiles.shape[1:], yb_tiles.dtype))
    def collect(y_hbm, idx_hbm, o_hbm):
        def body(idx_vmem, o_vmem):
            pltpu.sync_copy(y_hbm.at[idx_vmem.at[0, pl.ds(0, ROW_WINDOW)]], o_vmem)

        pltpu.emit_pipeline(
            body,
            grid=(n_pairs // ROW_WINDOW,),
            in_specs=[pl.BlockSpec((1, LANES), lambda i: (i, 0))],
            out_specs=[pl.BlockSpec((ROW_WINDOW,) + yb_tiles.shape[1:], lambda i: (i, 0, 0))],
            core_axis_name=("core", "subcore"),
            dimension_semantics=(pltpu.PARALLEL,),
        )(idx_hbm, o_hbm)

    return collect(yb_tiles, src_win)


def _expert_kernel(blk_exp_ref, n_used_ref, xb_ref, wgu_ref, bgu_ref, wd_ref, bd_ref, yb_ref, wgu_b, wd_b):
    i = pl.program_id(0)
    n_used = n_used_ref[0]
    bm = xb_ref.shape[0] // SUBLANES
    dff = wd_ref.shape[1]

    prev = blk_exp_ref[jnp.maximum(i - 1, 0)]
    changed = (i == 0) | (blk_exp_ref[i] != prev)

    @pl.when(changed & (i < n_used))
    def _():
        wgu_b[...] = wgu_ref[0].astype(BF16)
        wd_b[...] = wd_ref[0].astype(BF16)

    @pl.when(i < n_used)
    def _():
        xb = _load_token_tiles(xb_ref, bm).astype(BF16)
        hid = _dot(xb, wgu_b[...]) + bgu_ref[0]
        gate = jnp.minimum(hid[:, :dff], SWIGLU_LIMIT)
        up = jnp.clip(hid[:, dff:], -SWIGLU_LIMIT, SWIGLU_LIMIT)
        glu = gate * jax.nn.sigmoid(SWIGLU_ALPHA * gate)
        act = ((up + 1.0) * glu).astype(BF16)
        _store_token_tiles(yb_ref, _dot(act, wd_b[...]) + bd_ref[0])


def _experts(blk_exp, n_used, xb_tiles, wgu, bgu, wd, bd):
    d = wgu.shape[1]
    chunks = d // LANES
    n_blocks = xb_tiles.shape[0] // (EXPERT_BLOCK * chunks)
    two_f = wgu.shape[2]
    dff = wd.shape[1]
    blk = lambda i, be, nu: (jnp.minimum(i, nu[0] - 1), 0)
    exp3 = lambda i, be, nu: (be[jnp.minimum(i, nu[0] - 1)], 0, 0)
    grid_spec = pltpu.PrefetchScalarGridSpec(
        num_scalar_prefetch=2,
        grid=(n_blocks,),
        in_specs=[pl.BlockSpec((EXPERT_BLOCK * chunks, LANES), blk),
                  pl.BlockSpec((1, d, two_f), exp3),
                  pl.BlockSpec((1, 1, two_f), exp3),
                  pl.BlockSpec((1, dff, d), exp3),
                  pl.BlockSpec((1, 1, d), exp3)],
        out_specs=pl.BlockSpec((EXPERT_BLOCK * chunks, LANES), blk),
        scratch_shapes=[pltpu.VMEM((d, two_f), BF16), pltpu.VMEM((dff, d), BF16)])
    return pl.pallas_call(
        _expert_kernel,
        grid_spec=grid_spec,
        out_shape=jax.ShapeDtypeStruct(xb_tiles.shape, F32),
        compiler_params=pltpu.CompilerParams(
            dimension_semantics=("arbitrary",), vmem_limit_bytes=VMEM_LIMIT),
        name="experts",
    )(blk_exp, n_used, xb_tiles, wgu, bgu, wd, bd)


def _combine_kernel(y0_ref, y1_ref, y2_ref, y3_ref, gate_ref, x2_ref, fw_ref, out_ref):
    gates = gate_ref[...]
    lane = lax.broadcasted_iota(I32, gates.shape, 1)
    x3 = x2_ref[...]
    for kk, y_ref in enumerate((y0_ref, y1_ref, y2_ref, y3_ref)):
        gk = jnp.sum(jnp.where(lane == kk, gates, 0.0), axis=-1, keepdims=True)
        x3 = x3 + gk * _load_token_tiles(y_ref, x3.shape[0])
    out_ref[...] = x3 * lax.rsqrt(jnp.mean(x3 * x3, axis=-1, keepdims=True) + NORM_EPS) * fw_ref[...]


def _combine_into_kernel(y0_ref, y1_ref, y2_ref, y3_ref, gate_ref, x2_ref, fw_ref, prev_ref, out_ref):
    del prev_ref
    _combine_kernel(y0_ref, y1_ref, y2_ref, y3_ref, gate_ref, x2_ref, fw_ref, out_ref)


def _combine(y4, gates, x2, fw, out_prev, chunk, *, tile):
    t, d = x2.shape
    tc = y4.shape[0] // (TOP_K * (d // LANES))
    steps = tc // tile
    first = chunk * steps
    choice = lambda kk: pl.BlockSpec((tile * (d // LANES), LANES), lambda i: (kk * steps + i, 0))
    in_specs = ([choice(kk) for kk in range(TOP_K)]
                + [pl.BlockSpec((tile, LANES), lambda i: (first + i, 0)),
                   pl.BlockSpec((tile, d), lambda i: (first + i, 0)),
                   pl.BlockSpec((1, d), lambda i: (0, 0))])
    args = [y4, y4, y4, y4, gates, x2, fw]
    if out_prev is not None:
        in_specs.append(pl.BlockSpec(memory_space=pl.ANY))
        args.append(out_prev)
    return pl.pallas_call(
        _combine_kernel if out_prev is None else _combine_into_kernel,
        grid=(steps,),
        in_specs=in_specs,
        out_specs=pl.BlockSpec((tile, d), lambda i: (first + i, 0)),
        out_shape=jax.ShapeDtypeStruct((t, d), F32),
        input_output_aliases={} if out_prev is None else {len(args) - 1: 0},
        compiler_params=pltpu.CompilerParams(
            dimension_semantics=("arbitrary",), vmem_limit_bytes=VMEM_LIMIT),
        name="combine",
    )(*args)


def _pad_lanes(a, offset=0, fill=0.0):
    out = jnp.full((1, LANES), fill, a.dtype)
    return out.at[0, offset:offset + a.shape[0]].set(a)


def _layer(x, attn_norm_w, w_in, gdn_conv_w, gdn_a_log, gdn_dt_bias, gdn_norm_w,
           cf_dw_w, cf_dw_b, cf_ln_w, cf_ln_b, w_out, ffn_norm_w, w_router, b_router,
           w_gate_up, b_gate_up, w_down, b_down, final_norm_w, apply_final):
    b, s, d = x.shape
    t = b * s
    assert d == SUBLANES * LANES, "the token-tile layout needs one (8, 128) tile per token row"
    qk = GDN_HEADS * HEAD_DIM
    cfc = cf_dw_w.shape[1]
    off_b = 4 * qk
    off_cf = off_b + 2 * GDN_HEADS

    wm = w_in[:, :off_b].astype(BF16)
    wba = jnp.zeros((d, LANES), F32).at[:, :2 * GDN_HEADS].set(w_in[:, off_b:off_cf]).astype(BF16)
    wcf = w_in[:, off_cf:].astype(BF16)
    alog = _pad_lanes(gdn_a_log, GDN_HEADS)
    dtb = _pad_lanes(gdn_dt_bias, GDN_HEADS)

    q, k, v, z, bg, bgt, u = _inproj(
        x, attn_norm_w[None, :], wm, wba, wcf, gdn_conv_w, alog, dtb,
        cf_dw_w, cf_dw_b[None, :], cf_ln_w[None, :], cf_ln_b[None, :], tile=min(512, s))
    o = _gdn(q, k, v, bg, bgt, tile=min(256, s))

    wr = jnp.zeros((d, LANES), F32).at[:, :N_EXPERTS].set(w_router)
    br = _pad_lanes(b_router)
    x2, xn, route, gates, counts = _outproj_router(
        o.reshape(t, qk), z.reshape(t, qk), u.reshape(t, cfc), x.reshape(t, d),
        gdn_norm_w[None, :], w_out[:qk].astype(BF16), w_out[qk:].astype(BF16),
        ffn_norm_w[None, :], wr, br, tile=min(512, t))

    cnt = counts[0, :N_EXPERTS]
    nblk = (cnt + EXPERT_BLOCK - 1) // EXPERT_BLOCK
    blk_end = jnp.cumsum(nblk)
    pstart = (blk_end - nblk) * EXPERT_BLOCK
    n_blocks = (t * TOP_K) // EXPERT_BLOCK + N_EXPERTS
    blk_ids = jnp.arange(n_blocks, dtype=I32)
    blk_exp = jnp.minimum(
        jnp.sum((blk_end[None, :] <= blk_ids[:, None]).astype(I32), axis=1), N_EXPERTS - 1)
    n_used = blk_end[-1:].astype(I32)

    dest = _dest(route, _pad_lanes(pstart.astype(I32)), tile=min(2048, t))
    n_rows = n_blocks * EXPERT_BLOCK
    chunks = d // LANES
    windows = t // ROW_WINDOW
    dest_win = dest.reshape(windows, ROW_WINDOW, TOP_K).transpose(0, 2, 1).reshape(windows, TOP_K * ROW_WINDOW)
    xb = _dispatch_rows(xn.reshape(t, chunks, LANES), dest_win, n_rows)
    yb = _experts(blk_exp, n_used, xb.reshape(n_rows * chunks, LANES), w_gate_up, b_gate_up[:, None, :],
                  w_down, b_down[:, None, :]).reshape(n_rows, chunks, LANES)

    tc = t // COMBINE_CHUNKS
    out = None
    for c in range(COMBINE_CHUNKS):
        src = dest[c * tc:(c + 1) * tc].T.reshape(TOP_K * tc // ROW_WINDOW, ROW_WINDOW)
        src_win = jnp.pad(src, ((0, 0), (0, LANES - ROW_WINDOW)))
        y4 = _collect_rows(yb, src_win).reshape(TOP_K * tc * chunks, LANES)
        out = _combine(y4, gates, x2, final_norm_w[None, :], out, c, tile=min(EXPERT_BLOCK, tc))
    return out.reshape(b, s, d)


def kernel(x, attn_norm_w, w_in, gdn_conv_w, gdn_a_log, gdn_dt_bias, gdn_norm_w, cf_dw_w, cf_dw_b,
           cf_ln_w, cf_ln_b, w_out, ffn_norm_w, w_router, b_router, w_gate_up, b_gate_up, w_down,
           b_down, final_norm_w):
    depth = w_in.shape[0]
    assert depth == 1, "the fused final norm assumes a single trunk layer"
    return _layer(x, attn_norm_w[0], w_in[0], gdn_conv_w[0], gdn_a_log[0], gdn_dt_bias[0],
                  gdn_norm_w[0], cf_dw_w[0], cf_dw_b[0], cf_ln_w[0], cf_ln_b[0], w_out[0],
                  ffn_norm_w[0], w_router[0], b_router[0], w_gate_up[0], b_gate_up[0], w_down[0],
                  b_down[0], final_norm_w, True)
```

```python
import functools

import jax
import jax.numpy as jnp
from jax import lax
from jax.experimental import pallas as pl
from jax.experimental.pallas import tpu as pltpu
from jax.experimental.pallas import tpu_sc as plsc

F32 = jnp.float32
BF16 = jnp.bfloat16
I32 = jnp.int32

NORM_EPS = 1e-6
LANES = 128
SUBLANES = 8
GDN_HEADS = 4
HEAD_DIM = 128
GDN_CHUNK = 64
GDN_CONV = 4
CF_KERNEL = 31
N_EXPERTS = 32
TOP_K = 4
SWIGLU_LIMIT = 7.0
SWIGLU_ALPHA = 1.702

QKV_HALO = 8
CF_HALO = 32
EXPERT_BLOCK = 512
VMEM_LIMIT = 56 * 1024 * 1024


def _silu(x):
    return x * jax.nn.sigmoid(x)


def _dot(a, b):
    return jnp.dot(a, b, preferred_element_type=F32)


def _dot_nt(a, b):
    return lax.dot_general(a, b, (((1,), (1,)), ((), ())), preferred_element_type=F32)


def _dot_tn(a, b):
    return lax.dot_general(a, b, (((0,), (0,)), ((), ())), preferred_element_type=F32)


def _store_token_tiles(ref, x, lead=()):
    rows, d = x.shape
    chunks = d // LANES
    for c in range(chunks):
        ref[lead + (pl.ds(c, rows, stride=chunks), slice(None))] = x[:, c * LANES:(c + 1) * LANES]


def _load_token_tiles(ref, rows, lead=()):
    chunks = ref.shape[-2] // rows
    return jnp.concatenate(
        [ref[lead + (pl.ds(c, rows, stride=chunks), slice(None))] for c in range(chunks)], axis=1)


def _inproj_kernel(x_ref, nw_ref, wm_ref, wba_ref, wcf_ref, cw_ref, alog_ref, dtb_ref,
                   dww_ref, dwb_ref, lnw_ref, lnb_ref,
                   q_ref, k_ref, v_ref, z_ref, bg_ref, bgt_ref, u_ref,
                   qkv_buf, cf_buf, cf_shift):
    tt = x_ref.shape[1]
    qk = GDN_HEADS * HEAD_DIM
    cfc = u_ref.shape[2]

    @pl.when(pl.program_id(1) == 0)
    def _():
        qkv_buf[0:QKV_HALO, :] = jnp.zeros((QKV_HALO, qkv_buf.shape[1]), F32)
        cf_buf[0:CF_HALO, :] = jnp.zeros((CF_HALO, cf_buf.shape[1]), F32)

    x = x_ref[0]
    h = x * lax.rsqrt(jnp.mean(x * x, axis=-1, keepdims=True) + NORM_EPS) * nw_ref[...]
    h = h.astype(BF16)
    pm = _dot(h, wm_ref[...])
    pba = _dot(h, wba_ref[...])
    pcf = _dot(h, wcf_ref[...])

    z_ref[0] = pm[:, 3 * qk:]

    qkv_buf[QKV_HALO:QKV_HALO + tt, :] = pm[:, :3 * qk]
    acc = None
    for j in range(GDN_CONV):
        term = cw_ref[j:j + 1, :] * qkv_buf[pl.ds(QKV_HALO - (GDN_CONV - 1) + j, tt), :]
        acc = term if acc is None else acc + term
    qkv_buf[0:QKV_HALO, :] = qkv_buf[tt:tt + QKV_HALO, :]
    qkv = _silu(acc)
    for hd in range(GDN_HEADS):
        for base, ref in ((0, q_ref), (qk, k_ref)):
            t = qkv[:, base + hd * HEAD_DIM: base + (hd + 1) * HEAD_DIM]
            t = t * lax.rsqrt(jnp.sum(t * t, axis=-1, keepdims=True) + NORM_EPS)
            ref[0, :, hd * HEAD_DIM:(hd + 1) * HEAD_DIM] = t
    v_ref[0] = qkv[:, 2 * qk:]

    lane = lax.broadcasted_iota(I32, (tt, LANES), 1)
    row = lax.broadcasted_iota(I32, (tt, LANES), 0)
    beta = jax.nn.sigmoid(pba)
    sp_in = pba + dtb_ref[...]
    softplus = jnp.maximum(sp_in, 0.0) + jnp.log(1.0 + jnp.exp(-jnp.abs(sp_in)))
    g = -jnp.exp(alog_ref[...]) * softplus
    g = jnp.where((lane >= GDN_HEADS) & (lane < 2 * GDN_HEADS), g, 0.0)
    pos = row % GDN_CHUNK
    shift = 1
    while shift < GDN_CHUNK:
        g = g + jnp.where(pos >= shift, pltpu.roll(g, shift, 0), 0.0)
        shift *= 2
    bg = jnp.where(lane < GDN_HEADS, beta, g)
    bg_ref[0] = bg
    bgt_ref[0] = jnp.transpose(bg)[0:SUBLANES, :]

    glu = pcf[:, :cfc] * jax.nn.sigmoid(pcf[:, cfc:])
    cf_buf[CF_HALO:CF_HALO + tt, :] = glu
    lo = SUBLANES
    span = tt + CF_HALO - lo
    for r in range(1, SUBLANES):
        cf_shift[r - 1, lo:lo + span, :] = cf_buf[pl.ds(lo - r, span), :]
    rows = 64
    for r0 in range(0, tt, rows):
        acc = None
        for j in range(CF_KERNEL):
            a, r = divmod(CF_KERNEL - 1 - j, SUBLANES)
            start = CF_HALO + r0 - a * SUBLANES
            src = cf_buf[start:start + rows, :] if r == 0 else cf_shift[r - 1, start:start + rows, :]
            term = dww_ref[j:j + 1, :] * src
            acc = term if acc is None else acc + term
        c = acc + dwb_ref[...]
        mu = jnp.mean(c, axis=-1, keepdims=True)
        cc = c - mu
        y = cc * lax.rsqrt(jnp.mean(cc * cc, axis=-1, keepdims=True) + NORM_EPS)
        u_ref[0, r0:r0 + rows, :] = _silu(y * lnw_ref[...] + lnb_ref[...])
    cf_buf[0:CF_HALO, :] = cf_buf[tt:tt + CF_HALO, :]


def _inproj(x, nw, wm, wba, wcf, cw, alog, dtb, dww, dwb, lnw, lnb, *, tile):
    b, s, d = x.shape
    qk = GDN_HEADS * HEAD_DIM
    cfc = dww.shape[1]
    grid = (b, s // tile)
    full = lambda a: pl.BlockSpec(a.shape, lambda i, j: (0,) * a.ndim)
    tok = lambda w: pl.BlockSpec((1, tile, w), lambda i, j: (i, j, 0))
    out_shape = (
        jax.ShapeDtypeStruct((b, s, qk), F32), jax.ShapeDtypeStruct((b, s, qk), F32),
        jax.ShapeDtypeStruct((b, s, qk), F32), jax.ShapeDtypeStruct((b, s, qk), F32),
        jax.ShapeDtypeStruct((b, s, LANES), F32), jax.ShapeDtypeStruct((b, SUBLANES, s), F32),
        jax.ShapeDtypeStruct((b, s, cfc), F32))
    return pl.pallas_call(
        _inproj_kernel,
        grid=grid,
        in_specs=[tok(d)] + [full(a) for a in (nw, wm, wba, wcf, cw, alog, dtb, dww, dwb, lnw, lnb)],
        out_specs=(tok(qk), tok(qk), tok(qk), tok(qk), tok(LANES),
                   pl.BlockSpec((1, SUBLANES, tile), lambda i, j: (i, 0, j)), tok(cfc)),
        out_shape=out_shape,
        scratch_shapes=[pltpu.VMEM((QKV_HALO + tile, 3 * qk), F32),
                        pltpu.VMEM((CF_HALO + tile, cfc), F32),
                        pltpu.VMEM((SUBLANES - 1, CF_HALO + tile, cfc), F32)],
        compiler_params=pltpu.CompilerParams(
            dimension_semantics=("arbitrary", "arbitrary"), vmem_limit_bytes=VMEM_LIMIT),
        name="inproj",
    )(x, nw, wm, wba, wcf, cw, alog, dtb, dww, dwb, lnw, lnb)


def _bmm(a, b):
    return jnp.einsum("bmk,bkn->bmn", a, b, preferred_element_type=F32)


def _bmm_nt(a, b):
    return jnp.einsum("bmk,bnk->bmn", a, b, preferred_element_type=F32)


def _bmm_tn(a, b):
    return jnp.einsum("bkm,bkn->bmn", a, b, preferred_element_type=F32)


def _unit_lower_inverse(a):
    c = a.shape[-1]
    ii = lax.broadcasted_iota(I32, (c, c), 0)
    jj = lax.broadcasted_iota(I32, (c, c), 1)
    eye = (ii == jj).astype(F32)
    same16 = (ii // 16) == (jj // 16)
    same32 = (ii // 32) == (jj // 32)
    x = jnp.where(same16, -a, 0.0)
    t = eye + x
    xp = x
    for _ in range(3):
        xp_b = xp.astype(BF16)
        xp = _bmm(xp_b, xp_b)
        t = t + _bmm(t.astype(BF16), xp.astype(BF16))
    for off in (jnp.where(same32 & ~same16, a, 0.0), jnp.where(~same32, a, 0.0)):
        tb = t.astype(BF16)
        t = t - _bmm(tb, _bmm(off.astype(BF16), tb).astype(BF16))
    return t


def _gdn_kernel(q_ref, k_ref, v_ref, bg_ref, bgt_ref, o_ref, state, s_all):
    lt = q_ref.shape[1]
    c = GDN_CHUNK
    nh = GDN_HEADS
    nc = lt // c

    @pl.when(pl.program_id(1) == 0)
    def _():
        state[...] = jnp.zeros(state.shape, F32)

    def stack(fn):
        return jnp.stack([fn(slice(n * c, (n + 1) * c), h) for n in range(nc) for h in range(nh)])

    head = lambda h: slice(h * HEAD_DIM, (h + 1) * HEAD_DIM)
    q = stack(lambda r, h: q_ref[0, r, head(h)]) * (HEAD_DIM ** -0.5)
    k = stack(lambda r, h: k_ref[0, r, head(h)])
    v = stack(lambda r, h: v_ref[0, r, head(h)])
    beta = stack(lambda r, h: bg_ref[0, r, h:h + 1])
    gcol = stack(lambda r, h: bg_ref[0, r, nh + h:nh + h + 1])
    grow = stack(lambda r, h: bgt_ref[0, nh + h:nh + h + 1, r])

    ii = lax.broadcasted_iota(I32, (c, c), 0)
    jj = lax.broadcasted_iota(I32, (c, c), 1)
    glast = gcol[:, c - 1:c, :]
    eg = jnp.exp(gcol)
    decay = jnp.where(ii >= jj, jnp.exp(jnp.minimum(gcol - grow, 0.0)), 0.0)
    kb = k * beta
    k_b = k.astype(BF16)
    a = jnp.where(ii > jj, _bmm_nt(kb.astype(BF16), k_b) * decay, 0.0)
    t = _unit_lower_inverse(a)
    rhs = jnp.concatenate([v * beta, kb * eg], axis=-1).astype(BF16)
    sol = _bmm(t.astype(BF16), rhs)
    u_val = sol[..., :HEAD_DIM]
    w_key = sol[..., HEAD_DIM:]
    intra = _bmm_nt(q.astype(BF16), k_b) * decay
    k_tail = (k * jnp.exp(glast - gcol)).astype(BF16)
    upd = _bmm_tn(k_tail, sol.astype(BF16))
    b_mat = upd[..., :HEAD_DIM]
    p_mat = upd[..., HEAD_DIM:].astype(BF16)
    g_tot = jnp.exp(glast)

    s = state[...]
    for n in range(nc):
        grp = slice(n * nh, (n + 1) * nh)
        s_b = s.astype(BF16)
        s_all[grp] = s_b
        s = s * g_tot[grp] - _bmm(p_mat[grp], s_b) + b_mat[grp]
    state[...] = s

    wq = jnp.concatenate([w_key, q * eg], axis=1).astype(BF16)
    ws_qs = _bmm(wq, s_all[...])
    v_new = u_val - ws_qs[:, :c]
    o = ws_qs[:, c:] + _bmm(intra.astype(BF16), v_new.astype(BF16))
    for n in range(nc):
        for h in range(nh):
            o_ref[0, n * c:(n + 1) * c, head(h)] = o[n * nh + h]


def _gdn(q, k, v, bg, bgt, *, tile):
    b, s, qk = q.shape
    grid = (b, s // tile)
    tok = lambda w: pl.BlockSpec((1, tile, w), lambda i, j: (i, j, 0))
    n_prob = (tile // GDN_CHUNK) * GDN_HEADS
    return pl.pallas_call(
        _gdn_kernel,
        grid=grid,
        in_specs=[tok(qk), tok(qk), tok(qk), tok(LANES),
                  pl.BlockSpec((1, SUBLANES, tile), lambda i, j: (i, 0, j))],
        out_specs=tok(qk),
        out_shape=jax.ShapeDtypeStruct((b, s, qk), F32),
        scratch_shapes=[pltpu.VMEM((GDN_HEADS, HEAD_DIM, HEAD_DIM), F32),
                        pltpu.VMEM((n_prob, HEAD_DIM, HEAD_DIM), BF16)],
        compiler_params=pltpu.CompilerParams(
            dimension_semantics=("arbitrary", "arbitrary"), vmem_limit_bytes=VMEM_LIMIT),
        name="gdn",
    )(q, k, v, bg, bgt)


def _split_bf16(x):
    hi = x.astype(BF16)
    lo = (x - hi.astype(F32)).astype(BF16)
    return hi, lo


def _outproj_router_kernel(o_ref, z_ref, u_ref, x_ref, gnw_ref, wa_ref, wb_ref, fnw_ref,
                           wr_ref, br_ref,
                           x2_ref, xn_ref, route_ref, gate_ref, counts_ref, carry):
    tt = x_ref.shape[0]

    @pl.when(pl.program_id(0) == 0)
    def _():
        carry[...] = jnp.zeros(carry.shape, F32)

    parts = []
    for hd in range(GDN_HEADS):
        sl = slice(hd * HEAD_DIM, (hd + 1) * HEAD_DIM)
        oh = o_ref[:, sl]
        y = oh * lax.rsqrt(jnp.mean(oh * oh, axis=-1, keepdims=True) + NORM_EPS) * gnw_ref[...]
        parts.append((y * _silu(z_ref[:, sl])).astype(BF16))
    out_a = jnp.concatenate(parts, axis=-1)
    x2 = x_ref[...] + _dot(out_a, wa_ref[...]) + _dot(u_ref[...].astype(BF16), wb_ref[...])
    x2_ref[...] = x2

    xn = x2 * lax.rsqrt(jnp.mean(x2 * x2, axis=-1, keepdims=True) + NORM_EPS) * fnw_ref[...]
    _store_token_tiles(xn_ref, xn)

    xh, xl = _split_bf16(xn)
    wh, wl = _split_bf16(wr_ref[...])
    logits = _dot(xh, wh) + _dot(xh, wl) + _dot(xl, wh) + br_ref[...]

    lane = lax.broadcasted_iota(I32, (tt, LANES), 1)
    lane_f = lane.astype(F32)
    neg = jnp.float32(-jnp.inf)
    work = jnp.where(lane < N_EXPERTS, logits, neg)
    vals, idxs = [], []
    onehot = jnp.zeros((tt, LANES), F32)
    for _ in range(TOP_K):
        m = jnp.max(work, axis=-1, keepdims=True)
        idx = jnp.min(jnp.where(work == m, lane_f, float(LANES)), axis=-1, keepdims=True).astype(I32)
        sel = lane == idx
        vals.append(m)
        idxs.append(idx)
        onehot = onehot + sel.astype(F32)
        work = jnp.where(sel, neg, work)
    exps = [jnp.exp(v - vals[0]) for v in vals]
    denom = exps[0] + exps[1] + exps[2] + exps[3]

    ri = lax.broadcasted_iota(I32, (tt, tt), 0)
    ci = lax.broadcasted_iota(I32, (tt, tt), 1)
    strict = (ri > ci).astype(BF16)
    base = carry[...] + _dot(strict, onehot.astype(BF16))
    route = jnp.zeros((tt, LANES), I32)
    gates = jnp.zeros((tt, LANES), F32)
    for kk in range(TOP_K):
        rank = jnp.sum(jnp.where(lane == idxs[kk], base, 0.0), axis=-1, keepdims=True)
        route = jnp.where(lane == kk, idxs[kk], route)
        route = jnp.where(lane == kk + TOP_K, rank.astype(I32), route)
        gates = jnp.where(lane == kk, exps[kk] / denom, gates)
    route_ref[...] = route
    gate_ref[...] = gates
    new_carry = carry[...] + jnp.sum(onehot, axis=0, keepdims=True)
    carry[...] = new_carry
    counts_ref[...] = new_carry.astype(I32)


def _outproj_router(o, z, u, x, gnw, wa, wb, fnw, wr, br, *, tile):
    t, d = x.shape
    grid = (t // tile,)
    full = lambda a: pl.BlockSpec(a.shape, lambda i: (0,) * a.ndim)
    tok = lambda w: pl.BlockSpec((tile, w), lambda i: (i, 0))
    return pl.pallas_call(
        _outproj_router_kernel,
        grid=grid,
        in_specs=[tok(o.shape[1]), tok(z.shape[1]), tok(u.shape[1]), tok(d)]
                 + [full(a) for a in (gnw, wa, wb, fnw, wr, br)],
        out_specs=(tok(d), pl.BlockSpec((tile * (d // LANES), LANES), lambda i: (i, 0)),
                   tok(LANES), tok(LANES), pl.BlockSpec((1, LANES), lambda i: (0, 0))),
        out_shape=(jax.ShapeDtypeStruct((t, d), F32), jax.ShapeDtypeStruct((t * (d // LANES), LANES), F32),
                   jax.ShapeDtypeStruct((t, LANES), I32), jax.ShapeDtypeStruct((t, LANES), F32),
                   jax.ShapeDtypeStruct((1, LANES), I32)),
        scratch_shapes=[pltpu.VMEM((1, LANES), F32)],
        compiler_params=pltpu.CompilerParams(
            dimension_semantics=("arbitrary",), vmem_limit_bytes=VMEM_LIMIT),
        name="outproj_router",
    )(o, z, u, x, gnw, wa, wb, fnw, wr, br)


def _dest_kernel(route_ref, pstart_ref, dest_ref):
    route = route_ref[...].astype(F32)
    tt = route.shape[0]
    lane = lax.broadcasted_iota(I32, (tt, LANES), 1)
    pstart = pstart_ref[...].astype(F32)
    dest = jnp.zeros((tt, LANES), F32)
    for kk in range(TOP_K):
        idx = jnp.sum(jnp.where(lane == kk, route, 0.0), axis=-1, keepdims=True)
        rank = jnp.sum(jnp.where(lane == kk + TOP_K, route, 0.0), axis=-1, keepdims=True)
        start = jnp.sum(jnp.where(lane == idx.astype(I32), pstart, 0.0), axis=-1, keepdims=True)
        dest = jnp.where(lane == kk, start + rank, dest)
    dest_ref[...] = dest[:, :TOP_K].astype(I32)


def _dest(route, pstart, *, tile):
    t = route.shape[0]
    return pl.pallas_call(
        _dest_kernel,
        grid=(t // tile,),
        in_specs=[pl.BlockSpec((tile, LANES), lambda i: (i, 0)),
                  pl.BlockSpec((1, LANES), lambda i: (0, 0))],
        out_specs=pl.BlockSpec((tile, TOP_K), lambda i: (i, 0)),
        out_shape=jax.ShapeDtypeStruct((t, TOP_K), I32),
        compiler_params=pltpu.CompilerParams(dimension_semantics=("arbitrary",)),
        name="dest_rows",
    )(route, pstart)


ROW_WINDOW = 32
COMBINE_CHUNKS = 4


def _sc_mesh():
    return plsc.VectorSubcoreMesh(core_axis_name="core", subcore_axis_name="subcore")


def _dispatch_rows(xn_tiles, dest_win, n_rows):
    t = xn_tiles.shape[0]

    @functools.partial(
        pl.kernel, mesh=_sc_mesh(), scratch_types=[],
        out_type=jax.ShapeDtypeStruct((n_rows,) + xn_tiles.shape[1:], xn_tiles.dtype))
    def dispatch(x_hbm, idx_hbm, o_hbm):
        def body(x_vmem, idx_vmem):
            for kk in range(TOP_K):
                pltpu.sync_copy(x_vmem, o_hbm.at[idx_vmem.at[0, pl.ds(kk * ROW_WINDOW, ROW_WINDOW)]])

        pltpu.emit_pipeline(
            body,
            grid=(t // ROW_WINDOW,),
            in_specs=[pl.BlockSpec((ROW_WINDOW,) + xn_tiles.shape[1:], lambda i: (i, 0, 0)),
                      pl.BlockSpec((1, TOP_K * ROW_WINDOW), lambda i: (i, 0))],
            out_specs=[],
            core_axis_name=("core", "subcore"),
            dimension_semantics=(pltpu.PARALLEL,),
        )(x_hbm, idx_hbm)

    return dispatch(xn_tiles, dest_win)


def _collect_rows(yb_tiles, src_win):
    n_pairs = src_win.shape[0] * ROW_WINDOW

    @functools.partial(
        pl.kernel, mesh=_sc_mesh(), scratch_types=[],
        out_type=jax.ShapeDtypeStruct((n_pairs,) + yb_tiles.shape[1:], yb_tiles.dtype))
    def collect(y_hbm, idx_hbm, o_hbm):
        def body(idx_vmem, o_vmem):
            pltpu.sync_copy(y_hbm.at[idx_vmem.at[0, pl.ds(0, ROW_WINDOW)]], o_vmem)

        pltpu.emit_pipeline(
            body,
            grid=(n_pairs // ROW_WINDOW,),
            in_specs=[pl.BlockSpec((1, LANES), lambda i: (i, 0))],
            out_specs=[pl.BlockSpec((ROW_WINDOW,) + yb_tiles.shape[1:], lambda i: (i, 0, 0))],
            core_axis_name=("core", "subcore"),
            dimension_semantics=(pltpu.PARALLEL,),
        )(idx_hbm, o_hbm)

    return collect(yb_tiles, src_win)


def _expert_kernel(blk_exp_ref, n_used_ref, xb_ref, wgu_ref, bgu_ref, wd_ref, bd_ref, yb_ref, wgu_b, wd_b):
    i = pl.program_id(0)
    n_used = n_used_ref[0]
    bm = xb_ref.shape[0] // SUBLANES
    dff = wd_ref.shape[1]

    prev = blk_exp_ref[jnp.maximum(i - 1, 0)]
    changed = (i == 0) | (blk_exp_ref[i] != prev)

    @pl.when(changed & (i < n_used))
    def _():
        wgu_b[...] = wgu_ref[0].astype(BF16)
        wd_b[...] = wd_ref[0].astype(BF16)

    @pl.when(i < n_used)
    def _():
        xb = _load_token_tiles(xb_ref, bm).astype(BF16)
        hid = _dot(xb, wgu_b[...]) + bgu_ref[0]
        gate = jnp.minimum(hid[:, :dff], SWIGLU_LIMIT)
        up = jnp.clip(hid[:, dff:], -SWIGLU_LIMIT, SWIGLU_LIMIT)
        glu = gate * jax.nn.sigmoid(SWIGLU_ALPHA * gate)
        act = ((up + 1.0) * glu).astype(BF16)
        _store_token_tiles(yb_ref, _dot(act, wd_b[...]) + bd_ref[0])


def _experts(blk_exp, n_used, xb_tiles, wgu, bgu, wd, bd):
    d = wgu.shape[1]
    chunks = d // LANES
    n_blocks = xb_tiles.shape[0] // (EXPERT_BLOCK * chunks)
    two_f = wgu.shape[2]
    dff = wd.shape[1]
    blk = lambda i, be, nu: (jnp.minimum(i, nu[0] - 1), 0)
    exp3 = lambda i, be, nu: (be[jnp.minimum(i, nu[0] - 1)], 0, 0)
    grid_spec = pltpu.PrefetchScalarGridSpec(
        num_scalar_prefetch=2,
        grid=(n_blocks,),
        in_specs=[pl.BlockSpec((EXPERT_BLOCK * chunks, LANES), blk),
                  pl.BlockSpec((1, d, two_f), exp3),
                  pl.BlockSpec((1, 1, two_f), exp3),
                  pl.BlockSpec((1, dff, d), exp3),
                  pl.BlockSpec((1, 1, d), exp3)],
        out_specs=pl.BlockSpec((EXPERT_BLOCK * chunks, LANES), blk),
        scratch_shapes=[pltpu.VMEM((d, two_f), BF16), pltpu.VMEM((dff, d), BF16)])
    return pl.pallas_call(
        _expert_kernel,
        grid_spec=grid_spec,
        out_shape=jax.ShapeDtypeStruct(xb_tiles.shape, F32),
        compiler_params=pltpu.CompilerParams(
            dimension_semantics=("arbitrary",), vmem_limit_bytes=VMEM_LIMIT),
        name="experts",
    )(blk_exp, n_used, xb_tiles, wgu, bgu, wd, bd)


def _combine_kernel(y0_ref, y1_ref, y2_ref, y3_ref, gate_ref, x2_ref, fw_ref, out_ref):
    gates = gate_ref[...]
    lane = lax.broadcasted_iota(I32, gates.shape, 1)
    x3 = x2_ref[...]
    for kk, y_ref in enumerate((y0_ref, y1_ref, y2_ref, y3_ref)):
        gk = jnp.sum(jnp.where(lane == kk, gates, 0.0), axis=-1, keepdims=True)
        x3 = x3 + gk * _load_token_tiles(y_ref, x3.shape[0])
    out_ref[...] = x3 * lax.rsqrt(jnp.mean(x3 * x3, axis=-1, keepdims=True) + NORM_EPS) * fw_ref[...]


def _combine_into_kernel(y0_ref, y1_ref, y2_ref, y3_ref, gate_ref, x2_ref, fw_ref, prev_ref, out_ref):
    del prev_ref
    _combine_kernel(y0_ref, y1_ref, y2_ref, y3_ref, gate_ref, x2_ref, fw_ref, out_ref)


def _combine(y4, gates, x2, fw, out_prev, chunk, *, tile):
    t, d = x2.shape
    tc = y4.shape[0] // (TOP_K * (d // LANES))
    steps = tc // tile
    first = chunk * steps
    choice = lambda kk: pl.BlockSpec((tile * (d // LANES), LANES), lambda i: (kk * steps + i, 0))
    in_specs = ([choice(kk) for kk in range(TOP_K)]
                + [pl.BlockSpec((tile, LANES), lambda i: (first + i, 0)),
                   pl.BlockSpec((tile, d), lambda i: (first + i, 0)),
                   pl.BlockSpec((1, d), lambda i: (0, 0))])
    args = [y4, y4, y4, y4, gates, x2, fw]
    if out_prev is not None:
        in_specs.append(pl.BlockSpec(memory_space=pl.ANY))
        args.append(out_prev)
    return pl.pallas_call(
        _combine_kernel if out_prev is None else _combine_into_kernel,
        grid=(steps,),
        in_specs=in_specs,
        out_specs=pl.BlockSpec((tile, d), lambda i: (first + i, 0)),
        out_shape=jax.ShapeDtypeStruct((t, d), F32),
        input_output_aliases={} if out_prev is None else {len(args) - 1: 0},
        compiler_params=pltpu.CompilerParams(
            dimension_semantics=("arbitrary",), vmem_limit_bytes=VMEM_LIMIT),
        name="combine",
    )(*args)


def _pad_lanes(a, offset=0, fill=0.0):
    out = jnp.full((1, LANES), fill, a.dtype)
    return out.at[0, offset:offset + a.shape[0]].set(a)


def _layer(x, attn_norm_w, w_in, gdn_conv_w, gdn_a_log, gdn_dt_bias, gdn_norm_w,
           cf_dw_w, cf_dw_b, cf_ln_w, cf_ln_b, w_out, ffn_norm_w, w_router, b_router,
           w_gate_up, b_gate_up, w_down, b_down, final_norm_w, apply_final):
    b, s, d = x.shape
    t = b * s
    assert d == SUBLANES * LANES, "the token-tile layout needs one (8, 128) tile per token row"
    qk = GDN_HEADS * HEAD_DIM
    cfc = cf_dw_w.shape[1]
    off_b = 4 * qk
    off_cf = off_b + 2 * GDN_HEADS

    wm = w_in[:, :off_b].astype(BF16)
    wba = jnp.zeros((d, LANES), F32).at[:, :2 * GDN_HEADS].set(w_in[:, off_b:off_cf]).astype(BF16)
    wcf = w_in[:, off_cf:].astype(BF16)
    alog = _pad_lanes(gdn_a_log, GDN_HEADS)
    dtb = _pad_lanes(gdn_dt_bias, GDN_HEADS)

    q, k, v, z, bg, bgt, u = _inproj(
        x, attn_norm_w[None, :], wm, wba, wcf, gdn_conv_w, alog, dtb,
        cf_dw_w, cf_dw_b[None, :], cf_ln_w[None, :], cf_ln_b[None, :], tile=min(512, s))
    o = _gdn(q, k, v, bg, bgt, tile=min(256, s))

    wr = jnp.zeros((d, LANES), F32).at[:, :N_EXPERTS].set(w_router)
    br = _pad_lanes(b_router)
    x2, xn, route, gates, counts = _outproj_router(
        o.reshape(t, qk), z.reshape(t, qk), u.reshape(t, cfc), x.reshape(t, d),
        gdn_norm_w[None, :], w_out[:qk].astype(BF16), w_out[qk:].astype(BF16),
        ffn_norm_w[None, :], wr, br, tile=min(512, t))

    cnt = counts[0, :N_EXPERTS]
    nblk = (cnt + EXPERT_BLOCK - 1) // EXPERT_BLOCK
    blk_end = jnp.cumsum(nblk)
    pstart = (blk_end - nblk) * EXPERT_BLOCK
    n_blocks = (t * TOP_K) // EXPERT_BLOCK + N_EXPERTS
    blk_ids = jnp.arange(n_blocks, dtype=I32)
    blk_exp = jnp.minimum(
        jnp.sum((blk_end[None, :] <= blk_ids[:, None]).astype(I32), axis=1), N_EXPERTS - 1)
    n_used = blk_end[-1:].astype(I32)

    dest = _dest(route, _pad_lanes(pstart.astype(I32)), tile=min(2048, t))
    n_rows = n_blocks * EXPERT_BLOCK
    chunks = d // LANES
    windows = t // ROW_WINDOW
    dest_win = dest.reshape(windows, ROW_WINDOW, TOP_K).transpose(0, 2, 1).reshape(windows, TOP_K * ROW_WINDOW)
    xb = _dispatch_rows(xn.reshape(t, chunks, LANES), dest_win, n_rows)
    yb = _experts(blk_exp, n_used, xb.reshape(n_rows * chunks, LANES), w_gate_up, b_gate_up[:, None, :],
                  w_down, b_down[:, None, :]).reshape(n_rows, chunks, LANES)

    tc = t // COMBINE_CHUNKS
    out = None
    for c in range(COMBINE_CHUNKS):
        src = dest[c * tc:(c + 1) * tc].T.reshape(TOP_K * tc // ROW_WINDOW, ROW_WINDOW)
        src_win = jnp.pad(src, ((0, 0), (0, LANES - ROW_WINDOW)))
        y4 = _collect_rows(yb, src_win).reshape(TOP_K * tc * chunks, LANES)
        out = _combine(y4, gates, x2, final_norm_w[None, :], out, c, tile=min(EXPERT_BLOCK, tc))
    return out.reshape(b, s, d)


def kernel(x, attn_norm_w, w_in, gdn_conv_w, gdn_a_log, gdn_dt_bias, gdn_norm_w, cf_dw_w, cf_dw_b,
           cf_ln_w, cf_ln_b, w_out, ffn_norm_w, w_router, b_router, w_gate_up, b_gate_up, w_down,
           b_down, final_norm_w):
    depth = w_in.shape[0]
    assert depth == 1, "the fused final norm assumes a single trunk layer"
    return _layer(x, attn_norm_w[0], w_in[0], gdn_conv_w[0], gdn_a_log[0], gdn_dt_bias[0],
                  gdn_norm_w[0], cf_dw_w[0], cf_dw_b[0], cf_ln_w[0], cf_ln_b[0], w_out[0],
                  ffn_norm_w[0], w_router[0], b_router[0], w_gate_up[0], b_gate_up[0], w_down[0],
                  b_down[0], final_norm_w, True)
```

```python
import functools

import jax
import jax.numpy as jnp
from jax import lax
from jax.experimental import pallas as pl
from jax.experimental.pallas import tpu as pltpu
from jax.experimental.pallas import tpu_sc as plsc

F32 = jnp.float32
BF16 = jnp.bfloat16
I32 = jnp.int32

NORM_EPS = 1e-6
LANES = 128
SUBLANES = 8
GDN_HEADS = 4
HEAD_DIM = 128
GDN_CHUNK = 64
GDN_CONV = 4
CF_KERNEL = 31
N_EXPERTS = 32
TOP_K = 4
SWIGLU_LIMIT = 7.0
SWIGLU_ALPHA = 1.702

QKV_HALO = 8
CF_HALO = 32
EXPERT_BLOCK = 512
VMEM_LIMIT = 56 * 1024 * 1024


def _silu(x):
    return x * jax.nn.sigmoid(x)


def _dot(a, b):
    return jnp.dot(a, b, preferred_element_type=F32)


def _dot_nt(a, b):
    return lax.dot_general(a, b, (((1,), (1,)), ((), ())), preferred_element_type=F32)


def _dot_tn(a, b):
    return lax.dot_general(a, b, (((0,), (0,)), ((), ())), preferred_element_type=F32)


def _store_token_tiles(ref, x, lead=()):
    rows, d = x.shape
    chunks = d // LANES
    for c in range(chunks):
        ref[lead + (pl.ds(c, rows, stride=chunks), slice(None))] = x[:, c * LANES:(c + 1) * LANES]


def _load_token_tiles(ref, rows, lead=()):
    chunks = ref.shape[-2] // rows
    return jnp.concatenate(
        [ref[lead + (pl.ds(c, rows, stride=chunks), slice(None))] for c in range(chunks)], axis=1)


def _inproj_kernel(x_ref, nw_ref, wm_ref, wba_ref, wcf_ref, cw_ref, alog_ref, dtb_ref,
                   dww_ref, dwb_ref, lnw_ref, lnb_ref,
                   q_ref, k_ref, v_ref, z_ref, bg_ref, bgt_ref, u_ref,
                   qkv_buf, cf_buf, cf_shift):
    tt = x_ref.shape[1]
    qk = GDN_HEADS * HEAD_DIM
    cfc = u_ref.shape[2]

    @pl.when(pl.program_id(1) == 0)
    def _():
        qkv_buf[0:QKV_HALO, :] = jnp.zeros((QKV_HALO, qkv_buf.shape[1]), F32)
        cf_buf[0:CF_HALO, :] = jnp.zeros((CF_HALO, cf_buf.shape[1]), F32)

    x = x_ref[0]
    h = x * lax.rsqrt(jnp.mean(x * x, axis=-1, keepdims=True) + NORM_EPS) * nw_ref[...]
    h = h.astype(BF16)
    pm = _dot(h, wm_ref[...])
    pba = _dot(h, wba_ref[...])
    pcf = _dot(h, wcf_ref[...])

    z_ref[0] = pm[:, 3 * qk:]

    qkv_buf[QKV_HALO:QKV_HALO + tt, :] = pm[:, :3 * qk]
    acc = None
    for j in range(GDN_CONV):
        term = cw_ref[j:j + 1, :] * qkv_buf[pl.ds(QKV_HALO - (GDN_CONV - 1) + j, tt), :]
        acc = term if acc is None else acc + term
    qkv_buf[0:QKV_HALO, :] = qkv_buf[tt:tt + QKV_HALO, :]
    qkv = _silu(acc)
    for hd in range(GDN_HEADS):
        for base, ref in ((0, q_ref), (qk, k_ref)):
            t = qkv[:, base + hd * HEAD_DIM: base + (hd + 1) * HEAD_DIM]
            t = t * lax.rsqrt(jnp.sum(t * t, axis=-1, keepdims=True) + NORM_EPS)
            ref[0, :, hd * HEAD_DIM:(hd + 1) * HEAD_DIM] = t
    v_ref[0] = qkv[:, 2 * qk:]

    lane = lax.broadcasted_iota(I32, (tt, LANES), 1)
    row = lax.broadcasted_iota(I32, (tt, LANES), 0)
    beta = jax.nn.sigmoid(pba)
    sp_in = pba + dtb_ref[...]
    softplus = jnp.maximum(sp_in, 0.0) + jnp.log(1.0 + jnp.exp(-jnp.abs(sp_in)))
    g = -jnp.exp(alog_ref[...]) * softplus
    g = jnp.where((lane >= GDN_HEADS) & (lane < 2 * GDN_HEADS), g, 0.0)
    pos = row % GDN_CHUNK
    shift = 1
    while shift < GDN_CHUNK:
        g = g + jnp.where(pos >= shift, pltpu.roll(g, shift, 0), 0.0)
        shift *= 2
    bg = jnp.where(lane < GDN_HEADS, beta, g)
    bg_ref[0] = bg
    bgt_ref[0] = jnp.transpose(bg)[0:SUBLANES, :]

    glu = pcf[:, :cfc] * jax.nn.sigmoid(pcf[:, cfc:])
    cf_buf[CF_HALO:CF_HALO + tt, :] = glu
    lo = SUBLANES
    span = tt + CF_HALO - lo
    for r in range(1, SUBLANES):
        cf_shift[r - 1, lo:lo + span, :] = cf_buf[pl.ds(lo - r, span), :]
    rows = 64
    for r0 in range(0, tt, rows):
        acc = None
        for j in range(CF_KERNEL):
            a, r = divmod(CF_KERNEL - 1 - j, SUBLANES)
            start = CF_HALO + r0 - a * SUBLANES
            src = cf_buf[start:start + rows, :] if r == 0 else cf_shift[r - 1, start:start + rows, :]
            term = dww_ref[j:j + 1, :] * src
            acc = term if acc is None else acc + term
        c = acc + dwb_ref[...]
        mu = jnp.mean(c, axis=-1, keepdims=True)
        cc = c - mu
        y = cc * lax.rsqrt(jnp.mean(cc * cc, axis=-1, keepdims=True) + NORM_EPS)
        u_ref[0, r0:r0 + rows, :] = _silu(y * lnw_ref[...] + lnb_ref[...])
    cf_buf[0:CF_HALO, :] = cf_buf[tt:tt + CF_HALO, :]


def _inproj(x, nw, wm, wba, wcf, cw, alog, dtb, dww, dwb, lnw, lnb, *, tile):
    b, s, d = x.shape
    qk = GDN_HEADS * HEAD_DIM
    cfc = dww.shape[1]
    grid = (b, s // tile)
    full = lambda a: pl.BlockSpec(a.shape, lambda i, j: (0,) * a.ndim)
    tok = lambda w: pl.BlockSpec((1, tile, w), lambda i, j: (i, j, 0))
    out_shape = (
        jax.ShapeDtypeStruct((b, s, qk), F32), jax.ShapeDtypeStruct((b, s, qk), F32),
        jax.ShapeDtypeStruct((b, s, qk), F32), jax.ShapeDtypeStruct((b, s, qk), F32),
        jax.ShapeDtypeStruct((b, s, LANES), F32), jax.ShapeDtypeStruct((b, SUBLANES, s), F32),
        jax.ShapeDtypeStruct((b, s, cfc), F32))
    return pl.pallas_call(
        _inproj_kernel,
        grid=grid,
        in_specs=[tok(d)] + [full(a) for a in (nw, wm, wba, wcf, cw, alog, dtb, dww, dwb, lnw, lnb)],
        out_specs=(tok(qk), tok(qk), tok(qk), tok(qk), tok(LANES),
                   pl.BlockSpec((1, SUBLANES, tile), lambda i, j: (i, 0, j)), tok(cfc)),
        out_shape=out_shape,
        scratch_shapes=[pltpu.VMEM((QKV_HALO + tile, 3 * qk), F32),
                        pltpu.VMEM((CF_HALO + tile, cfc), F32),
                        pltpu.VMEM((SUBLANES - 1, CF_HALO + tile, cfc), F32)],
        compiler_params=pltpu.CompilerParams(
            dimension_semantics=("arbitrary", "arbitrary"), vmem_limit_bytes=VMEM_LIMIT),
        name="inproj",
    )(x, nw, wm, wba, wcf, cw, alog, dtb, dww, dwb, lnw, lnb)


def _bmm(a, b):
    return jnp.einsum("bmk,bkn->bmn", a, b, preferred_element_type=F32)


def _bmm_nt(a, b):
    return jnp.einsum("bmk,bnk->bmn", a, b, preferred_element_type=F32)


def _bmm_tn(a, b):
    return jnp.einsum("bkm,bkn->bmn", a, b, preferred_element_type=F32)


def _unit_lower_inverse(a):
    c = a.shape[-1]
    ii = lax.broadcasted_iota(I32, (c, c), 0)
    jj = lax.broadcasted_iota(I32, (c, c), 1)
    eye = (ii == jj).astype(F32)
    same16 = (ii // 16) == (jj // 16)
    same32 = (ii // 32) == (jj // 32)
    x = jnp.where(same16, -a, 0.0)
    t = eye + x
    xp = x
    for _ in range(3):
        xp_b = xp.astype(BF16)
        xp = _bmm(xp_b, xp_b)
        t = t + _bmm(t.astype(BF16), xp.astype(BF16))
    for off in (jnp.where(same32 & ~same16, a, 0.0), jnp.where(~same32, a, 0.0)):
        tb = t.astype(BF16)
        t = t - _bmm(tb, _bmm(off.astype(BF16), tb).astype(BF16))
    return t


def _gdn_kernel(q_ref, k_ref, v_ref, bg_ref, bgt_ref, o_ref, state, s_all):
    lt = q_ref.shape[1]
    c = GDN_CHUNK
    nh = GDN_HEADS
    nc = lt // c

    @pl.when(pl.program_id(1) == 0)
    def _():
        state[...] = jnp.zeros(state.shape, F32)

    def stack(fn):
        return jnp.stack([fn(slice(n * c, (n + 1) * c), h) for n in range(nc) for h in range(nh)])

    head = lambda h: slice(h * HEAD_DIM, (h + 1) * HEAD_DIM)
    q = stack(lambda r, h: q_ref[0, r, head(h)]) * (HEAD_DIM ** -0.5)
    k = stack(lambda r, h: k_ref[0, r, head(h)])
    v = stack(lambda r, h: v_ref[0, r, head(h)])
    beta = stack(lambda r, h: bg_ref[0, r, h:h + 1])
    gcol = stack(lambda r, h: bg_ref[0, r, nh + h:nh + h + 1])
    grow = stack(lambda r, h: bgt_ref[0, nh + h:nh + h + 1, r])

    ii = lax.broadcasted_iota(I32, (c, c), 0)
    jj = lax.broadcasted_iota(I32, (c, c), 1)
    glast = gcol[:, c - 1:c, :]
    eg = jnp.exp(gcol)
    decay = jnp.where(ii >= jj, jnp.exp(jnp.minimum(gcol - grow, 0.0)), 0.0)
    kb = k * beta
    k_b = k.astype(BF16)
    a = jnp.where(ii > jj, _bmm_nt(kb.astype(BF16), k_b) * decay, 0.0)
    t = _unit_lower_inverse(a)
    rhs = jnp.concatenate([v * beta, kb * eg], axis=-1).astype(BF16)
    sol = _bmm(t.astype(BF16), rhs)
    u_val = sol[..., :HEAD_DIM]
    w_key = sol[..., HEAD_DIM:]
    intra = _bmm_nt(q.astype(BF16), k_b) * decay
    k_tail = (k * jnp.exp(glast - gcol)).astype(BF16)
    upd = _bmm_tn(k_tail, sol.astype(BF16))
    b_mat = upd[..., :HEAD_DIM]
    p_mat = upd[..., HEAD_DIM:].astype(BF16)
    g_tot = jnp.exp(glast)

    s = state[...]
    for n in range(nc):
        grp = slice(n * nh, (n + 1) * nh)
        s_b = s.astype(BF16)
        s_all[grp] = s_b
        s = s * g_tot[grp] - _bmm(p_mat[grp], s_b) + b_mat[grp]
    state[...] = s

    wq = jnp.concatenate([w_key, q * eg], axis=1).astype(BF16)
    ws_qs = _bmm(wq, s_all[...])
    v_new = u_val - ws_qs[:, :c]
    o = ws_qs[:, c:] + _bmm(intra.astype(BF16), v_new.astype(BF16))
    for n in range(nc):
        for h in range(nh):
            o_ref[0, n * c:(n + 1) * c, head(h)] = o[n * nh + h]


def _gdn(q, k, v, bg, bgt, *, tile):
    b, s, qk = q.shape
    grid = (b, s // tile)
    tok = lambda w: pl.BlockSpec((1, tile, w), lambda i, j: (i, j, 0))
    n_prob = (tile // GDN_CHUNK) * GDN_HEADS
    return pl.pallas_call(
        _gdn_kernel,
        grid=grid,
        in_specs=[tok(qk), tok(qk), tok(qk), tok(LANES),
                  pl.BlockSpec((1, SUBLANES, tile), lambda i, j: (i, 0, j))],
        out_specs=tok(qk),
        out_shape=jax.ShapeDtypeStruct((b, s, qk), F32),
        scratch_shapes=[pltpu.VMEM((GDN_HEADS, HEAD_DIM, HEAD_DIM), F32),
                        pltpu.VMEM((n_prob, HEAD_DIM, HEAD_DIM), BF16)],
        compiler_params=pltpu.CompilerParams(
            dimension_semantics=("arbitrary", "arbitrary"), vmem_limit_bytes=VMEM_LIMIT),
        name="gdn",
    )(q, k, v, bg, bgt)


def _split_bf16(x):
    hi = x.astype(BF16)
    lo = (x - hi.astype(F32)).astype(BF16)
    return hi, lo


def _outproj_router_kernel(o_ref, z_ref, u_ref, x_ref, gnw_ref, wa_ref, wb_ref, fnw_ref,
                           wr_ref, br_ref,
                           x2_ref, xn_ref, route_ref, gate_ref, counts_ref, carry, *, group_steps):
    tt = x_ref.shape[0]

    @pl.when(pl.program_id(0) % group_steps == 0)
    def _():
        carry[...] = jnp.zeros(carry.shape, F32)

    parts = []
    for hd in range(GDN_HEADS):
        sl = slice(hd * HEAD_DIM, (hd + 1) * HEAD_DIM)
        oh = o_ref[:, sl]
        y = oh * lax.rsqrt(jnp.mean(oh * oh, axis=-1, keepdims=True) + NORM_EPS) * gnw_ref[...]
        parts.append((y * _silu(z_ref[:, sl])).astype(BF16))
    out_a = jnp.concatenate(parts, axis=-1)
    x2 = x_ref[...] + _dot(out_a, wa_ref[...]) + _dot(u_ref[...].astype(BF16), wb_ref[...])
    x2_ref[...] = x2

    xn = x2 * lax.rsqrt(jnp.mean(x2 * x2, axis=-1, keepdims=True) + NORM_EPS) * fnw_ref[...]
    _store_token_tiles(xn_ref, xn)

    xh, xl = _split_bf16(xn)
    wh, wl = _split_bf16(wr_ref[...])
    logits = _dot(xh, wh) + _dot(xh, wl) + _dot(xl, wh) + br_ref[...]

    lane = lax.broadcasted_iota(I32, (tt, LANES), 1)
    lane_f = lane.astype(F32)
    neg = jnp.float32(-jnp.inf)
    work = jnp.where(lane < N_EXPERTS, logits, neg)
    vals, idxs = [], []
    onehot = jnp.zeros((tt, LANES), F32)
    for _ in range(TOP_K):
        m = jnp.max(work, axis=-1, keepdims=True)
        idx = jnp.min(jnp.where(work == m, lane_f, float(LANES)), axis=-1, keepdims=True).astype(I32)
        sel = lane == idx
        vals.append(m)
        idxs.append(idx)
        onehot = onehot + sel.astype(F32)
        work = jnp.where(sel, neg, work)
    exps = [jnp.exp(v - vals[0]) for v in vals]
    denom = exps[0] + exps[1] + exps[2] + exps[3]

    ri = lax.broadcasted_iota(I32, (tt, tt), 0)
    ci = lax.broadcasted_iota(I32, (tt, tt), 1)
    strict = (ri > ci).astype(BF16)
    base = carry[...] + _dot(strict, onehot.astype(BF16))
    route = jnp.zeros((tt, LANES), I32)
    gates = jnp.zeros((tt, LANES), F32)
    for kk in range(TOP_K):
        rank = jnp.sum(jnp.where(lane == idxs[kk], base, 0.0), axis=-1, keepdims=True)
        route = jnp.where(lane == kk, idxs[kk], route)
        route = jnp.where(lane == kk + TOP_K, rank.astype(I32), route)
        gates = jnp.where(lane == kk, exps[kk] / denom, gates)
    route_ref[...] = route
    gate_ref[...] = gates
    new_carry = carry[...] + jnp.sum(onehot, axis=0, keepdims=True)
    carry[...] = new_carry
    counts_ref[0] = new_carry.astype(I32)


def _outproj_router(o, z, u, x, gnw, wa, wb, fnw, wr, br, *, tile, groups):
    t, d = x.shape
    grid = (t // tile,)
    group_steps = t // (tile * groups)
    full = lambda a: pl.BlockSpec(a.shape, lambda i: (0,) * a.ndim)
    tok = lambda w: pl.BlockSpec((tile, w), lambda i: (i, 0))
    return pl.pallas_call(
        functools.partial(_outproj_router_kernel, group_steps=group_steps),
        grid=grid,
        in_specs=[tok(o.shape[1]), tok(z.shape[1]), tok(u.shape[1]), tok(d)]
                 + [full(a) for a in (gnw, wa, wb, fnw, wr, br)],
        out_specs=(tok(d), pl.BlockSpec((tile * (d // LANES), LANES), lambda i: (i, 0)),
                   tok(LANES), tok(LANES), pl.BlockSpec((1, 1, LANES), lambda i: (i // group_steps, 0, 0))),
        out_shape=(jax.ShapeDtypeStruct((t, d), F32), jax.ShapeDtypeStruct((t * (d // LANES), LANES), F32),
                   jax.ShapeDtypeStruct((t, LANES), I32), jax.ShapeDtypeStruct((t, LANES), F32),
                   jax.ShapeDtypeStruct((groups, 1, LANES), I32)),
        scratch_shapes=[pltpu.VMEM((1, LANES), F32)],
        compiler_params=pltpu.CompilerParams(
            dimension_semantics=("arbitrary",), vmem_limit_bytes=VMEM_LIMIT),
        name="outproj_router",
    )(o, z, u, x, gnw, wa, wb, fnw, wr, br)


def _dest_kernel(route_ref, pstart_ref, dest_ref):
    route = route_ref[...].astype(F32)
    tt = route.shape[0]
    lane = lax.broadcasted_iota(I32, (tt, LANES), 1)
    pstart = pstart_ref[0].astype(F32)
    dest = jnp.zeros((tt, LANES), F32)
    for kk in range(TOP_K):
        idx = jnp.sum(jnp.where(lane == kk, route, 0.0), axis=-1, keepdims=True)
        rank = jnp.sum(jnp.where(lane == kk + TOP_K, route, 0.0), axis=-1, keepdims=True)
        start = jnp.sum(jnp.where(lane == idx.astype(I32), pstart, 0.0), axis=-1, keepdims=True)
        dest = jnp.where(lane == kk, start + rank, dest)
    dest_ref[...] = dest[:, :TOP_K].astype(I32)


def _dest(route, pstart, *, tile):
    t = route.shape[0]
    group_steps = t // (tile * pstart.shape[0])
    return pl.pallas_call(
        _dest_kernel,
        grid=(t // tile,),
        in_specs=[pl.BlockSpec((tile, LANES), lambda i: (i, 0)),
                  pl.BlockSpec((1, 1, LANES), lambda i: (i // group_steps, 0, 0))],
        out_specs=pl.BlockSpec((tile, TOP_K), lambda i: (i, 0)),
        out_shape=jax.ShapeDtypeStruct((t, TOP_K), I32),
        compiler_params=pltpu.CompilerParams(dimension_semantics=("arbitrary",)),
        name="dest_rows",
    )(route, pstart)


ROW_WINDOW = 32
COMBINE_CHUNKS = 2
MOE_GROUPS = 2


def _sc_mesh():
    return plsc.VectorSubcoreMesh(core_axis_name="core", subcore_axis_name="subcore")


def _dispatch_rows(xn_tiles, dest_win, n_rows, first_window):
    windows = dest_win.shape[0]

    @functools.partial(
        pl.kernel, mesh=_sc_mesh(), scratch_types=[],
        out_type=jax.ShapeDtypeStruct((n_rows,) + xn_tiles.shape[1:], xn_tiles.dtype))
    def dispatch(x_hbm, idx_hbm, o_hbm):
        def body(x_vmem, idx_vmem):
            for kk in range(TOP_K):
                pltpu.sync_copy(x_vmem, o_hbm.at[idx_vmem.at[0, pl.ds(kk * ROW_WINDOW, ROW_WINDOW)]])

        pltpu.emit_pipeline(
            body,
            grid=(windows,),
            in_specs=[pl.BlockSpec((ROW_WINDOW,) + xn_tiles.shape[1:], lambda i: (first_window + i, 0, 0)),
                      pl.BlockSpec((1, TOP_K * ROW_WINDOW), lambda i: (i, 0))],
            out_specs=[],
            core_axis_name=("core", "subcore"),
            dimension_semantics=(pltpu.PARALLEL,),
        )(x_hbm, idx_hbm)

    return dispatch(xn_tiles, dest_win)


def _collect_rows(yb_tiles, src_win):
    n_pairs = src_win.shape[0] * ROW_WINDOW

    @functools.partial(
        pl.kernel, mesh=_sc_mesh(), scratch_types=[],
        out_type=jax.ShapeDtypeStruct((n_pairs,) + yb_tiles.shape[1:], yb_tiles.dtype))
    def collect(y_hbm, idx_hbm, o_hbm):
        def body(idx_vmem, o_vmem):
            pltpu.sync_copy(y_hbm.at[idx_vmem.at[0, pl.ds(0, ROW_WINDOW)]], o_vmem)

        pltpu.emit_pipeline(
            body,
            grid=(n_pairs // ROW_WINDOW,),
            in_specs=[pl.BlockSpec((1, LANES), lambda i: (i, 0))],
            out_specs=[pl.BlockSpec((ROW_WINDOW,) + yb_tiles.shape[1:], lambda i: (i, 0, 0))],
            core_axis_name=("core", "subcore"),
            dimension_semantics=(pltpu.PARALLEL,),
        )(idx_hbm, o_hbm)

    return collect(yb_tiles, src_win)


def _expert_kernel(blk_exp_ref, n_used_ref, xb_ref, wgu_ref, bgu_ref, wd_ref, bd_ref, yb_ref, wgu_b, wd_b):
    i = pl.program_id(0)
    n_used = n_used_ref[0]
    bm = xb_ref.shape[0] // SUBLANES
    dff = wd_ref.shape[1]

    prev = blk_exp_ref[jnp.maximum(i - 1, 0)]
    changed = (i == 0) | (blk_exp_ref[i] != prev)

    @pl.when(changed & (i < n_used))
    def _():
        wgu_b[...] = wgu_ref[0].astype(BF16)
        wd_b[...] = wd_ref[0].astype(BF16)

    @pl.when(i < n_used)
    def _():
        xb = _load_token_tiles(xb_ref, bm).astype(BF16)
        hid = _dot(xb, wgu_b[...]) + bgu_ref[0]
        gate = jnp.minimum(hid[:, :dff], SWIGLU_LIMIT)
        up = jnp.clip(hid[:, dff:], -SWIGLU_LIMIT, SWIGLU_LIMIT)
        glu = gate * jax.nn.sigmoid(SWIGLU_ALPHA * gate)
        act = ((up + 1.0) * glu).astype(BF16)
        _store_token_tiles(yb_ref, _dot(act, wd_b[...]) + bd_ref[0])


def _experts(blk_exp, n_used, xb_tiles, wgu, bgu, wd, bd):
    d = wgu.shape[1]
    chunks = d // LANES
    n_blocks = xb_tiles.shape[0] // (EXPERT_BLOCK * chunks)
    two_f = wgu.shape[2]
    dff = wd.shape[1]
    blk = lambda i, be, nu: (jnp.minimum(i, nu[0] - 1), 0)
    exp3 = lambda i, be, nu: (be[jnp.minimum(i, nu[0] - 1)], 0, 0)
    grid_spec = pltpu.PrefetchScalarGridSpec(
        num_scalar_prefetch=2,
        grid=(n_blocks,),
        in_specs=[pl.BlockSpec((EXPERT_BLOCK * chunks, LANES), blk),
                  pl.BlockSpec((1, d, two_f), exp3),
                  pl.BlockSpec((1, 1, two_f), exp3),
                  pl.BlockSpec((1, dff, d), exp3),
                  pl.BlockSpec((1, 1, d), exp3)],
        out_specs=pl.BlockSpec((EXPERT_BLOCK * chunks, LANES), blk),
        scratch_shapes=[pltpu.VMEM((d, two_f), BF16), pltpu.VMEM((dff, d), BF16)])
    return pl.pallas_call(
        _expert_kernel,
        grid_spec=grid_spec,
        out_shape=jax.ShapeDtypeStruct(xb_tiles.shape, F32),
        compiler_params=pltpu.CompilerParams(
            dimension_semantics=("arbitrary",), vmem_limit_bytes=VMEM_LIMIT),
        name="experts",
    )(blk_exp, n_used, xb_tiles, wgu, bgu, wd, bd)


def _combine_kernel(y0_ref, y1_ref, y2_ref, y3_ref, gate_ref, x2_ref, fw_ref, out_ref):
    gates = gate_ref[...]
    lane = lax.broadcasted_iota(I32, gates.shape, 1)
    x3 = x2_ref[...]
    for kk, y_ref in enumerate((y0_ref, y1_ref, y2_ref, y3_ref)):
        gk = jnp.sum(jnp.where(lane == kk, gates, 0.0), axis=-1, keepdims=True)
        x3 = x3 + gk * _load_token_tiles(y_ref, x3.shape[0])
    out_ref[...] = x3 * lax.rsqrt(jnp.mean(x3 * x3, axis=-1, keepdims=True) + NORM_EPS) * fw_ref[...]


def _combine_into_kernel(y0_ref, y1_ref, y2_ref, y3_ref, gate_ref, x2_ref, fw_ref, prev_ref, out_ref):
    del prev_ref
    _combine_kernel(y0_ref, y1_ref, y2_ref, y3_ref, gate_ref, x2_ref, fw_ref, out_ref)


def _combine(y4, gates, x2, fw, out_prev, chunk, *, tile):
    t, d = x2.shape
    tc = y4.shape[0] // (TOP_K * (d // LANES))
    steps = tc // tile
    first = chunk * steps
    choice = lambda kk: pl.BlockSpec((tile * (d // LANES), LANES), lambda i: (kk * steps + i, 0))
    in_specs = ([choice(kk) for kk in range(TOP_K)]
                + [pl.BlockSpec((tile, LANES), lambda i: (first + i, 0)),
                   pl.BlockSpec((tile, d), lambda i: (first + i, 0)),
                   pl.BlockSpec((1, d), lambda i: (0, 0))])
    args = [y4, y4, y4, y4, gates, x2, fw]
    if out_prev is not None:
        in_specs.append(pl.BlockSpec(memory_space=pl.ANY))
        args.append(out_prev)
    return pl.pallas_call(
        _combine_kernel if out_prev is None else _combine_into_kernel,
        grid=(steps,),
        in_specs=in_specs,
        out_specs=pl.BlockSpec((tile, d), lambda i: (first + i, 0)),
        out_shape=jax.ShapeDtypeStruct((t, d), F32),
        input_output_aliases={} if out_prev is None else {len(args) - 1: 0},
        compiler_params=pltpu.CompilerParams(
            dimension_semantics=("arbitrary",), vmem_limit_bytes=VMEM_LIMIT),
        name="combine",
    )(*args)


def _pad_lanes(a, offset=0, fill=0.0):
    out = jnp.full((1, LANES), fill, a.dtype)
    return out.at[0, offset:offset + a.shape[0]].set(a)


def _layer(x, attn_norm_w, w_in, gdn_conv_w, gdn_a_log, gdn_dt_bias, gdn_norm_w,
           cf_dw_w, cf_dw_b, cf_ln_w, cf_ln_b, w_out, ffn_norm_w, w_router, b_router,
           w_gate_up, b_gate_up, w_down, b_down, final_norm_w, apply_final):
    b, s, d = x.shape
    t = b * s
    assert d == SUBLANES * LANES, "the token-tile layout needs one (8, 128) tile per token row"
    qk = GDN_HEADS * HEAD_DIM
    cfc = cf_dw_w.shape[1]
    off_b = 4 * qk
    off_cf = off_b + 2 * GDN_HEADS

    wm = w_in[:, :off_b].astype(BF16)
    wba = jnp.zeros((d, LANES), F32).at[:, :2 * GDN_HEADS].set(w_in[:, off_b:off_cf]).astype(BF16)
    wcf = w_in[:, off_cf:].astype(BF16)
    alog = _pad_lanes(gdn_a_log, GDN_HEADS)
    dtb = _pad_lanes(gdn_dt_bias, GDN_HEADS)

    q, k, v, z, bg, bgt, u = _inproj(
        x, attn_norm_w[None, :], wm, wba, wcf, gdn_conv_w, alog, dtb,
        cf_dw_w, cf_dw_b[None, :], cf_ln_w[None, :], cf_ln_b[None, :], tile=min(512, s))
    o = _gdn(q, k, v, bg, bgt, tile=min(256, s))

    wr = jnp.zeros((d, LANES), F32).at[:, :N_EXPERTS].set(w_router)
    br = _pad_lanes(b_router)
    groups = MOE_GROUPS if t % (MOE_GROUPS * COMBINE_CHUNKS * EXPERT_BLOCK) == 0 else 1
    x2, xn, route, gates, counts = _outproj_router(
        o.reshape(t, qk), z.reshape(t, qk), u.reshape(t, cfc), x.reshape(t, d),
        gdn_norm_w[None, :], w_out[:qk].astype(BF16), w_out[qk:].astype(BF16),
        ffn_norm_w[None, :], wr, br, tile=min(512, t // groups), groups=groups)

    tg = t // groups
    cnt = counts[:, 0, :N_EXPERTS]
    nblk = (cnt + EXPERT_BLOCK - 1) // EXPERT_BLOCK
    blk_end = jnp.cumsum(nblk, axis=1)
    pstart = (blk_end - nblk) * EXPERT_BLOCK
    n_blocks = (tg * TOP_K) // EXPERT_BLOCK + N_EXPERTS
    blk_ids = jnp.arange(n_blocks, dtype=I32)
    blk_exp = jnp.minimum(
        jnp.sum((blk_end[:, None, :] <= blk_ids[None, :, None]).astype(I32), axis=2), N_EXPERTS - 1)
    n_used = blk_end[:, -1:].astype(I32)
    pstart_lanes = jnp.zeros((groups, 1, LANES), I32).at[:, 0, :N_EXPERTS].set(pstart.astype(I32))

    dest = _dest(route, pstart_lanes, tile=min(2048, tg))
    n_rows = n_blocks * EXPERT_BLOCK
    chunks = d // LANES
    xn_tiles = xn.reshape(t, chunks, LANES)
    windows = tg // ROW_WINDOW
    tc = tg // COMBINE_CHUNKS
    out = None
    for g in range(groups):
        dest_g = dest[g * tg:(g + 1) * tg]
        dest_win = dest_g.reshape(windows, ROW_WINDOW, TOP_K).transpose(0, 2, 1)
        xb = _dispatch_rows(xn_tiles, dest_win.reshape(windows, TOP_K * ROW_WINDOW), n_rows, g * windows)
        yb = _experts(blk_exp[g], n_used[g], xb.reshape(n_rows * chunks, LANES), w_gate_up,
                      b_gate_up[:, None, :], w_down, b_down[:, None, :]).reshape(n_rows, chunks, LANES)
        for c in range(COMBINE_CHUNKS):
            src = dest_g[c * tc:(c + 1) * tc].T.reshape(TOP_K * tc // ROW_WINDOW, ROW_WINDOW)
            src_win = jnp.pad(src, ((0, 0), (0, LANES - ROW_WINDOW)))
            y4 = _collect_rows(yb, src_win).reshape(TOP_K * tc * chunks, LANES)
            out = _combine(y4, gates, x2, final_norm_w[None, :], out, g * COMBINE_CHUNKS + c,
                           tile=min(EXPERT_BLOCK, tc))
    return out.reshape(b, s, d)


def kernel(x, attn_norm_w, w_in, gdn_conv_w, gdn_a_log, gdn_dt_bias, gdn_norm_w, cf_dw_w, cf_dw_b,
           cf_ln_w, cf_ln_b, w_out, ffn_norm_w, w_router, b_router, w_gate_up, b_gate_up, w_down,
           b_down, final_norm_w):
    depth = w_in.shape[0]
    assert depth == 1, "the fused final norm assumes a single trunk layer"
    return _layer(x, attn_norm_w[0], w_in[0], gdn_conv_w[0], gdn_a_log[0], gdn_dt_bias[0],
                  gdn_norm_w[0], cf_dw_w[0], cf_dw_b[0], cf_ln_w[0], cf_ln_b[0], w_out[0],
                  ffn_norm_w[0], w_router[0], b_router[0], w_gate_up[0], b_gate_up[0], w_down[0],
                  b_down[0], final_norm_w, True)
```

```python
import functools

import jax
import jax.numpy as jnp
from jax import lax
from jax.experimental import pallas as pl
from jax.experimental.pallas import tpu as pltpu
from jax.experimental.pallas import tpu_sc as plsc

F32 = jnp.float32
BF16 = jnp.bfloat16
I32 = jnp.int32

NORM_EPS = 1e-6
LANES = 128
SUBLANES = 8
GDN_HEADS = 4
HEAD_DIM = 128
GDN_CHUNK = 64
GDN_CONV = 4
CF_KERNEL = 31
N_EXPERTS = 32
TOP_K = 4
SWIGLU_LIMIT = 7.0
SWIGLU_ALPHA = 1.702

QKV_HALO = 8
CF_HALO = 32
EXPERT_BLOCK = 512
VMEM_LIMIT = 56 * 1024 * 1024


def _silu(x):
    return x * jax.nn.sigmoid(x)


def _dot(a, b):
    return jnp.dot(a, b, preferred_element_type=F32)


def _dot_nt(a, b):
    return lax.dot_general(a, b, (((1,), (1,)), ((), ())), preferred_element_type=F32)


def _dot_tn(a, b):
    return lax.dot_general(a, b, (((0,), (0,)), ((), ())), preferred_element_type=F32)


def _store_token_tiles(ref, x, lead=()):
    rows, d = x.shape
    chunks = d // LANES
    for c in range(chunks):
        ref[lead + (pl.ds(c, rows, stride=chunks), slice(None))] = x[:, c * LANES:(c + 1) * LANES]


def _load_token_tiles(ref, rows, lead=()):
    chunks = ref.shape[-2] // rows
    return jnp.concatenate(
        [ref[lead + (pl.ds(c, rows, stride=chunks), slice(None))] for c in range(chunks)], axis=1)


def _inproj_kernel(x_ref, nw_ref, wm_ref, wba_ref, wcf_ref, cw_ref, alog_ref, dtb_ref,
                   dww_ref, dwb_ref, lnw_ref, lnb_ref,
                   q_ref, k_ref, v_ref, z_ref, bg_ref, bgt_ref, u_ref,
                   qkv_buf, cf_buf, cf_shift):
    tt = x_ref.shape[1]
    qk = GDN_HEADS * HEAD_DIM
    cfc = u_ref.shape[2]

    @pl.when(pl.program_id(1) == 0)
    def _():
        qkv_buf[0:QKV_HALO, :] = jnp.zeros((QKV_HALO, qkv_buf.shape[1]), F32)
        cf_buf[0:CF_HALO, :] = jnp.zeros((CF_HALO, cf_buf.shape[1]), F32)

    x = x_ref[0]
    h = x * lax.rsqrt(jnp.mean(x * x, axis=-1, keepdims=True) + NORM_EPS) * nw_ref[...]
    h = h.astype(BF16)
    pm = _dot(h, wm_ref[...])
    pba = _dot(h, wba_ref[...])
    pcf = _dot(h, wcf_ref[...])

    z_ref[0] = pm[:, 3 * qk:]

    qkv_buf[QKV_HALO:QKV_HALO + tt, :] = pm[:, :3 * qk]
    acc = None
    for j in range(GDN_CONV):
        term = cw_ref[j:j + 1, :] * qkv_buf[pl.ds(QKV_HALO - (GDN_CONV - 1) + j, tt), :]
        acc = term if acc is None else acc + term
    qkv_buf[0:QKV_HALO, :] = qkv_buf[tt:tt + QKV_HALO, :]
    qkv = _silu(acc)
    for hd in range(GDN_HEADS):
        for base, ref in ((0, q_ref), (qk, k_ref)):
            t = qkv[:, base + hd * HEAD_DIM: base + (hd + 1) * HEAD_DIM]
            t = t * lax.rsqrt(jnp.sum(t * t, axis=-1, keepdims=True) + NORM_EPS)
            ref[0, :, hd * HEAD_DIM:(hd + 1) * HEAD_DIM] = t
    v_ref[0] = qkv[:, 2 * qk:]

    lane = lax.broadcasted_iota(I32, (tt, LANES), 1)
    row = lax.broadcasted_iota(I32, (tt, LANES), 0)
    beta = jax.nn.sigmoid(pba)
    sp_in = pba + dtb_ref[...]
    softplus = jnp.maximum(sp_in, 0.0) + jnp.log(1.0 + jnp.exp(-jnp.abs(sp_in)))
    g = -jnp.exp(alog_ref[...]) * softplus
    g = jnp.where((lane >= GDN_HEADS) & (lane < 2 * GDN_HEADS), g, 0.0)
    pos = row % GDN_CHUNK
    shift = 1
    while shift < GDN_CHUNK:
        g = g + jnp.where(pos >= shift, pltpu.roll(g, shift, 0), 0.0)
        shift *= 2
    bg = jnp.where(lane < GDN_HEADS, beta, g)
    bg_ref[0] = bg
    bgt_ref[0] = jnp.transpose(bg)[0:SUBLANES, :]

    glu = pcf[:, :cfc] * jax.nn.sigmoid(pcf[:, cfc:])
    cf_buf[CF_HALO:CF_HALO + tt, :] = glu
    lo = SUBLANES
    span = tt + CF_HALO - lo
    for r in range(1, SUBLANES):
        cf_shift[r - 1, lo:lo + span, :] = cf_buf[pl.ds(lo - r, span), :]
    rows = 64
    for r0 in range(0, tt, rows):
        acc = None
        for j in range(CF_KERNEL):
            a, r = divmod(CF_KERNEL - 1 - j, SUBLANES)
            start = CF_HALO + r0 - a * SUBLANES
            src = cf_buf[start:start + rows, :] if r == 0 else cf_shift[r - 1, start:start + rows, :]
            term = dww_ref[j:j + 1, :] * src
            acc = term if acc is None else acc + term
        c = acc + dwb_ref[...]
        mu = jnp.mean(c, axis=-1, keepdims=True)
        cc = c - mu
        y = cc * lax.rsqrt(jnp.mean(cc * cc, axis=-1, keepdims=True) + NORM_EPS)
        u_ref[0, r0:r0 + rows, :] = _silu(y * lnw_ref[...] + lnb_ref[...])
    cf_buf[0:CF_HALO, :] = cf_buf[tt:tt + CF_HALO, :]


def _inproj(x, nw, wm, wba, wcf, cw, alog, dtb, dww, dwb, lnw, lnb, *, tile):
    b, s, d = x.shape
    qk = GDN_HEADS * HEAD_DIM
    cfc = dww.shape[1]
    grid = (b, s // tile)
    full = lambda a: pl.BlockSpec(a.shape, lambda i, j: (0,) * a.ndim)
    tok = lambda w: pl.BlockSpec((1, tile, w), lambda i, j: (i, j, 0))
    out_shape = (
        jax.ShapeDtypeStruct((b, s, qk), F32), jax.ShapeDtypeStruct((b, s, qk), F32),
        jax.ShapeDtypeStruct((b, s, qk), F32), jax.ShapeDtypeStruct((b, s, qk), F32),
        jax.ShapeDtypeStruct((b, s, LANES), F32), jax.ShapeDtypeStruct((b, SUBLANES, s), F32),
        jax.ShapeDtypeStruct((b, s, cfc), F32))
    return pl.pallas_call(
        _inproj_kernel,
        grid=grid,
        in_specs=[tok(d)] + [full(a) for a in (nw, wm, wba, wcf, cw, alog, dtb, dww, dwb, lnw, lnb)],
        out_specs=(tok(qk), tok(qk), tok(qk), tok(qk), tok(LANES),
                   pl.BlockSpec((1, SUBLANES, tile), lambda i, j: (i, 0, j)), tok(cfc)),
        out_shape=out_shape,
        scratch_shapes=[pltpu.VMEM((QKV_HALO + tile, 3 * qk), F32),
                        pltpu.VMEM((CF_HALO + tile, cfc), F32),
                        pltpu.VMEM((SUBLANES - 1, CF_HALO + tile, cfc), F32)],
        compiler_params=pltpu.CompilerParams(
            dimension_semantics=("arbitrary", "arbitrary"), vmem_limit_bytes=VMEM_LIMIT),
        name="inproj",
    )(x, nw, wm, wba, wcf, cw, alog, dtb, dww, dwb, lnw, lnb)


def _bmm(a, b):
    return jnp.einsum("bmk,bkn->bmn", a, b, preferred_element_type=F32)


def _bmm_nt(a, b):
    return jnp.einsum("bmk,bnk->bmn", a, b, preferred_element_type=F32)


def _bmm_tn(a, b):
    return jnp.einsum("bkm,bkn->bmn", a, b, preferred_element_type=F32)


def _unit_lower_inverse(a):
    c = a.shape[-1]
    ii = lax.broadcasted_iota(I32, (c, c), 0)
    jj = lax.broadcasted_iota(I32, (c, c), 1)
    eye = (ii == jj).astype(F32)
    same16 = (ii // 16) == (jj // 16)
    same32 = (ii // 32) == (jj // 32)
    x = jnp.where(same16, -a, 0.0)
    t = eye + x
    xp = x
    for _ in range(3):
        xp_b = xp.astype(BF16)
        xp = _bmm(xp_b, xp_b)
        t = t + _bmm(t.astype(BF16), xp.astype(BF16))
    for off in (jnp.where(same32 & ~same16, a, 0.0), jnp.where(~same32, a, 0.0)):
        tb = t.astype(BF16)
        t = t - _bmm(tb, _bmm(off.astype(BF16), tb).astype(BF16))
    return t


def _gdn_kernel(q_ref, k_ref, v_ref, bg_ref, bgt_ref, o_ref, state, s_all):
    lt = q_ref.shape[1]
    c = GDN_CHUNK
    nh = GDN_HEADS
    nc = lt // c

    @pl.when(pl.program_id(1) == 0)
    def _():
        state[...] = jnp.zeros(state.shape, F32)

    def stack(fn):
        return jnp.stack([fn(slice(n * c, (n + 1) * c), h) for n in range(nc) for h in range(nh)])

    head = lambda h: slice(h * HEAD_DIM, (h + 1) * HEAD_DIM)
    q = stack(lambda r, h: q_ref[0, r, head(h)]) * (HEAD_DIM ** -0.5)
    k = stack(lambda r, h: k_ref[0, r, head(h)])
    v = stack(lambda r, h: v_ref[0, r, head(h)])
    beta = stack(lambda r, h: bg_ref[0, r, h:h + 1])
    gcol = stack(lambda r, h: bg_ref[0, r, nh + h:nh + h + 1])
    grow = stack(lambda r, h: bgt_ref[0, nh + h:nh + h + 1, r])

    ii = lax.broadcasted_iota(I32, (c, c), 0)
    jj = lax.broadcasted_iota(I32, (c, c), 1)
    glast = gcol[:, c - 1:c, :]
    eg = jnp.exp(gcol)
    decay = jnp.where(ii >= jj, jnp.exp(jnp.minimum(gcol - grow, 0.0)), 0.0)
    kb = k * beta
    k_b = k.astype(BF16)
    a = jnp.where(ii > jj, _bmm_nt(kb.astype(BF16), k_b) * decay, 0.0)
    t = _unit_lower_inverse(a)
    rhs = jnp.concatenate([v * beta, kb * eg], axis=-1).astype(BF16)
    sol = _bmm(t.astype(BF16), rhs)
    u_val = sol[..., :HEAD_DIM]
    w_key = sol[..., HEAD_DIM:]
    intra = _bmm_nt(q.astype(BF16), k_b) * decay
    k_tail = (k * jnp.exp(glast - gcol)).astype(BF16)
    upd = _bmm_tn(k_tail, sol.astype(BF16))
    b_mat = upd[..., :HEAD_DIM]
    p_mat = upd[..., HEAD_DIM:].astype(BF16)
    g_tot = jnp.exp(glast)

    s = state[...]
    for n in range(nc):
        grp = slice(n * nh, (n + 1) * nh)
        s_b = s.astype(BF16)
        s_all[grp] = s_b
        s = s * g_tot[grp] - _bmm(p_mat[grp], s_b) + b_mat[grp]
    state[...] = s

    wq = jnp.concatenate([w_key, q * eg], axis=1).astype(BF16)
    ws_qs = _bmm(wq, s_all[...])
    v_new = u_val - ws_qs[:, :c]
    o = ws_qs[:, c:] + _bmm(intra.astype(BF16), v_new.astype(BF16))
    for n in range(nc):
        for h in range(nh):
            o_ref[0, n * c:(n + 1) * c, head(h)] = o[n * nh + h]


def _gdn(q, k, v, bg, bgt, *, tile):
    b, s, qk = q.shape
    grid = (b, s // tile)
    tok = lambda w: pl.BlockSpec((1, tile, w), lambda i, j: (i, j, 0))
    n_prob = (tile // GDN_CHUNK) * GDN_HEADS
    return pl.pallas_call(
        _gdn_kernel,
        grid=grid,
        in_specs=[tok(qk), tok(qk), tok(qk), tok(LANES),
                  pl.BlockSpec((1, SUBLANES, tile), lambda i, j: (i, 0, j))],
        out_specs=tok(qk),
        out_shape=jax.ShapeDtypeStruct((b, s, qk), F32),
        scratch_shapes=[pltpu.VMEM((GDN_HEADS, HEAD_DIM, HEAD_DIM), F32),
                        pltpu.VMEM((n_prob, HEAD_DIM, HEAD_DIM), BF16)],
        compiler_params=pltpu.CompilerParams(
            dimension_semantics=("arbitrary", "arbitrary"), vmem_limit_bytes=VMEM_LIMIT),
        name="gdn",
    )(q, k, v, bg, bgt)


def _split_bf16(x):
    hi = x.astype(BF16)
    lo = (x - hi.astype(F32)).astype(BF16)
    return hi, lo


def _outproj_router_kernel(o_ref, z_ref, u_ref, x_ref, gnw_ref, wa_ref, wb_ref, fnw_ref,
                           wr_ref, br_ref,
                           x2_ref, xn_ref, route_ref, gate_ref, counts_ref, carry):
    tt = x_ref.shape[0]

    @pl.when(pl.program_id(0) == 0)
    def _():
        carry[...] = jnp.zeros(carry.shape, F32)

    parts = []
    for hd in range(GDN_HEADS):
        sl = slice(hd * HEAD_DIM, (hd + 1) * HEAD_DIM)
        oh = o_ref[:, sl]
        y = oh * lax.rsqrt(jnp.mean(oh * oh, axis=-1, keepdims=True) + NORM_EPS) * gnw_ref[...]
        parts.append((y * _silu(z_ref[:, sl])).astype(BF16))
    out_a = jnp.concatenate(parts, axis=-1)
    x2 = x_ref[...] + _dot(out_a, wa_ref[...]) + _dot(u_ref[...].astype(BF16), wb_ref[...])
    x2_ref[...] = x2

    xn = x2 * lax.rsqrt(jnp.mean(x2 * x2, axis=-1, keepdims=True) + NORM_EPS) * fnw_ref[...]
    _store_token_tiles(xn_ref, xn)

    xh, xl = _split_bf16(xn)
    wh, wl = _split_bf16(wr_ref[...])
    logits = _dot(xh, wh) + _dot(xh, wl) + _dot(xl, wh) + br_ref[...]

    lane = lax.broadcasted_iota(I32, (tt, LANES), 1)
    lane_f = lane.astype(F32)
    neg = jnp.float32(-jnp.inf)
    work = jnp.where(lane < N_EXPERTS, logits, neg)
    vals, idxs = [], []
    onehot = jnp.zeros((tt, LANES), F32)
    for _ in range(TOP_K):
        m = jnp.max(work, axis=-1, keepdims=True)
        idx = jnp.min(jnp.where(work == m, lane_f, float(LANES)), axis=-1, keepdims=True).astype(I32)
        sel = lane == idx
        vals.append(m)
        idxs.append(idx)
        onehot = onehot + sel.astype(F32)
        work = jnp.where(sel, neg, work)
    exps = [jnp.exp(v - vals[0]) for v in vals]
    denom = exps[0] + exps[1] + exps[2] + exps[3]

    ri = lax.broadcasted_iota(I32, (tt, tt), 0)
    ci = lax.broadcasted_iota(I32, (tt, tt), 1)
    strict = (ri > ci).astype(BF16)
    base = carry[...] + _dot(strict, onehot.astype(BF16))
    route = jnp.zeros((tt, LANES), I32)
    gates = jnp.zeros((tt, LANES), F32)
    for kk in range(TOP_K):
        rank = jnp.sum(jnp.where(lane == idxs[kk], base, 0.0), axis=-1, keepdims=True)
        route = jnp.where(lane == kk, idxs[kk], route)
        route = jnp.where(lane == kk + TOP_K, rank.astype(I32), route)
        gates = jnp.where(lane == kk, exps[kk] / denom, gates)
    route_ref[...] = route
    gate_ref[...] = gates
    new_carry = carry[...] + jnp.sum(onehot, axis=0, keepdims=True)
    carry[...] = new_carry
    counts_ref[...] = new_carry.astype(I32)


def _outproj_router(o, z, u, x, gnw, wa, wb, fnw, wr, br, *, tile):
    t, d = x.shape
    grid = (t // tile,)
    full = lambda a: pl.BlockSpec(a.shape, lambda i: (0,) * a.ndim)
    tok = lambda w: pl.BlockSpec((tile, w), lambda i: (i, 0))
    return pl.pallas_call(
        _outproj_router_kernel,
        grid=grid,
        in_specs=[tok(o.shape[1]), tok(z.shape[1]), tok(u.shape[1]), tok(d)]
                 + [full(a) for a in (gnw, wa, wb, fnw, wr, br)],
        out_specs=(tok(d), pl.BlockSpec((tile * (d // LANES), LANES), lambda i: (i, 0)),
                   tok(LANES), tok(LANES), pl.BlockSpec((1, LANES), lambda i: (0, 0))),
        out_shape=(jax.ShapeDtypeStruct((t, d), F32), jax.ShapeDtypeStruct((t * (d // LANES), LANES), F32),
                   jax.ShapeDtypeStruct((t, LANES), I32), jax.ShapeDtypeStruct((t, LANES), F32),
                   jax.ShapeDtypeStruct((1, LANES), I32)),
        scratch_shapes=[pltpu.VMEM((1, LANES), F32)],
        compiler_params=pltpu.CompilerParams(
            dimension_semantics=("arbitrary",), vmem_limit_bytes=VMEM_LIMIT),
        name="outproj_router",
    )(o, z, u, x, gnw, wa, wb, fnw, wr, br)


def _dest_kernel(route_ref, pstart_ref, dest_ref):
    route = route_ref[...].astype(F32)
    tt = route.shape[0]
    lane = lax.broadcasted_iota(I32, (tt, LANES), 1)
    pstart = pstart_ref[...].astype(F32)
    dest = jnp.zeros((tt, LANES), F32)
    for kk in range(TOP_K):
        idx = jnp.sum(jnp.where(lane == kk, route, 0.0), axis=-1, keepdims=True)
        rank = jnp.sum(jnp.where(lane == kk + TOP_K, route, 0.0), axis=-1, keepdims=True)
        start = jnp.sum(jnp.where(lane == idx.astype(I32), pstart, 0.0), axis=-1, keepdims=True)
        dest = jnp.where(lane == kk, start + rank, dest)
    dest_ref[...] = dest[:, :TOP_K].astype(I32)


def _dest(route, pstart, *, tile):
    t = route.shape[0]
    return pl.pallas_call(
        _dest_kernel,
        grid=(t // tile,),
        in_specs=[pl.BlockSpec((tile, LANES), lambda i: (i, 0)),
                  pl.BlockSpec((1, LANES), lambda i: (0, 0))],
        out_specs=pl.BlockSpec((tile, TOP_K), lambda i: (i, 0)),
        out_shape=jax.ShapeDtypeStruct((t, TOP_K), I32),
        compiler_params=pltpu.CompilerParams(dimension_semantics=("arbitrary",)),
        name="dest_rows",
    )(route, pstart)


ROW_WINDOW = 32
COMBINE_CHUNKS = 8


def _sc_mesh():
    return plsc.VectorSubcoreMesh(core_axis_name="core", subcore_axis_name="subcore")


def _dispatch_rows(xn_tiles, dest_win, n_rows):
    t = xn_tiles.shape[0]

    @functools.partial(
        pl.kernel, mesh=_sc_mesh(), scratch_types=[],
        out_type=jax.ShapeDtypeStruct((n_rows,) + xn_tiles.shape[1:], xn_tiles.dtype))
    def dispatch(x_hbm, idx_hbm, o_hbm):
        def body(x_vmem, idx_vmem):
            for kk in range(TOP_K):
                pltpu.sync_copy(x_vmem, o_hbm.at[idx_vmem.at[0, pl.ds(kk * ROW_WINDOW, ROW_WINDOW)]])

        pltpu.emit_pipeline(
            body,
            grid=(t // ROW_WINDOW,),
            in_specs=[pl.BlockSpec((ROW_WINDOW,) + xn_tiles.shape[1:], lambda i: (i, 0, 0)),
                      pl.BlockSpec((1, TOP_K * ROW_WINDOW), lambda i: (i, 0))],
            out_specs=[],
            core_axis_name=("core", "subcore"),
            dimension_semantics=(pltpu.PARALLEL,),
        )(x_hbm, idx_hbm)

    return dispatch(xn_tiles, dest_win)


def _collect_rows(yb_tiles, src_win):
    n_pairs = src_win.shape[0] * ROW_WINDOW

    @functools.partial(
        pl.kernel, mesh=_sc_mesh(), scratch_types=[],
        out_type=jax.ShapeDtypeStruct((n_pairs,) + yb_tiles.shape[1:], yb_tiles.dtype))
    def collect(y_hbm, idx_hbm, o_hbm):
        def body(idx_vmem, o_vmem):
            pltpu.sync_copy(y_hbm.at[idx_vmem.at[0, pl.ds(0, ROW_WINDOW)]], o_vmem)

        pltpu.emit_pipeline(
            body,
            grid=(n_pairs // ROW_WINDOW,),
            in_specs=[pl.BlockSpec((1, LANES), lambda i: (i, 0))],
            out_specs=[pl.BlockSpec((ROW_WINDOW,) + yb_tiles.shape[1:], lambda i: (i, 0, 0))],
            core_axis_name=("core", "subcore"),
            dimension_semantics=(pltpu.PARALLEL,),
        )(idx_hbm, o_hbm)

    return collect(yb_tiles, src_win)


def _expert_kernel(blk_exp_ref, n_used_ref, xb_ref, wgu_ref, bgu_ref, wd_ref, bd_ref, yb_ref, wgu_b, wd_b):
    i = pl.program_id(0)
    n_used = n_used_ref[0]
    bm = xb_ref.shape[0] // SUBLANES
    dff = wd_ref.shape[1]

    prev = blk_exp_ref[jnp.maximum(i - 1, 0)]
    changed = (i == 0) | (blk_exp_ref[i] != prev)

    @pl.when(changed & (i < n_used))
    def _():
        wgu_b[...] = wgu_ref[0].astype(BF16)
        wd_b[...] = wd_ref[0].astype(BF16)

    @pl.when(i < n_used)
    def _():
        xb = _load_token_tiles(xb_ref, bm).astype(BF16)
        hid = _dot(xb, wgu_b[...]) + bgu_ref[0]
        gate = jnp.minimum(hid[:, :dff], SWIGLU_LIMIT)
        up = jnp.clip(hid[:, dff:], -SWIGLU_LIMIT, SWIGLU_LIMIT)
        glu = gate * jax.nn.sigmoid(SWIGLU_ALPHA * gate)
        act = ((up + 1.0) * glu).astype(BF16)
        _store_token_tiles(yb_ref, _dot(act, wd_b[...]) + bd_ref[0])


def _experts(blk_exp, n_used, xb_tiles, wgu, bgu, wd, bd):
    d = wgu.shape[1]
    chunks = d // LANES
    n_blocks = xb_tiles.shape[0] // (EXPERT_BLOCK * chunks)
    two_f = wgu.shape[2]
    dff = wd.shape[1]
    blk = lambda i, be, nu: (jnp.minimum(i, nu[0] - 1), 0)
    exp3 = lambda i, be, nu: (be[jnp.minimum(i, nu[0] - 1)], 0, 0)
    grid_spec = pltpu.PrefetchScalarGridSpec(
        num_scalar_prefetch=2,
        grid=(n_blocks,),
        in_specs=[pl.BlockSpec((EXPERT_BLOCK * chunks, LANES), blk),
                  pl.BlockSpec((1, d, two_f), exp3),
                  pl.BlockSpec((1, 1, two_f), exp3),
                  pl.BlockSpec((1, dff, d), exp3),
                  pl.BlockSpec((1, 1, d), exp3)],
        out_specs=pl.BlockSpec((EXPERT_BLOCK * chunks, LANES), blk),
        scratch_shapes=[pltpu.VMEM((d, two_f), BF16), pltpu.VMEM((dff, d), BF16)])
    return pl.pallas_call(
        _expert_kernel,
        grid_spec=grid_spec,
        out_shape=jax.ShapeDtypeStruct(xb_tiles.shape, F32),
        compiler_params=pltpu.CompilerParams(
            dimension_semantics=("arbitrary",), vmem_limit_bytes=VMEM_LIMIT),
        name="experts",
    )(blk_exp, n_used, xb_tiles, wgu, bgu, wd, bd)


def _combine_kernel(y0_ref, y1_ref, y2_ref, y3_ref, gate_ref, x2_ref, fw_ref, out_ref):
    gates = gate_ref[...]
    lane = lax.broadcasted_iota(I32, gates.shape, 1)
    x3 = x2_ref[...]
    for kk, y_ref in enumerate((y0_ref, y1_ref, y2_ref, y3_ref)):
        gk = jnp.sum(jnp.where(lane == kk, gates, 0.0), axis=-1, keepdims=True)
        x3 = x3 + gk * _load_token_tiles(y_ref, x3.shape[0])
    out_ref[...] = x3 * lax.rsqrt(jnp.mean(x3 * x3, axis=-1, keepdims=True) + NORM_EPS) * fw_ref[...]


def _combine_into_kernel(y0_ref, y1_ref, y2_ref, y3_ref, gate_ref, x2_ref, fw_ref, prev_ref, out_ref):
    del prev_ref
    _combine_kernel(y0_ref, y1_ref, y2_ref, y3_ref, gate_ref, x2_ref, fw_ref, out_ref)


def _combine(y4, gates, x2, fw, out_prev, chunk, *, tile):
    t, d = x2.shape
    tc = y4.shape[0] // (TOP_K * (d // LANES))
    steps = tc // tile
    first = chunk * steps
    choice = lambda kk: pl.BlockSpec((tile * (d // LANES), LANES), lambda i: (kk * steps + i, 0))
    in_specs = ([choice(kk) for kk in range(TOP_K)]
                + [pl.BlockSpec((tile, LANES), lambda i: (first + i, 0)),
                   pl.BlockSpec((tile, d), lambda i: (first + i, 0)),
                   pl.BlockSpec((1, d), lambda i: (0, 0))])
    args = [y4, y4, y4, y4, gates, x2, fw]
    if out_prev is not None:
        in_specs.append(pl.BlockSpec(memory_space=pl.ANY))
        args.append(out_prev)
    return pl.pallas_call(
        _combine_kernel if out_prev is None else _combine_into_kernel,
        grid=(steps,),
        in_specs=in_specs,
        out_specs=pl.BlockSpec((tile, d), lambda i: (first + i, 0)),
        out_shape=jax.ShapeDtypeStruct((t, d), F32),
        input_output_aliases={} if out_prev is None else {len(args) - 1: 0},
        compiler_params=pltpu.CompilerParams(
            dimension_semantics=("arbitrary",), vmem_limit_bytes=VMEM_LIMIT),
        name="combine",
    )(*args)


def _pad_lanes(a, offset=0, fill=0.0):
    out = jnp.full((1, LANES), fill, a.dtype)
    return out.at[0, offset:offset + a.shape[0]].set(a)


def _layer(x, attn_norm_w, w_in, gdn_conv_w, gdn_a_log, gdn_dt_bias, gdn_norm_w,
           cf_dw_w, cf_dw_b, cf_ln_w, cf_ln_b, w_out, ffn_norm_w, w_router, b_router,
           w_gate_up, b_gate_up, w_down, b_down, final_norm_w, apply_final):
    b, s, d = x.shape
    t = b * s
    assert d == SUBLANES * LANES, "the token-tile layout needs one (8, 128) tile per token row"
    qk = GDN_HEADS * HEAD_DIM
    cfc = cf_dw_w.shape[1]
    off_b = 4 * qk
    off_cf = off_b + 2 * GDN_HEADS

    wm = w_in[:, :off_b].astype(BF16)
    wba = jnp.zeros((d, LANES), F32).at[:, :2 * GDN_HEADS].set(w_in[:, off_b:off_cf]).astype(BF16)
    wcf = w_in[:, off_cf:].astype(BF16)
    alog = _pad_lanes(gdn_a_log, GDN_HEADS)
    dtb = _pad_lanes(gdn_dt_bias, GDN_HEADS)

    q, k, v, z, bg, bgt, u = _inproj(
        x, attn_norm_w[None, :], wm, wba, wcf, gdn_conv_w, alog, dtb,
        cf_dw_w, cf_dw_b[None, :], cf_ln_w[None, :], cf_ln_b[None, :], tile=min(512, s))
    o = _gdn(q, k, v, bg, bgt, tile=min(512, s))

    wr = jnp.zeros((d, LANES), F32).at[:, :N_EXPERTS].set(w_router)
    br = _pad_lanes(b_router)
    x2, xn, route, gates, counts = _outproj_router(
        o.reshape(t, qk), z.reshape(t, qk), u.reshape(t, cfc), x.reshape(t, d),
        gdn_norm_w[None, :], w_out[:qk].astype(BF16), w_out[qk:].astype(BF16),
        ffn_norm_w[None, :], wr, br, tile=min(512, t))

    cnt = counts[0, :N_EXPERTS]
    nblk = (cnt + EXPERT_BLOCK - 1) // EXPERT_BLOCK
    blk_end = jnp.cumsum(nblk)
    pstart = (blk_end - nblk) * EXPERT_BLOCK
    n_blocks = (t * TOP_K) // EXPERT_BLOCK + N_EXPERTS
    blk_ids = jnp.arange(n_blocks, dtype=I32)
    blk_exp = jnp.minimum(
        jnp.sum((blk_end[None, :] <= blk_ids[:, None]).astype(I32), axis=1), N_EXPERTS - 1)
    n_used = blk_end[-1:].astype(I32)

    dest = _dest(route, _pad_lanes(pstart.astype(I32)), tile=min(2048, t))
    n_rows = n_blocks * EXPERT_BLOCK
    chunks = d // LANES
    windows = t // ROW_WINDOW
    dest_win = dest.reshape(windows, ROW_WINDOW, TOP_K).transpose(0, 2, 1).reshape(windows, TOP_K * ROW_WINDOW)
    xb = _dispatch_rows(xn.reshape(t, chunks, LANES), dest_win, n_rows)
    yb = _experts(blk_exp, n_used, xb.reshape(n_rows * chunks, LANES), w_gate_up, b_gate_up[:, None, :],
                  w_down, b_down[:, None, :]).reshape(n_rows, chunks, LANES)

    tc = t // COMBINE_CHUNKS
    out = None
    for c in range(COMBINE_CHUNKS):
        src = dest[c * tc:(c + 1) * tc].T.reshape(TOP_K * tc // ROW_WINDOW, ROW_WINDOW)
        src_win = jnp.pad(src, ((0, 0), (0, LANES - ROW_WINDOW)))
        y4 = _collect_rows(yb, src_win).reshape(TOP_K * tc * chunks, LANES)
        out = _combine(y4, gates, x2, final_norm_w[None, :], out, c, tile=min(EXPERT_BLOCK, tc))
    return out.reshape(b, s, d)


def kernel(x, attn_norm_w, w_in, gdn_conv_w, gdn_a_log, gdn_dt_bias, gdn_norm_w, cf_dw_w, cf_dw_b,
           cf_ln_w, cf_ln_b, w_out, ffn_norm_w, w_router, b_router, w_gate_up, b_gate_up, w_down,
           b_down, final_norm_w):
    depth = w_in.shape[0]
    assert depth == 1, "the fused final norm assumes a single trunk layer"
    return _layer(x, attn_norm_w[0], w_in[0], gdn_conv_w[0], gdn_a_log[0], gdn_dt_bias[0],
                  gdn_norm_w[0], cf_dw_w[0], cf_dw_b[0], cf_ln_w[0], cf_ln_b[0], w_out[0],
                  ffn_norm_w[0], w_router[0], b_router[0], w_gate_up[0], b_gate_up[0], w_down[0],
                  b_down[0], final_norm_w, True)
```

```python
import functools

import jax
import jax.numpy as jnp
from jax import lax
from jax.experimental import pallas as pl
from jax.experimental.pallas import tpu as pltpu
from jax.experimental.pallas import tpu_sc as plsc

F32 = jnp.float32
BF16 = jnp.bfloat16
I32 = jnp.int32

NORM_EPS = 1e-6
LANES = 128
SUBLANES = 8
GDN_HEADS = 4
HEAD_DIM = 128
GDN_CHUNK = 64
GDN_CONV = 4
CF_KERNEL = 31
N_EXPERTS = 32
TOP_K = 4
SWIGLU_LIMIT = 7.0
SWIGLU_ALPHA = 1.702

QKV_HALO = 8
CF_HALO = 32
EXPERT_BLOCK = 512
VMEM_LIMIT = 56 * 1024 * 1024


def _silu(x):
    return x * jax.nn.sigmoid(x)


def _dot(a, b):
    return jnp.dot(a, b, preferred_element_type=F32)


def _dot_nt(a, b):
    return lax.dot_general(a, b, (((1,), (1,)), ((), ())), preferred_element_type=F32)


def _dot_tn(a, b):
    return lax.dot_general(a, b, (((0,), (0,)), ((), ())), preferred_element_type=F32)


def _store_token_tiles(ref, x, lead=()):
    rows, d = x.shape
    chunks = d // LANES
    for c in range(chunks):
        ref[lead + (pl.ds(c, rows, stride=chunks), slice(None))] = x[:, c * LANES:(c + 1) * LANES]


def _load_token_tiles(ref, rows, lead=()):
    chunks = ref.shape[-2] // rows
    return jnp.concatenate(
        [ref[lead + (pl.ds(c, rows, stride=chunks), slice(None))] for c in range(chunks)], axis=1)


def _inproj_kernel(x_ref, nw_ref, wm_ref, wba_ref, wcf_ref, cw_ref, alog_ref, dtb_ref,
                   dww_ref, dwb_ref, lnw_ref, lnb_ref,
                   q_ref, k_ref, v_ref, z_ref, bg_ref, bgt_ref, u_ref,
                   qkv_buf, cf_buf, cf_shift):
    tt = x_ref.shape[1]
    qk = GDN_HEADS * HEAD_DIM
    cfc = u_ref.shape[2]

    @pl.when(pl.program_id(1) == 0)
    def _():
        qkv_buf[0:QKV_HALO, :] = jnp.zeros((QKV_HALO, qkv_buf.shape[1]), F32)
        cf_buf[0:CF_HALO, :] = jnp.zeros((CF_HALO, cf_buf.shape[1]), F32)

    x = x_ref[0]
    h = x * lax.rsqrt(jnp.mean(x * x, axis=-1, keepdims=True) + NORM_EPS) * nw_ref[...]
    h = h.astype(BF16)
    pm = _dot(h, wm_ref[...])
    pba = _dot(h, wba_ref[...])
    pcf = _dot(h, wcf_ref[...])

    z_ref[0] = pm[:, 3 * qk:]

    qkv_buf[QKV_HALO:QKV_HALO + tt, :] = pm[:, :3 * qk]
    acc = None
    for j in range(GDN_CONV):
        term = cw_ref[j:j + 1, :] * qkv_buf[pl.ds(QKV_HALO - (GDN_CONV - 1) + j, tt), :]
        acc = term if acc is None else acc + term
    qkv_buf[0:QKV_HALO, :] = qkv_buf[tt:tt + QKV_HALO, :]
    qkv = _silu(acc)
    for hd in range(GDN_HEADS):
        for base, ref in ((0, q_ref), (qk, k_ref)):
            t = qkv[:, base + hd * HEAD_DIM: base + (hd + 1) * HEAD_DIM]
            t = t * lax.rsqrt(jnp.sum(t * t, axis=-1, keepdims=True) + NORM_EPS)
            ref[0, :, hd * HEAD_DIM:(hd + 1) * HEAD_DIM] = t
    v_ref[0] = qkv[:, 2 * qk:]

    lane = lax.broadcasted_iota(I32, (tt, LANES), 1)
    row = lax.broadcasted_iota(I32, (tt, LANES), 0)
    beta = jax.nn.sigmoid(pba)
    sp_in = pba + dtb_ref[...]
    softplus = jnp.maximum(sp_in, 0.0) + jnp.log(1.0 + jnp.exp(-jnp.abs(sp_in)))
    g = -jnp.exp(alog_ref[...]) * softplus
    g = jnp.where((lane >= GDN_HEADS) & (lane < 2 * GDN_HEADS), g, 0.0)
    pos = row % GDN_CHUNK
    shift = 1
    while shift < GDN_CHUNK:
        g = g + jnp.where(pos >= shift, pltpu.roll(g, shift, 0), 0.0)
        shift *= 2
    bg = jnp.where(lane < GDN_HEADS, beta, g)
    bg_ref[0] = bg
    bgt_ref[0] = jnp.transpose(bg)[0:SUBLANES, :]

    glu = pcf[:, :cfc] * jax.nn.sigmoid(pcf[:, cfc:])
    cf_buf[CF_HALO:CF_HALO + tt, :] = glu
    lo = SUBLANES
    span = tt + CF_HALO - lo
    for r in range(1, SUBLANES):
        cf_shift[r - 1, lo:lo + span, :] = cf_buf[pl.ds(lo - r, span), :]
    rows = 64
    for r0 in range(0, tt, rows):
        acc = None
        for j in range(CF_KERNEL):
            a, r = divmod(CF_KERNEL - 1 - j, SUBLANES)
            start = CF_HALO + r0 - a * SUBLANES
            src = cf_buf[start:start + rows, :] if r == 0 else cf_shift[r - 1, start:start + rows, :]
            term = dww_ref[j:j + 1, :] * src
            acc = term if acc is None else acc + term
        c = acc + dwb_ref[...]
        mu = jnp.mean(c, axis=-1, keepdims=True)
        cc = c - mu
        y = cc * lax.rsqrt(jnp.mean(cc * cc, axis=-1, keepdims=True) + NORM_EPS)
        u_ref[0, r0:r0 + rows, :] = _silu(y * lnw_ref[...] + lnb_ref[...])
    cf_buf[0:CF_HALO, :] = cf_buf[tt:tt + CF_HALO, :]


def _inproj(x, nw, wm, wba, wcf, cw, alog, dtb, dww, dwb, lnw, lnb, *, tile):
    b, s, d = x.shape
    qk = GDN_HEADS * HEAD_DIM
    cfc = dww.shape[1]
    grid = (b, s // tile)
    full = lambda a: pl.BlockSpec(a.shape, lambda i, j: (0,) * a.ndim)
    tok = lambda w: pl.BlockSpec((1, tile, w), lambda i, j: (i, j, 0))
    out_shape = (
        jax.ShapeDtypeStruct((b, s, qk), F32), jax.ShapeDtypeStruct((b, s, qk), F32),
        jax.ShapeDtypeStruct((b, s, qk), F32), jax.ShapeDtypeStruct((b, s, qk), F32),
        jax.ShapeDtypeStruct((b, s, LANES), F32), jax.ShapeDtypeStruct((b, SUBLANES, s), F32),
        jax.ShapeDtypeStruct((b, s, cfc), F32))
    return pl.pallas_call(
        _inproj_kernel,
        grid=grid,
        in_specs=[tok(d)] + [full(a) for a in (nw, wm, wba, wcf, cw, alog, dtb, dww, dwb, lnw, lnb)],
        out_specs=(tok(qk), tok(qk), tok(qk), tok(qk), tok(LANES),
                   pl.BlockSpec((1, SUBLANES, tile), lambda i, j: (i, 0, j)), tok(cfc)),
        out_shape=out_shape,
        scratch_shapes=[pltpu.VMEM((QKV_HALO + tile, 3 * qk), F32),
                        pltpu.VMEM((CF_HALO + tile, cfc), F32),
                        pltpu.VMEM((SUBLANES - 1, CF_HALO + tile, cfc), F32)],
        compiler_params=pltpu.CompilerParams(
            dimension_semantics=("arbitrary", "arbitrary"), vmem_limit_bytes=VMEM_LIMIT),
        name="inproj",
    )(x, nw, wm, wba, wcf, cw, alog, dtb, dww, dwb, lnw, lnb)


def _bmm(a, b):
    return jnp.einsum("bmk,bkn->bmn", a, b, preferred_element_type=F32)


def _bmm_nt(a, b):
    return jnp.einsum("bmk,bnk->bmn", a, b, preferred_element_type=F32)


def _bmm_tn(a, b):
    return jnp.einsum("bkm,bkn->bmn", a, b, preferred_element_type=F32)


def _unit_lower_inverse(a):
    c = a.shape[-1]
    ii = lax.broadcasted_iota(I32, (c, c), 0)
    jj = lax.broadcasted_iota(I32, (c, c), 1)
    eye = (ii == jj).astype(F32)
    same16 = (ii // 16) == (jj // 16)
    same32 = (ii // 32) == (jj // 32)
    x = jnp.where(same16, -a, 0.0)
    t = eye + x
    xp = x
    for _ in range(3):
        xp_b = xp.astype(BF16)
        xp = _bmm(xp_b, xp_b)
        t = t + _bmm(t.astype(BF16), xp.astype(BF16))
    for off in (jnp.where(same32 & ~same16, a, 0.0), jnp.where(~same32, a, 0.0)):
        tb = t.astype(BF16)
        t = t - _bmm(tb, _bmm(off.astype(BF16), tb).astype(BF16))
    return t


def _gdn_kernel(q_ref, k_ref, v_ref, bg_ref, bgt_ref, o_ref, state, s_all):
    lt = q_ref.shape[1]
    c = GDN_CHUNK
    nh = GDN_HEADS
    nc = lt // c

    @pl.when(pl.program_id(1) == 0)
    def _():
        state[...] = jnp.zeros(state.shape, F32)

    def stack(fn):
        return jnp.stack([fn(slice(n * c, (n + 1) * c), h) for n in range(nc) for h in range(nh)])

    head = lambda h: slice(h * HEAD_DIM, (h + 1) * HEAD_DIM)
    q = stack(lambda r, h: q_ref[0, r, head(h)]) * (HEAD_DIM ** -0.5)
    k = stack(lambda r, h: k_ref[0, r, head(h)])
    v = stack(lambda r, h: v_ref[0, r, head(h)])
    beta = stack(lambda r, h: bg_ref[0, r, h:h + 1])
    gcol = stack(lambda r, h: bg_ref[0, r, nh + h:nh + h + 1])
    grow = stack(lambda r, h: bgt_ref[0, nh + h:nh + h + 1, r])

    ii = lax.broadcasted_iota(I32, (c, c), 0)
    jj = lax.broadcasted_iota(I32, (c, c), 1)
    glast = gcol[:, c - 1:c, :]
    eg = jnp.exp(gcol)
    decay = jnp.where(ii >= jj, jnp.exp(jnp.minimum(gcol - grow, 0.0)), 0.0)
    kb = k * beta
    k_b = k.astype(BF16)
    a = jnp.where(ii > jj, _bmm_nt(kb.astype(BF16), k_b) * decay, 0.0)
    t = _unit_lower_inverse(a)
    rhs = jnp.concatenate([v * beta, kb * eg], axis=-1).astype(BF16)
    sol = _bmm(t.astype(BF16), rhs)
    u_val = sol[..., :HEAD_DIM]
    w_key = sol[..., HEAD_DIM:]
    intra = _bmm_nt(q.astype(BF16), k_b) * decay
    k_tail = (k * jnp.exp(glast - gcol)).astype(BF16)
    upd = _bmm_tn(k_tail, sol.astype(BF16))
    b_mat = upd[..., :HEAD_DIM]
    p_mat = upd[..., HEAD_DIM:].astype(BF16)
    g_tot = jnp.exp(glast)

    s = state[...]
    for n in range(nc):
        grp = slice(n * nh, (n + 1) * nh)
        s_b = s.astype(BF16)
        s_all[grp] = s_b
        s = s * g_tot[grp] - _bmm(p_mat[grp], s_b) + b_mat[grp]
    state[...] = s

    wq = jnp.concatenate([w_key, q * eg], axis=1).astype(BF16)
    ws_qs = _bmm(wq, s_all[...])
    v_new = u_val - ws_qs[:, :c]
    o = ws_qs[:, c:] + _bmm(intra.astype(BF16), v_new.astype(BF16))
    for n in range(nc):
        for h in range(nh):
            o_ref[0, n * c:(n + 1) * c, head(h)] = o[n * nh + h]


def _gdn(q, k, v, bg, bgt, *, tile):
    b, s, qk = q.shape
    grid = (b, s // tile)
    tok = lambda w: pl.BlockSpec((1, tile, w), lambda i, j: (i, j, 0))
    n_prob = (tile // GDN_CHUNK) * GDN_HEADS
    return pl.pallas_call(
        _gdn_kernel,
        grid=grid,
        in_specs=[tok(qk), tok(qk), tok(qk), tok(LANES),
                  pl.BlockSpec((1, SUBLANES, tile), lambda i, j: (i, 0, j))],
        out_specs=tok(qk),
        out_shape=jax.ShapeDtypeStruct((b, s, qk), F32),
        scratch_shapes=[pltpu.VMEM((GDN_HEADS, HEAD_DIM, HEAD_DIM), F32),
                        pltpu.VMEM((n_prob, HEAD_DIM, HEAD_DIM), BF16)],
        compiler_params=pltpu.CompilerParams(
            dimension_semantics=("arbitrary", "arbitrary"), vmem_limit_bytes=VMEM_LIMIT),
        name="gdn",
    )(q, k, v, bg, bgt)


def _split_bf16(x):
    hi = x.astype(BF16)
    lo = (x - hi.astype(F32)).astype(BF16)
    return hi, lo


def _outproj_router_kernel(o_ref, z_ref, u_ref, x_ref, gnw_ref, wa_ref, wb_ref, fnw_ref,
                           wr_ref, br_ref,
                           x2_ref, xn_ref, route_ref, gate_ref, counts_ref, carry):
    tt = x_ref.shape[0]

    @pl.when(pl.program_id(0) == 0)
    def _():
        carry[...] = jnp.zeros(carry.shape, F32)

    parts = []
    for hd in range(GDN_HEADS):
        sl = slice(hd * HEAD_DIM, (hd + 1) * HEAD_DIM)
        oh = o_ref[:, sl]
        y = oh * lax.rsqrt(jnp.mean(oh * oh, axis=-1, keepdims=True) + NORM_EPS) * gnw_ref[...]
        parts.append((y * _silu(z_ref[:, sl])).astype(BF16))
    out_a = jnp.concatenate(parts, axis=-1)
    x2 = x_ref[...] + _dot(out_a, wa_ref[...]) + _dot(u_ref[...].astype(BF16), wb_ref[...])
    x2_ref[...] = x2

    xn = x2 * lax.rsqrt(jnp.mean(x2 * x2, axis=-1, keepdims=True) + NORM_EPS) * fnw_ref[...]
    _store_token_tiles(xn_ref, xn)

    xh, xl = _split_bf16(xn)
    wh, wl = _split_bf16(wr_ref[...])
    logits = _dot(xh, wh) + _dot(xh, wl) + _dot(xl, wh) + br_ref[...]

    lane = lax.broadcasted_iota(I32, (tt, LANES), 1)
    lane_f = lane.astype(F32)
    neg = jnp.float32(-jnp.inf)
    work = jnp.where(lane < N_EXPERTS, logits, neg)
    vals, idxs = [], []
    onehot = jnp.zeros((tt, LANES), F32)
    for _ in range(TOP_K):
        m = jnp.max(work, axis=-1, keepdims=True)
        idx = jnp.min(jnp.where(work == m, lane_f, float(LANES)), axis=-1, keepdims=True).astype(I32)
        sel = lane == idx
        vals.append(m)
        idxs.append(idx)
        onehot = onehot + sel.astype(F32)
        work = jnp.where(sel, neg, work)
    exps = [jnp.exp(v - vals[0]) for v in vals]
    denom = exps[0] + exps[1] + exps[2] + exps[3]

    ri = lax.broadcasted_iota(I32, (tt, tt), 0)
    ci = lax.broadcasted_iota(I32, (tt, tt), 1)
    strict = (ri > ci).astype(BF16)
    base = carry[...] + _dot(strict, onehot.astype(BF16))
    route = jnp.zeros((tt, LANES), I32)
    gates = jnp.zeros((tt, LANES), F32)
    for kk in range(TOP_K):
        rank = jnp.sum(jnp.where(lane == idxs[kk], base, 0.0), axis=-1, keepdims=True)
        route = jnp.where(lane == kk, idxs[kk], route)
        route = jnp.where(lane == kk + TOP_K, rank.astype(I32), route)
        gates = jnp.where(lane == kk, exps[kk] / denom, gates)
    route_ref[...] = route
    gate_ref[...] = gates
    new_carry = carry[...] + jnp.sum(onehot, axis=0, keepdims=True)
    carry[...] = new_carry
    counts_ref[...] = new_carry.astype(I32)


def _outproj_router(o, z, u, x, gnw, wa, wb, fnw, wr, br, *, tile):
    t, d = x.shape
    grid = (t // tile,)
    full = lambda a: pl.BlockSpec(a.shape, lambda i: (0,) * a.ndim)
    tok = lambda w: pl.BlockSpec((tile, w), lambda i: (i, 0))
    return pl.pallas_call(
        _outproj_router_kernel,
        grid=grid,
        in_specs=[tok(o.shape[1]), tok(z.shape[1]), tok(u.shape[1]), tok(d)]
                 + [full(a) for a in (gnw, wa, wb, fnw, wr, br)],
        out_specs=(tok(d), pl.BlockSpec((tile * (d // LANES), LANES), lambda i: (i, 0)),
                   tok(LANES), tok(LANES), pl.BlockSpec((1, LANES), lambda i: (0, 0))),
        out_shape=(jax.ShapeDtypeStruct((t, d), F32), jax.ShapeDtypeStruct((t * (d // LANES), LANES), F32),
                   jax.ShapeDtypeStruct((t, LANES), I32), jax.ShapeDtypeStruct((t, LANES), F32),
                   jax.ShapeDtypeStruct((1, LANES), I32)),
        scratch_shapes=[pltpu.VMEM((1, LANES), F32)],
        compiler_params=pltpu.CompilerParams(
            dimension_semantics=("arbitrary",), vmem_limit_bytes=VMEM_LIMIT),
        name="outproj_router",
    )(o, z, u, x, gnw, wa, wb, fnw, wr, br)


def _dest_kernel(route_ref, pstart_ref, dest_ref):
    route = route_ref[...].astype(F32)
    tt = route.shape[0]
    lane = lax.broadcasted_iota(I32, (tt, LANES), 1)
    pstart = pstart_ref[...].astype(F32)
    dest = jnp.zeros((tt, LANES), F32)
    for kk in range(TOP_K):
        idx = jnp.sum(jnp.where(lane == kk, route, 0.0), axis=-1, keepdims=True)
        rank = jnp.sum(jnp.where(lane == kk + TOP_K, route, 0.0), axis=-1, keepdims=True)
        start = jnp.sum(jnp.where(lane == idx.astype(I32), pstart, 0.0), axis=-1, keepdims=True)
        dest = jnp.where(lane == kk, start + rank, dest)
    dest_ref[...] = dest[:, :TOP_K].astype(I32)


def _dest(route, pstart, *, tile):
    t = route.shape[0]
    return pl.pallas_call(
        _dest_kernel,
        grid=(t // tile,),
        in_specs=[pl.BlockSpec((tile, LANES), lambda i: (i, 0)),
                  pl.BlockSpec((1, LANES), lambda i: (0, 0))],
        out_specs=pl.BlockSpec((tile, TOP_K), lambda i: (i, 0)),
        out_shape=jax.ShapeDtypeStruct((t, TOP_K), I32),
        compiler_params=pltpu.CompilerParams(dimension_semantics=("arbitrary",)),
        name="dest_rows",
    )(route, pstart)


ROW_WINDOW = 32
COMBINE_CHUNKS = 8


def _sc_mesh():
    return plsc.VectorSubcoreMesh(core_axis_name="core", subcore_axis_name="subcore")


def _dispatch_rows(xn_tiles, dest_win, n_rows):
    t = xn_tiles.shape[0]

    @functools.partial(
        pl.kernel, mesh=_sc_mesh(), scratch_types=[],
        out_type=jax.ShapeDtypeStruct((n_rows,) + xn_tiles.shape[1:], xn_tiles.dtype))
    def dispatch(x_hbm, idx_hbm, o_hbm):
        def body(x_vmem, idx_vmem):
            for kk in range(TOP_K):
                pltpu.sync_copy(x_vmem, o_hbm.at[idx_vmem.at[0, pl.ds(kk * ROW_WINDOW, ROW_WINDOW)]])

        pltpu.emit_pipeline(
            body,
            grid=(t // ROW_WINDOW,),
            in_specs=[pl.BlockSpec((ROW_WINDOW,) + xn_tiles.shape[1:], lambda i: (i, 0, 0)),
                      pl.BlockSpec((1, TOP_K * ROW_WINDOW), lambda i: (i, 0))],
            out_specs=[],
            core_axis_name=("core", "subcore"),
            dimension_semantics=(pltpu.PARALLEL,),
        )(x_hbm, idx_hbm)

    return dispatch(xn_tiles, dest_win)


def _collect_rows(yb_tiles, src_win):
    n_pairs = src_win.shape[0] * ROW_WINDOW

    @functools.partial(
        pl.kernel, mesh=_sc_mesh(), scratch_types=[],
        out_type=jax.ShapeDtypeStruct((n_pairs,) + yb_tiles.shape[1:], yb_tiles.dtype))
    def collect(y_hbm, idx_hbm, o_hbm):
        def body(idx_vmem, o_vmem):
            pltpu.sync_copy(y_hbm.at[idx_vmem.at[0, pl.ds(0, ROW_WINDOW)]], o_vmem)

        pltpu.emit_pipeline(
            body,
            grid=(n_pairs // ROW_WINDOW,),
            in_specs=[pl.BlockSpec((1, LANES), lambda i: (i, 0))],
            out_specs=[pl.BlockSpec((ROW_WINDOW,) + yb_tiles.shape[1:], lambda i: (i, 0, 0))],
            core_axis_name=("core", "subcore"),
            dimension_semantics=(pltpu.PARALLEL,),
        )(idx_hbm, o_hbm)

    return collect(yb_tiles, src_win)


def _expert_kernel(blk_exp_ref, n_used_ref, blk_rows_ref, xb_ref, wgu_ref, bgu_ref, wd_ref, bd_ref, yb_ref,
                   wgu_b, wd_b):
    i = pl.program_id(0)
    n_used = n_used_ref[0]
    bm = xb_ref.shape[0] // SUBLANES
    dff = wd_ref.shape[1]

    prev = blk_exp_ref[jnp.maximum(i - 1, 0)]
    changed = (i == 0) | (blk_exp_ref[i] != prev)

    @pl.when(changed & (i < n_used))
    def _():
        wgu_b[...] = wgu_ref[0].astype(BF16)
        wd_b[...] = wd_ref[0].astype(BF16)

    def swiglu_rows(rows):
        x_rows = xb_ref.at[pl.ds(0, rows * SUBLANES)]
        y_rows = yb_ref.at[pl.ds(0, rows * SUBLANES)]
        xb = _load_token_tiles(x_rows, rows).astype(BF16)
        hid = _dot(xb, wgu_b[...]) + bgu_ref[0]
        gate = jnp.minimum(hid[:, :dff], SWIGLU_LIMIT)
        up = jnp.clip(hid[:, dff:], -SWIGLU_LIMIT, SWIGLU_LIMIT)
        glu = gate * jax.nn.sigmoid(SWIGLU_ALPHA * gate)
        act = ((up + 1.0) * glu).astype(BF16)
        _store_token_tiles(y_rows, _dot(act, wd_b[...]) + bd_ref[0])

    used = i < n_used
    half_full = blk_rows_ref[i] <= bm // 2
    pl.when(used & jnp.logical_not(half_full))(functools.partial(swiglu_rows, bm))
    pl.when(used & half_full)(functools.partial(swiglu_rows, bm // 2))


def _experts(blk_exp, n_used, blk_rows, xb_tiles, wgu, bgu, wd, bd):
    d = wgu.shape[1]
    chunks = d // LANES
    n_blocks = xb_tiles.shape[0] // (EXPERT_BLOCK * chunks)
    two_f = wgu.shape[2]
    dff = wd.shape[1]
    blk = lambda i, be, nu, br: (jnp.minimum(i, nu[0] - 1), 0)
    exp3 = lambda i, be, nu, br: (be[jnp.minimum(i, nu[0] - 1)], 0, 0)
    grid_spec = pltpu.PrefetchScalarGridSpec(
        num_scalar_prefetch=3,
        grid=(n_blocks,),
        in_specs=[pl.BlockSpec((EXPERT_BLOCK * chunks, LANES), blk),
                  pl.BlockSpec((1, d, two_f), exp3),
                  pl.BlockSpec((1, 1, two_f), exp3),
                  pl.BlockSpec((1, dff, d), exp3),
                  pl.BlockSpec((1, 1, d), exp3)],
        out_specs=pl.BlockSpec((EXPERT_BLOCK * chunks, LANES), blk),
        scratch_shapes=[pltpu.VMEM((d, two_f), BF16), pltpu.VMEM((dff, d), BF16)])
    return pl.pallas_call(
        _expert_kernel,
        grid_spec=grid_spec,
        out_shape=jax.ShapeDtypeStruct(xb_tiles.shape, F32),
        compiler_params=pltpu.CompilerParams(
            dimension_semantics=("arbitrary",), vmem_limit_bytes=VMEM_LIMIT),
        name="experts",
    )(blk_exp, n_used, blk_rows, xb_tiles, wgu, bgu, wd, bd)


def _combine_kernel(y0_ref, y1_ref, y2_ref, y3_ref, gate_ref, x2_ref, fw_ref, out_ref):
    gates = gate_ref[...]
    lane = lax.broadcasted_iota(I32, gates.shape, 1)
    x3 = x2_ref[...]
    for kk, y_ref in enumerate((y0_ref, y1_ref, y2_ref, y3_ref)):
        gk = jnp.sum(jnp.where(lane == kk, gates, 0.0), axis=-1, keepdims=True)
        x3 = x3 + gk * _load_token_tiles(y_ref, x3.shape[0])
    out_ref[...] = x3 * lax.rsqrt(jnp.mean(x3 * x3, axis=-1, keepdims=True) + NORM_EPS) * fw_ref[...]


def _combine_into_kernel(y0_ref, y1_ref, y2_ref, y3_ref, gate_ref, x2_ref, fw_ref, prev_ref, out_ref):
    del prev_ref
    _combine_kernel(y0_ref, y1_ref, y2_ref, y3_ref, gate_ref, x2_ref, fw_ref, out_ref)


def _combine(y4, gates, x2, fw, out_prev, chunk, *, tile):
    t, d = x2.shape
    tc = y4.shape[0] // (TOP_K * (d // LANES))
    steps = tc // tile
    first = chunk * steps
    choice = lambda kk: pl.BlockSpec((tile * (d // LANES), LANES), lambda i: (kk * steps + i, 0))
    in_specs = ([choice(kk) for kk in range(TOP_K)]
                + [pl.BlockSpec((tile, LANES), lambda i: (first + i, 0)),
                   pl.BlockSpec((tile, d), lambda i: (first + i, 0)),
                   pl.BlockSpec((1, d), lambda i: (0, 0))])
    args = [y4, y4, y4, y4, gates, x2, fw]
    if out_prev is not None:
        in_specs.append(pl.BlockSpec(memory_space=pl.ANY))
        args.append(out_prev)
    return pl.pallas_call(
        _combine_kernel if out_prev is None else _combine_into_kernel,
        grid=(steps,),
        in_specs=in_specs,
        out_specs=pl.BlockSpec((tile, d), lambda i: (first + i, 0)),
        out_shape=jax.ShapeDtypeStruct((t, d), F32),
        input_output_aliases={} if out_prev is None else {len(args) - 1: 0},
        compiler_params=pltpu.CompilerParams(
            dimension_semantics=("arbitrary",), vmem_limit_bytes=VMEM_LIMIT),
        name="combine",
    )(*args)


def _pad_lanes(a, offset=0, fill=0.0):
    out = jnp.full((1, LANES), fill, a.dtype)
    return out.at[0, offset:offset + a.shape[0]].set(a)


def _layer(x, attn_norm_w, w_in, gdn_conv_w, gdn_a_log, gdn_dt_bias, gdn_norm_w,
           cf_dw_w, cf_dw_b, cf_ln_w, cf_ln_b, w_out, ffn_norm_w, w_router, b_router,
           w_gate_up, b_gate_up, w_down, b_down, final_norm_w, apply_final):
    b, s, d = x.shape
    t = b * s
    assert d == SUBLANES * LANES, "the token-tile layout needs one (8, 128) tile per token row"
    qk = GDN_HEADS * HEAD_DIM
    cfc = cf_dw_w.shape[1]
    off_b = 4 * qk
    off_cf = off_b + 2 * GDN_HEADS

    wm = w_in[:, :off_b].astype(BF16)
    wba = jnp.zeros((d, LANES), F32).at[:, :2 * GDN_HEADS].set(w_in[:, off_b:off_cf]).astype(BF16)
    wcf = w_in[:, off_cf:].astype(BF16)
    alog = _pad_lanes(gdn_a_log, GDN_HEADS)
    dtb = _pad_lanes(gdn_dt_bias, GDN_HEADS)

    q, k, v, z, bg, bgt, u = _inproj(
        x, attn_norm_w[None, :], wm, wba, wcf, gdn_conv_w, alog, dtb,
        cf_dw_w, cf_dw_b[None, :], cf_ln_w[None, :], cf_ln_b[None, :], tile=min(512, s))
    o = _gdn(q, k, v, bg, bgt, tile=min(512, s))

    wr = jnp.zeros((d, LANES), F32).at[:, :N_EXPERTS].set(w_router)
    br = _pad_lanes(b_router)
    x2, xn, route, gates, counts = _outproj_router(
        o.reshape(t, qk), z.reshape(t, qk), u.reshape(t, cfc), x.reshape(t, d),
        gdn_norm_w[None, :], w_out[:qk].astype(BF16), w_out[qk:].astype(BF16),
        ffn_norm_w[None, :], wr, br, tile=min(512, t))

    cnt = counts[0, :N_EXPERTS]
    nblk = (cnt + EXPERT_BLOCK - 1) // EXPERT_BLOCK
    blk_end = jnp.cumsum(nblk)
    pstart = (blk_end - nblk) * EXPERT_BLOCK
    n_blocks = (t * TOP_K) // EXPERT_BLOCK + N_EXPERTS
    blk_ids = jnp.arange(n_blocks, dtype=I32)
    blk_exp = jnp.minimum(
        jnp.sum((blk_end[None, :] <= blk_ids[:, None]).astype(I32), axis=1), N_EXPERTS - 1)
    n_used = blk_end[-1:].astype(I32)
    blk_rows = jnp.clip((pstart + cnt)[blk_exp] - blk_ids * EXPERT_BLOCK, 0, EXPERT_BLOCK).astype(I32)

    dest = _dest(route, _pad_lanes(pstart.astype(I32)), tile=min(2048, t))
    n_rows = n_blocks * EXPERT_BLOCK
    chunks = d // LANES
    windows = t // ROW_WINDOW
    dest_win = dest.reshape(windows, ROW_WINDOW, TOP_K).transpose(0, 2, 1).reshape(windows, TOP_K * ROW_WINDOW)
    xb = _dispatch_rows(xn.reshape(t, chunks, LANES), dest_win, n_rows)
    yb = _experts(blk_exp, n_used, blk_rows, xb.reshape(n_rows * chunks, LANES), w_gate_up, b_gate_up[:, None, :],
                  w_down, b_down[:, None, :]).reshape(n_rows, chunks, LANES)

    tc = t // COMBINE_CHUNKS
    out = None
    for c in range(COMBINE_CHUNKS):
        src = dest[c * tc:(c + 1) * tc].T.reshape(TOP_K * tc // ROW_WINDOW, ROW_WINDOW)
        src_win = jnp.pad(src, ((0, 0), (0, LANES - ROW_WINDOW)))
        y4 = _collect_rows(yb, src_win).reshape(TOP_K * tc * chunks, LANES)
        out = _combine(y4, gates, x2, final_norm_w[None, :], out, c, tile=min(EXPERT_BLOCK, tc))
    return out.reshape(b, s, d)


def kernel(x, attn_norm_w, w_in, gdn_conv_w, gdn_a_log, gdn_dt_bias, gdn_norm_w, cf_dw_w, cf_dw_b,
           cf_ln_w, cf_ln_b, w_out, ffn_norm_w, w_router, b_router, w_gate_up, b_gate_up, w_down,
           b_down, final_norm_w):
    depth = w_in.shape[0]
    assert depth == 1, "the fused final norm assumes a single trunk layer"
    return _layer(x, attn_norm_w[0], w_in[0], gdn_conv_w[0], gdn_a_log[0], gdn_dt_bias[0],
                  gdn_norm_w[0], cf_dw_w[0], cf_dw_b[0], cf_ln_w[0], cf_ln_b[0], w_out[0],
                  ffn_norm_w[0], w_router[0], b_router[0], w_gate_up[0], b_gate_up[0], w_down[0],
                  b_down[0], final_norm_w, True)
```

```python
import functools

import jax
import jax.numpy as jnp
from jax import lax
from jax.experimental import pallas as pl
from jax.experimental.pallas import tpu as pltpu
from jax.experimental.pallas import tpu_sc as plsc

F32 = jnp.float32
BF16 = jnp.bfloat16
I32 = jnp.int32

NORM_EPS = 1e-6
LANES = 128
SUBLANES = 8
GDN_HEADS = 4
HEAD_DIM = 128
GDN_CHUNK = 64
GDN_CONV = 4
CF_KERNEL = 31
N_EXPERTS = 32
TOP_K = 4
SWIGLU_LIMIT = 7.0
SWIGLU_ALPHA = 1.702

QKV_HALO = 8
CF_HALO = 32
EXPERT_BLOCK = 512
VMEM_LIMIT = 56 * 1024 * 1024


def _silu(x):
    return x * jax.nn.sigmoid(x)


def _dot(a, b):
    return jnp.dot(a, b, preferred_element_type=F32)


def _dot_nt(a, b):
    return lax.dot_general(a, b, (((1,), (1,)), ((), ())), preferred_element_type=F32)


def _dot_tn(a, b):
    return lax.dot_general(a, b, (((0,), (0,)), ((), ())), preferred_element_type=F32)


def _store_token_tiles(ref, x, lead=()):
    rows, d = x.shape
    chunks = d // LANES
    for c in range(chunks):
        ref[lead + (pl.ds(c, rows, stride=chunks), slice(None))] = x[:, c * LANES:(c + 1) * LANES]


def _load_token_tiles(ref, rows, lead=()):
    chunks = ref.shape[-2] // rows
    return jnp.concatenate(
        [ref[lead + (pl.ds(c, rows, stride=chunks), slice(None))] for c in range(chunks)], axis=1)


def _inproj_kernel(x_ref, nw_ref, wm_ref, wba_ref, wcf_ref, cw_ref, alog_ref, dtb_ref,
                   dww_ref, dwb_ref, lnw_ref, lnb_ref,
                   q_ref, k_ref, v_ref, z_ref, bg_ref, bgt_ref, u_ref,
                   qkv_buf, cf_buf, cf_shift):
    tt = x_ref.shape[1]
    qk = GDN_HEADS * HEAD_DIM
    cfc = u_ref.shape[2]

    @pl.when(pl.program_id(1) == 0)
    def _():
        qkv_buf[0:QKV_HALO, :] = jnp.zeros((QKV_HALO, qkv_buf.shape[1]), F32)
        cf_buf[0:CF_HALO, :] = jnp.zeros((CF_HALO, cf_buf.shape[1]), F32)

    x = x_ref[0]
    h = x * lax.rsqrt(jnp.mean(x * x, axis=-1, keepdims=True) + NORM_EPS) * nw_ref[...]
    h = h.astype(BF16)
    pm = _dot(h, wm_ref[...])
    pba = _dot(h, wba_ref[...])
    pcf = _dot(h, wcf_ref[...])

    z_ref[0] = pm[:, 3 * qk:]

    qkv_buf[QKV_HALO:QKV_HALO + tt, :] = pm[:, :3 * qk]
    acc = None
    for j in range(GDN_CONV):
        term = cw_ref[j:j + 1, :] * qkv_buf[pl.ds(QKV_HALO - (GDN_CONV - 1) + j, tt), :]
        acc = term if acc is None else acc + term
    qkv_buf[0:QKV_HALO, :] = qkv_buf[tt:tt + QKV_HALO, :]
    qkv = _silu(acc)
    for hd in range(GDN_HEADS):
        for base, ref in ((0, q_ref), (qk, k_ref)):
            t = qkv[:, base + hd * HEAD_DIM: base + (hd + 1) * HEAD_DIM]
            t = t * lax.rsqrt(jnp.sum(t * t, axis=-1, keepdims=True) + NORM_EPS)
            ref[0, :, hd * HEAD_DIM:(hd + 1) * HEAD_DIM] = t
    v_ref[0] = qkv[:, 2 * qk:]

    lane = lax.broadcasted_iota(I32, (tt, LANES), 1)
    row = lax.broadcasted_iota(I32, (tt, LANES), 0)
    beta = jax.nn.sigmoid(pba)
    sp_in = pba + dtb_ref[...]
    softplus = jnp.maximum(sp_in, 0.0) + jnp.log(1.0 + jnp.exp(-jnp.abs(sp_in)))
    g = -jnp.exp(alog_ref[...]) * softplus
    g = jnp.where((lane >= GDN_HEADS) & (lane < 2 * GDN_HEADS), g, 0.0)
    pos = row % GDN_CHUNK
    shift = 1
    while shift < GDN_CHUNK:
        g = g + jnp.where(pos >= shift, pltpu.roll(g, shift, 0), 0.0)
        shift *= 2
    bg = jnp.where(lane < GDN_HEADS, beta, g)
    bg_ref[0] = bg
    bgt_ref[0] = jnp.transpose(bg)[0:SUBLANES, :]

    glu = pcf[:, :cfc] * jax.nn.sigmoid(pcf[:, cfc:])
    cf_buf[CF_HALO:CF_HALO + tt, :] = glu
    lo = SUBLANES
    span = tt + CF_HALO - lo
    for r in range(1, SUBLANES):
        cf_shift[r - 1, lo:lo + span, :] = cf_buf[pl.ds(lo - r, span), :]
    rows = 64
    for r0 in range(0, tt, rows):
        acc = None
        for j in range(CF_KERNEL):
            a, r = divmod(CF_KERNEL - 1 - j, SUBLANES)
            start = CF_HALO + r0 - a * SUBLANES
            src = cf_buf[start:start + rows, :] if r == 0 else cf_shift[r - 1, start:start + rows, :]
            term = dww_ref[j:j + 1, :] * src
            acc = term if acc is None else acc + term
        c = acc + dwb_ref[...]
        mu = jnp.mean(c, axis=-1, keepdims=True)
        cc = c - mu
        y = cc * lax.rsqrt(jnp.mean(cc * cc, axis=-1, keepdims=True) + NORM_EPS)
        u_ref[0, r0:r0 + rows, :] = _silu(y * lnw_ref[...] + lnb_ref[...])
    cf_buf[0:CF_HALO, :] = cf_buf[tt:tt + CF_HALO, :]


def _inproj(x, nw, wm, wba, wcf, cw, alog, dtb, dww, dwb, lnw, lnb, *, tile):
    b, s, d = x.shape
    qk = GDN_HEADS * HEAD_DIM
    cfc = dww.shape[1]
    grid = (b, s // tile)
    full = lambda a: pl.BlockSpec(a.shape, lambda i, j: (0,) * a.ndim)
    tok = lambda w: pl.BlockSpec((1, tile, w), lambda i, j: (i, j, 0))
    out_shape = (
        jax.ShapeDtypeStruct((b, s, qk), F32), jax.ShapeDtypeStruct((b, s, qk), F32),
        jax.ShapeDtypeStruct((b, s, qk), F32), jax.ShapeDtypeStruct((b, s, qk), F32),
        jax.ShapeDtypeStruct((b, s, LANES), F32), jax.ShapeDtypeStruct((b, SUBLANES, s), F32),
        jax.ShapeDtypeStruct((b, s, cfc), F32))
    return pl.pallas_call(
        _inproj_kernel,
        grid=grid,
        in_specs=[tok(d)] + [full(a) for a in (nw, wm, wba, wcf, cw, alog, dtb, dww, dwb, lnw, lnb)],
        out_specs=(tok(qk), tok(qk), tok(qk), tok(qk), tok(LANES),
                   pl.BlockSpec((1, SUBLANES, tile), lambda i, j: (i, 0, j)), tok(cfc)),
        out_shape=out_shape,
        scratch_shapes=[pltpu.VMEM((QKV_HALO + tile, 3 * qk), F32),
                        pltpu.VMEM((CF_HALO + tile, cfc), F32),
                        pltpu.VMEM((SUBLANES - 1, CF_HALO + tile, cfc), F32)],
        compiler_params=pltpu.CompilerParams(
            dimension_semantics=("arbitrary", "arbitrary"), vmem_limit_bytes=VMEM_LIMIT),
        name="inproj",
    )(x, nw, wm, wba, wcf, cw, alog, dtb, dww, dwb, lnw, lnb)


def _bmm(a, b):
    return jnp.einsum("bmk,bkn->bmn", a, b, preferred_element_type=F32)


def _bmm_nt(a, b):
    return jnp.einsum("bmk,bnk->bmn", a, b, preferred_element_type=F32)


def _bmm_tn(a, b):
    return jnp.einsum("bkm,bkn->bmn", a, b, preferred_element_type=F32)


def _unit_lower_inverse(a):
    c = a.shape[-1]
    ii = lax.broadcasted_iota(I32, (c, c), 0)
    jj = lax.broadcasted_iota(I32, (c, c), 1)
    eye = (ii == jj).astype(F32)
    same16 = (ii // 16) == (jj // 16)
    same32 = (ii // 32) == (jj // 32)
    x = jnp.where(same16, -a, 0.0)
    t = eye + x
    xp = x
    for _ in range(3):
        xp_b = xp.astype(BF16)
        xp = _bmm(xp_b, xp_b)
        t = t + _bmm(t.astype(BF16), xp.astype(BF16))
    for off in (jnp.where(same32 & ~same16, a, 0.0), jnp.where(~same32, a, 0.0)):
        tb = t.astype(BF16)
        t = t - _bmm(tb, _bmm(off.astype(BF16), tb).astype(BF16))
    return t


def _gdn_kernel(q_ref, k_ref, v_ref, bg_ref, bgt_ref, o_ref, state, s_all):
    lt = q_ref.shape[1]
    c = GDN_CHUNK
    nh = GDN_HEADS
    nc = lt // c

    @pl.when(pl.program_id(1) == 0)
    def _():
        state[...] = jnp.zeros(state.shape, F32)

    def stack(fn):
        return jnp.stack([fn(slice(n * c, (n + 1) * c), h) for n in range(nc) for h in range(nh)])

    head = lambda h: slice(h * HEAD_DIM, (h + 1) * HEAD_DIM)
    q = stack(lambda r, h: q_ref[0, r, head(h)]) * (HEAD_DIM ** -0.5)
    k = stack(lambda r, h: k_ref[0, r, head(h)])
    v = stack(lambda r, h: v_ref[0, r, head(h)])
    beta = stack(lambda r, h: bg_ref[0, r, h:h + 1])
    gcol = stack(lambda r, h: bg_ref[0, r, nh + h:nh + h + 1])
    grow = stack(lambda r, h: bgt_ref[0, nh + h:nh + h + 1, r])

    ii = lax.broadcasted_iota(I32, (c, c), 0)
    jj = lax.broadcasted_iota(I32, (c, c), 1)
    glast = gcol[:, c - 1:c, :]
    eg = jnp.exp(gcol)
    decay = jnp.where(ii >= jj, jnp.exp(jnp.minimum(gcol - grow, 0.0)), 0.0)
    kb = k * beta
    k_b = k.astype(BF16)
    a = jnp.where(ii > jj, _bmm_nt(kb.astype(BF16), k_b) * decay, 0.0)
    t = _unit_lower_inverse(a)
    rhs = jnp.concatenate([v * beta, kb * eg], axis=-1).astype(BF16)
    sol = _bmm(t.astype(BF16), rhs)
    u_val = sol[..., :HEAD_DIM]
    w_key = sol[..., HEAD_DIM:]
    intra = _bmm_nt(q.astype(BF16), k_b) * decay
    k_tail = (k * jnp.exp(glast - gcol)).astype(BF16)
    upd = _bmm_tn(k_tail, sol.astype(BF16))
    b_mat = upd[..., :HEAD_DIM]
    p_mat = upd[..., HEAD_DIM:].astype(BF16)
    g_tot = jnp.exp(glast)

    s = state[...]
    for n in range(nc):
        grp = slice(n * nh, (n + 1) * nh)
        s_b = s.astype(BF16)
        s_all[grp] = s_b
        s = s * g_tot[grp] - _bmm(p_mat[grp], s_b) + b_mat[grp]
    state[...] = s

    wq = jnp.concatenate([w_key, q * eg], axis=1).astype(BF16)
    ws_qs = _bmm(wq, s_all[...])
    v_new = u_val - ws_qs[:, :c]
    o = ws_qs[:, c:] + _bmm(intra.astype(BF16), v_new.astype(BF16))
    for n in range(nc):
        for h in range(nh):
            o_ref[0, n * c:(n + 1) * c, head(h)] = o[n * nh + h]


def _gdn(q, k, v, bg, bgt, *, tile):
    b, s, qk = q.shape
    grid = (b, s // tile)
    tok = lambda w: pl.BlockSpec((1, tile, w), lambda i, j: (i, j, 0))
    n_prob = (tile // GDN_CHUNK) * GDN_HEADS
    return pl.pallas_call(
        _gdn_kernel,
        grid=grid,
        in_specs=[tok(qk), tok(qk), tok(qk), tok(LANES),
                  pl.BlockSpec((1, SUBLANES, tile), lambda i, j: (i, 0, j))],
        out_specs=tok(qk),
        out_shape=jax.ShapeDtypeStruct((b, s, qk), F32),
        scratch_shapes=[pltpu.VMEM((GDN_HEADS, HEAD_DIM, HEAD_DIM), F32),
                        pltpu.VMEM((n_prob, HEAD_DIM, HEAD_DIM), BF16)],
        compiler_params=pltpu.CompilerParams(
            dimension_semantics=("arbitrary", "arbitrary"), vmem_limit_bytes=VMEM_LIMIT),
        name="gdn",
    )(q, k, v, bg, bgt)


def _split_bf16(x):
    hi = x.astype(BF16)
    lo = (x - hi.astype(F32)).astype(BF16)
    return hi, lo


def _outproj_router_kernel(o_ref, z_ref, u_ref, x_ref, gnw_ref, wa_ref, wb_ref, fnw_ref,
                           wr_ref, br_ref,
                           x2_ref, xn_ref, route_ref, gate_ref, counts_ref, carry):
    tt = x_ref.shape[0]

    @pl.when(pl.program_id(0) == 0)
    def _():
        carry[...] = jnp.zeros(carry.shape, F32)

    parts = []
    for hd in range(GDN_HEADS):
        sl = slice(hd * HEAD_DIM, (hd + 1) * HEAD_DIM)
        oh = o_ref[:, sl]
        y = oh * lax.rsqrt(jnp.mean(oh * oh, axis=-1, keepdims=True) + NORM_EPS) * gnw_ref[...]
        parts.append((y * _silu(z_ref[:, sl])).astype(BF16))
    out_a = jnp.concatenate(parts, axis=-1)
    x2 = x_ref[...] + _dot(out_a, wa_ref[...]) + _dot(u_ref[...].astype(BF16), wb_ref[...])
    x2_ref[...] = x2

    xn = x2 * lax.rsqrt(jnp.mean(x2 * x2, axis=-1, keepdims=True) + NORM_EPS) * fnw_ref[...]
    _store_token_tiles(xn_ref, xn)

    xh, xl = _split_bf16(xn)
    wh, wl = _split_bf16(wr_ref[...])
    logits = _dot(xh, wh) + _dot(xh, wl) + _dot(xl, wh) + br_ref[...]

    lane = lax.broadcasted_iota(I32, (tt, LANES), 1)
    lane_f = lane.astype(F32)
    neg = jnp.float32(-jnp.inf)
    work = jnp.where(lane < N_EXPERTS, logits, neg)
    vals, idxs = [], []
    onehot = jnp.zeros((tt, LANES), F32)
    for _ in range(TOP_K):
        m = jnp.max(work, axis=-1, keepdims=True)
        idx = jnp.min(jnp.where(work == m, lane_f, float(LANES)), axis=-1, keepdims=True).astype(I32)
        sel = lane == idx
        vals.append(m)
        idxs.append(idx)
        onehot = onehot + sel.astype(F32)
        work = jnp.where(sel, neg, work)
    exps = [jnp.exp(v - vals[0]) for v in vals]
    denom = exps[0] + exps[1] + exps[2] + exps[3]

    ri = lax.broadcasted_iota(I32, (tt, tt), 0)
    ci = lax.broadcasted_iota(I32, (tt, tt), 1)
    strict = (ri > ci).astype(BF16)
    base = carry[...] + _dot(strict, onehot.astype(BF16))
    route = jnp.zeros((tt, LANES), I32)
    gates = jnp.zeros((tt, LANES), F32)
    for kk in range(TOP_K):
        rank = jnp.sum(jnp.where(lane == idxs[kk], base, 0.0), axis=-1, keepdims=True)
        route = jnp.where(lane == kk, idxs[kk], route)
        route = jnp.where(lane == kk + TOP_K, rank.astype(I32), route)
        gates = jnp.where(lane == kk, exps[kk] / denom, gates)
    route_ref[...] = route
    gate_ref[...] = gates
    new_carry = carry[...] + jnp.sum(onehot, axis=0, keepdims=True)
    carry[...] = new_carry
    counts_ref[...] = new_carry.astype(I32)


def _outproj_router(o, z, u, x, gnw, wa, wb, fnw, wr, br, *, tile):
    t, d = x.shape
    grid = (t // tile,)
    full = lambda a: pl.BlockSpec(a.shape, lambda i: (0,) * a.ndim)
    tok = lambda w: pl.BlockSpec((tile, w), lambda i: (i, 0))
    return pl.pallas_call(
        _outproj_router_kernel,
        grid=grid,
        in_specs=[tok(o.shape[1]), tok(z.shape[1]), tok(u.shape[1]), tok(d)]
                 + [full(a) for a in (gnw, wa, wb, fnw, wr, br)],
        out_specs=(tok(d), pl.BlockSpec((tile * (d // LANES), LANES), lambda i: (i, 0)),
                   tok(LANES), tok(LANES), pl.BlockSpec((1, LANES), lambda i: (0, 0))),
        out_shape=(jax.ShapeDtypeStruct((t, d), F32), jax.ShapeDtypeStruct((t * (d // LANES), LANES), F32),
                   jax.ShapeDtypeStruct((t, LANES), I32), jax.ShapeDtypeStruct((t, LANES), F32),
                   jax.ShapeDtypeStruct((1, LANES), I32)),
        scratch_shapes=[pltpu.VMEM((1, LANES), F32)],
        compiler_params=pltpu.CompilerParams(
            dimension_semantics=("arbitrary",), vmem_limit_bytes=VMEM_LIMIT),
        name="outproj_router",
    )(o, z, u, x, gnw, wa, wb, fnw, wr, br)


def _dest_kernel(route_ref, pstart_ref, dest_ref):
    route = route_ref[...].astype(F32)
    tt = route.shape[0]
    lane = lax.broadcasted_iota(I32, (tt, LANES), 1)
    pstart = pstart_ref[...].astype(F32)
    dest = jnp.zeros((tt, LANES), F32)
    for kk in range(TOP_K):
        idx = jnp.sum(jnp.where(lane == kk, route, 0.0), axis=-1, keepdims=True)
        rank = jnp.sum(jnp.where(lane == kk + TOP_K, route, 0.0), axis=-1, keepdims=True)
        start = jnp.sum(jnp.where(lane == idx.astype(I32), pstart, 0.0), axis=-1, keepdims=True)
        dest = jnp.where(lane == kk, start + rank, dest)
    dest_ref[...] = dest[:, :TOP_K].astype(I32)


def _dest(route, pstart, *, tile):
    t = route.shape[0]
    return pl.pallas_call(
        _dest_kernel,
        grid=(t // tile,),
        in_specs=[pl.BlockSpec((tile, LANES), lambda i: (i, 0)),
                  pl.BlockSpec((1, LANES), lambda i: (0, 0))],
        out_specs=pl.BlockSpec((tile, TOP_K), lambda i: (i, 0)),
        out_shape=jax.ShapeDtypeStruct((t, TOP_K), I32),
        compiler_params=pltpu.CompilerParams(dimension_semantics=("arbitrary",)),
        name="dest_rows",
    )(route, pstart)


ROW_WINDOW = 32
COMBINE_CHUNKS = 8


def _sc_mesh():
    return plsc.VectorSubcoreMesh(core_axis_name="core", subcore_axis_name="subcore")


def _dispatch_rows(xn_tiles, dest_win, n_rows):
    t = xn_tiles.shape[0]

    @functools.partial(
        pl.kernel, mesh=_sc_mesh(), scratch_types=[],
        out_type=jax.ShapeDtypeStruct((n_rows,) + xn_tiles.shape[1:], xn_tiles.dtype))
    def dispatch(x_hbm, idx_hbm, o_hbm):
        def body(x_vmem, idx_vmem):
            for kk in range(TOP_K):
                pltpu.sync_copy(x_vmem, o_hbm.at[idx_vmem.at[0, pl.ds(kk * ROW_WINDOW, ROW_WINDOW)]])

        pltpu.emit_pipeline(
            body,
            grid=(t // ROW_WINDOW,),
            in_specs=[pl.BlockSpec((ROW_WINDOW,) + xn_tiles.shape[1:], lambda i: (i, 0, 0)),
                      pl.BlockSpec((1, TOP_K * ROW_WINDOW), lambda i: (i, 0))],
            out_specs=[],
            core_axis_name=("core", "subcore"),
            dimension_semantics=(pltpu.PARALLEL,),
        )(x_hbm, idx_hbm)

    return dispatch(xn_tiles, dest_win)


def _collect_rows(yb_tiles, src_win):
    n_pairs = src_win.shape[0] * ROW_WINDOW

    @functools.partial(
        pl.kernel, mesh=_sc_mesh(), scratch_types=[],
        out_type=jax.ShapeDtypeStruct((n_pairs,) + yb_tiles.shape[1:], yb_tiles.dtype))
    def collect(y_hbm, idx_hbm, o_hbm):
        def body(idx_vmem, o_vmem):
            pltpu.sync_copy(y_hbm.at[idx_vmem.at[0, pl.ds(0, ROW_WINDOW)]], o_vmem)

        pltpu.emit_pipeline(
            body,
            grid=(n_pairs // ROW_WINDOW,),
            in_specs=[pl.BlockSpec((1, LANES), lambda i: (i, 0))],
            out_specs=[pl.BlockSpec((ROW_WINDOW,) + yb_tiles.shape[1:], lambda i: (i, 0, 0))],
            core_axis_name=("core", "subcore"),
            dimension_semantics=(pltpu.PARALLEL,),
        )(idx_hbm, o_hbm)

    return collect(yb_tiles, src_win)


def _expert_kernel(blk_exp_ref, n_used_ref, next_exp_ref, slot_ref,
                   xb_ref, wgu_hbm, bgu_ref, wd_hbm, bd_ref, yb_ref,
                   wgu_f, wd_f, wgu_b, wd_b, wsem):
    i = pl.program_id(0)
    n_used = n_used_ref[0]
    bm = xb_ref.shape[0] // SUBLANES
    dff = wd_f.shape[1]

    def weight_copies(e, s):
        return (pltpu.make_async_copy(wgu_hbm.at[e], wgu_f.at[s], wsem.at[0, s]),
                pltpu.make_async_copy(wd_hbm.at[e], wd_f.at[s], wsem.at[1, s]))

    prev = blk_exp_ref[jnp.maximum(i - 1, 0)]
    first_of_expert = (i == 0) | (blk_exp_ref[i] != prev)

    @pl.when(first_of_expert & (i < n_used))
    def _():
        e = blk_exp_ref[i]
        s = slot_ref[i]

        @pl.when(i == 0)
        def _():
            for cp in weight_copies(e, s):
                cp.start()

        for cp in weight_copies(e, s):
            cp.wait()
        wgu_b[...] = wgu_f[s].astype(BF16)
        wd_b[...] = wd_f[s].astype(BF16)

        nxt = next_exp_ref[i]

        @pl.when(nxt >= 0)
        def _():
            for cp in weight_copies(nxt, 1 - s):
                cp.start()

    @pl.when(i < n_used)
    def _():
        xb = _load_token_tiles(xb_ref, bm).astype(BF16)
        hid = _dot(xb, wgu_b[...]) + bgu_ref[0]
        gate = jnp.minimum(hid[:, :dff], SWIGLU_LIMIT)
        up = jnp.clip(hid[:, dff:], -SWIGLU_LIMIT, SWIGLU_LIMIT)
        glu = gate * jax.nn.sigmoid(SWIGLU_ALPHA * gate)
        act = ((up + 1.0) * glu).astype(BF16)
        _store_token_tiles(yb_ref, _dot(act, wd_b[...]) + bd_ref[0])


def _experts(blk_exp, n_used, next_exp, slot, xb_tiles, wgu, bgu, wd, bd):
    d = wgu.shape[1]
    chunks = d // LANES
    n_blocks = xb_tiles.shape[0] // (EXPERT_BLOCK * chunks)
    two_f = wgu.shape[2]
    dff = wd.shape[1]
    blk = lambda i, be, nu, ne, sl: (jnp.minimum(i, nu[0] - 1), 0)
    exp3 = lambda i, be, nu, ne, sl: (be[jnp.minimum(i, nu[0] - 1)], 0, 0)
    grid_spec = pltpu.PrefetchScalarGridSpec(
        num_scalar_prefetch=4,
        grid=(n_blocks,),
        in_specs=[pl.BlockSpec((EXPERT_BLOCK * chunks, LANES), blk),
                  pl.BlockSpec(memory_space=pl.ANY),
                  pl.BlockSpec((1, 1, two_f), exp3),
                  pl.BlockSpec(memory_space=pl.ANY),
                  pl.BlockSpec((1, 1, d), exp3)],
        out_specs=pl.BlockSpec((EXPERT_BLOCK * chunks, LANES), blk),
        scratch_shapes=[pltpu.VMEM((2, d, two_f), F32), pltpu.VMEM((2, dff, d), F32),
                        pltpu.VMEM((d, two_f), BF16), pltpu.VMEM((dff, d), BF16),
                        pltpu.SemaphoreType.DMA((2, 2))])
    return pl.pallas_call(
        _expert_kernel,
        grid_spec=grid_spec,
        out_shape=jax.ShapeDtypeStruct(xb_tiles.shape, F32),
        compiler_params=pltpu.CompilerParams(
            dimension_semantics=("arbitrary",), vmem_limit_bytes=VMEM_LIMIT),
        name="experts",
    )(blk_exp, n_used, next_exp, slot, xb_tiles, wgu, bgu, wd, bd)


def _combine_kernel(y0_ref, y1_ref, y2_ref, y3_ref, gate_ref, x2_ref, fw_ref, out_ref):
    gates = gate_ref[...]
    lane = lax.broadcasted_iota(I32, gates.shape, 1)
    x3 = x2_ref[...]
    for kk, y_ref in enumerate((y0_ref, y1_ref, y2_ref, y3_ref)):
        gk = jnp.sum(jnp.where(lane == kk, gates, 0.0), axis=-1, keepdims=True)
        x3 = x3 + gk * _load_token_tiles(y_ref, x3.shape[0])
    out_ref[...] = x3 * lax.rsqrt(jnp.mean(x3 * x3, axis=-1, keepdims=True) + NORM_EPS) * fw_ref[...]


def _combine_into_kernel(y0_ref, y1_ref, y2_ref, y3_ref, gate_ref, x2_ref, fw_ref, prev_ref, out_ref):
    del prev_ref
    _combine_kernel(y0_ref, y1_ref, y2_ref, y3_ref, gate_ref, x2_ref, fw_ref, out_ref)


def _combine(y4, gates, x2, fw, out_prev, chunk, *, tile):
    t, d = x2.shape
    tc = y4.shape[0] // (TOP_K * (d // LANES))
    steps = tc // tile
    first = chunk * steps
    choice = lambda kk: pl.BlockSpec((tile * (d // LANES), LANES), lambda i: (kk * steps + i, 0))
    in_specs = ([choice(kk) for kk in range(TOP_K)]
                + [pl.BlockSpec((tile, LANES), lambda i: (first + i, 0)),
                   pl.BlockSpec((tile, d), lambda i: (first + i, 0)),
                   pl.BlockSpec((1, d), lambda i: (0, 0))])
    args = [y4, y4, y4, y4, gates, x2, fw]
    if out_prev is not None:
        in_specs.append(pl.BlockSpec(memory_space=pl.ANY))
        args.append(out_prev)
    return pl.pallas_call(
        _combine_kernel if out_prev is None else _combine_into_kernel,
        grid=(steps,),
        in_specs=in_specs,
        out_specs=pl.BlockSpec((tile, d), lambda i: (first + i, 0)),
        out_shape=jax.ShapeDtypeStruct((t, d), F32),
        input_output_aliases={} if out_prev is None else {len(args) - 1: 0},
        compiler_params=pltpu.CompilerParams(
            dimension_semantics=("arbitrary",), vmem_limit_bytes=VMEM_LIMIT),
        name="combine",
    )(*args)


def _pad_lanes(a, offset=0, fill=0.0):
    out = jnp.full((1, LANES), fill, a.dtype)
    return out.at[0, offset:offset + a.shape[0]].set(a)


def _layer(x, attn_norm_w, w_in, gdn_conv_w, gdn_a_log, gdn_dt_bias, gdn_norm_w,
           cf_dw_w, cf_dw_b, cf_ln_w, cf_ln_b, w_out, ffn_norm_w, w_router, b_router,
           w_gate_up, b_gate_up, w_down, b_down, final_norm_w, apply_final):
    b, s, d = x.shape
    t = b * s
    assert d == SUBLANES * LANES, "the token-tile layout needs one (8, 128) tile per token row"
    qk = GDN_HEADS * HEAD_DIM
    cfc = cf_dw_w.shape[1]
    off_b = 4 * qk
    off_cf = off_b + 2 * GDN_HEADS

    wm = w_in[:, :off_b].astype(BF16)
    wba = jnp.zeros((d, LANES), F32).at[:, :2 * GDN_HEADS].set(w_in[:, off_b:off_cf]).astype(BF16)
    wcf = w_in[:, off_cf:].astype(BF16)
    alog = _pad_lanes(gdn_a_log, GDN_HEADS)
    dtb = _pad_lanes(gdn_dt_bias, GDN_HEADS)

    q, k, v, z, bg, bgt, u = _inproj(
        x, attn_norm_w[None, :], wm, wba, wcf, gdn_conv_w, alog, dtb,
        cf_dw_w, cf_dw_b[None, :], cf_ln_w[None, :], cf_ln_b[None, :], tile=min(512, s))
    o = _gdn(q, k, v, bg, bgt, tile=min(512, s))

    wr = jnp.zeros((d, LANES), F32).at[:, :N_EXPERTS].set(w_router)
    br = _pad_lanes(b_router)
    x2, xn, route, gates, counts = _outproj_router(
        o.reshape(t, qk), z.reshape(t, qk), u.reshape(t, cfc), x.reshape(t, d),
        gdn_norm_w[None, :], w_out[:qk].astype(BF16), w_out[qk:].astype(BF16),
        ffn_norm_w[None, :], wr, br, tile=min(512, t))

    cnt = counts[0, :N_EXPERTS]
    nblk = (cnt + EXPERT_BLOCK - 1) // EXPERT_BLOCK
    blk_end = jnp.cumsum(nblk)
    pstart = (blk_end - nblk) * EXPERT_BLOCK
    n_blocks = (t * TOP_K) // EXPERT_BLOCK + N_EXPERTS
    blk_ids = jnp.arange(n_blocks, dtype=I32)
    blk_exp = jnp.minimum(
        jnp.sum((blk_end[None, :] <= blk_ids[:, None]).astype(I32), axis=1), N_EXPERTS - 1)
    n_used = blk_end[-1:].astype(I32)
    after = blk_end[blk_exp]
    next_exp = jnp.where(after < n_used[0], blk_exp[jnp.minimum(after, n_blocks - 1)], -1).astype(I32)
    slot = ((jnp.cumsum((nblk > 0).astype(I32)) - 1)[blk_exp] % 2).astype(I32)

    dest = _dest(route, _pad_lanes(pstart.astype(I32)), tile=min(2048, t))
    n_rows = n_blocks * EXPERT_BLOCK
    chunks = d // LANES
    windows = t // ROW_WINDOW
    dest_win = dest.reshape(windows, ROW_WINDOW, TOP_K).transpose(0, 2, 1).reshape(windows, TOP_K * ROW_WINDOW)
    xb = _dispatch_rows(xn.reshape(t, chunks, LANES), dest_win, n_rows)
    yb = _experts(blk_exp, n_used, next_exp, slot, xb.reshape(n_rows * chunks, LANES), w_gate_up, b_gate_up[:, None, :],
                  w_down, b_down[:, None, :]).reshape(n_rows, chunks, LANES)

    tc = t // COMBINE_CHUNKS
    out = None
    for c in range(COMBINE_CHUNKS):
        src = dest[c * tc:(c + 1) * tc].T.reshape(TOP_K * tc // ROW_WINDOW, ROW_WINDOW)
        src_win = jnp.pad(src, ((0, 0), (0, LANES - ROW_WINDOW)))
        y4 = _collect_rows(yb, src_win).reshape(TOP_K * tc * chunks, LANES)
        out = _combine(y4, gates, x2, final_norm_w[None, :], out, c, tile=min(EXPERT_BLOCK, tc))
    return out.reshape(b, s, d)


def kernel(x, attn_norm_w, w_in, gdn_conv_w, gdn_a_log, gdn_dt_bias, gdn_norm_w, cf_dw_w, cf_dw_b,
           cf_ln_w, cf_ln_b, w_out, ffn_norm_w, w_router, b_router, w_gate_up, b_gate_up, w_down,
           b_down, final_norm_w):
    depth = w_in.shape[0]
    assert depth == 1, "the fused final norm assumes a single trunk layer"
    return _layer(x, attn_norm_w[0], w_in[0], gdn_conv_w[0], gdn_a_log[0], gdn_dt_bias[0],
                  gdn_norm_w[0], cf_dw_w[0], cf_dw_b[0], cf_ln_w[0], cf_ln_b[0], w_out[0],
                  ffn_norm_w[0], w_router[0], b_router[0], w_gate_up[0], b_gate_up[0], w_down[0],
                  b_down[0], final_norm_w, True)
```

```python
import functools

import jax
import jax.numpy as jnp
from jax import lax
from jax.experimental import pallas as pl
from jax.experimental.pallas import tpu as pltpu
from jax.experimental.pallas import tpu_sc as plsc

F32 = jnp.float32
BF16 = jnp.bfloat16
I32 = jnp.int32

NORM_EPS = 1e-6
LANES = 128
SUBLANES = 8
GDN_HEADS = 4
HEAD_DIM = 128
GDN_CHUNK = 64
GDN_CONV = 4
CF_KERNEL = 31
N_EXPERTS = 32
TOP_K = 4
SWIGLU_LIMIT = 7.0
SWIGLU_ALPHA = 1.702

QKV_HALO = 8
CF_HALO = 32
EXPERT_BLOCK = 512
VMEM_LIMIT = 56 * 1024 * 1024


def _silu(x):
    return x * jax.nn.sigmoid(x)


def _dot(a, b):
    return jnp.dot(a, b, preferred_element_type=F32)


def _dot_nt(a, b):
    return lax.dot_general(a, b, (((1,), (1,)), ((), ())), preferred_element_type=F32)


def _dot_tn(a, b):
    return lax.dot_general(a, b, (((0,), (0,)), ((), ())), preferred_element_type=F32)


def _store_token_tiles(ref, x, lead=()):
    rows, d = x.shape
    chunks = d // LANES
    for c in range(chunks):
        ref[lead + (pl.ds(c, rows, stride=chunks), slice(None))] = x[:, c * LANES:(c + 1) * LANES]


def _load_token_tiles(ref, rows, lead=()):
    chunks = ref.shape[-2] // rows
    return jnp.concatenate(
        [ref[lead + (pl.ds(c, rows, stride=chunks), slice(None))] for c in range(chunks)], axis=1)


def _inproj_kernel(x_ref, nw_ref, wm_ref, wba_ref, wcf_ref, cw_ref, alog_ref, dtb_ref,
                   dww_ref, dwb_ref, lnw_ref, lnb_ref,
                   q_ref, k_ref, v_ref, z_ref, bg_ref, bgt_ref, u_ref,
                   qkv_buf, cf_buf, cf_shift):
    tt = x_ref.shape[1]
    qk = GDN_HEADS * HEAD_DIM
    cfc = u_ref.shape[2]

    @pl.when(pl.program_id(1) == 0)
    def _():
        qkv_buf[0:QKV_HALO, :] = jnp.zeros((QKV_HALO, qkv_buf.shape[1]), F32)
        cf_buf[0:CF_HALO, :] = jnp.zeros((CF_HALO, cf_buf.shape[1]), F32)

    x = x_ref[0]
    h = x * lax.rsqrt(jnp.mean(x * x, axis=-1, keepdims=True) + NORM_EPS) * nw_ref[...]
    h = h.astype(BF16)
    pm = _dot(h, wm_ref[...])
    pba = _dot(h, wba_ref[...])
    pcf = _dot(h, wcf_ref[...])

    z_ref[0] = pm[:, 3 * qk:]

    qkv_buf[QKV_HALO:QKV_HALO + tt, :] = pm[:, :3 * qk]
    acc = None
    for j in range(GDN_CONV):
        term = cw_ref[j:j + 1, :] * qkv_buf[pl.ds(QKV_HALO - (GDN_CONV - 1) + j, tt), :]
        acc = term if acc is None else acc + term
    qkv_buf[0:QKV_HALO, :] = qkv_buf[tt:tt + QKV_HALO, :]
    qkv = _silu(acc)
    for hd in range(GDN_HEADS):
        for base, ref in ((0, q_ref), (qk, k_ref)):
            t = qkv[:, base + hd * HEAD_DIM: base + (hd + 1) * HEAD_DIM]
            t = t * lax.rsqrt(jnp.sum(t * t, axis=-1, keepdims=True) + NORM_EPS)
            ref[0, :, hd * HEAD_DIM:(hd + 1) * HEAD_DIM] = t
    v_ref[0] = qkv[:, 2 * qk:]

    lane = lax.broadcasted_iota(I32, (tt, LANES), 1)
    row = lax.broadcasted_iota(I32, (tt, LANES), 0)
    beta = jax.nn.sigmoid(pba)
    sp_in = pba + dtb_ref[...]
    softplus = jnp.maximum(sp_in, 0.0) + jnp.log(1.0 + jnp.exp(-jnp.abs(sp_in)))
    g = -jnp.exp(alog_ref[...]) * softplus
    g = jnp.where((lane >= GDN_HEADS) & (lane < 2 * GDN_HEADS), g, 0.0)
    pos = row % GDN_CHUNK
    shift = 1
    while shift < GDN_CHUNK:
        g = g + jnp.where(pos >= shift, pltpu.roll(g, shift, 0), 0.0)
        shift *= 2
    bg = jnp.where(lane < GDN_HEADS, beta, g)
    bg_ref[0] = bg
    bgt_ref[0] = jnp.transpose(bg)[0:SUBLANES, :]

    glu = pcf[:, :cfc] * jax.nn.sigmoid(pcf[:, cfc:])
    cf_buf[CF_HALO:CF_HALO + tt, :] = glu
    lo = SUBLANES
    span = tt + CF_HALO - lo
    for r in range(1, SUBLANES):
        cf_shift[r - 1, lo:lo + span, :] = cf_buf[pl.ds(lo - r, span), :]
    rows = 64
    for r0 in range(0, tt, rows):
        acc = None
        for j in range(CF_KERNEL):
            a, r = divmod(CF_KERNEL - 1 - j, SUBLANES)
            start = CF_HALO + r0 - a * SUBLANES
            src = cf_buf[start:start + rows, :] if r == 0 else cf_shift[r - 1, start:start + rows, :]
            term = dww_ref[j:j + 1, :] * src
            acc = term if acc is None else acc + term
        c = acc + dwb_ref[...]
        mu = jnp.mean(c, axis=-1, keepdims=True)
        cc = c - mu
        y = cc * lax.rsqrt(jnp.mean(cc * cc, axis=-1, keepdims=True) + NORM_EPS)
        u_ref[0, r0:r0 + rows, :] = _silu(y * lnw_ref[...] + lnb_ref[...])
    cf_buf[0:CF_HALO, :] = cf_buf[tt:tt + CF_HALO, :]


def _inproj(x, nw, wm, wba, wcf, cw, alog, dtb, dww, dwb, lnw, lnb, *, tile):
    b, s, d = x.shape
    qk = GDN_HEADS * HEAD_DIM
    cfc = dww.shape[1]
    grid = (b, s // tile)
    full = lambda a: pl.BlockSpec(a.shape, lambda i, j: (0,) * a.ndim)
    tok = lambda w: pl.BlockSpec((1, tile, w), lambda i, j: (i, j, 0))
    out_shape = (
        jax.ShapeDtypeStruct((b, s, qk), F32), jax.ShapeDtypeStruct((b, s, qk), F32),
        jax.ShapeDtypeStruct((b, s, qk), F32), jax.ShapeDtypeStruct((b, s, qk), F32),
        jax.ShapeDtypeStruct((b, s, LANES), F32), jax.ShapeDtypeStruct((b, SUBLANES, s), F32),
        jax.ShapeDtypeStruct((b, s, cfc), F32))
    return pl.pallas_call(
        _inproj_kernel,
        grid=grid,
        in_specs=[tok(d)] + [full(a) for a in (nw, wm, wba, wcf, cw, alog, dtb, dww, dwb, lnw, lnb)],
        out_specs=(tok(qk), tok(qk), tok(qk), tok(qk), tok(LANES),
                   pl.BlockSpec((1, SUBLANES, tile), lambda i, j: (i, 0, j)), tok(cfc)),
        out_shape=out_shape,
        scratch_shapes=[pltpu.VMEM((QKV_HALO + tile, 3 * qk), F32),
                        pltpu.VMEM((CF_HALO + tile, cfc), F32),
                        pltpu.VMEM((SUBLANES - 1, CF_HALO + tile, cfc), F32)],
        compiler_params=pltpu.CompilerParams(
            dimension_semantics=("arbitrary", "arbitrary"), vmem_limit_bytes=VMEM_LIMIT),
        name="inproj",
    )(x, nw, wm, wba, wcf, cw, alog, dtb, dww, dwb, lnw, lnb)


def _bmm(a, b):
    return jnp.einsum("bmk,bkn->bmn", a, b, preferred_element_type=F32)


def _bmm_nt(a, b):
    return jnp.einsum("bmk,bnk->bmn", a, b, preferred_element_type=F32)


def _bmm_tn(a, b):
    return jnp.einsum("bkm,bkn->bmn", a, b, preferred_element_type=F32)


def _unit_lower_inverse(a):
    c = a.shape[-1]
    ii = lax.broadcasted_iota(I32, (c, c), 0)
    jj = lax.broadcasted_iota(I32, (c, c), 1)
    eye = (ii == jj).astype(F32)
    same16 = (ii // 16) == (jj // 16)
    same32 = (ii // 32) == (jj // 32)
    x = jnp.where(same16, -a, 0.0)
    t = eye + x
    xp = x
    for _ in range(3):
        xp_b = xp.astype(BF16)
        xp = _bmm(xp_b, xp_b)
        t = t + _bmm(t.astype(BF16), xp.astype(BF16))
    for off in (jnp.where(same32 & ~same16, a, 0.0), jnp.where(~same32, a, 0.0)):
        tb = t.astype(BF16)
        t = t - _bmm(tb, _bmm(off.astype(BF16), tb).astype(BF16))
    return t


def _gdn_kernel(q_ref, k_ref, v_ref, bg_ref, bgt_ref, o_ref, state, s_all):
    lt = q_ref.shape[1]
    c = GDN_CHUNK
    nh = GDN_HEADS
    nc = lt // c

    @pl.when(pl.program_id(1) == 0)
    def _():
        state[...] = jnp.zeros(state.shape, F32)

    def stack(fn):
        return jnp.stack([fn(slice(n * c, (n + 1) * c), h) for n in range(nc) for h in range(nh)])

    head = lambda h: slice(h * HEAD_DIM, (h + 1) * HEAD_DIM)
    q = stack(lambda r, h: q_ref[0, r, head(h)]) * (HEAD_DIM ** -0.5)
    k = stack(lambda r, h: k_ref[0, r, head(h)])
    v = stack(lambda r, h: v_ref[0, r, head(h)])
    beta = stack(lambda r, h: bg_ref[0, r, h:h + 1])
    gcol = stack(lambda r, h: bg_ref[0, r, nh + h:nh + h + 1])
    grow = stack(lambda r, h: bgt_ref[0, nh + h:nh + h + 1, r])

    ii = lax.broadcasted_iota(I32, (c, c), 0)
    jj = lax.broadcasted_iota(I32, (c, c), 1)
    glast = gcol[:, c - 1:c, :]
    eg = jnp.exp(gcol)
    decay = jnp.where(ii >= jj, jnp.exp(jnp.minimum(gcol - grow, 0.0)), 0.0)
    kb = k * beta
    k_b = k.astype(BF16)
    a = jnp.where(ii > jj, _bmm_nt(kb.astype(BF16), k_b) * decay, 0.0)
    t = _unit_lower_inverse(a)
    rhs = jnp.concatenate([v * beta, kb * eg], axis=-1).astype(BF16)
    sol = _bmm(t.astype(BF16), rhs)
    u_val = sol[..., :HEAD_DIM]
    w_key = sol[..., HEAD_DIM:]
    intra = _bmm_nt(q.astype(BF16), k_b) * decay
    k_tail = (k * jnp.exp(glast - gcol)).astype(BF16)
    upd = _bmm_tn(k_tail, sol.astype(BF16))
    b_mat = upd[..., :HEAD_DIM]
    p_mat = upd[..., HEAD_DIM:].astype(BF16)
    g_tot = jnp.exp(glast)

    s = state[...]
    for n in range(nc):
        grp = slice(n * nh, (n + 1) * nh)
        s_b = s.astype(BF16)
        s_all[grp] = s_b
        s = s * g_tot[grp] - _bmm(p_mat[grp], s_b) + b_mat[grp]
    state[...] = s

    wq = jnp.concatenate([w_key, q * eg], axis=1).astype(BF16)
    ws_qs = _bmm(wq, s_all[...])
    v_new = u_val - ws_qs[:, :c]
    o = ws_qs[:, c:] + _bmm(intra.astype(BF16), v_new.astype(BF16))
    for n in range(nc):
        for h in range(nh):
            o_ref[0, n * c:(n + 1) * c, head(h)] = o[n * nh + h]


def _gdn(q, k, v, bg, bgt, *, tile):
    b, s, qk = q.shape
    grid = (b, s // tile)
    tok = lambda w: pl.BlockSpec((1, tile, w), lambda i, j: (i, j, 0))
    n_prob = (tile // GDN_CHUNK) * GDN_HEADS
    return pl.pallas_call(
        _gdn_kernel,
        grid=grid,
        in_specs=[tok(qk), tok(qk), tok(qk), tok(LANES),
                  pl.BlockSpec((1, SUBLANES, tile), lambda i, j: (i, 0, j))],
        out_specs=tok(qk),
        out_shape=jax.ShapeDtypeStruct((b, s, qk), F32),
        scratch_shapes=[pltpu.VMEM((GDN_HEADS, HEAD_DIM, HEAD_DIM), F32),
                        pltpu.VMEM((n_prob, HEAD_DIM, HEAD_DIM), BF16)],
        compiler_params=pltpu.CompilerParams(
            dimension_semantics=("arbitrary", "arbitrary"), vmem_limit_bytes=VMEM_LIMIT),
        name="gdn",
    )(q, k, v, bg, bgt)


def _split_bf16(x):
    hi = x.astype(BF16)
    lo = (x - hi.astype(F32)).astype(BF16)
    return hi, lo


def _outproj_router_kernel(o_ref, z_ref, u_ref, x_ref, gnw_ref, wa_ref, wb_ref, fnw_ref,
                           wr_ref, br_ref,
                           x2_ref, xn_ref, route_ref, gate_ref, counts_ref, carry):
    tt = x_ref.shape[0]

    @pl.when(pl.program_id(0) == 0)
    def _():
        carry[...] = jnp.zeros(carry.shape, F32)

    parts = []
    for hd in range(GDN_HEADS):
        sl = slice(hd * HEAD_DIM, (hd + 1) * HEAD_DIM)
        oh = o_ref[:, sl]
        y = oh * lax.rsqrt(jnp.mean(oh * oh, axis=-1, keepdims=True) + NORM_EPS) * gnw_ref[...]
        parts.append((y * _silu(z_ref[:, sl])).astype(BF16))
    out_a = jnp.concatenate(parts, axis=-1)
    x2 = x_ref[...] + _dot(out_a, wa_ref[...]) + _dot(u_ref[...].astype(BF16), wb_ref[...])
    x2_ref[...] = x2

    xn = x2 * lax.rsqrt(jnp.mean(x2 * x2, axis=-1, keepdims=True) + NORM_EPS) * fnw_ref[...]
    _store_token_tiles(xn_ref, xn)

    xh, xl = _split_bf16(xn)
    wh, wl = _split_bf16(wr_ref[...])
    logits = _dot(xh, wh) + _dot(xh, wl) + _dot(xl, wh) + br_ref[...]

    lane = lax.broadcasted_iota(I32, (tt, LANES), 1)
    lane_f = lane.astype(F32)
    neg = jnp.float32(-jnp.inf)
    work = jnp.where(lane < N_EXPERTS, logits, neg)
    vals, idxs = [], []
    onehot = jnp.zeros((tt, LANES), F32)
    for _ in range(TOP_K):
        m = jnp.max(work, axis=-1, keepdims=True)
        idx = jnp.min(jnp.where(work == m, lane_f, float(LANES)), axis=-1, keepdims=True).astype(I32)
        sel = lane == idx
        vals.append(m)
        idxs.append(idx)
        onehot = onehot + sel.astype(F32)
        work = jnp.where(sel, neg, work)
    exps = [jnp.exp(v - vals[0]) for v in vals]
    denom = exps[0] + exps[1] + exps[2] + exps[3]

    ri = lax.broadcasted_iota(I32, (tt, tt), 0)
    ci = lax.broadcasted_iota(I32, (tt, tt), 1)
    strict = (ri > ci).astype(BF16)
    base = carry[...] + _dot(strict, onehot.astype(BF16))
    route = jnp.zeros((tt, LANES), I32)
    gates = jnp.zeros((tt, LANES), F32)
    for kk in range(TOP_K):
        rank = jnp.sum(jnp.where(lane == idxs[kk], base, 0.0), axis=-1, keepdims=True)
        route = jnp.where(lane == kk, idxs[kk], route)
        route = jnp.where(lane == kk + TOP_K, rank.astype(I32), route)
        gates = jnp.where(lane == kk, exps[kk] / denom, gates)
    route_ref[...] = route
    gate_ref[...] = gates
    new_carry = carry[...] + jnp.sum(onehot, axis=0, keepdims=True)
    carry[...] = new_carry
    counts_ref[...] = new_carry.astype(I32)


def _outproj_router(o, z, u, x, gnw, wa, wb, fnw, wr, br, *, tile):
    t, d = x.shape
    grid = (t // tile,)
    full = lambda a: pl.BlockSpec(a.shape, lambda i: (0,) * a.ndim)
    tok = lambda w: pl.BlockSpec((tile, w), lambda i: (i, 0))
    return pl.pallas_call(
        _outproj_router_kernel,
        grid=grid,
        in_specs=[tok(o.shape[1]), tok(z.shape[1]), tok(u.shape[1]), tok(d)]
                 + [full(a) for a in (gnw, wa, wb, fnw, wr, br)],
        out_specs=(tok(d), pl.BlockSpec((tile * (d // LANES), LANES), lambda i: (i, 0)),
                   tok(LANES), tok(LANES), pl.BlockSpec((1, LANES), lambda i: (0, 0))),
        out_shape=(jax.ShapeDtypeStruct((t, d), F32), jax.ShapeDtypeStruct((t * (d // LANES), LANES), F32),
                   jax.ShapeDtypeStruct((t, LANES), I32), jax.ShapeDtypeStruct((t, LANES), F32),
                   jax.ShapeDtypeStruct((1, LANES), I32)),
        scratch_shapes=[pltpu.VMEM((1, LANES), F32)],
        compiler_params=pltpu.CompilerParams(
            dimension_semantics=("arbitrary",), vmem_limit_bytes=VMEM_LIMIT),
        name="outproj_router",
    )(o, z, u, x, gnw, wa, wb, fnw, wr, br)


def _dest_kernel(route_ref, pstart_ref, dest_ref):
    route = route_ref[...].astype(F32)
    tt = route.shape[0]
    lane = lax.broadcasted_iota(I32, (tt, LANES), 1)
    pstart = pstart_ref[...].astype(F32)
    dest = jnp.zeros((tt, LANES), F32)
    for kk in range(TOP_K):
        idx = jnp.sum(jnp.where(lane == kk, route, 0.0), axis=-1, keepdims=True)
        rank = jnp.sum(jnp.where(lane == kk + TOP_K, route, 0.0), axis=-1, keepdims=True)
        start = jnp.sum(jnp.where(lane == idx.astype(I32), pstart, 0.0), axis=-1, keepdims=True)
        dest = jnp.where(lane == kk, start + rank, dest)
    dest_ref[...] = dest[:, :TOP_K].astype(I32)


def _dest(route, pstart, *, tile):
    t = route.shape[0]
    return pl.pallas_call(
        _dest_kernel,
        grid=(t // tile,),
        in_specs=[pl.BlockSpec((tile, LANES), lambda i: (i, 0)),
                  pl.BlockSpec((1, LANES), lambda i: (0, 0))],
        out_specs=pl.BlockSpec((tile, TOP_K), lambda i: (i, 0)),
        out_shape=jax.ShapeDtypeStruct((t, TOP_K), I32),
        compiler_params=pltpu.CompilerParams(dimension_semantics=("arbitrary",)),
        name="dest_rows",
    )(route, pstart)


ROW_WINDOW = 32
COMBINE_CHUNKS = 8


def _sc_mesh():
    return plsc.VectorSubcoreMesh(core_axis_name="core", subcore_axis_name="subcore")


def _dispatch_rows(xn_tiles, dest_win, n_rows):
    t = xn_tiles.shape[0]

    @functools.partial(
        pl.kernel, mesh=_sc_mesh(), scratch_types=[],
        out_type=jax.ShapeDtypeStruct((n_rows,) + xn_tiles.shape[1:], xn_tiles.dtype))
    def dispatch(x_hbm, idx_hbm, o_hbm):
        def body(x_vmem, idx_vmem):
            for kk in range(TOP_K):
                pltpu.sync_copy(x_vmem, o_hbm.at[idx_vmem.at[0, pl.ds(kk * ROW_WINDOW, ROW_WINDOW)]])

        pltpu.emit_pipeline(
            body,
            grid=(t // ROW_WINDOW,),
            in_specs=[pl.BlockSpec((ROW_WINDOW,) + xn_tiles.shape[1:], lambda i: (i, 0, 0)),
                      pl.BlockSpec((1, TOP_K * ROW_WINDOW), lambda i: (i, 0))],
            out_specs=[],
            core_axis_name=("core", "subcore"),
            dimension_semantics=(pltpu.PARALLEL,),
        )(x_hbm, idx_hbm)

    return dispatch(xn_tiles, dest_win)


def _collect_rows(yb_tiles, src_win):
    n_pairs = src_win.shape[0] * ROW_WINDOW

    @functools.partial(
        pl.kernel, mesh=_sc_mesh(), scratch_types=[],
        out_type=jax.ShapeDtypeStruct((n_pairs,) + yb_tiles.shape[1:], yb_tiles.dtype))
    def collect(y_hbm, idx_hbm, o_hbm):
        def body(idx_vmem, o_vmem):
            pltpu.sync_copy(y_hbm.at[idx_vmem.at[0, pl.ds(0, ROW_WINDOW)]], o_vmem)

        pltpu.emit_pipeline(
            body,
            grid=(n_pairs // ROW_WINDOW,),
            in_specs=[pl.BlockSpec((1, LANES), lambda i: (i, 0))],
            out_specs=[pl.BlockSpec((ROW_WINDOW,) + yb_tiles.shape[1:], lambda i: (i, 0, 0))],
            core_axis_name=("core", "subcore"),
            dimension_semantics=(pltpu.PARALLEL,),
        )(idx_hbm, o_hbm)

    return collect(yb_tiles, src_win)


def _expert_kernel(blk_exp_ref, n_used_ref, next_exp_ref, slot_ref, blk_rows_ref,
                   xb_ref, wgu_hbm, bgu_ref, wd_hbm, bd_ref, yb_ref,
                   wgu_f, wd_f, wgu_b, wd_b, wsem):
    i = pl.program_id(0)
    n_used = n_used_ref[0]
    bm = xb_ref.shape[0] // SUBLANES
    dff = wd_f.shape[1]

    def weight_copies(e, s):
        return (pltpu.make_async_copy(wgu_hbm.at[e], wgu_f.at[s], wsem.at[0, s]),
                pltpu.make_async_copy(wd_hbm.at[e], wd_f.at[s], wsem.at[1, s]))

    prev = blk_exp_ref[jnp.maximum(i - 1, 0)]
    first_of_expert = (i == 0) | (blk_exp_ref[i] != prev)

    @pl.when(first_of_expert & (i < n_used))
    def _():
        e = blk_exp_ref[i]
        s = slot_ref[i]

        @pl.when(i == 0)
        def _():
            for cp in weight_copies(e, s):
                cp.start()

        for cp in weight_copies(e, s):
            cp.wait()
        wgu_b[...] = wgu_f[s].astype(BF16)
        wd_b[...] = wd_f[s].astype(BF16)

        nxt = next_exp_ref[i]

        @pl.when(nxt >= 0)
        def _():
            for cp in weight_copies(nxt, 1 - s):
                cp.start()

    def swiglu_rows(rows):
        x_rows = xb_ref.at[pl.ds(0, rows * SUBLANES)]
        y_rows = yb_ref.at[pl.ds(0, rows * SUBLANES)]
        xb = _load_token_tiles(x_rows, rows).astype(BF16)
        hid = _dot(xb, wgu_b[...]) + bgu_ref[0]
        gate = jnp.minimum(hid[:, :dff], SWIGLU_LIMIT)
        up = jnp.clip(hid[:, dff:], -SWIGLU_LIMIT, SWIGLU_LIMIT)
        glu = gate * jax.nn.sigmoid(SWIGLU_ALPHA * gate)
        act = ((up + 1.0) * glu).astype(BF16)
        _store_token_tiles(y_rows, _dot(act, wd_b[...]) + bd_ref[0])

    used = i < n_used
    half_full = blk_rows_ref[i] <= bm // 2
    pl.when(used & jnp.logical_not(half_full))(functools.partial(swiglu_rows, bm))
    pl.when(used & half_full)(functools.partial(swiglu_rows, bm // 2))


def _experts(blk_exp, n_used, next_exp, slot, blk_rows, xb_tiles, wgu, bgu, wd, bd):
    d = wgu.shape[1]
    chunks = d // LANES
    n_blocks = xb_tiles.shape[0] // (EXPERT_BLOCK * chunks)
    two_f = wgu.shape[2]
    dff = wd.shape[1]
    blk = lambda i, be, nu, ne, sl, br: (jnp.minimum(i, nu[0] - 1), 0)
    exp3 = lambda i, be, nu, ne, sl, br: (be[jnp.minimum(i, nu[0] - 1)], 0, 0)
    grid_spec = pltpu.PrefetchScalarGridSpec(
        num_scalar_prefetch=5,
        grid=(n_blocks,),
        in_specs=[pl.BlockSpec((EXPERT_BLOCK * chunks, LANES), blk),
                  pl.BlockSpec(memory_space=pl.ANY),
                  pl.BlockSpec((1, 1, two_f), exp3),
                  pl.BlockSpec(memory_space=pl.ANY),
                  pl.BlockSpec((1, 1, d), exp3)],
        out_specs=pl.BlockSpec((EXPERT_BLOCK * chunks, LANES), blk),
        scratch_shapes=[pltpu.VMEM((2, d, two_f), F32), pltpu.VMEM((2, dff, d), F32),
                        pltpu.VMEM((d, two_f), BF16), pltpu.VMEM((dff, d), BF16),
                        pltpu.SemaphoreType.DMA((2, 2))])
    return pl.pallas_call(
        _expert_kernel,
        grid_spec=grid_spec,
        out_shape=jax.ShapeDtypeStruct(xb_tiles.shape, F32),
        compiler_params=pltpu.CompilerParams(
            dimension_semantics=("arbitrary",), vmem_limit_bytes=VMEM_LIMIT),
        name="experts",
    )(blk_exp, n_used, next_exp, slot, blk_rows, xb_tiles, wgu, bgu, wd, bd)


def _combine_kernel(y0_ref, y1_ref, y2_ref, y3_ref, gate_ref, x2_ref, fw_ref, out_ref):
    gates = gate_ref[...]
    lane = lax.broadcasted_iota(I32, gates.shape, 1)
    x3 = x2_ref[...]
    for kk, y_ref in enumerate((y0_ref, y1_ref, y2_ref, y3_ref)):
        gk = jnp.sum(jnp.where(lane == kk, gates, 0.0), axis=-1, keepdims=True)
        x3 = x3 + gk * _load_token_tiles(y_ref, x3.shape[0])
    out_ref[...] = x3 * lax.rsqrt(jnp.mean(x3 * x3, axis=-1, keepdims=True) + NORM_EPS) * fw_ref[...]


def _combine_into_kernel(y0_ref, y1_ref, y2_ref, y3_ref, gate_ref, x2_ref, fw_ref, prev_ref, out_ref):
    del prev_ref
    _combine_kernel(y0_ref, y1_ref, y2_ref, y3_ref, gate_ref, x2_ref, fw_ref, out_ref)


def _combine(y4, gates, x2, fw, out_prev, chunk, *, tile):
    t, d = x2.shape
    tc = y4.shape[0] // (TOP_K * (d // LANES))
    steps = tc // tile
    first = chunk * steps
    choice = lambda kk: pl.BlockSpec((tile * (d // LANES), LANES), lambda i: (kk * steps + i, 0))
    in_specs = ([choice(kk) for kk in range(TOP_K)]
                + [pl.BlockSpec((tile, LANES), lambda i: (first + i, 0)),
                   pl.BlockSpec((tile, d), lambda i: (first + i, 0)),
                   pl.BlockSpec((1, d), lambda i: (0, 0))])
    args = [y4, y4, y4, y4, gates, x2, fw]
    if out_prev is not None:
        in_specs.append(pl.BlockSpec(memory_space=pl.ANY))
        args.append(out_prev)
    return pl.pallas_call(
        _combine_kernel if out_prev is None else _combine_into_kernel,
        grid=(steps,),
        in_specs=in_specs,
        out_specs=pl.BlockSpec((tile, d), lambda i: (first + i, 0)),
        out_shape=jax.ShapeDtypeStruct((t, d), F32),
        input_output_aliases={} if out_prev is None else {len(args) - 1: 0},
        compiler_params=pltpu.CompilerParams(
            dimension_semantics=("arbitrary",), vmem_limit_bytes=VMEM_LIMIT),
        name="combine",
    )(*args)


def _pad_lanes(a, offset=0, fill=0.0):
    out = jnp.full((1, LANES), fill, a.dtype)
    return out.at[0, offset:offset + a.shape[0]].set(a)


def _layer(x, attn_norm_w, w_in, gdn_conv_w, gdn_a_log, gdn_dt_bias, gdn_norm_w,
           cf_dw_w, cf_dw_b, cf_ln_w, cf_ln_b, w_out, ffn_norm_w, w_router, b_router,
           w_gate_up, b_gate_up, w_down, b_down, final_norm_w, apply_final):
    b, s, d = x.shape
    t = b * s
    assert d == SUBLANES * LANES, "the token-tile layout needs one (8, 128) tile per token row"
    qk = GDN_HEADS * HEAD_DIM
    cfc = cf_dw_w.shape[1]
    off_b = 4 * qk
    off_cf = off_b + 2 * GDN_HEADS

    wm = w_in[:, :off_b].astype(BF16)
    wba = jnp.zeros((d, LANES), F32).at[:, :2 * GDN_HEADS].set(w_in[:, off_b:off_cf]).astype(BF16)
    wcf = w_in[:, off_cf:].astype(BF16)
    alog = _pad_lanes(gdn_a_log, GDN_HEADS)
    dtb = _pad_lanes(gdn_dt_bias, GDN_HEADS)

    q, k, v, z, bg, bgt, u = _inproj(
        x, attn_norm_w[None, :], wm, wba, wcf, gdn_conv_w, alog, dtb,
        cf_dw_w, cf_dw_b[None, :], cf_ln_w[None, :], cf_ln_b[None, :], tile=min(512, s))
    o = _gdn(q, k, v, bg, bgt, tile=min(512, s))

    wr = jnp.zeros((d, LANES), F32).at[:, :N_EXPERTS].set(w_router)
    br = _pad_lanes(b_router)
    x2, xn, route, gates, counts = _outproj_router(
        o.reshape(t, qk), z.reshape(t, qk), u.reshape(t, cfc), x.reshape(t, d),
        gdn_norm_w[None, :], w_out[:qk].astype(BF16), w_out[qk:].astype(BF16),
        ffn_norm_w[None, :], wr, br, tile=min(512, t))

    cnt = counts[0, :N_EXPERTS]
    nblk = (cnt + EXPERT_BLOCK - 1) // EXPERT_BLOCK
    blk_end = jnp.cumsum(nblk)
    pstart = (blk_end - nblk) * EXPERT_BLOCK
    n_blocks = (t * TOP_K) // EXPERT_BLOCK + N_EXPERTS
    blk_ids = jnp.arange(n_blocks, dtype=I32)
    blk_exp = jnp.minimum(
        jnp.sum((blk_end[None, :] <= blk_ids[:, None]).astype(I32), axis=1), N_EXPERTS - 1)
    n_used = blk_end[-1:].astype(I32)
    after = blk_end[blk_exp]
    next_exp = jnp.where(after < n_used[0], blk_exp[jnp.minimum(after, n_blocks - 1)], -1).astype(I32)
    slot = ((jnp.cumsum((nblk > 0).astype(I32)) - 1)[blk_exp] % 2).astype(I32)
    blk_rows = jnp.clip((pstart + cnt)[blk_exp] - blk_ids * EXPERT_BLOCK, 0, EXPERT_BLOCK).astype(I32)

    dest = _dest(route, _pad_lanes(pstart.astype(I32)), tile=min(2048, t))
    n_rows = n_blocks * EXPERT_BLOCK
    chunks = d // LANES
    windows = t // ROW_WINDOW
    dest_win = dest.reshape(windows, ROW_WINDOW, TOP_K).transpose(0, 2, 1).reshape(windows, TOP_K * ROW_WINDOW)
    xb = _dispatch_rows(xn.reshape(t, chunks, LANES), dest_win, n_rows)
    yb = _experts(blk_exp, n_used, next_exp, slot, blk_rows, xb.reshape(n_rows * chunks, LANES), w_gate_up, b_gate_up[:, None, :],
                  w_down, b_down[:, None, :]).reshape(n_rows, chunks, LANES)

    tc = t // COMBINE_CHUNKS
    out = None
    for c in range(COMBINE_CHUNKS):
        src = dest[c * tc:(c + 1) * tc].T.reshape(TOP_K * tc // ROW_WINDOW, ROW_WINDOW)
        src_win = jnp.pad(src, ((0, 0), (0, LANES - ROW_WINDOW)))
        y4 = _collect_rows(yb, src_win).reshape(TOP_K * tc * chunks, LANES)
        out = _combine(y4, gates, x2, final_norm_w[None, :], out, c, tile=min(EXPERT_BLOCK, tc))
    return out.reshape(b, s, d)


def kernel(x, attn_norm_w, w_in, gdn_conv_w, gdn_a_log, gdn_dt_bias, gdn_norm_w, cf_dw_w, cf_dw_b,
           cf_ln_w, cf_ln_b, w_out, ffn_norm_w, w_router, b_router, w_gate_up, b_gate_up, w_down,
           b_down, final_norm_w):
    depth = w_in.shape[0]
    assert depth == 1, "the fused final norm assumes a single trunk layer"
    return _layer(x, attn_norm_w[0], w_in[0], gdn_conv_w[0], gdn_a_log[0], gdn_dt_bias[0],
                  gdn_norm_w[0], cf_dw_w[0], cf_dw_b[0], cf_ln_w[0], cf_ln_b[0], w_out[0],
                  ffn_norm_w[0], w_router[0], b_router[0], w_gate_up[0], b_gate_up[0], w_down[0],
                  b_down[0], final_norm_w, True)
```

```python
import functools

import jax
import jax.numpy as jnp
from jax import lax
from jax.experimental import pallas as pl
from jax.experimental.pallas import tpu as pltpu
from jax.experimental.pallas import tpu_sc as plsc

F32 = jnp.float32
BF16 = jnp.bfloat16
I32 = jnp.int32

NORM_EPS = 1e-6
LANES = 128
SUBLANES = 8
GDN_HEADS = 4
HEAD_DIM = 128
GDN_CHUNK = 64
GDN_CONV = 4
CF_KERNEL = 31
N_EXPERTS = 32
TOP_K = 4
SWIGLU_LIMIT = 7.0
SWIGLU_ALPHA = 1.702

QKV_HALO = 8
CF_HALO = 32
CF_ROWS = 64
EXPERT_BLOCK = 512
TOKEN_TILE = 512
DEST_TILE = 2048
VMEM_LIMIT = 56 * 1024 * 1024


def _silu(x):
    return x * jax.nn.sigmoid(x)


def _dot(a, b):
    return jnp.dot(a, b, preferred_element_type=F32)


def _store_token_tiles(ref, x):
    rows, d = x.shape
    chunks = d // LANES
    for c in range(chunks):
        ref[pl.ds(c, rows, stride=chunks), :] = x[:, c * LANES:(c + 1) * LANES]


def _load_token_tiles(ref, rows):
    chunks = ref.shape[0] // rows
    return jnp.concatenate([ref[pl.ds(c, rows, stride=chunks), :] for c in range(chunks)], axis=1)


def _inproj_kernel(x_ref, nw_ref, wm_ref, wba_ref, wcf_ref, cw_ref, alog_ref, dtb_ref,
                   dww_ref, dwb_ref, lnw_ref, lnb_ref,
                   q_ref, k_ref, v_ref, z_ref, bg_ref, bgt_ref, u_ref,
                   qkv_buf, cf_buf, cf_shift):
    tt = x_ref.shape[1]
    qk = GDN_HEADS * HEAD_DIM
    cfc = u_ref.shape[2]

    @pl.when(pl.program_id(1) == 0)
    def _():
        qkv_buf[0:QKV_HALO, :] = jnp.zeros((QKV_HALO, qkv_buf.shape[1]), F32)
        cf_buf[0:CF_HALO, :] = jnp.zeros((CF_HALO, cf_buf.shape[1]), F32)

    x = x_ref[0]
    h = x * lax.rsqrt(jnp.mean(x * x, axis=-1, keepdims=True) + NORM_EPS) * nw_ref[...]
    h = h.astype(BF16)
    pm = _dot(h, wm_ref[...])
    pba = _dot(h, wba_ref[...])
    pcf = _dot(h, wcf_ref[...])

    z_ref[0] = pm[:, 3 * qk:]

    qkv_buf[QKV_HALO:QKV_HALO + tt, :] = pm[:, :3 * qk]
    acc = None
    for j in range(GDN_CONV):
        term = cw_ref[j:j + 1, :] * qkv_buf[pl.ds(QKV_HALO - (GDN_CONV - 1) + j, tt), :]
        acc = term if acc is None else acc + term
    qkv_buf[0:QKV_HALO, :] = qkv_buf[tt:tt + QKV_HALO, :]
    qkv = _silu(acc)
    for hd in range(GDN_HEADS):
        for base, ref in ((0, q_ref), (qk, k_ref)):
            t = qkv[:, base + hd * HEAD_DIM: base + (hd + 1) * HEAD_DIM]
            t = t * lax.rsqrt(jnp.sum(t * t, axis=-1, keepdims=True) + NORM_EPS)
            ref[0, :, hd * HEAD_DIM:(hd + 1) * HEAD_DIM] = t
    v_ref[0] = qkv[:, 2 * qk:]

    lane = lax.broadcasted_iota(I32, (tt, LANES), 1)
    row = lax.broadcasted_iota(I32, (tt, LANES), 0)
    beta = jax.nn.sigmoid(pba)
    sp_in = pba + dtb_ref[...]
    softplus = jnp.maximum(sp_in, 0.0) + jnp.log(1.0 + jnp.exp(-jnp.abs(sp_in)))
    g = -jnp.exp(alog_ref[...]) * softplus
    g = jnp.where((lane >= GDN_HEADS) & (lane < 2 * GDN_HEADS), g, 0.0)
    pos = row % GDN_CHUNK
    shift = 1
    while shift < GDN_CHUNK:
        g = g + jnp.where(pos >= shift, pltpu.roll(g, shift, 0), 0.0)
        shift *= 2
    bg = jnp.where(lane < GDN_HEADS, beta, g)
    bg_ref[0] = bg
    bgt_ref[0] = jnp.transpose(bg)[0:SUBLANES, :]

    glu = pcf[:, :cfc] * jax.nn.sigmoid(pcf[:, cfc:])
    cf_buf[CF_HALO:CF_HALO + tt, :] = glu
    lo = SUBLANES
    span = tt + CF_HALO - lo
    for r in range(1, SUBLANES):
        cf_shift[r - 1, lo:lo + span, :] = cf_buf[pl.ds(lo - r, span), :]
    rows = CF_ROWS
    for r0 in range(0, tt, rows):
        acc = None
        for j in range(CF_KERNEL):
            a, r = divmod(CF_KERNEL - 1 - j, SUBLANES)
            start = CF_HALO + r0 - a * SUBLANES
            src = cf_buf[start:start + rows, :] if r == 0 else cf_shift[r - 1, start:start + rows, :]
            term = dww_ref[j:j + 1, :] * src
            acc = term if acc is None else acc + term
        c = acc + dwb_ref[...]
        mu = jnp.mean(c, axis=-1, keepdims=True)
        cc = c - mu
        y = cc * lax.rsqrt(jnp.mean(cc * cc, axis=-1, keepdims=True) + NORM_EPS)
        u_ref[0, r0:r0 + rows, :] = _silu(y * lnw_ref[...] + lnb_ref[...])
    cf_buf[0:CF_HALO, :] = cf_buf[tt:tt + CF_HALO, :]


def _inproj(x, nw, wm, wba, wcf, cw, alog, dtb, dww, dwb, lnw, lnb, *, tile):
    b, s, d = x.shape
    qk = GDN_HEADS * HEAD_DIM
    cfc = dww.shape[1]
    grid = (b, s // tile)
    full = lambda a: pl.BlockSpec(a.shape, lambda i, j: (0,) * a.ndim)
    tok = lambda w: pl.BlockSpec((1, tile, w), lambda i, j: (i, j, 0))
    out_shape = (
        jax.ShapeDtypeStruct((b, s, qk), F32), jax.ShapeDtypeStruct((b, s, qk), F32),
        jax.ShapeDtypeStruct((b, s, qk), F32), jax.ShapeDtypeStruct((b, s, qk), F32),
        jax.ShapeDtypeStruct((b, s, LANES), F32), jax.ShapeDtypeStruct((b, SUBLANES, s), F32),
        jax.ShapeDtypeStruct((b, s, cfc), F32))
    return pl.pallas_call(
        _inproj_kernel,
        grid=grid,
        in_specs=[tok(d)] + [full(a) for a in (nw, wm, wba, wcf, cw, alog, dtb, dww, dwb, lnw, lnb)],
        out_specs=(tok(qk), tok(qk), tok(qk), tok(qk), tok(LANES),
                   pl.BlockSpec((1, SUBLANES, tile), lambda i, j: (i, 0, j)), tok(cfc)),
        out_shape=out_shape,
        scratch_shapes=[pltpu.VMEM((QKV_HALO + tile, 3 * qk), F32),
                        pltpu.VMEM((CF_HALO + tile, cfc), F32),
                        pltpu.VMEM((SUBLANES - 1, CF_HALO + tile, cfc), F32)],
        compiler_params=pltpu.CompilerParams(
            dimension_semantics=("arbitrary", "arbitrary"), vmem_limit_bytes=VMEM_LIMIT),
        name="inproj",
    )(x, nw, wm, wba, wcf, cw, alog, dtb, dww, dwb, lnw, lnb)


def _bmm(a, b):
    return jnp.einsum("bmk,bkn->bmn", a, b, preferred_element_type=F32)


def _bmm_nt(a, b):
    return jnp.einsum("bmk,bnk->bmn", a, b, preferred_element_type=F32)


def _bmm_tn(a, b):
    return jnp.einsum("bkm,bkn->bmn", a, b, preferred_element_type=F32)


def _unit_lower_inverse(a):
    c = a.shape[-1]
    ii = lax.broadcasted_iota(I32, (c, c), 0)
    jj = lax.broadcasted_iota(I32, (c, c), 1)
    eye = (ii == jj).astype(F32)
    same16 = (ii // 16) == (jj // 16)
    same32 = (ii // 32) == (jj // 32)
    x = jnp.where(same16, -a, 0.0)
    t = eye + x
    xp = x
    for _ in range(3):
        xp_b = xp.astype(BF16)
        xp = _bmm(xp_b, xp_b)
        t = t + _bmm(t.astype(BF16), xp.astype(BF16))
    for off in (jnp.where(same32 & ~same16, a, 0.0), jnp.where(~same32, a, 0.0)):
        tb = t.astype(BF16)
        t = t - _bmm(tb, _bmm(off.astype(BF16), tb).astype(BF16))
    return t


def _gdn_kernel(q_ref, k_ref, v_ref, bg_ref, bgt_ref, o_ref, state, s_all):
    lt = q_ref.shape[1]
    c = GDN_CHUNK
    nh = GDN_HEADS
    nc = lt // c

    @pl.when(pl.program_id(1) == 0)
    def _():
        state[...] = jnp.zeros(state.shape, F32)

    def stack(fn):
        return jnp.stack([fn(slice(n * c, (n + 1) * c), h) for n in range(nc) for h in range(nh)])

    head = lambda h: slice(h * HEAD_DIM, (h + 1) * HEAD_DIM)
    q = stack(lambda r, h: q_ref[0, r, head(h)]) * (HEAD_DIM ** -0.5)
    k = stack(lambda r, h: k_ref[0, r, head(h)])
    v = stack(lambda r, h: v_ref[0, r, head(h)])
    beta = stack(lambda r, h: bg_ref[0, r, h:h + 1])
    gcol = stack(lambda r, h: bg_ref[0, r, nh + h:nh + h + 1])
    grow = stack(lambda r, h: bgt_ref[0, nh + h:nh + h + 1, r])

    ii = lax.broadcasted_iota(I32, (c, c), 0)
    jj = lax.broadcasted_iota(I32, (c, c), 1)
    glast = gcol[:, c - 1:c, :]
    eg = jnp.exp(gcol)
    decay = jnp.where(ii >= jj, jnp.exp(jnp.minimum(gcol - grow, 0.0)), 0.0)
    kb = k * beta
    k_b = k.astype(BF16)
    a = jnp.where(ii > jj, _bmm_nt(kb.astype(BF16), k_b) * decay, 0.0)
    t = _unit_lower_inverse(a)
    rhs = jnp.concatenate([v * beta, kb * eg], axis=-1).astype(BF16)
    sol = _bmm(t.astype(BF16), rhs)
    u_val = sol[..., :HEAD_DIM]
    w_key = sol[..., HEAD_DIM:]
    intra = _bmm_nt(q.astype(BF16), k_b) * decay
    k_tail = (k * jnp.exp(glast - gcol)).astype(BF16)
    upd = _bmm_tn(k_tail, sol.astype(BF16))
    b_mat = upd[..., :HEAD_DIM]
    p_mat = upd[..., HEAD_DIM:].astype(BF16)
    g_tot = jnp.exp(glast)

    s = state[...]
    for n in range(nc):
        grp = slice(n * nh, (n + 1) * nh)
        s_b = s.astype(BF16)
        s_all[grp] = s_b
        s = s * g_tot[grp] - _bmm(p_mat[grp], s_b) + b_mat[grp]
    state[...] = s

    wq = jnp.concatenate([w_key, q * eg], axis=1).astype(BF16)
    ws_qs = _bmm(wq, s_all[...])
    v_new = u_val - ws_qs[:, :c]
    o = ws_qs[:, c:] + _bmm(intra.astype(BF16), v_new.astype(BF16))
    for n in range(nc):
        for h in range(nh):
            o_ref[0, n * c:(n + 1) * c, head(h)] = o[n * nh + h]


def _gdn(q, k, v, bg, bgt, *, tile):
    b, s, qk = q.shape
    grid = (b, s // tile)
    tok = lambda w: pl.BlockSpec((1, tile, w), lambda i, j: (i, j, 0))
    n_prob = (tile // GDN_CHUNK) * GDN_HEADS
    return pl.pallas_call(
        _gdn_kernel,
        grid=grid,
        in_specs=[tok(qk), tok(qk), tok(qk), tok(LANES),
                  pl.BlockSpec((1, SUBLANES, tile), lambda i, j: (i, 0, j))],
        out_specs=tok(qk),
        out_shape=jax.ShapeDtypeStruct((b, s, qk), F32),
        scratch_shapes=[pltpu.VMEM((GDN_HEADS, HEAD_DIM, HEAD_DIM), F32),
                        pltpu.VMEM((n_prob, HEAD_DIM, HEAD_DIM), BF16)],
        compiler_params=pltpu.CompilerParams(
            dimension_semantics=("arbitrary", "arbitrary"), vmem_limit_bytes=VMEM_LIMIT),
        name="gdn",
    )(q, k, v, bg, bgt)


def _split_bf16(x):
    hi = x.astype(BF16)
    lo = (x - hi.astype(F32)).astype(BF16)
    return hi, lo


def _outproj_router_kernel(o_ref, z_ref, u_ref, x_ref, gnw_ref, wa_ref, wb_ref, fnw_ref,
                           wr_ref, br_ref,
                           x2_ref, xn_ref, route_ref, gate_ref, counts_ref, carry):
    tt = x_ref.shape[0]

    @pl.when(pl.program_id(0) == 0)
    def _():
        carry[...] = jnp.zeros(carry.shape, F32)

    parts = []
    for hd in range(GDN_HEADS):
        sl = slice(hd * HEAD_DIM, (hd + 1) * HEAD_DIM)
        oh = o_ref[:, sl]
        y = oh * lax.rsqrt(jnp.mean(oh * oh, axis=-1, keepdims=True) + NORM_EPS) * gnw_ref[...]
        parts.append((y * _silu(z_ref[:, sl])).astype(BF16))
    out_a = jnp.concatenate(parts, axis=-1)
    x2 = x_ref[...] + _dot(out_a, wa_ref[...]) + _dot(u_ref[...].astype(BF16), wb_ref[...])
    x2_ref[...] = x2

    xn = x2 * lax.rsqrt(jnp.mean(x2 * x2, axis=-1, keepdims=True) + NORM_EPS) * fnw_ref[...]
    _store_token_tiles(xn_ref, xn)

    xh, xl = _split_bf16(xn)
    wh, wl = _split_bf16(wr_ref[...])
    logits = _dot(xh, wh) + _dot(xh, wl) + _dot(xl, wh) + br_ref[...]

    lane = lax.broadcasted_iota(I32, (tt, LANES), 1)
    lane_f = lane.astype(F32)
    neg = jnp.float32(-jnp.inf)
    work = jnp.where(lane < N_EXPERTS, logits, neg)
    vals, idxs = [], []
    onehot = jnp.zeros((tt, LANES), F32)
    for _ in range(TOP_K):
        m = jnp.max(work, axis=-1, keepdims=True)
        idx = jnp.min(jnp.where(work == m, lane_f, float(LANES)), axis=-1, keepdims=True).astype(I32)
        sel = lane == idx
        vals.append(m)
        idxs.append(idx)
        onehot = onehot + sel.astype(F32)
        work = jnp.where(sel, neg, work)
    exps = [jnp.exp(v - vals[0]) for v in vals]
    denom = exps[0] + exps[1] + exps[2] + exps[3]

    ri = lax.broadcasted_iota(I32, (tt, tt), 0)
    ci = lax.broadcasted_iota(I32, (tt, tt), 1)
    strict = (ri > ci).astype(BF16)
    base = carry[...] + _dot(strict, onehot.astype(BF16))
    route = jnp.zeros((tt, LANES), I32)
    gates = jnp.zeros((tt, LANES), F32)
    for kk in range(TOP_K):
        rank = jnp.sum(jnp.where(lane == idxs[kk], base, 0.0), axis=-1, keepdims=True)
        route = jnp.where(lane == kk, idxs[kk], route)
        route = jnp.where(lane == kk + TOP_K, rank.astype(I32), route)
        gates = jnp.where(lane == kk, exps[kk] / denom, gates)
    route_ref[...] = route
    gate_ref[...] = gates
    new_carry = carry[...] + jnp.sum(onehot, axis=0, keepdims=True)
    carry[...] = new_carry
    counts_ref[...] = new_carry.astype(I32)


def _outproj_router(o, z, u, x, gnw, wa, wb, fnw, wr, br, *, tile):
    t, d = x.shape
    grid = (t // tile,)
    full = lambda a: pl.BlockSpec(a.shape, lambda i: (0,) * a.ndim)
    tok = lambda w: pl.BlockSpec((tile, w), lambda i: (i, 0))
    return pl.pallas_call(
        _outproj_router_kernel,
        grid=grid,
        in_specs=[tok(o.shape[1]), tok(z.shape[1]), tok(u.shape[1]), tok(d)]
                 + [full(a) for a in (gnw, wa, wb, fnw, wr, br)],
        out_specs=(tok(d), pl.BlockSpec((tile * (d // LANES), LANES), lambda i: (i, 0)),
                   tok(LANES), tok(LANES), pl.BlockSpec((1, LANES), lambda i: (0, 0))),
        out_shape=(jax.ShapeDtypeStruct((t, d), F32), jax.ShapeDtypeStruct((t * (d // LANES), LANES), F32),
                   jax.ShapeDtypeStruct((t, LANES), I32), jax.ShapeDtypeStruct((t, LANES), F32),
                   jax.ShapeDtypeStruct((1, LANES), I32)),
        scratch_shapes=[pltpu.VMEM((1, LANES), F32)],
        compiler_params=pltpu.CompilerParams(
            dimension_semantics=("arbitrary",), vmem_limit_bytes=VMEM_LIMIT),
        name="outproj_router",
    )(o, z, u, x, gnw, wa, wb, fnw, wr, br)


def _dest_kernel(route_ref, pstart_ref, dest_ref):
    route = route_ref[...].astype(F32)
    tt = route.shape[0]
    lane = lax.broadcasted_iota(I32, (tt, LANES), 1)
    pstart = pstart_ref[...].astype(F32)
    dest = jnp.zeros((tt, LANES), F32)
    for kk in range(TOP_K):
        idx = jnp.sum(jnp.where(lane == kk, route, 0.0), axis=-1, keepdims=True)
        rank = jnp.sum(jnp.where(lane == kk + TOP_K, route, 0.0), axis=-1, keepdims=True)
        start = jnp.sum(jnp.where(lane == idx.astype(I32), pstart, 0.0), axis=-1, keepdims=True)
        dest = jnp.where(lane == kk, start + rank, dest)
    dest_ref[...] = dest[:, :TOP_K].astype(I32)


def _dest(route, pstart, *, tile):
    t = route.shape[0]
    return pl.pallas_call(
        _dest_kernel,
        grid=(t // tile,),
        in_specs=[pl.BlockSpec((tile, LANES), lambda i: (i, 0)),
                  pl.BlockSpec((1, LANES), lambda i: (0, 0))],
        out_specs=pl.BlockSpec((tile, TOP_K), lambda i: (i, 0)),
        out_shape=jax.ShapeDtypeStruct((t, TOP_K), I32),
        compiler_params=pltpu.CompilerParams(dimension_semantics=("arbitrary",)),
        name="dest_rows",
    )(route, pstart)


ROW_WINDOW = 32
COMBINE_CHUNKS = 16


def _sc_mesh():
    return plsc.VectorSubcoreMesh(core_axis_name="core", subcore_axis_name="subcore")


def _dispatch_rows(xn_tiles, dest_win, n_rows):
    t = xn_tiles.shape[0]

    @functools.partial(
        pl.kernel, mesh=_sc_mesh(), scratch_types=[],
        out_type=jax.ShapeDtypeStruct((n_rows,) + xn_tiles.shape[1:], xn_tiles.dtype))
    def dispatch(x_hbm, idx_hbm, o_hbm):
        def body(x_vmem, idx_vmem):
            for kk in range(TOP_K):
                pltpu.sync_copy(x_vmem, o_hbm.at[idx_vmem.at[0, pl.ds(kk * ROW_WINDOW, ROW_WINDOW)]])

        pltpu.emit_pipeline(
            body,
            grid=(t // ROW_WINDOW,),
            in_specs=[pl.BlockSpec((ROW_WINDOW,) + xn_tiles.shape[1:], lambda i: (i, 0, 0)),
                      pl.BlockSpec((1, TOP_K * ROW_WINDOW), lambda i: (i, 0))],
            out_specs=[],
            core_axis_name=("core", "subcore"),
            dimension_semantics=(pltpu.PARALLEL,),
        )(x_hbm, idx_hbm)

    return dispatch(xn_tiles, dest_win)


def _collect_rows(yb_tiles, src_win):
    n_pairs = src_win.shape[0] * ROW_WINDOW

    @functools.partial(
        pl.kernel, mesh=_sc_mesh(), scratch_types=[],
        out_type=jax.ShapeDtypeStruct((n_pairs,) + yb_tiles.shape[1:], yb_tiles.dtype))
    def collect(y_hbm, idx_hbm, o_hbm):
        def body(idx_vmem, o_vmem):
            pltpu.sync_copy(y_hbm.at[idx_vmem.at[0, pl.ds(0, ROW_WINDOW)]], o_vmem)

        pltpu.emit_pipeline(
            body,
            grid=(n_pairs // ROW_WINDOW,),
            in_specs=[pl.BlockSpec((1, LANES), lambda i: (i, 0))],
            out_specs=[pl.BlockSpec((ROW_WINDOW,) + yb_tiles.shape[1:], lambda i: (i, 0, 0))],
            core_axis_name=("core", "subcore"),
            dimension_semantics=(pltpu.PARALLEL,),
        )(idx_hbm, o_hbm)

    return collect(yb_tiles, src_win)


def _expert_kernel(blk_exp_ref, n_used_ref, next_exp_ref, slot_ref, blk_rows_ref,
                   xb_ref, wgu_hbm, bgu_ref, wd_hbm, bd_ref, yb_ref,
                   wgu_f, wd_f, wgu_b, wd_b, wsem):
    i = pl.program_id(0)
    n_used = n_used_ref[0]
    bm = xb_ref.shape[0] // SUBLANES
    dff = wd_f.shape[1]

    def weight_copies(e, s):
        return (pltpu.make_async_copy(wgu_hbm.at[e], wgu_f.at[s], wsem.at[0, s]),
                pltpu.make_async_copy(wd_hbm.at[e], wd_f.at[s], wsem.at[1, s]))

    prev = blk_exp_ref[jnp.maximum(i - 1, 0)]
    first_of_expert = (i == 0) | (blk_exp_ref[i] != prev)

    @pl.when(first_of_expert & (i < n_used))
    def _():
        e = blk_exp_ref[i]
        s = slot_ref[i]

        @pl.when(i == 0)
        def _():
            for cp in weight_copies(e, s):
                cp.start()

        for cp in weight_copies(e, s):
            cp.wait()
        wgu_b[...] = wgu_f[s].astype(BF16)
        wd_b[...] = wd_f[s].astype(BF16)

        nxt = next_exp_ref[i]

        @pl.when(nxt >= 0)
        def _():
            for cp in weight_copies(nxt, 1 - s):
                cp.start()

    def swiglu_rows(rows):
        x_rows = xb_ref.at[pl.ds(0, rows * SUBLANES)]
        y_rows = yb_ref.at[pl.ds(0, rows * SUBLANES)]
        xb = _load_token_tiles(x_rows, rows).astype(BF16)
        hid = _dot(xb, wgu_b[...]) + bgu_ref[0]
        gate = jnp.minimum(hid[:, :dff], SWIGLU_LIMIT)
        up = jnp.clip(hid[:, dff:], -SWIGLU_LIMIT, SWIGLU_LIMIT)
        glu = gate * jax.nn.sigmoid(SWIGLU_ALPHA * gate)
        act = ((up + 1.0) * glu).astype(BF16)
        _store_token_tiles(y_rows, _dot(act, wd_b[...]) + bd_ref[0])

    used = i < n_used
    half_full = blk_rows_ref[i] <= bm // 2
    pl.when(used & jnp.logical_not(half_full))(functools.partial(swiglu_rows, bm))
    pl.when(used & half_full)(functools.partial(swiglu_rows, bm // 2))


def _experts(blk_exp, n_used, next_exp, slot, blk_rows, xb_tiles, wgu, bgu, wd, bd):
    d = wgu.shape[1]
    chunks = d // LANES
    n_blocks = xb_tiles.shape[0] // (EXPERT_BLOCK * chunks)
    two_f = wgu.shape[2]
    dff = wd.shape[1]
    blk = lambda i, be, nu, ne, sl, br: (jnp.minimum(i, nu[0] - 1), 0)
    exp3 = lambda i, be, nu, ne, sl, br: (be[jnp.minimum(i, nu[0] - 1)], 0, 0)
    grid_spec = pltpu.PrefetchScalarGridSpec(
        num_scalar_prefetch=5,
        grid=(n_blocks,),
        in_specs=[pl.BlockSpec((EXPERT_BLOCK * chunks, LANES), blk),
                  pl.BlockSpec(memory_space=pl.ANY),
                  pl.BlockSpec((1, 1, two_f), exp3),
                  pl.BlockSpec(memory_space=pl.ANY),
                  pl.BlockSpec((1, 1, d), exp3)],
        out_specs=pl.BlockSpec((EXPERT_BLOCK * chunks, LANES), blk),
        scratch_shapes=[pltpu.VMEM((2, d, two_f), F32), pltpu.VMEM((2, dff, d), F32),
                        pltpu.VMEM((d, two_f), BF16), pltpu.VMEM((dff, d), BF16),
                        pltpu.SemaphoreType.DMA((2, 2))])
    return pl.pallas_call(
        _expert_kernel,
        grid_spec=grid_spec,
        out_shape=jax.ShapeDtypeStruct(xb_tiles.shape, F32),
        compiler_params=pltpu.CompilerParams(
            dimension_semantics=("arbitrary",), vmem_limit_bytes=VMEM_LIMIT),
        name="experts",
    )(blk_exp, n_used, next_exp, slot, blk_rows, xb_tiles, wgu, bgu, wd, bd)


def _combine_kernel(y0_ref, y1_ref, y2_ref, y3_ref, gate_ref, x2_ref, fw_ref, out_ref):
    gates = gate_ref[...]
    lane = lax.broadcasted_iota(I32, gates.shape, 1)
    x3 = x2_ref[...]
    for kk, y_ref in enumerate((y0_ref, y1_ref, y2_ref, y3_ref)):
        gk = jnp.sum(jnp.where(lane == kk, gates, 0.0), axis=-1, keepdims=True)
        x3 = x3 + gk * _load_token_tiles(y_ref, x3.shape[0])
    out_ref[...] = x3 * lax.rsqrt(jnp.mean(x3 * x3, axis=-1, keepdims=True) + NORM_EPS) * fw_ref[...]


def _combine_into_kernel(y0_ref, y1_ref, y2_ref, y3_ref, gate_ref, x2_ref, fw_ref, prev_ref, out_ref):
    del prev_ref
    _combine_kernel(y0_ref, y1_ref, y2_ref, y3_ref, gate_ref, x2_ref, fw_ref, out_ref)


def _combine(y4, gates, x2, fw, out_prev, chunk, *, tile):
    t, d = x2.shape
    tc = y4.shape[0] // (TOP_K * (d // LANES))
    steps = tc // tile
    first = chunk * steps
    choice = lambda kk: pl.BlockSpec((tile * (d // LANES), LANES), lambda i: (kk * steps + i, 0))
    in_specs = ([choice(kk) for kk in range(TOP_K)]
                + [pl.BlockSpec((tile, LANES), lambda i: (first + i, 0)),
                   pl.BlockSpec((tile, d), lambda i: (first + i, 0)),
                   pl.BlockSpec((1, d), lambda i: (0, 0))])
    args = [y4, y4, y4, y4, gates, x2, fw]
    if out_prev is not None:
        in_specs.append(pl.BlockSpec(memory_space=pl.ANY))
        args.append(out_prev)
    return pl.pallas_call(
        _combine_kernel if out_prev is None else _combine_into_kernel,
        grid=(steps,),
        in_specs=in_specs,
        out_specs=pl.BlockSpec((tile, d), lambda i: (first + i, 0)),
        out_shape=jax.ShapeDtypeStruct((t, d), F32),
        input_output_aliases={} if out_prev is None else {len(args) - 1: 0},
        compiler_params=pltpu.CompilerParams(
            dimension_semantics=("arbitrary",), vmem_limit_bytes=VMEM_LIMIT),
        name="combine",
    )(*args)


def _pad_lanes(a, offset=0, fill=0.0):
    out = jnp.full((1, LANES), fill, a.dtype)
    return out.at[0, offset:offset + a.shape[0]].set(a)


def _layer(x, attn_norm_w, w_in, gdn_conv_w, gdn_a_log, gdn_dt_bias, gdn_norm_w,
           cf_dw_w, cf_dw_b, cf_ln_w, cf_ln_b, w_out, ffn_norm_w, w_router, b_router,
           w_gate_up, b_gate_up, w_down, b_down, final_norm_w):
    b, s, d = x.shape
    t = b * s
    assert d == SUBLANES * LANES, "the token-tile layout needs one (8, 128) tile per token row"
    qk = GDN_HEADS * HEAD_DIM
    cfc = cf_dw_w.shape[1]
    off_b = 4 * qk
    off_cf = off_b + 2 * GDN_HEADS

    wm = w_in[:, :off_b].astype(BF16)
    wba = jnp.zeros((d, LANES), F32).at[:, :2 * GDN_HEADS].set(w_in[:, off_b:off_cf]).astype(BF16)
    wcf = w_in[:, off_cf:].astype(BF16)
    alog = _pad_lanes(gdn_a_log, GDN_HEADS)
    dtb = _pad_lanes(gdn_dt_bias, GDN_HEADS)

    q, k, v, z, bg, bgt, u = _inproj(
        x, attn_norm_w[None, :], wm, wba, wcf, gdn_conv_w, alog, dtb,
        cf_dw_w, cf_dw_b[None, :], cf_ln_w[None, :], cf_ln_b[None, :], tile=min(TOKEN_TILE, s))
    o = _gdn(q, k, v, bg, bgt, tile=min(TOKEN_TILE, s))

    wr = jnp.zeros((d, LANES), F32).at[:, :N_EXPERTS].set(w_router)
    br = _pad_lanes(b_router)
    x2, xn, route, gates, counts = _outproj_router(
        o.reshape(t, qk), z.reshape(t, qk), u.reshape(t, cfc), x.reshape(t, d),
        gdn_norm_w[None, :], w_out[:qk].astype(BF16), w_out[qk:].astype(BF16),
        ffn_norm_w[None, :], wr, br, tile=min(TOKEN_TILE, t))

    cnt = counts[0, :N_EXPERTS]
    nblk = (cnt + EXPERT_BLOCK - 1) // EXPERT_BLOCK
    blk_end = jnp.cumsum(nblk)
    pstart = (blk_end - nblk) * EXPERT_BLOCK
    n_blocks = (t * TOP_K) // EXPERT_BLOCK + N_EXPERTS
    blk_ids = jnp.arange(n_blocks, dtype=I32)
    blk_exp = jnp.minimum(
        jnp.sum((blk_end[None, :] <= blk_ids[:, None]).astype(I32), axis=1), N_EXPERTS - 1)
    n_used = blk_end[-1:].astype(I32)
    after = blk_end[blk_exp]
    next_exp = jnp.where(after < n_used[0], blk_exp[jnp.minimum(after, n_blocks - 1)], -1).astype(I32)
    slot = ((jnp.cumsum((nblk > 0).astype(I32)) - 1)[blk_exp] % 2).astype(I32)
    blk_rows = jnp.clip((pstart + cnt)[blk_exp] - blk_ids * EXPERT_BLOCK, 0, EXPERT_BLOCK).astype(I32)

    dest = _dest(route, _pad_lanes(pstart.astype(I32)), tile=min(DEST_TILE, t))
    n_rows = n_blocks * EXPERT_BLOCK
    chunks = d // LANES
    windows = t // ROW_WINDOW
    dest_win = dest.reshape(windows, ROW_WINDOW, TOP_K).transpose(0, 2, 1).reshape(windows, TOP_K * ROW_WINDOW)
    xb = _dispatch_rows(xn.reshape(t, chunks, LANES), dest_win, n_rows)
    yb = _experts(blk_exp, n_used, next_exp, slot, blk_rows, xb.reshape(n_rows * chunks, LANES), w_gate_up, b_gate_up[:, None, :],
                  w_down, b_down[:, None, :]).reshape(n_rows, chunks, LANES)

    tc = t // COMBINE_CHUNKS
    out = None
    for c in range(COMBINE_CHUNKS):
        src = dest[c * tc:(c + 1) * tc].T.reshape(TOP_K * tc // ROW_WINDOW, ROW_WINDOW)
        src_win = jnp.pad(src, ((0, 0), (0, LANES - ROW_WINDOW)))
        y4 = _collect_rows(yb, src_win).reshape(TOP_K * tc * chunks, LANES)
        out = _combine(y4, gates, x2, final_norm_w[None, :], out, c, tile=min(EXPERT_BLOCK, tc))
    return out.reshape(b, s, d)


def kernel(x, attn_norm_w, w_in, gdn_conv_w, gdn_a_log, gdn_dt_bias, gdn_norm_w, cf_dw_w, cf_dw_b,
           cf_ln_w, cf_ln_b, w_out, ffn_norm_w, w_router, b_router, w_gate_up, b_gate_up, w_down,
           b_down, final_norm_w):
    depth = w_in.shape[0]
    assert depth == 1, "the fused final norm assumes a single trunk layer"
    return _layer(x, attn_norm_w[0], w_in[0], gdn_conv_w[0], gdn_a_log[0], gdn_dt_bias[0],
                  gdn_norm_w[0], cf_dw_w[0], cf_dw_b[0], cf_ln_w[0], cf_ln_b[0], w_out[0],
                  ffn_norm_w[0], w_router[0], b_router[0], w_gate_up[0], b_gate_up[0], w_down[0],
                  b_down[0], final_norm_w)
```

```python
import functools

import jax
import jax.numpy as jnp
from jax import lax
from jax.experimental import pallas as pl
from jax.experimental.pallas import tpu as pltpu
from jax.experimental.pallas import tpu_sc as plsc

F32 = jnp.float32
BF16 = jnp.bfloat16
I32 = jnp.int32

NORM_EPS = 1e-6
LANES = 128
SUBLANES = 8
GDN_HEADS = 4
HEAD_DIM = 128
GDN_CHUNK = 64
GDN_CONV = 4
CF_KERNEL = 31
N_EXPERTS = 32
TOP_K = 4
SWIGLU_LIMIT = 7.0
SWIGLU_ALPHA = 1.702

QKV_HALO = 8
CF_HALO = 32
CF_ROWS = 64
EXPERT_BLOCK = 512
TOKEN_TILE = 512
DEST_TILE = 2048
VMEM_LIMIT = 56 * 1024 * 1024


def _silu(x):
    return x * jax.nn.sigmoid(x)


def _dot(a, b):
    return jnp.dot(a, b, preferred_element_type=F32)


def _store_token_tiles(ref, x):
    rows, d = x.shape
    chunks = d // LANES
    for c in range(chunks):
        ref[pl.ds(c, rows, stride=chunks), :] = x[:, c * LANES:(c + 1) * LANES]


def _load_token_tiles(ref, rows):
    chunks = ref.shape[0] // rows
    return jnp.concatenate([ref[pl.ds(c, rows, stride=chunks), :] for c in range(chunks)], axis=1)


def _inproj_kernel(x_ref, nw_ref, wm_ref, wba_ref, wcf_ref, cw_ref, alog_ref, dtb_ref,
                   dww_ref, dwb_ref, lnw_ref, lnb_ref,
                   q_ref, k_ref, v_ref, z_ref, bg_ref, bgt_ref, u_ref,
                   qkv_buf, cf_buf, cf_shift):
    tt = x_ref.shape[1]
    qk = GDN_HEADS * HEAD_DIM
    cfc = u_ref.shape[2]

    @pl.when(pl.program_id(1) == 0)
    def _():
        qkv_buf[0:QKV_HALO, :] = jnp.zeros((QKV_HALO, qkv_buf.shape[1]), F32)
        cf_buf[0:CF_HALO, :] = jnp.zeros((CF_HALO, cf_buf.shape[1]), F32)

    x = x_ref[0]
    h = x * lax.rsqrt(jnp.mean(x * x, axis=-1, keepdims=True) + NORM_EPS) * nw_ref[...]
    h = h.astype(BF16)
    pm = _dot(h, wm_ref[...])
    pba = _dot(h, wba_ref[...])
    pcf = _dot(h, wcf_ref[...])

    z_ref[0] = pm[:, 3 * qk:]

    qkv_buf[QKV_HALO:QKV_HALO + tt, :] = pm[:, :3 * qk]
    acc = None
    for j in range(GDN_CONV):
        term = cw_ref[j:j + 1, :] * qkv_buf[pl.ds(QKV_HALO - (GDN_CONV - 1) + j, tt), :]
        acc = term if acc is None else acc + term
    qkv_buf[0:QKV_HALO, :] = qkv_buf[tt:tt + QKV_HALO, :]
    qkv = _silu(acc)
    for hd in range(GDN_HEADS):
        for base, ref in ((0, q_ref), (qk, k_ref)):
            t = qkv[:, base + hd * HEAD_DIM: base + (hd + 1) * HEAD_DIM]
            t = t * lax.rsqrt(jnp.sum(t * t, axis=-1, keepdims=True) + NORM_EPS)
            ref[0, :, hd * HEAD_DIM:(hd + 1) * HEAD_DIM] = t
    v_ref[0] = qkv[:, 2 * qk:]

    lane = lax.broadcasted_iota(I32, (tt, LANES), 1)
    row = lax.broadcasted_iota(I32, (tt, LANES), 0)
    beta = jax.nn.sigmoid(pba)
    sp_in = pba + dtb_ref[...]
    softplus = jnp.maximum(sp_in, 0.0) + jnp.log(1.0 + jnp.exp(-jnp.abs(sp_in)))
    g = -jnp.exp(alog_ref[...]) * softplus
    g = jnp.where((lane >= GDN_HEADS) & (lane < 2 * GDN_HEADS), g, 0.0)
    pos = row % GDN_CHUNK
    shift = 1
    while shift < GDN_CHUNK:
        g = g + jnp.where(pos >= shift, pltpu.roll(g, shift, 0), 0.0)
        shift *= 2
    bg = jnp.where(lane < GDN_HEADS, beta, g)
    bg_ref[0] = bg
    bgt_ref[0] = jnp.transpose(bg)[0:SUBLANES, :]

    glu = pcf[:, :cfc] * jax.nn.sigmoid(pcf[:, cfc:])
    cf_buf[CF_HALO:CF_HALO + tt, :] = glu
    lo = SUBLANES
    span = tt + CF_HALO - lo
    for r in range(1, SUBLANES):
        cf_shift[r - 1, lo:lo + span, :] = cf_buf[pl.ds(lo - r, span), :]
    rows = CF_ROWS
    for r0 in range(0, tt, rows):
        acc = None
        for j in range(CF_KERNEL):
            a, r = divmod(CF_KERNEL - 1 - j, SUBLANES)
            start = CF_HALO + r0 - a * SUBLANES
            src = cf_buf[start:start + rows, :] if r == 0 else cf_shift[r - 1, start:start + rows, :]
            term = dww_ref[j:j + 1, :] * src
            acc = term if acc is None else acc + term
        c = acc + dwb_ref[...]
        mu = jnp.mean(c, axis=-1, keepdims=True)
        cc = c - mu
        y = cc * lax.rsqrt(jnp.mean(cc * cc, axis=-1, keepdims=True) + NORM_EPS)
        u_ref[0, r0:r0 + rows, :] = _silu(y * lnw_ref[...] + lnb_ref[...])
    cf_buf[0:CF_HALO, :] = cf_buf[tt:tt + CF_HALO, :]


def _inproj(x, nw, wm, wba, wcf, cw, alog, dtb, dww, dwb, lnw, lnb, *, tile):
    b, s, d = x.shape
    qk = GDN_HEADS * HEAD_DIM
    cfc = dww.shape[1]
    grid = (b, s // tile)
    full = lambda a: pl.BlockSpec(a.shape, lambda i, j: (0,) * a.ndim)
    tok = lambda w: pl.BlockSpec((1, tile, w), lambda i, j: (i, j, 0))
    out_shape = (
        jax.ShapeDtypeStruct((b, s, qk), F32), jax.ShapeDtypeStruct((b, s, qk), F32),
        jax.ShapeDtypeStruct((b, s, qk), F32), jax.ShapeDtypeStruct((b, s, qk), F32),
        jax.ShapeDtypeStruct((b, s, LANES), F32), jax.ShapeDtypeStruct((b, SUBLANES, s), F32),
        jax.ShapeDtypeStruct((b, s, cfc), F32))
    return pl.pallas_call(
        _inproj_kernel,
        grid=grid,
        in_specs=[tok(d)] + [full(a) for a in (nw, wm, wba, wcf, cw, alog, dtb, dww, dwb, lnw, lnb)],
        out_specs=(tok(qk), tok(qk), tok(qk), tok(qk), tok(LANES),
                   pl.BlockSpec((1, SUBLANES, tile), lambda i, j: (i, 0, j)), tok(cfc)),
        out_shape=out_shape,
        scratch_shapes=[pltpu.VMEM((QKV_HALO + tile, 3 * qk), F32),
                        pltpu.VMEM((CF_HALO + tile, cfc), F32),
                        pltpu.VMEM((SUBLANES - 1, CF_HALO + tile, cfc), F32)],
        compiler_params=pltpu.CompilerParams(
            dimension_semantics=("arbitrary", "arbitrary"), vmem_limit_bytes=VMEM_LIMIT),
        name="inproj",
    )(x, nw, wm, wba, wcf, cw, alog, dtb, dww, dwb, lnw, lnb)


def _bmm(a, b):
    return jnp.einsum("bmk,bkn->bmn", a, b, preferred_element_type=F32)


def _bmm_nt(a, b):
    return jnp.einsum("bmk,bnk->bmn", a, b, preferred_element_type=F32)


def _bmm_tn(a, b):
    return jnp.einsum("bkm,bkn->bmn", a, b, preferred_element_type=F32)


def _unit_lower_inverse(a):
    c = a.shape[-1]
    ii = lax.broadcasted_iota(I32, (c, c), 0)
    jj = lax.broadcasted_iota(I32, (c, c), 1)
    eye = (ii == jj).astype(F32)
    same16 = (ii // 16) == (jj // 16)
    same32 = (ii // 32) == (jj // 32)
    x = jnp.where(same16, -a, 0.0)
    t = eye + x
    xp = x
    for _ in range(3):
        xp_b = xp.astype(BF16)
        xp = _bmm(xp_b, xp_b)
        t = t + _bmm(t.astype(BF16), xp.astype(BF16))
    for off in (jnp.where(same32 & ~same16, a, 0.0), jnp.where(~same32, a, 0.0)):
        tb = t.astype(BF16)
        t = t - _bmm(tb, _bmm(off.astype(BF16), tb).astype(BF16))
    return t


def _gdn_kernel(q_ref, k_ref, v_ref, bg_ref, bgt_ref, o_ref, state, s_all):
    lt = q_ref.shape[1]
    c = GDN_CHUNK
    nh = GDN_HEADS
    nc = lt // c

    @pl.when(pl.program_id(1) == 0)
    def _():
        state[...] = jnp.zeros(state.shape, F32)

    def stack(fn):
        return jnp.stack([fn(slice(n * c, (n + 1) * c), h) for n in range(nc) for h in range(nh)])

    head = lambda h: slice(h * HEAD_DIM, (h + 1) * HEAD_DIM)
    q = stack(lambda r, h: q_ref[0, r, head(h)]) * (HEAD_DIM ** -0.5)
    k = stack(lambda r, h: k_ref[0, r, head(h)])
    v = stack(lambda r, h: v_ref[0, r, head(h)])
    beta = stack(lambda r, h: bg_ref[0, r, h:h + 1])
    gcol = stack(lambda r, h: bg_ref[0, r, nh + h:nh + h + 1])
    grow = stack(lambda r, h: bgt_ref[0, nh + h:nh + h + 1, r])

    ii = lax.broadcasted_iota(I32, (c, c), 0)
    jj = lax.broadcasted_iota(I32, (c, c), 1)
    glast = gcol[:, c - 1:c, :]
    eg = jnp.exp(gcol)
    decay = jnp.where(ii >= jj, jnp.exp(jnp.minimum(gcol - grow, 0.0)), 0.0)
    kb = k * beta
    k_b = k.astype(BF16)
    a = jnp.where(ii > jj, _bmm_nt(kb.astype(BF16), k_b) * decay, 0.0)
    t = _unit_lower_inverse(a)
    rhs = jnp.concatenate([v * beta, kb * eg], axis=-1).astype(BF16)
    sol = _bmm(t.astype(BF16), rhs)
    u_val = sol[..., :HEAD_DIM]
    w_key = sol[..., HEAD_DIM:]
    intra = _bmm_nt(q.astype(BF16), k_b) * decay
    k_tail = (k * jnp.exp(glast - gcol)).astype(BF16)
    upd = _bmm_tn(k_tail, sol.astype(BF16))
    b_mat = upd[..., :HEAD_DIM]
    p_mat = upd[..., HEAD_DIM:].astype(BF16)
    g_tot = jnp.exp(glast)

    s = state[...]
    for n in range(nc):
        grp = slice(n * nh, (n + 1) * nh)
        s_b = s.astype(BF16)
        s_all[grp] = s_b
        s = s * g_tot[grp] - _bmm(p_mat[grp], s_b) + b_mat[grp]
    state[...] = s

    wq = jnp.concatenate([w_key, q * eg], axis=1).astype(BF16)
    ws_qs = _bmm(wq, s_all[...])
    v_new = u_val - ws_qs[:, :c]
    o = ws_qs[:, c:] + _bmm(intra.astype(BF16), v_new.astype(BF16))
    for n in range(nc):
        for h in range(nh):
            o_ref[0, n * c:(n + 1) * c, head(h)] = o[n * nh + h]


def _gdn(q, k, v, bg, bgt, *, tile):
    b, s, qk = q.shape
    grid = (b, s // tile)
    tok = lambda w: pl.BlockSpec((1, tile, w), lambda i, j: (i, j, 0))
    n_prob = (tile // GDN_CHUNK) * GDN_HEADS
    return pl.pallas_call(
        _gdn_kernel,
        grid=grid,
        in_specs=[tok(qk), tok(qk), tok(qk), tok(LANES),
                  pl.BlockSpec((1, SUBLANES, tile), lambda i, j: (i, 0, j))],
        out_specs=tok(qk),
        out_shape=jax.ShapeDtypeStruct((b, s, qk), F32),
        scratch_shapes=[pltpu.VMEM((GDN_HEADS, HEAD_DIM, HEAD_DIM), F32),
                        pltpu.VMEM((n_prob, HEAD_DIM, HEAD_DIM), BF16)],
        compiler_params=pltpu.CompilerParams(
            dimension_semantics=("arbitrary", "arbitrary"), vmem_limit_bytes=VMEM_LIMIT),
        name="gdn",
    )(q, k, v, bg, bgt)


def _split_bf16(x):
    hi = x.astype(BF16)
    lo = (x - hi.astype(F32)).astype(BF16)
    return hi, lo


def _outproj_router_kernel(o_ref, z_ref, u_ref, x_ref, gnw_ref, wa_ref, wb_ref, fnw_ref,
                           wr_ref, br_ref,
                           x2_ref, xn_ref, route_ref, gate_ref, counts_ref, carry):
    tt = x_ref.shape[0]

    @pl.when(pl.program_id(0) == 0)
    def _():
        carry[...] = jnp.zeros(carry.shape, F32)

    parts = []
    for hd in range(GDN_HEADS):
        sl = slice(hd * HEAD_DIM, (hd + 1) * HEAD_DIM)
        oh = o_ref[:, sl]
        y = oh * lax.rsqrt(jnp.mean(oh * oh, axis=-1, keepdims=True) + NORM_EPS) * gnw_ref[...]
        parts.append((y * _silu(z_ref[:, sl])).astype(BF16))
    out_a = jnp.concatenate(parts, axis=-1)
    x2 = x_ref[...] + _dot(out_a, wa_ref[...]) + _dot(u_ref[...].astype(BF16), wb_ref[...])
    x2_ref[...] = x2

    xn = x2 * lax.rsqrt(jnp.mean(x2 * x2, axis=-1, keepdims=True) + NORM_EPS) * fnw_ref[...]
    _store_token_tiles(xn_ref, xn)

    xh, xl = _split_bf16(xn)
    wh, wl = _split_bf16(wr_ref[...])
    logits = _dot(xh, wh) + _dot(xh, wl) + _dot(xl, wh) + br_ref[...]

    lane = lax.broadcasted_iota(I32, (tt, LANES), 1)
    lane_f = lane.astype(F32)
    neg = jnp.float32(-jnp.inf)
    work = jnp.where(lane < N_EXPERTS, logits, neg)
    vals, idxs = [], []
    onehot = jnp.zeros((tt, LANES), F32)
    for _ in range(TOP_K):
        m = jnp.max(work, axis=-1, keepdims=True)
        idx = jnp.min(jnp.where(work == m, lane_f, float(LANES)), axis=-1, keepdims=True).astype(I32)
        sel = lane == idx
        vals.append(m)
        idxs.append(idx)
        onehot = onehot + sel.astype(F32)
        work = jnp.where(sel, neg, work)
    exps = [jnp.exp(v - vals[0]) for v in vals]
    denom = exps[0] + exps[1] + exps[2] + exps[3]

    ri = lax.broadcasted_iota(I32, (tt, tt), 0)
    ci = lax.broadcasted_iota(I32, (tt, tt), 1)
    strict = (ri > ci).astype(BF16)
    base = carry[...] + _dot(strict, onehot.astype(BF16))
    route = jnp.zeros((tt, LANES), I32)
    gates = jnp.zeros((tt, LANES), F32)
    for kk in range(TOP_K):
        rank = jnp.sum(jnp.where(lane == idxs[kk], base, 0.0), axis=-1, keepdims=True)
        route = jnp.where(lane == kk, idxs[kk], route)
        route = jnp.where(lane == kk + TOP_K, rank.astype(I32), route)
        gates = jnp.where(lane == kk, exps[kk] / denom, gates)
    route_ref[...] = route
    gate_ref[...] = gates
    new_carry = carry[...] + jnp.sum(onehot, axis=0, keepdims=True)
    carry[...] = new_carry
    counts_ref[...] = new_carry.astype(I32)


def _outproj_router(o, z, u, x, gnw, wa, wb, fnw, wr, br, *, tile):
    t, d = x.shape
    grid = (t // tile,)
    full = lambda a: pl.BlockSpec(a.shape, lambda i: (0,) * a.ndim)
    tok = lambda w: pl.BlockSpec((tile, w), lambda i: (i, 0))
    return pl.pallas_call(
        _outproj_router_kernel,
        grid=grid,
        in_specs=[tok(o.shape[1]), tok(z.shape[1]), tok(u.shape[1]), tok(d)]
                 + [full(a) for a in (gnw, wa, wb, fnw, wr, br)],
        out_specs=(tok(d), pl.BlockSpec((tile * (d // LANES), LANES), lambda i: (i, 0)),
                   tok(LANES), tok(LANES), pl.BlockSpec((1, LANES), lambda i: (0, 0))),
        out_shape=(jax.ShapeDtypeStruct((t, d), F32), jax.ShapeDtypeStruct((t * (d // LANES), LANES), F32),
                   jax.ShapeDtypeStruct((t, LANES), I32), jax.ShapeDtypeStruct((t, LANES), F32),
                   jax.ShapeDtypeStruct((1, LANES), I32)),
        scratch_shapes=[pltpu.VMEM((1, LANES), F32)],
        compiler_params=pltpu.CompilerParams(
            dimension_semantics=("arbitrary",), vmem_limit_bytes=VMEM_LIMIT),
        name="outproj_router",
    )(o, z, u, x, gnw, wa, wb, fnw, wr, br)


def _dest_kernel(route_ref, pstart_ref, dest_ref):
    route = route_ref[...].astype(F32)
    tt = route.shape[0]
    lane = lax.broadcasted_iota(I32, (tt, LANES), 1)
    pstart = pstart_ref[...].astype(F32)
    dest = jnp.zeros((tt, LANES), F32)
    for kk in range(TOP_K):
        idx = jnp.sum(jnp.where(lane == kk, route, 0.0), axis=-1, keepdims=True)
        rank = jnp.sum(jnp.where(lane == kk + TOP_K, route, 0.0), axis=-1, keepdims=True)
        start = jnp.sum(jnp.where(lane == idx.astype(I32), pstart, 0.0), axis=-1, keepdims=True)
        dest = jnp.where(lane == kk, start + rank, dest)
    dest_ref[...] = dest[:, :TOP_K].astype(I32)


def _dest(route, pstart, *, tile):
    t = route.shape[0]
    return pl.pallas_call(
        _dest_kernel,
        grid=(t // tile,),
        in_specs=[pl.BlockSpec((tile, LANES), lambda i: (i, 0)),
                  pl.BlockSpec((1, LANES), lambda i: (0, 0))],
        out_specs=pl.BlockSpec((tile, TOP_K), lambda i: (i, 0)),
        out_shape=jax.ShapeDtypeStruct((t, TOP_K), I32),
        compiler_params=pltpu.CompilerParams(dimension_semantics=("arbitrary",)),
        name="dest_rows",
    )(route, pstart)


ROW_WINDOW = 32
COMBINE_CHUNKS = 8


def _sc_mesh():
    return plsc.VectorSubcoreMesh(core_axis_name="core", subcore_axis_name="subcore")


def _dispatch_rows(xn_tiles, dest_win, n_rows):
    t = xn_tiles.shape[0]

    @functools.partial(
        pl.kernel, mesh=_sc_mesh(), scratch_types=[],
        out_type=jax.ShapeDtypeStruct((n_rows,) + xn_tiles.shape[1:], xn_tiles.dtype))
    def dispatch(x_hbm, idx_hbm, o_hbm):
        def body(x_vmem, idx_vmem):
            for kk in range(TOP_K):
                pltpu.sync_copy(x_vmem, o_hbm.at[idx_vmem.at[0, pl.ds(kk * ROW_WINDOW, ROW_WINDOW)]])

        pltpu.emit_pipeline(
            body,
            grid=(t // ROW_WINDOW,),
            in_specs=[pl.BlockSpec((ROW_WINDOW,) + xn_tiles.shape[1:], lambda i: (i, 0, 0)),
                      pl.BlockSpec((1, TOP_K * ROW_WINDOW), lambda i: (i, 0))],
            out_specs=[],
            core_axis_name=("core", "subcore"),
            dimension_semantics=(pltpu.PARALLEL,),
        )(x_hbm, idx_hbm)

    return dispatch(xn_tiles, dest_win)


def _collect_rows(yb_tiles, src_win):
    n_pairs = src_win.shape[0] * ROW_WINDOW

    @functools.partial(
        pl.kernel, mesh=_sc_mesh(), scratch_types=[],
        out_type=jax.ShapeDtypeStruct((n_pairs,) + yb_tiles.shape[1:], yb_tiles.dtype))
    def collect(y_hbm, idx_hbm, o_hbm):
        def body(idx_vmem, o_vmem):
            pltpu.sync_copy(y_hbm.at[idx_vmem.at[0, pl.ds(0, ROW_WINDOW)]], o_vmem)

        pltpu.emit_pipeline(
            body,
            grid=(n_pairs // ROW_WINDOW,),
            in_specs=[pl.BlockSpec((1, LANES), lambda i: (i, 0))],
            out_specs=[pl.BlockSpec((ROW_WINDOW,) + yb_tiles.shape[1:], lambda i: (i, 0, 0))],
            core_axis_name=("core", "subcore"),
            dimension_semantics=(pltpu.PARALLEL,),
        )(idx_hbm, o_hbm)

    return collect(yb_tiles, src_win)


def _expert_kernel(blk_exp_ref, n_used_ref, next_exp_ref, slot_ref, blk_rows_ref,
                   xb_ref, wgu_hbm, bgu_ref, wd_hbm, bd_ref, yb_ref,
                   wgu_f, wd_f, wgu_b, wd_b, wsem):
    i = pl.program_id(0)
    n_used = n_used_ref[0]
    bm = xb_ref.shape[0] // SUBLANES
    dff = wd_f.shape[1]

    def weight_copies(e, s):
        return (pltpu.make_async_copy(wgu_hbm.at[e], wgu_f.at[s], wsem.at[0, s]),
                pltpu.make_async_copy(wd_hbm.at[e], wd_f.at[s], wsem.at[1, s]))

    prev = blk_exp_ref[jnp.maximum(i - 1, 0)]
    first_of_expert = (i == 0) | (blk_exp_ref[i] != prev)

    @pl.when(first_of_expert & (i < n_used))
    def _():
        e = blk_exp_ref[i]
        s = slot_ref[i]

        @pl.when(i == 0)
        def _():
            for cp in weight_copies(e, s):
                cp.start()

        for cp in weight_copies(e, s):
            cp.wait()
        wgu_b[...] = wgu_f[s].astype(BF16)
        wd_b[...] = wd_f[s].astype(BF16)

        nxt = next_exp_ref[i]

        @pl.when(nxt >= 0)
        def _():
            for cp in weight_copies(nxt, 1 - s):
                cp.start()

    def swiglu_rows(rows):
        x_rows = xb_ref.at[pl.ds(0, rows * SUBLANES)]
        y_rows = yb_ref.at[pl.ds(0, rows * SUBLANES)]
        xb = _load_token_tiles(x_rows, rows).astype(BF16)
        hid = _dot(xb, wgu_b[...]) + bgu_ref[0]
        gate = jnp.minimum(hid[:, :dff], SWIGLU_LIMIT)
        up = jnp.clip(hid[:, dff:], -SWIGLU_LIMIT, SWIGLU_LIMIT)
        glu = gate * jax.nn.sigmoid(SWIGLU_ALPHA * gate)
        act = ((up + 1.0) * glu).astype(BF16)
        _store_token_tiles(y_rows, _dot(act, wd_b[...]) + bd_ref[0])

    used = i < n_used
    half_full = blk_rows_ref[i] <= bm // 2
    pl.when(used & jnp.logical_not(half_full))(functools.partial(swiglu_rows, bm))
    pl.when(used & half_full)(functools.partial(swiglu_rows, bm // 2))


def _experts(blk_exp, n_used, next_exp, slot, blk_rows, xb_tiles, wgu, bgu, wd, bd):
    d = wgu.shape[1]
    chunks = d // LANES
    n_blocks = xb_tiles.shape[0] // (EXPERT_BLOCK * chunks)
    two_f = wgu.shape[2]
    dff = wd.shape[1]
    blk = lambda i, be, nu, ne, sl, br: (jnp.minimum(i, nu[0] - 1), 0)
    exp3 = lambda i, be, nu, ne, sl, br: (be[jnp.minimum(i, nu[0] - 1)], 0, 0)
    grid_spec = pltpu.PrefetchScalarGridSpec(
        num_scalar_prefetch=5,
        grid=(n_blocks,),
        in_specs=[pl.BlockSpec((EXPERT_BLOCK * chunks, LANES), blk),
                  pl.BlockSpec(memory_space=pl.ANY),
                  pl.BlockSpec((1, 1, two_f), exp3),
                  pl.BlockSpec(memory_space=pl.ANY),
                  pl.BlockSpec((1, 1, d), exp3)],
        out_specs=pl.BlockSpec((EXPERT_BLOCK * chunks, LANES), blk),
        scratch_shapes=[pltpu.VMEM((2, d, two_f), F32), pltpu.VMEM((2, dff, d), F32),
                        pltpu.VMEM((d, two_f), BF16), pltpu.VMEM((dff, d), BF16),
                        pltpu.SemaphoreType.DMA((2, 2))])
    return pl.pallas_call(
        _expert_kernel,
        grid_spec=grid_spec,
        out_shape=jax.ShapeDtypeStruct(xb_tiles.shape, F32),
        compiler_params=pltpu.CompilerParams(
            dimension_semantics=("arbitrary",), vmem_limit_bytes=VMEM_LIMIT),
        name="experts",
    )(blk_exp, n_used, next_exp, slot, blk_rows, xb_tiles, wgu, bgu, wd, bd)


def _combine_kernel(y0_ref, y1_ref, y2_ref, y3_ref, gate_ref, x2_ref, fw_ref, out_ref):
    gates = gate_ref[...]
    lane = lax.broadcasted_iota(I32, gates.shape, 1)
    x3 = x2_ref[...]
    for kk, y_ref in enumerate((y0_ref, y1_ref, y2_ref, y3_ref)):
        gk = jnp.sum(jnp.where(lane == kk, gates, 0.0), axis=-1, keepdims=True)
        x3 = x3 + gk * _load_token_tiles(y_ref, x3.shape[0])
    out_ref[...] = x3 * lax.rsqrt(jnp.mean(x3 * x3, axis=-1, keepdims=True) + NORM_EPS) * fw_ref[...]


def _combine_into_kernel(y0_ref, y1_ref, y2_ref, y3_ref, gate_ref, x2_ref, fw_ref, prev_ref, out_ref):
    del prev_ref
    _combine_kernel(y0_ref, y1_ref, y2_ref, y3_ref, gate_ref, x2_ref, fw_ref, out_ref)


def _combine(y4, gates, x2, fw, out_prev, chunk, *, tile):
    t, d = x2.shape
    tc = y4.shape[0] // (TOP_K * (d // LANES))
    steps = tc // tile
    first = chunk * steps
    choice = lambda kk: pl.BlockSpec((tile * (d // LANES), LANES), lambda i: (kk * steps + i, 0))
    in_specs = ([choice(kk) for kk in range(TOP_K)]
                + [pl.BlockSpec((tile, LANES), lambda i: (first + i, 0)),
                   pl.BlockSpec((tile, d), lambda i: (first + i, 0)),
                   pl.BlockSpec((1, d), lambda i: (0, 0))])
    args = [y4, y4, y4, y4, gates, x2, fw]
    if out_prev is not None:
        in_specs.append(pl.BlockSpec(memory_space=pl.ANY))
        args.append(out_prev)
    return pl.pallas_call(
        _combine_kernel if out_prev is None else _combine_into_kernel,
        grid=(steps,),
        in_specs=in_specs,
        out_specs=pl.BlockSpec((tile, d), lambda i: (first + i, 0)),
        out_shape=jax.ShapeDtypeStruct((t, d), F32),
        input_output_aliases={} if out_prev is None else {len(args) - 1: 0},
        compiler_params=pltpu.CompilerParams(
            dimension_semantics=("arbitrary",), vmem_limit_bytes=VMEM_LIMIT),
        name="combine",
    )(*args)


def _pad_lanes(a, offset=0, fill=0.0):
    out = jnp.full((1, LANES), fill, a.dtype)
    return out.at[0, offset:offset + a.shape[0]].set(a)


def _layer(x, attn_norm_w, w_in, gdn_conv_w, gdn_a_log, gdn_dt_bias, gdn_norm_w,
           cf_dw_w, cf_dw_b, cf_ln_w, cf_ln_b, w_out, ffn_norm_w, w_router, b_router,
           w_gate_up, b_gate_up, w_down, b_down, final_norm_w):
    b, s, d = x.shape
    t = b * s
    assert d == SUBLANES * LANES, "the token-tile layout needs one (8, 128) tile per token row"
    qk = GDN_HEADS * HEAD_DIM
    cfc = cf_dw_w.shape[1]
    off_b = 4 * qk
    off_cf = off_b + 2 * GDN_HEADS

    wm = w_in[:, :off_b].astype(BF16)
    wba = jnp.zeros((d, LANES), F32).at[:, :2 * GDN_HEADS].set(w_in[:, off_b:off_cf]).astype(BF16)
    wcf = w_in[:, off_cf:].astype(BF16)
    alog = _pad_lanes(gdn_a_log, GDN_HEADS)
    dtb = _pad_lanes(gdn_dt_bias, GDN_HEADS)

    q, k, v, z, bg, bgt, u = _inproj(
        x, attn_norm_w[None, :], wm, wba, wcf, gdn_conv_w, alog, dtb,
        cf_dw_w, cf_dw_b[None, :], cf_ln_w[None, :], cf_ln_b[None, :], tile=min(TOKEN_TILE, s))
    o = _gdn(q, k, v, bg, bgt, tile=min(TOKEN_TILE, s))

    wr = jnp.zeros((d, LANES), F32).at[:, :N_EXPERTS].set(w_router)
    br = _pad_lanes(b_router)
    x2, xn, route, gates, counts = _outproj_router(
        o.reshape(t, qk), z.reshape(t, qk), u.reshape(t, cfc), x.reshape(t, d),
        gdn_norm_w[None, :], w_out[:qk].astype(BF16), w_out[qk:].astype(BF16),
        ffn_norm_w[None, :], wr, br, tile=min(TOKEN_TILE, t))

    cnt = counts[0, :N_EXPERTS]
    nblk = (cnt + EXPERT_BLOCK - 1) // EXPERT_BLOCK
    blk_end = jnp.cumsum(nblk)
    pstart = (blk_end - nblk) * EXPERT_BLOCK
    n_blocks = (t * TOP_K) // EXPERT_BLOCK + N_EXPERTS
    blk_ids = jnp.arange(n_blocks, dtype=I32)
    blk_exp = jnp.minimum(
        jnp.sum((blk_end[None, :] <= blk_ids[:, None]).astype(I32), axis=1), N_EXPERTS - 1)
    n_used = blk_end[-1:].astype(I32)
    experts = jnp.arange(N_EXPERTS, dtype=I32)
    present = nblk > 0
    later = jnp.where((experts[None, :] > experts[:, None]) & present[None, :], experts[None, :], N_EXPERTS)
    next_of = jnp.min(later, axis=1)
    next_of = jnp.where(next_of < N_EXPERTS, next_of, -1)
    slot_of = (jnp.cumsum(present.astype(I32)) - 1) % 2
    of_block = (blk_exp[:, None] == experts[None, :]).astype(I32)
    next_exp = jnp.sum(of_block * next_of[None, :], axis=1).astype(I32)
    slot = jnp.sum(of_block * slot_of[None, :], axis=1).astype(I32)
    row_end = jnp.sum(of_block * (pstart + cnt)[None, :], axis=1)
    blk_rows = jnp.clip(row_end - blk_ids * EXPERT_BLOCK, 0, EXPERT_BLOCK).astype(I32)

    dest = _dest(route, _pad_lanes(pstart.astype(I32)), tile=min(DEST_TILE, t))
    n_rows = n_blocks * EXPERT_BLOCK
    chunks = d // LANES
    windows = t // ROW_WINDOW
    dest_win = dest.reshape(windows, ROW_WINDOW, TOP_K).transpose(0, 2, 1).reshape(windows, TOP_K * ROW_WINDOW)
    xb = _dispatch_rows(xn.reshape(t, chunks, LANES), dest_win, n_rows)
    yb = _experts(blk_exp, n_used, next_exp, slot, blk_rows, xb.reshape(n_rows * chunks, LANES), w_gate_up, b_gate_up[:, None, :],
                  w_down, b_down[:, None, :]).reshape(n_rows, chunks, LANES)

    tc = t // COMBINE_CHUNKS
    out = None
    for c in range(COMBINE_CHUNKS):
        src = dest[c * tc:(c + 1) * tc].T.reshape(TOP_K * tc // ROW_WINDOW, ROW_WINDOW)
        src_win = jnp.pad(src, ((0, 0), (0, LANES - ROW_WINDOW)))
        y4 = _collect_rows(yb, src_win).reshape(TOP_K * tc * chunks, LANES)
        out = _combine(y4, gates, x2, final_norm_w[None, :], out, c, tile=min(EXPERT_BLOCK, tc))
    return out.reshape(b, s, d)


def kernel(x, attn_norm_w, w_in, gdn_conv_w, gdn_a_log, gdn_dt_bias, gdn_norm_w, cf_dw_w, cf_dw_b,
           cf_ln_w, cf_ln_b, w_out, ffn_norm_w, w_router, b_router, w_gate_up, b_gate_up, w_down,
           b_down, final_norm_w):
    depth = w_in.shape[0]
    assert depth == 1, "the fused final norm assumes a single trunk layer"
    return _layer(x, attn_norm_w[0], w_in[0], gdn_conv_w[0], gdn_a_log[0], gdn_dt_bias[0],
                  gdn_norm_w[0], cf_dw_w[0], cf_dw_b[0], cf_ln_w[0], cf_ln_b[0], w_out[0],
                  ffn_norm_w[0], w_router[0], b_router[0], w_gate_up[0], b_gate_up[0], w_down[0],
                  b_down[0], final_norm_w)
```

```python
import functools

import jax
import jax.numpy as jnp
from jax import lax
from jax.experimental import pallas as pl
from jax.experimental.pallas import tpu as pltpu
from jax.experimental.pallas import tpu_sc as plsc

F32 = jnp.float32
BF16 = jnp.bfloat16
I32 = jnp.int32

NORM_EPS = 1e-6
LANES = 128
SUBLANES = 8
GDN_HEADS = 4
HEAD_DIM = 128
GDN_CHUNK = 64
GDN_CONV = 4
CF_KERNEL = 31
N_EXPERTS = 32
TOP_K = 4
SWIGLU_LIMIT = 7.0
SWIGLU_ALPHA = 1.702

QKV_HALO = 8
CF_HALO = 32
CF_ROWS = 128
EXPERT_BLOCK = 512
TOKEN_TILE = 512
DEST_TILE = 2048
VMEM_LIMIT = 56 * 1024 * 1024


def _silu(x):
    return x * jax.nn.sigmoid(x)


def _dot(a, b):
    return jnp.dot(a, b, preferred_element_type=F32)


def _store_token_tiles(ref, x):
    rows, d = x.shape
    chunks = d // LANES
    for c in range(chunks):
        ref[pl.ds(c, rows, stride=chunks), :] = x[:, c * LANES:(c + 1) * LANES]


def _load_token_tiles(ref, rows):
    chunks = ref.shape[0] // rows
    return jnp.concatenate([ref[pl.ds(c, rows, stride=chunks), :] for c in range(chunks)], axis=1)


def _inproj_kernel(x_ref, nw_ref, wm_ref, wba_ref, wcf_ref, cw_ref, alog_ref, dtb_ref,
                   dww_ref, dwb_ref, lnw_ref, lnb_ref,
                   q_ref, k_ref, v_ref, z_ref, bg_ref, bgt_ref, u_ref,
                   qkv_buf, cf_buf, cf_shift):
    tt = x_ref.shape[1]
    qk = GDN_HEADS * HEAD_DIM
    cfc = u_ref.shape[2]

    @pl.when(pl.program_id(1) == 0)
    def _():
        qkv_buf[0:QKV_HALO, :] = jnp.zeros((QKV_HALO, qkv_buf.shape[1]), F32)
        cf_buf[0:CF_HALO, :] = jnp.zeros((CF_HALO, cf_buf.shape[1]), F32)

    x = x_ref[0]
    h = x * lax.rsqrt(jnp.mean(x * x, axis=-1, keepdims=True) + NORM_EPS) * nw_ref[...]
    h = h.astype(BF16)
    pm = _dot(h, wm_ref[...])
    pba = _dot(h, wba_ref[...])
    pcf = _dot(h, wcf_ref[...])

    z_ref[0] = pm[:, 3 * qk:]

    qkv_buf[QKV_HALO:QKV_HALO + tt, :] = pm[:, :3 * qk]
    acc = None
    for j in range(GDN_CONV):
        term = cw_ref[j:j + 1, :] * qkv_buf[pl.ds(QKV_HALO - (GDN_CONV - 1) + j, tt), :]
        acc = term if acc is None else acc + term
    qkv_buf[0:QKV_HALO, :] = qkv_buf[tt:tt + QKV_HALO, :]
    qkv = _silu(acc)
    for hd in range(GDN_HEADS):
        for base, ref in ((0, q_ref), (qk, k_ref)):
            t = qkv[:, base + hd * HEAD_DIM: base + (hd + 1) * HEAD_DIM]
            t = t * lax.rsqrt(jnp.sum(t * t, axis=-1, keepdims=True) + NORM_EPS)
            ref[0, :, hd * HEAD_DIM:(hd + 1) * HEAD_DIM] = t
    v_ref[0] = qkv[:, 2 * qk:]

    lane = lax.broadcasted_iota(I32, (tt, LANES), 1)
    row = lax.broadcasted_iota(I32, (tt, LANES), 0)
    beta = jax.nn.sigmoid(pba)
    sp_in = pba + dtb_ref[...]
    softplus = jnp.maximum(sp_in, 0.0) + jnp.log(1.0 + jnp.exp(-jnp.abs(sp_in)))
    g = -jnp.exp(alog_ref[...]) * softplus
    g = jnp.where((lane >= GDN_HEADS) & (lane < 2 * GDN_HEADS), g, 0.0)
    pos = row % GDN_CHUNK
    shift = 1
    while shift < GDN_CHUNK:
        g = g + jnp.where(pos >= shift, pltpu.roll(g, shift, 0), 0.0)
        shift *= 2
    bg = jnp.where(lane < GDN_HEADS, beta, g)
    bg_ref[0] = bg
    bgt_ref[0] = jnp.transpose(bg)[0:SUBLANES, :]

    glu = pcf[:, :cfc] * jax.nn.sigmoid(pcf[:, cfc:])
    cf_buf[CF_HALO:CF_HALO + tt, :] = glu
    lo = SUBLANES
    span = tt + CF_HALO - lo
    for r in range(1, SUBLANES):
        cf_shift[r - 1, lo:lo + span, :] = cf_buf[pl.ds(lo - r, span), :]
    rows = CF_ROWS
    for r0 in range(0, tt, rows):
        acc = None
        for j in range(CF_KERNEL):
            a, r = divmod(CF_KERNEL - 1 - j, SUBLANES)
            start = CF_HALO + r0 - a * SUBLANES
            src = cf_buf[start:start + rows, :] if r == 0 else cf_shift[r - 1, start:start + rows, :]
            term = dww_ref[j:j + 1, :] * src
            acc = term if acc is None else acc + term
        c = acc + dwb_ref[...]
        mu = jnp.mean(c, axis=-1, keepdims=True)
        cc = c - mu
        y = cc * lax.rsqrt(jnp.mean(cc * cc, axis=-1, keepdims=True) + NORM_EPS)
        u_ref[0, r0:r0 + rows, :] = _silu(y * lnw_ref[...] + lnb_ref[...])
    cf_buf[0:CF_HALO, :] = cf_buf[tt:tt + CF_HALO, :]


def _inproj(x, nw, wm, wba, wcf, cw, alog, dtb, dww, dwb, lnw, lnb, *, tile):
    b, s, d = x.shape
    qk = GDN_HEADS * HEAD_DIM
    cfc = dww.shape[1]
    grid = (b, s // tile)
    full = lambda a: pl.BlockSpec(a.shape, lambda i, j: (0,) * a.ndim)
    tok = lambda w: pl.BlockSpec((1, tile, w), lambda i, j: (i, j, 0))
    out_shape = (
        jax.ShapeDtypeStruct((b, s, qk), F32), jax.ShapeDtypeStruct((b, s, qk), F32),
        jax.ShapeDtypeStruct((b, s, qk), F32), jax.ShapeDtypeStruct((b, s, qk), F32),
        jax.ShapeDtypeStruct((b, s, LANES), F32), jax.ShapeDtypeStruct((b, SUBLANES, s), F32),
        jax.ShapeDtypeStruct((b, s, cfc), F32))
    return pl.pallas_call(
        _inproj_kernel,
        grid=grid,
        in_specs=[tok(d)] + [full(a) for a in (nw, wm, wba, wcf, cw, alog, dtb, dww, dwb, lnw, lnb)],
        out_specs=(tok(qk), tok(qk), tok(qk), tok(qk), tok(LANES),
                   pl.BlockSpec((1, SUBLANES, tile), lambda i, j: (i, 0, j)), tok(cfc)),
        out_shape=out_shape,
        scratch_shapes=[pltpu.VMEM((QKV_HALO + tile, 3 * qk), F32),
                        pltpu.VMEM((CF_HALO + tile, cfc), F32),
                        pltpu.VMEM((SUBLANES - 1, CF_HALO + tile, cfc), F32)],
        compiler_params=pltpu.CompilerParams(
            dimension_semantics=("arbitrary", "arbitrary"), vmem_limit_bytes=VMEM_LIMIT),
        name="inproj",
    )(x, nw, wm, wba, wcf, cw, alog, dtb, dww, dwb, lnw, lnb)


def _bmm(a, b):
    return jnp.einsum("bmk,bkn->bmn", a, b, preferred_element_type=F32)


def _bmm_nt(a, b):
    return jnp.einsum("bmk,bnk->bmn", a, b, preferred_element_type=F32)


def _bmm_tn(a, b):
    return jnp.einsum("bkm,bkn->bmn", a, b, preferred_element_type=F32)


def _unit_lower_inverse(a):
    c = a.shape[-1]
    ii = lax.broadcasted_iota(I32, (c, c), 0)
    jj = lax.broadcasted_iota(I32, (c, c), 1)
    eye = (ii == jj).astype(F32)
    same16 = (ii // 16) == (jj // 16)
    same32 = (ii // 32) == (jj // 32)
    x = jnp.where(same16, -a, 0.0)
    t = eye + x
    xp = x
    for _ in range(3):
        xp_b = xp.astype(BF16)
        xp = _bmm(xp_b, xp_b)
        t = t + _bmm(t.astype(BF16), xp.astype(BF16))
    for off in (jnp.where(same32 & ~same16, a, 0.0), jnp.where(~same32, a, 0.0)):
        tb = t.astype(BF16)
        t = t - _bmm(tb, _bmm(off.astype(BF16), tb).astype(BF16))
    return t


def _gdn_kernel(q_ref, k_ref, v_ref, bg_ref, bgt_ref, o_ref, state, s_all):
    lt = q_ref.shape[1]
    c = GDN_CHUNK
    nh = GDN_HEADS
    nc = lt // c

    @pl.when(pl.program_id(1) == 0)
    def _():
        state[...] = jnp.zeros(state.shape, F32)

    def stack(fn):
        return jnp.stack([fn(slice(n * c, (n + 1) * c), h) for n in range(nc) for h in range(nh)])

    head = lambda h: slice(h * HEAD_DIM, (h + 1) * HEAD_DIM)
    q = stack(lambda r, h: q_ref[0, r, head(h)]) * (HEAD_DIM ** -0.5)
    k = stack(lambda r, h: k_ref[0, r, head(h)])
    v = stack(lambda r, h: v_ref[0, r, head(h)])
    beta = stack(lambda r, h: bg_ref[0, r, h:h + 1])
    gcol = stack(lambda r, h: bg_ref[0, r, nh + h:nh + h + 1])
    grow = stack(lambda r, h: bgt_ref[0, nh + h:nh + h + 1, r])

    ii = lax.broadcasted_iota(I32, (c, c), 0)
    jj = lax.broadcasted_iota(I32, (c, c), 1)
    glast = gcol[:, c - 1:c, :]
    eg = jnp.exp(gcol)
    decay = jnp.where(ii >= jj, jnp.exp(jnp.minimum(gcol - grow, 0.0)), 0.0)
    kb = k * beta
    k_b = k.astype(BF16)
    a = jnp.where(ii > jj, _bmm_nt(kb.astype(BF16), k_b) * decay, 0.0)
    t = _unit_lower_inverse(a)
    rhs = jnp.concatenate([v * beta, kb * eg], axis=-1).astype(BF16)
    sol = _bmm(t.astype(BF16), rhs)
    u_val = sol[..., :HEAD_DIM]
    w_key = sol[..., HEAD_DIM:]
    intra = _bmm_nt(q.astype(BF16), k_b) * decay
    k_tail = (k * jnp.exp(glast - gcol)).astype(BF16)
    upd = _bmm_tn(k_tail, sol.astype(BF16))
    b_mat = upd[..., :HEAD_DIM]
    p_mat = upd[..., HEAD_DIM:].astype(BF16)
    g_tot = jnp.exp(glast)

    s = state[...]
    for n in range(nc):
        grp = slice(n * nh, (n + 1) * nh)
        s_b = s.astype(BF16)
        s_all[grp] = s_b
        s = s * g_tot[grp] - _bmm(p_mat[grp], s_b) + b_mat[grp]
    state[...] = s

    wq = jnp.concatenate([w_key, q * eg], axis=1).astype(BF16)
    ws_qs = _bmm(wq, s_all[...])
    v_new = u_val - ws_qs[:, :c]
    o = ws_qs[:, c:] + _bmm(intra.astype(BF16), v_new.astype(BF16))
    for n in range(nc):
        for h in range(nh):
            o_ref[0, n * c:(n + 1) * c, head(h)] = o[n * nh + h]


def _gdn(q, k, v, bg, bgt, *, tile):
    b, s, qk = q.shape
    grid = (b, s // tile)
    tok = lambda w: pl.BlockSpec((1, tile, w), lambda i, j: (i, j, 0))
    n_prob = (tile // GDN_CHUNK) * GDN_HEADS
    return pl.pallas_call(
        _gdn_kernel,
        grid=grid,
        in_specs=[tok(qk), tok(qk), tok(qk), tok(LANES),
                  pl.BlockSpec((1, SUBLANES, tile), lambda i, j: (i, 0, j))],
        out_specs=tok(qk),
        out_shape=jax.ShapeDtypeStruct((b, s, qk), F32),
        scratch_shapes=[pltpu.VMEM((GDN_HEADS, HEAD_DIM, HEAD_DIM), F32),
                        pltpu.VMEM((n_prob, HEAD_DIM, HEAD_DIM), BF16)],
        compiler_params=pltpu.CompilerParams(
            dimension_semantics=("arbitrary", "arbitrary"), vmem_limit_bytes=VMEM_LIMIT),
        name="gdn",
    )(q, k, v, bg, bgt)


def _split_bf16(x):
    hi = x.astype(BF16)
    lo = (x - hi.astype(F32)).astype(BF16)
    return hi, lo


def _outproj_router_kernel(o_ref, z_ref, u_ref, x_ref, gnw_ref, wa_ref, wb_ref, fnw_ref,
                           wr_ref, br_ref,
                           x2_ref, xn_ref, route_ref, gate_ref, counts_ref, carry):
    tt = x_ref.shape[0]

    @pl.when(pl.program_id(0) == 0)
    def _():
        carry[...] = jnp.zeros(carry.shape, F32)

    parts = []
    for hd in range(GDN_HEADS):
        sl = slice(hd * HEAD_DIM, (hd + 1) * HEAD_DIM)
        oh = o_ref[:, sl]
        y = oh * lax.rsqrt(jnp.mean(oh * oh, axis=-1, keepdims=True) + NORM_EPS) * gnw_ref[...]
        parts.append((y * _silu(z_ref[:, sl])).astype(BF16))
    out_a = jnp.concatenate(parts, axis=-1)
    x2 = x_ref[...] + _dot(out_a, wa_ref[...]) + _dot(u_ref[...].astype(BF16), wb_ref[...])
    x2_ref[...] = x2

    xn = x2 * lax.rsqrt(jnp.mean(x2 * x2, axis=-1, keepdims=True) + NORM_EPS) * fnw_ref[...]
    _store_token_tiles(xn_ref, xn)

    xh, xl = _split_bf16(xn)
    wh, wl = _split_bf16(wr_ref[...])
    logits = _dot(xh, wh) + _dot(xh, wl) + _dot(xl, wh) + br_ref[...]

    lane = lax.broadcasted_iota(I32, (tt, LANES), 1)
    lane_f = lane.astype(F32)
    neg = jnp.float32(-jnp.inf)
    work = jnp.where(lane < N_EXPERTS, logits, neg)
    vals, idxs = [], []
    onehot = jnp.zeros((tt, LANES), F32)
    for _ in range(TOP_K):
        m = jnp.max(work, axis=-1, keepdims=True)
        idx = jnp.min(jnp.where(work == m, lane_f, float(LANES)), axis=-1, keepdims=True).astype(I32)
        sel = lane == idx
        vals.append(m)
        idxs.append(idx)
        onehot = onehot + sel.astype(F32)
        work = jnp.where(sel, neg, work)
    exps = [jnp.exp(v - vals[0]) for v in vals]
    denom = exps[0] + exps[1] + exps[2] + exps[3]

    ri = lax.broadcasted_iota(I32, (tt, tt), 0)
    ci = lax.broadcasted_iota(I32, (tt, tt), 1)
    strict = (ri > ci).astype(BF16)
    base = carry[...] + _dot(strict, onehot.astype(BF16))
    route = jnp.zeros((tt, LANES), I32)
    gates = jnp.zeros((tt, LANES), F32)
    for kk in range(TOP_K):
        rank = jnp.sum(jnp.where(lane == idxs[kk], base, 0.0), axis=-1, keepdims=True)
        route = jnp.where(lane == kk, idxs[kk], route)
        route = jnp.where(lane == kk + TOP_K, rank.astype(I32), route)
        gates = jnp.where(lane == kk, exps[kk] / denom, gates)
    route_ref[...] = route
    gate_ref[...] = gates
    new_carry = carry[...] + jnp.sum(onehot, axis=0, keepdims=True)
    carry[...] = new_carry
    counts_ref[...] = new_carry.astype(I32)


def _outproj_router(o, z, u, x, gnw, wa, wb, fnw, wr, br, *, tile):
    t, d = x.shape
    grid = (t // tile,)
    full = lambda a: pl.BlockSpec(a.shape, lambda i: (0,) * a.ndim)
    tok = lambda w: pl.BlockSpec((tile, w), lambda i: (i, 0))
    return pl.pallas_call(
        _outproj_router_kernel,
        grid=grid,
        in_specs=[tok(o.shape[1]), tok(z.shape[1]), tok(u.shape[1]), tok(d)]
                 + [full(a) for a in (gnw, wa, wb, fnw, wr, br)],
        out_specs=(tok(d), pl.BlockSpec((tile * (d // LANES), LANES), lambda i: (i, 0)),
                   tok(LANES), tok(LANES), pl.BlockSpec((1, LANES), lambda i: (0, 0))),
        out_shape=(jax.ShapeDtypeStruct((t, d), F32), jax.ShapeDtypeStruct((t * (d // LANES), LANES), F32),
                   jax.ShapeDtypeStruct((t, LANES), I32), jax.ShapeDtypeStruct((t, LANES), F32),
                   jax.ShapeDtypeStruct((1, LANES), I32)),
        scratch_shapes=[pltpu.VMEM((1, LANES), F32)],
        compiler_params=pltpu.CompilerParams(
            dimension_semantics=("arbitrary",), vmem_limit_bytes=VMEM_LIMIT),
        name="outproj_router",
    )(o, z, u, x, gnw, wa, wb, fnw, wr, br)


def _dest_kernel(route_ref, pstart_ref, dest_ref):
    route = route_ref[...].astype(F32)
    tt = route.shape[0]
    lane = lax.broadcasted_iota(I32, (tt, LANES), 1)
    pstart = pstart_ref[...].astype(F32)
    dest = jnp.zeros((tt, LANES), F32)
    for kk in range(TOP_K):
        idx = jnp.sum(jnp.where(lane == kk, route, 0.0), axis=-1, keepdims=True)
        rank = jnp.sum(jnp.where(lane == kk + TOP_K, route, 0.0), axis=-1, keepdims=True)
        start = jnp.sum(jnp.where(lane == idx.astype(I32), pstart, 0.0), axis=-1, keepdims=True)
        dest = jnp.where(lane == kk, start + rank, dest)
    dest_ref[...] = jnp.transpose(dest)[0:SUBLANES, :].astype(I32)


def _dest(route, pstart, *, tile):
    t = route.shape[0]
    return pl.pallas_call(
        _dest_kernel,
        grid=(t // tile,),
        in_specs=[pl.BlockSpec((tile, LANES), lambda i: (i, 0)),
                  pl.BlockSpec((1, LANES), lambda i: (0, 0))],
        out_specs=pl.BlockSpec((SUBLANES, tile), lambda i: (0, i)),
        out_shape=jax.ShapeDtypeStruct((SUBLANES, t), I32),
        compiler_params=pltpu.CompilerParams(dimension_semantics=("arbitrary",)),
        name="dest_rows",
    )(route, pstart)


ROW_WINDOW = 32
COMBINE_CHUNKS = 8


def _sc_mesh():
    return plsc.VectorSubcoreMesh(core_axis_name="core", subcore_axis_name="subcore")


def _dispatch_rows(xn_tiles, dest_win, n_rows):
    t = xn_tiles.shape[0]

    @functools.partial(
        pl.kernel, mesh=_sc_mesh(), scratch_types=[],
        out_type=jax.ShapeDtypeStruct((n_rows,) + xn_tiles.shape[1:], xn_tiles.dtype))
    def dispatch(x_hbm, idx_hbm, o_hbm):
        def body(x_vmem, idx_vmem):
            for kk in range(TOP_K):
                pltpu.sync_copy(x_vmem, o_hbm.at[idx_vmem.at[0, pl.ds(kk * ROW_WINDOW, ROW_WINDOW)]])

        pltpu.emit_pipeline(
            body,
            grid=(t // ROW_WINDOW,),
            in_specs=[pl.BlockSpec((ROW_WINDOW,) + xn_tiles.shape[1:], lambda i: (i, 0, 0)),
                      pl.BlockSpec((1, TOP_K * ROW_WINDOW), lambda i: (i, 0))],
            out_specs=[],
            core_axis_name=("core", "subcore"),
            dimension_semantics=(pltpu.PARALLEL,),
        )(x_hbm, idx_hbm)

    return dispatch(xn_tiles, dest_win)


def _collect_rows(yb_tiles, src_win):
    n_pairs = src_win.shape[0] * ROW_WINDOW

    @functools.partial(
        pl.kernel, mesh=_sc_mesh(), scratch_types=[],
        out_type=jax.ShapeDtypeStruct((n_pairs,) + yb_tiles.shape[1:], yb_tiles.dtype))
    def collect(y_hbm, idx_hbm, o_hbm):
        def body(idx_vmem, o_vmem):
            pltpu.sync_copy(y_hbm.at[idx_vmem.at[0, pl.ds(0, ROW_WINDOW)]], o_vmem)

        pltpu.emit_pipeline(
            body,
            grid=(n_pairs // ROW_WINDOW,),
            in_specs=[pl.BlockSpec((1, LANES), lambda i: (i, 0))],
            out_specs=[pl.BlockSpec((ROW_WINDOW,) + yb_tiles.shape[1:], lambda i: (i, 0, 0))],
            core_axis_name=("core", "subcore"),
            dimension_semantics=(pltpu.PARALLEL,),
        )(idx_hbm, o_hbm)

    return collect(yb_tiles, src_win)


def _expert_kernel(blk_exp_ref, n_used_ref, next_exp_ref, slot_ref, blk_rows_ref,
                   xb_ref, wgu_hbm, bgu_ref, wd_hbm, bd_ref, yb_ref,
                   wgu_f, wd_f, wgu_b, wd_b, wsem):
    i = pl.program_id(0)
    n_used = n_used_ref[0]
    bm = xb_ref.shape[0] // SUBLANES
    dff = wd_f.shape[1]

    def weight_copies(e, s):
        return (pltpu.make_async_copy(wgu_hbm.at[e], wgu_f.at[s], wsem.at[0, s]),
                pltpu.make_async_copy(wd_hbm.at[e], wd_f.at[s], wsem.at[1, s]))

    prev = blk_exp_ref[jnp.maximum(i - 1, 0)]
    first_of_expert = (i == 0) | (blk_exp_ref[i] != prev)

    @pl.when(first_of_expert & (i < n_used))
    def _():
        e = blk_exp_ref[i]
        s = slot_ref[i]

        @pl.when(i == 0)
        def _():
            for cp in weight_copies(e, s):
                cp.start()

        for cp in weight_copies(e, s):
            cp.wait()
        wgu_b[...] = wgu_f[s].astype(BF16)
        wd_b[...] = wd_f[s].astype(BF16)

        nxt = next_exp_ref[i]

        @pl.when(nxt >= 0)
        def _():
            for cp in weight_copies(nxt, 1 - s):
                cp.start()

    def swiglu_rows(rows):
        x_rows = xb_ref.at[pl.ds(0, rows * SUBLANES)]
        y_rows = yb_ref.at[pl.ds(0, rows * SUBLANES)]
        xb = _load_token_tiles(x_rows, rows).astype(BF16)
        hid = _dot(xb, wgu_b[...]) + bgu_ref[0]
        gate = jnp.minimum(hid[:, :dff], SWIGLU_LIMIT)
        up = jnp.clip(hid[:, dff:], -SWIGLU_LIMIT, SWIGLU_LIMIT)
        glu = gate * jax.nn.sigmoid(SWIGLU_ALPHA * gate)
        act = ((up + 1.0) * glu).astype(BF16)
        _store_token_tiles(y_rows, _dot(act, wd_b[...]) + bd_ref[0])

    used = i < n_used
    half_full = blk_rows_ref[i] <= bm // 2
    pl.when(used & jnp.logical_not(half_full))(functools.partial(swiglu_rows, bm))
    pl.when(used & half_full)(functools.partial(swiglu_rows, bm // 2))


def _experts(blk_exp, n_used, next_exp, slot, blk_rows, xb_tiles, wgu, bgu, wd, bd):
    d = wgu.shape[1]
    chunks = d // LANES
    n_blocks = xb_tiles.shape[0] // (EXPERT_BLOCK * chunks)
    two_f = wgu.shape[2]
    dff = wd.shape[1]
    blk = lambda i, be, nu, ne, sl, br: (jnp.minimum(i, nu[0] - 1), 0)
    exp3 = lambda i, be, nu, ne, sl, br: (be[jnp.minimum(i, nu[0] - 1)], 0, 0)
    grid_spec = pltpu.PrefetchScalarGridSpec(
        num_scalar_prefetch=5,
        grid=(n_blocks,),
        in_specs=[pl.BlockSpec((EXPERT_BLOCK * chunks, LANES), blk),
                  pl.BlockSpec(memory_space=pl.ANY),
                  pl.BlockSpec((1, 1, two_f), exp3),
                  pl.BlockSpec(memory_space=pl.ANY),
                  pl.BlockSpec((1, 1, d), exp3)],
        out_specs=pl.BlockSpec((EXPERT_BLOCK * chunks, LANES), blk),
        scratch_shapes=[pltpu.VMEM((2, d, two_f), F32), pltpu.VMEM((2, dff, d), F32),
                        pltpu.VMEM((d, two_f), BF16), pltpu.VMEM((dff, d), BF16),
                        pltpu.SemaphoreType.DMA((2, 2))])
    return pl.pallas_call(
        _expert_kernel,
        grid_spec=grid_spec,
        out_shape=jax.ShapeDtypeStruct(xb_tiles.shape, F32),
        compiler_params=pltpu.CompilerParams(
            dimension_semantics=("arbitrary",), vmem_limit_bytes=VMEM_LIMIT),
        name="experts",
    )(blk_exp, n_used, next_exp, slot, blk_rows, xb_tiles, wgu, bgu, wd, bd)


def _combine_kernel(y0_ref, y1_ref, y2_ref, y3_ref, gate_ref, x2_ref, fw_ref, out_ref):
    gates = gate_ref[...]
    lane = lax.broadcasted_iota(I32, gates.shape, 1)
    x3 = x2_ref[...]
    for kk, y_ref in enumerate((y0_ref, y1_ref, y2_ref, y3_ref)):
        gk = jnp.sum(jnp.where(lane == kk, gates, 0.0), axis=-1, keepdims=True)
        x3 = x3 + gk * _load_token_tiles(y_ref, x3.shape[0])
    out_ref[...] = x3 * lax.rsqrt(jnp.mean(x3 * x3, axis=-1, keepdims=True) + NORM_EPS) * fw_ref[...]


def _combine_into_kernel(y0_ref, y1_ref, y2_ref, y3_ref, gate_ref, x2_ref, fw_ref, prev_ref, out_ref):
    del prev_ref
    _combine_kernel(y0_ref, y1_ref, y2_ref, y3_ref, gate_ref, x2_ref, fw_ref, out_ref)


def _combine(y4, gates, x2, fw, out_prev, chunk, *, tile):
    t, d = x2.shape
    tc = y4.shape[0] // (TOP_K * (d // LANES))
    steps = tc // tile
    first = chunk * steps
    choice = lambda kk: pl.BlockSpec((tile * (d // LANES), LANES), lambda i: (kk * steps + i, 0))
    in_specs = ([choice(kk) for kk in range(TOP_K)]
                + [pl.BlockSpec((tile, LANES), lambda i: (first + i, 0)),
                   pl.BlockSpec((tile, d), lambda i: (first + i, 0)),
                   pl.BlockSpec((1, d), lambda i: (0, 0))])
    args = [y4, y4, y4, y4, gates, x2, fw]
    if out_prev is not None:
        in_specs.append(pl.BlockSpec(memory_space=pl.ANY))
        args.append(out_prev)
    return pl.pallas_call(
        _combine_kernel if out_prev is None else _combine_into_kernel,
        grid=(steps,),
        in_specs=in_specs,
        out_specs=pl.BlockSpec((tile, d), lambda i: (first + i, 0)),
        out_shape=jax.ShapeDtypeStruct((t, d), F32),
        input_output_aliases={} if out_prev is None else {len(args) - 1: 0},
        compiler_params=pltpu.CompilerParams(
            dimension_semantics=("arbitrary",), vmem_limit_bytes=VMEM_LIMIT),
        name="combine",
    )(*args)


def _pad_lanes(a, offset=0, fill=0.0):
    out = jnp.full((1, LANES), fill, a.dtype)
    return out.at[0, offset:offset + a.shape[0]].set(a)


def _layer(x, attn_norm_w, w_in, gdn_conv_w, gdn_a_log, gdn_dt_bias, gdn_norm_w,
           cf_dw_w, cf_dw_b, cf_ln_w, cf_ln_b, w_out, ffn_norm_w, w_router, b_router,
           w_gate_up, b_gate_up, w_down, b_down, final_norm_w):
    b, s, d = x.shape
    t = b * s
    assert d == SUBLANES * LANES, "the token-tile layout needs one (8, 128) tile per token row"
    qk = GDN_HEADS * HEAD_DIM
    cfc = cf_dw_w.shape[1]
    off_b = 4 * qk
    off_cf = off_b + 2 * GDN_HEADS

    wm = w_in[:, :off_b].astype(BF16)
    wba = jnp.zeros((d, LANES), F32).at[:, :2 * GDN_HEADS].set(w_in[:, off_b:off_cf]).astype(BF16)
    wcf = w_in[:, off_cf:].astype(BF16)
    alog = _pad_lanes(gdn_a_log, GDN_HEADS)
    dtb = _pad_lanes(gdn_dt_bias, GDN_HEADS)

    q, k, v, z, bg, bgt, u = _inproj(
        x, attn_norm_w[None, :], wm, wba, wcf, gdn_conv_w, alog, dtb,
        cf_dw_w, cf_dw_b[None, :], cf_ln_w[None, :], cf_ln_b[None, :], tile=min(TOKEN_TILE, s))
    o = _gdn(q, k, v, bg, bgt, tile=min(TOKEN_TILE, s))

    wr = jnp.zeros((d, LANES), F32).at[:, :N_EXPERTS].set(w_router)
    br = _pad_lanes(b_router)
    x2, xn, route, gates, counts = _outproj_router(
        o.reshape(t, qk), z.reshape(t, qk), u.reshape(t, cfc), x.reshape(t, d),
        gdn_norm_w[None, :], w_out[:qk].astype(BF16), w_out[qk:].astype(BF16),
        ffn_norm_w[None, :], wr, br, tile=min(TOKEN_TILE, t))

    cnt = counts[0, :N_EXPERTS]
    nblk = (cnt + EXPERT_BLOCK - 1) // EXPERT_BLOCK
    blk_end = jnp.cumsum(nblk)
    pstart = (blk_end - nblk) * EXPERT_BLOCK
    n_blocks = (t * TOP_K) // EXPERT_BLOCK + N_EXPERTS
    blk_ids = jnp.arange(n_blocks, dtype=I32)
    blk_exp = jnp.minimum(
        jnp.sum((blk_end[None, :] <= blk_ids[:, None]).astype(I32), axis=1), N_EXPERTS - 1)
    n_used = blk_end[-1:].astype(I32)
    experts = jnp.arange(N_EXPERTS, dtype=I32)
    present = nblk > 0
    later = jnp.where((experts[None, :] > experts[:, None]) & present[None, :], experts[None, :], N_EXPERTS)
    next_of = jnp.min(later, axis=1)
    next_of = jnp.where(next_of < N_EXPERTS, next_of, -1)
    slot_of = (jnp.cumsum(present.astype(I32)) - 1) % 2
    of_block = (blk_exp[:, None] == experts[None, :]).astype(I32)
    next_exp = jnp.sum(of_block * next_of[None, :], axis=1).astype(I32)
    slot = jnp.sum(of_block * slot_of[None, :], axis=1).astype(I32)
    row_end = jnp.sum(of_block * (pstart + cnt)[None, :], axis=1)
    blk_rows = jnp.clip(row_end - blk_ids * EXPERT_BLOCK, 0, EXPERT_BLOCK).astype(I32)

    dest = _dest(route, _pad_lanes(pstart.astype(I32)), tile=min(DEST_TILE, t))[:TOP_K]
    n_rows = n_blocks * EXPERT_BLOCK
    chunks = d // LANES
    windows = t // ROW_WINDOW
    dest_win = dest.reshape(TOP_K, windows, ROW_WINDOW).transpose(1, 0, 2).reshape(windows, TOP_K * ROW_WINDOW)
    xb = _dispatch_rows(xn.reshape(t, chunks, LANES), dest_win, n_rows)
    yb = _experts(blk_exp, n_used, next_exp, slot, blk_rows, xb.reshape(n_rows * chunks, LANES), w_gate_up, b_gate_up[:, None, :],
                  w_down, b_down[:, None, :]).reshape(n_rows, chunks, LANES)

    tc = t // COMBINE_CHUNKS
    out = None
    for c in range(COMBINE_CHUNKS):
        src = dest[:, c * tc:(c + 1) * tc].reshape(TOP_K * tc // ROW_WINDOW, ROW_WINDOW)
        src_win = jnp.pad(src, ((0, 0), (0, LANES - ROW_WINDOW)))
        y4 = _collect_rows(yb, src_win).reshape(TOP_K * tc * chunks, LANES)
        out = _combine(y4, gates, x2, final_norm_w[None, :], out, c, tile=min(EXPERT_BLOCK, tc))
    return out.reshape(b, s, d)


def kernel(x, attn_norm_w, w_in, gdn_conv_w, gdn_a_log, gdn_dt_bias, gdn_norm_w, cf_dw_w, cf_dw_b,
           cf_ln_w, cf_ln_b, w_out, ffn_norm_w, w_router, b_router, w_gate_up, b_gate_up, w_down,
           b_down, final_norm_w):
    depth = w_in.shape[0]
    assert depth == 1, "the fused final norm assumes a single trunk layer"
    return _layer(x, attn_norm_w[0], w_in[0], gdn_conv_w[0], gdn_a_log[0], gdn_dt_bias[0],
                  gdn_norm_w[0], cf_dw_w[0], cf_dw_b[0], cf_ln_w[0], cf_ln_b[0], w_out[0],
                  ffn_norm_w[0], w_router[0], b_router[0], w_gate_up[0], b_gate_up[0], w_down[0],
                  b_down[0], final_norm_w)
```

```python
import functools

import jax
import jax.numpy as jnp
from jax import lax
from jax.experimental import pallas as pl
from jax.experimental.pallas import tpu as pltpu
from jax.experimental.pallas import tpu_sc as plsc

F32 = jnp.float32
BF16 = jnp.bfloat16
I32 = jnp.int32

NORM_EPS = 1e-6
LANES = 128
SUBLANES = 8
GDN_HEADS = 4
HEAD_DIM = 128
GDN_CHUNK = 64
GDN_CONV = 4
CF_KERNEL = 31
N_EXPERTS = 32
TOP_K = 4
SWIGLU_LIMIT = 7.0
SWIGLU_ALPHA = 1.702

QKV_HALO = 8
CF_HALO = 32
CF_ROWS = 128
EXPERT_BLOCK = 512
TOKEN_TILE = 512
DEST_TILE = 2048
VMEM_LIMIT = 56 * 1024 * 1024


def _silu(x):
    return x * jax.nn.sigmoid(x)


def _dot(a, b):
    return jnp.dot(a, b, preferred_element_type=F32)


def _store_token_tiles(ref, x):
    rows, d = x.shape
    chunks = d // LANES
    for c in range(chunks):
        ref[pl.ds(c, rows, stride=chunks), :] = x[:, c * LANES:(c + 1) * LANES]


def _load_token_tiles(ref, rows):
    chunks = ref.shape[0] // rows
    return jnp.concatenate([ref[pl.ds(c, rows, stride=chunks), :] for c in range(chunks)], axis=1)


def _inproj_kernel(x_ref, nw_ref, wm_ref, wba_ref, wcf_ref, cw_ref, alog_ref, dtb_ref,
                   dww_ref, dwb_ref, lnw_ref, lnb_ref,
                   q_ref, k_ref, v_ref, z_ref, bg_ref, bgt_ref, u_ref,
                   qkv_buf, cf_buf, cf_shift):
    tt = x_ref.shape[1]
    qk = GDN_HEADS * HEAD_DIM
    cfc = u_ref.shape[2]

    @pl.when(pl.program_id(1) == 0)
    def _():
        qkv_buf[0:QKV_HALO, :] = jnp.zeros((QKV_HALO, qkv_buf.shape[1]), F32)
        cf_buf[0:CF_HALO, :] = jnp.zeros((CF_HALO, cf_buf.shape[1]), F32)

    x = x_ref[0]
    h = x * lax.rsqrt(jnp.mean(x * x, axis=-1, keepdims=True) + NORM_EPS) * nw_ref[...]
    h = h.astype(BF16)
    pm = _dot(h, wm_ref[...])
    pba = _dot(h, wba_ref[...])
    pcf = _dot(h, wcf_ref[...])

    z_ref[0] = pm[:, 3 * qk:]

    qkv_buf[QKV_HALO:QKV_HALO + tt, :] = pm[:, :3 * qk]
    acc = None
    for j in range(GDN_CONV):
        term = cw_ref[j:j + 1, :] * qkv_buf[pl.ds(QKV_HALO - (GDN_CONV - 1) + j, tt), :]
        acc = term if acc is None else acc + term
    qkv_buf[0:QKV_HALO, :] = qkv_buf[tt:tt + QKV_HALO, :]
    qkv = _silu(acc)
    for hd in range(GDN_HEADS):
        for base, ref in ((0, q_ref), (qk, k_ref)):
            t = qkv[:, base + hd * HEAD_DIM: base + (hd + 1) * HEAD_DIM]
            t = t * lax.rsqrt(jnp.sum(t * t, axis=-1, keepdims=True) + NORM_EPS)
            ref[0, :, hd * HEAD_DIM:(hd + 1) * HEAD_DIM] = t
    v_ref[0] = qkv[:, 2 * qk:]

    lane = lax.broadcasted_iota(I32, (tt, LANES), 1)
    row = lax.broadcasted_iota(I32, (tt, LANES), 0)
    beta = jax.nn.sigmoid(pba)
    sp_in = pba + dtb_ref[...]
    softplus = jnp.maximum(sp_in, 0.0) + jnp.log(1.0 + jnp.exp(-jnp.abs(sp_in)))
    g = -jnp.exp(alog_ref[...]) * softplus
    g = jnp.where((lane >= GDN_HEADS) & (lane < 2 * GDN_HEADS), g, 0.0)
    pos = row % GDN_CHUNK
    shift = 1
    while shift < GDN_CHUNK:
        g = g + jnp.where(pos >= shift, pltpu.roll(g, shift, 0), 0.0)
        shift *= 2
    bg = jnp.where(lane < GDN_HEADS, beta, g)
    bg_ref[0] = bg
    bgt_ref[0] = jnp.transpose(bg)[0:SUBLANES, :]

    glu = pcf[:, :cfc] * jax.nn.sigmoid(pcf[:, cfc:])
    cf_buf[CF_HALO:CF_HALO + tt, :] = glu
    lo = SUBLANES
    span = tt + CF_HALO - lo
    for r in range(1, SUBLANES):
        cf_shift[r - 1, lo:lo + span, :] = cf_buf[pl.ds(lo - r, span), :]
    rows = CF_ROWS
    for r0 in range(0, tt, rows):
        acc = None
        for j in range(CF_KERNEL):
            a, r = divmod(CF_KERNEL - 1 - j, SUBLANES)
            start = CF_HALO + r0 - a * SUBLANES
            src = cf_buf[start:start + rows, :] if r == 0 else cf_shift[r - 1, start:start + rows, :]
            term = dww_ref[j:j + 1, :] * src
            acc = term if acc is None else acc + term
        c = acc + dwb_ref[...]
        mu = jnp.mean(c, axis=-1, keepdims=True)
        cc = c - mu
        y = cc * lax.rsqrt(jnp.mean(cc * cc, axis=-1, keepdims=True) + NORM_EPS)
        u_ref[0, r0:r0 + rows, :] = _silu(y * lnw_ref[...] + lnb_ref[...])
    cf_buf[0:CF_HALO, :] = cf_buf[tt:tt + CF_HALO, :]


def _inproj(x, nw, wm, wba, wcf, cw, alog, dtb, dww, dwb, lnw, lnb, *, tile):
    b, s, d = x.shape
    qk = GDN_HEADS * HEAD_DIM
    cfc = dww.shape[1]
    grid = (b, s // tile)
    full = lambda a: pl.BlockSpec(a.shape, lambda i, j: (0,) * a.ndim)
    tok = lambda w: pl.BlockSpec((1, tile, w), lambda i, j: (i, j, 0))
    out_shape = (
        jax.ShapeDtypeStruct((b, s, qk), F32), jax.ShapeDtypeStruct((b, s, qk), F32),
        jax.ShapeDtypeStruct((b, s, qk), F32), jax.ShapeDtypeStruct((b, s, qk), F32),
        jax.ShapeDtypeStruct((b, s, LANES), F32), jax.ShapeDtypeStruct((b, SUBLANES, s), F32),
        jax.ShapeDtypeStruct((b, s, cfc), F32))
    return pl.pallas_call(
        _inproj_kernel,
        grid=grid,
        in_specs=[tok(d)] + [full(a) for a in (nw, wm, wba, wcf, cw, alog, dtb, dww, dwb, lnw, lnb)],
        out_specs=(tok(qk), tok(qk), tok(qk), tok(qk), tok(LANES),
                   pl.BlockSpec((1, SUBLANES, tile), lambda i, j: (i, 0, j)), tok(cfc)),
        out_shape=out_shape,
        scratch_shapes=[pltpu.VMEM((QKV_HALO + tile, 3 * qk), F32),
                        pltpu.VMEM((CF_HALO + tile, cfc), F32),
                        pltpu.VMEM((SUBLANES - 1, CF_HALO + tile, cfc), F32)],
        compiler_params=pltpu.CompilerParams(
            dimension_semantics=("arbitrary", "arbitrary"), vmem_limit_bytes=VMEM_LIMIT),
        name="inproj",
    )(x, nw, wm, wba, wcf, cw, alog, dtb, dww, dwb, lnw, lnb)


def _bmm(a, b):
    return jnp.einsum("bmk,bkn->bmn", a, b, preferred_element_type=F32)


def _bmm_nt(a, b):
    return jnp.einsum("bmk,bnk->bmn", a, b, preferred_element_type=F32)


def _bmm_tn(a, b):
    return jnp.einsum("bkm,bkn->bmn", a, b, preferred_element_type=F32)


def _unit_lower_inverse(a):
    c = a.shape[-1]
    ii = lax.broadcasted_iota(I32, (c, c), 0)
    jj = lax.broadcasted_iota(I32, (c, c), 1)
    eye = (ii == jj).astype(F32)
    same16 = (ii // 16) == (jj // 16)
    same32 = (ii // 32) == (jj // 32)
    x = jnp.where(same16, -a, 0.0)
    t = eye + x
    xp = x
    for _ in range(3):
        xp_b = xp.astype(BF16)
        xp = _bmm(xp_b, xp_b)
        t = t + _bmm(t.astype(BF16), xp.astype(BF16))
    for off in (jnp.where(same32 & ~same16, a, 0.0), jnp.where(~same32, a, 0.0)):
        tb = t.astype(BF16)
        t = t - _bmm(tb, _bmm(off.astype(BF16), tb).astype(BF16))
    return t


def _gdn_kernel(q_ref, k_ref, v_ref, bg_ref, bgt_ref, o_ref, state, s_all):
    lt = q_ref.shape[1]
    c = GDN_CHUNK
    nh = GDN_HEADS
    nc = lt // c

    @pl.when(pl.program_id(1) == 0)
    def _():
        state[...] = jnp.zeros(state.shape, F32)

    def stack(fn):
        return jnp.stack([fn(slice(n * c, (n + 1) * c), h) for n in range(nc) for h in range(nh)])

    head = lambda h: slice(h * HEAD_DIM, (h + 1) * HEAD_DIM)
    q = stack(lambda r, h: q_ref[0, r, head(h)]) * (HEAD_DIM ** -0.5)
    k = stack(lambda r, h: k_ref[0, r, head(h)])
    v = stack(lambda r, h: v_ref[0, r, head(h)])
    beta = stack(lambda r, h: bg_ref[0, r, h:h + 1])
    gcol = stack(lambda r, h: bg_ref[0, r, nh + h:nh + h + 1])
    grow = stack(lambda r, h: bgt_ref[0, nh + h:nh + h + 1, r])

    ii = lax.broadcasted_iota(I32, (c, c), 0)
    jj = lax.broadcasted_iota(I32, (c, c), 1)
    glast = gcol[:, c - 1:c, :]
    eg = jnp.exp(gcol)
    decay = jnp.where(ii >= jj, jnp.exp(jnp.minimum(gcol - grow, 0.0)), 0.0)
    kb = k * beta
    k_b = k.astype(BF16)
    a = jnp.where(ii > jj, _bmm_nt(kb.astype(BF16), k_b) * decay, 0.0)
    t = _unit_lower_inverse(a)
    rhs = jnp.concatenate([v * beta, kb * eg], axis=-1).astype(BF16)
    sol = _bmm(t.astype(BF16), rhs)
    u_val = sol[..., :HEAD_DIM]
    w_key = sol[..., HEAD_DIM:]
    intra = _bmm_nt(q.astype(BF16), k_b) * decay
    k_tail = (k * jnp.exp(glast - gcol)).astype(BF16)
    upd = _bmm_tn(k_tail, sol.astype(BF16))
    b_mat = upd[..., :HEAD_DIM]
    p_mat = upd[..., HEAD_DIM:].astype(BF16)
    g_tot = jnp.exp(glast)

    s = state[...]
    for n in range(nc):
        grp = slice(n * nh, (n + 1) * nh)
        s_b = s.astype(BF16)
        s_all[grp] = s_b
        s = s * g_tot[grp] - _bmm(p_mat[grp], s_b) + b_mat[grp]
    state[...] = s

    wq = jnp.concatenate([w_key, q * eg], axis=1).astype(BF16)
    ws_qs = _bmm(wq, s_all[...])
    v_new = u_val - ws_qs[:, :c]
    o = ws_qs[:, c:] + _bmm(intra.astype(BF16), v_new.astype(BF16))
    for n in range(nc):
        for h in range(nh):
            o_ref[0, n * c:(n + 1) * c, head(h)] = o[n * nh + h]


def _gdn(q, k, v, bg, bgt, *, tile):
    b, s, qk = q.shape
    grid = (b, s // tile)
    tok = lambda w: pl.BlockSpec((1, tile, w), lambda i, j: (i, j, 0))
    n_prob = (tile // GDN_CHUNK) * GDN_HEADS
    return pl.pallas_call(
        _gdn_kernel,
        grid=grid,
        in_specs=[tok(qk), tok(qk), tok(qk), tok(LANES),
                  pl.BlockSpec((1, SUBLANES, tile), lambda i, j: (i, 0, j))],
        out_specs=tok(qk),
        out_shape=jax.ShapeDtypeStruct((b, s, qk), F32),
        scratch_shapes=[pltpu.VMEM((GDN_HEADS, HEAD_DIM, HEAD_DIM), F32),
                        pltpu.VMEM((n_prob, HEAD_DIM, HEAD_DIM), BF16)],
        compiler_params=pltpu.CompilerParams(
            dimension_semantics=("arbitrary", "arbitrary"), vmem_limit_bytes=VMEM_LIMIT),
        name="gdn",
    )(q, k, v, bg, bgt)


def _split_bf16(x):
    hi = x.astype(BF16)
    lo = (x - hi.astype(F32)).astype(BF16)
    return hi, lo


def _outproj_router_kernel(o_ref, z_ref, u_ref, x_ref, gnw_ref, wa_ref, wb_ref, fnw_ref,
                           wr_ref, br_ref,
                           x2_ref, xn_ref, route_ref, gate_ref, counts_ref, carry):
    tt = x_ref.shape[0]

    @pl.when(pl.program_id(0) == 0)
    def _():
        carry[...] = jnp.zeros(carry.shape, F32)

    parts = []
    for hd in range(GDN_HEADS):
        sl = slice(hd * HEAD_DIM, (hd + 1) * HEAD_DIM)
        oh = o_ref[:, sl]
        y = oh * lax.rsqrt(jnp.mean(oh * oh, axis=-1, keepdims=True) + NORM_EPS) * gnw_ref[...]
        parts.append((y * _silu(z_ref[:, sl])).astype(BF16))
    out_a = jnp.concatenate(parts, axis=-1)
    x2 = x_ref[...] + _dot(out_a, wa_ref[...]) + _dot(u_ref[...].astype(BF16), wb_ref[...])
    x2_ref[...] = x2

    xn = x2 * lax.rsqrt(jnp.mean(x2 * x2, axis=-1, keepdims=True) + NORM_EPS) * fnw_ref[...]
    _store_token_tiles(xn_ref, xn)

    xh, xl = _split_bf16(xn)
    wh, wl = _split_bf16(wr_ref[...])
    logits = _dot(xh, wh) + _dot(xh, wl) + _dot(xl, wh) + br_ref[...]

    lane = lax.broadcasted_iota(I32, (tt, LANES), 1)
    lane_f = lane.astype(F32)
    neg = jnp.float32(-jnp.inf)
    work = jnp.where(lane < N_EXPERTS, logits, neg)
    vals, idxs = [], []
    onehot = jnp.zeros((tt, LANES), F32)
    for _ in range(TOP_K):
        m = jnp.max(work, axis=-1, keepdims=True)
        idx = jnp.min(jnp.where(work == m, lane_f, float(LANES)), axis=-1, keepdims=True).astype(I32)
        sel = lane == idx
        vals.append(m)
        idxs.append(idx)
        onehot = onehot + sel.astype(F32)
        work = jnp.where(sel, neg, work)
    exps = [jnp.exp(v - vals[0]) for v in vals]
    denom = exps[0] + exps[1] + exps[2] + exps[3]

    ri = lax.broadcasted_iota(I32, (tt, tt), 0)
    ci = lax.broadcasted_iota(I32, (tt, tt), 1)
    strict = (ri > ci).astype(BF16)
    base = carry[...] + _dot(strict, onehot.astype(BF16))
    route = jnp.zeros((tt, LANES), I32)
    gates = jnp.zeros((tt, LANES), F32)
    for kk in range(TOP_K):
        rank = jnp.sum(jnp.where(lane == idxs[kk], base, 0.0), axis=-1, keepdims=True)
        route = jnp.where(lane == kk, idxs[kk], route)
        route = jnp.where(lane == kk + TOP_K, rank.astype(I32), route)
        gates = jnp.where(lane == kk, exps[kk] / denom, gates)
    route_ref[...] = route
    gate_ref[...] = gates
    new_carry = carry[...] + jnp.sum(onehot, axis=0, keepdims=True)
    carry[...] = new_carry
    counts_ref[...] = new_carry.astype(I32)


def _outproj_router(o, z, u, x, gnw, wa, wb, fnw, wr, br, *, tile):
    t, d = x.shape
    grid = (t // tile,)
    full = lambda a: pl.BlockSpec(a.shape, lambda i: (0,) * a.ndim)
    tok = lambda w: pl.BlockSpec((tile, w), lambda i: (i, 0))
    return pl.pallas_call(
        _outproj_router_kernel,
        grid=grid,
        in_specs=[tok(o.shape[1]), tok(z.shape[1]), tok(u.shape[1]), tok(d)]
                 + [full(a) for a in (gnw, wa, wb, fnw, wr, br)],
        out_specs=(tok(d), pl.BlockSpec((tile * (d // LANES), LANES), lambda i: (i, 0)),
                   tok(LANES), tok(LANES), pl.BlockSpec((1, LANES), lambda i: (0, 0))),
        out_shape=(jax.ShapeDtypeStruct((t, d), F32), jax.ShapeDtypeStruct((t * (d // LANES), LANES), F32),
                   jax.ShapeDtypeStruct((t, LANES), I32), jax.ShapeDtypeStruct((t, LANES), F32),
                   jax.ShapeDtypeStruct((1, LANES), I32)),
        scratch_shapes=[pltpu.VMEM((1, LANES), F32)],
        compiler_params=pltpu.CompilerParams(
            dimension_semantics=("arbitrary",), vmem_limit_bytes=VMEM_LIMIT),
        name="outproj_router",
    )(o, z, u, x, gnw, wa, wb, fnw, wr, br)


def _dest_kernel(route_ref, pstart_ref, dest_ref):
    route = route_ref[...].astype(F32)
    tt = route.shape[0]
    lane = lax.broadcasted_iota(I32, (tt, LANES), 1)
    pstart = pstart_ref[...].astype(F32)
    dest = jnp.zeros((tt, LANES), F32)
    for kk in range(TOP_K):
        idx = jnp.sum(jnp.where(lane == kk, route, 0.0), axis=-1, keepdims=True)
        rank = jnp.sum(jnp.where(lane == kk + TOP_K, route, 0.0), axis=-1, keepdims=True)
        start = jnp.sum(jnp.where(lane == idx.astype(I32), pstart, 0.0), axis=-1, keepdims=True)
        dest = jnp.where(lane == kk, start + rank, dest)
    dest_ref[...] = jnp.transpose(dest)[0:SUBLANES, :].astype(I32)


def _dest(route, pstart, *, tile):
    t = route.shape[0]
    return pl.pallas_call(
        _dest_kernel,
        grid=(t // tile,),
        in_specs=[pl.BlockSpec((tile, LANES), lambda i: (i, 0)),
                  pl.BlockSpec((1, LANES), lambda i: (0, 0))],
        out_specs=pl.BlockSpec((SUBLANES, tile), lambda i: (0, i)),
        out_shape=jax.ShapeDtypeStruct((SUBLANES, t), I32),
        compiler_params=pltpu.CompilerParams(dimension_semantics=("arbitrary",)),
        name="dest_rows",
    )(route, pstart)


ROW_WINDOW = 32
COMBINE_CHUNKS = 8


def _sc_mesh():
    return plsc.VectorSubcoreMesh(core_axis_name="core", subcore_axis_name="subcore")


def _dispatch_rows(xn_tiles, dest_win, n_rows):
    t = xn_tiles.shape[0]

    @functools.partial(
        pl.kernel, mesh=_sc_mesh(), scratch_types=[],
        out_type=jax.ShapeDtypeStruct((n_rows,) + xn_tiles.shape[1:], xn_tiles.dtype))
    def dispatch(x_hbm, idx_hbm, o_hbm):
        def body(x_vmem, idx_vmem):
            for kk in range(TOP_K):
                pltpu.sync_copy(x_vmem, o_hbm.at[idx_vmem.at[0, pl.ds(kk * ROW_WINDOW, ROW_WINDOW)]])

        pltpu.emit_pipeline(
            body,
            grid=(t // ROW_WINDOW,),
            in_specs=[pl.BlockSpec((ROW_WINDOW,) + xn_tiles.shape[1:], lambda i: (i, 0, 0)),
                      pl.BlockSpec((1, TOP_K * ROW_WINDOW), lambda i: (i, 0))],
            out_specs=[],
            core_axis_name=("core", "subcore"),
            dimension_semantics=(pltpu.PARALLEL,),
        )(x_hbm, idx_hbm)

    return dispatch(xn_tiles, dest_win)


def _collect_rows(yb_tiles, src_win):
    n_pairs = src_win.shape[0] * ROW_WINDOW

    @functools.partial(
        pl.kernel, mesh=_sc_mesh(), scratch_types=[],
        out_type=jax.ShapeDtypeStruct((n_pairs,) + yb_tiles.shape[1:], yb_tiles.dtype))
    def collect(y_hbm, idx_hbm, o_hbm):
        def body(idx_vmem, o_vmem):
            pltpu.sync_copy(y_hbm.at[idx_vmem.at[0, pl.ds(0, ROW_WINDOW)]], o_vmem)

        pltpu.emit_pipeline(
            body,
            grid=(n_pairs // ROW_WINDOW,),
            in_specs=[pl.BlockSpec((1, LANES), lambda i: (i, 0))],
            out_specs=[pl.BlockSpec((ROW_WINDOW,) + yb_tiles.shape[1:], lambda i: (i, 0, 0))],
            core_axis_name=("core", "subcore"),
            dimension_semantics=(pltpu.PARALLEL,),
        )(idx_hbm, o_hbm)

    return collect(yb_tiles, src_win)


def _expert_kernel(blk_exp_ref, n_used_ref, next_exp_ref, slot_ref, blk_rows_ref,
                   xb_ref, wgu_hbm, bgu_ref, wd_hbm, bd_ref, yb_ref,
                   wgu_f, wd_f, wgu_b, wd_b, wsem):
    i = pl.program_id(0)
    n_used = n_used_ref[0]
    bm = xb_ref.shape[0] // SUBLANES
    dff = wd_f.shape[1]

    def weight_copies(e, s):
        return (pltpu.make_async_copy(wgu_hbm.at[e], wgu_f.at[s], wsem.at[0, s]),
                pltpu.make_async_copy(wd_hbm.at[e], wd_f.at[s], wsem.at[1, s]))

    prev = blk_exp_ref[jnp.maximum(i - 1, 0)]
    first_of_expert = (i == 0) | (blk_exp_ref[i] != prev)

    @pl.when(first_of_expert & (i < n_used))
    def _():
        e = blk_exp_ref[i]
        s = slot_ref[i]

        @pl.when(i == 0)
        def _():
            for cp in weight_copies(e, s):
                cp.start()

        for cp in weight_copies(e, s):
            cp.wait()
        wgu_b[...] = wgu_f[s].astype(BF16)
        wd_b[...] = wd_f[s].astype(BF16)

        nxt = next_exp_ref[i]

        @pl.when(nxt >= 0)
        def _():
            for cp in weight_copies(nxt, 1 - s):
                cp.start()

    def swiglu_rows(rows):
        x_rows = xb_ref.at[pl.ds(0, rows * SUBLANES)]
        y_rows = yb_ref.at[pl.ds(0, rows * SUBLANES)]
        xb = _load_token_tiles(x_rows, rows).astype(BF16)
        hid = _dot(xb, wgu_b[...]) + bgu_ref[0]
        gate = jnp.minimum(hid[:, :dff], SWIGLU_LIMIT)
        up = jnp.clip(hid[:, dff:], -SWIGLU_LIMIT, SWIGLU_LIMIT)
        glu = gate * jax.nn.sigmoid(SWIGLU_ALPHA * gate)
        act = ((up + 1.0) * glu).astype(BF16)
        _store_token_tiles(y_rows, _dot(act, wd_b[...]) + bd_ref[0])

    used = i < n_used
    half_full = blk_rows_ref[i] <= bm // 2
    pl.when(used & jnp.logical_not(half_full))(functools.partial(swiglu_rows, bm))
    pl.when(used & half_full)(functools.partial(swiglu_rows, bm // 2))


def _experts(blk_exp, n_used, next_exp, slot, blk_rows, xb_tiles, wgu, bgu, wd, bd):
    d = wgu.shape[1]
    chunks = d // LANES
    n_blocks = xb_tiles.shape[0] // (EXPERT_BLOCK * chunks)
    two_f = wgu.shape[2]
    dff = wd.shape[1]
    blk = lambda i, be, nu, ne, sl, br: (jnp.minimum(i, nu[0] - 1), 0)
    exp3 = lambda i, be, nu, ne, sl, br: (be[jnp.minimum(i, nu[0] - 1)], 0, 0)
    grid_spec = pltpu.PrefetchScalarGridSpec(
        num_scalar_prefetch=5,
        grid=(n_blocks,),
        in_specs=[pl.BlockSpec((EXPERT_BLOCK * chunks, LANES), blk),
                  pl.BlockSpec(memory_space=pl.ANY),
                  pl.BlockSpec((1, 1, two_f), exp3),
                  pl.BlockSpec(memory_space=pl.ANY),
                  pl.BlockSpec((1, 1, d), exp3)],
        out_specs=pl.BlockSpec((EXPERT_BLOCK * chunks, LANES), blk),
        scratch_shapes=[pltpu.VMEM((2, d, two_f), F32), pltpu.VMEM((2, dff, d), F32),
                        pltpu.VMEM((d, two_f), BF16), pltpu.VMEM((dff, d), BF16),
                        pltpu.SemaphoreType.DMA((2, 2))])
    return pl.pallas_call(
        _expert_kernel,
        grid_spec=grid_spec,
        out_shape=jax.ShapeDtypeStruct(xb_tiles.shape, F32),
        compiler_params=pltpu.CompilerParams(
            dimension_semantics=("arbitrary",), vmem_limit_bytes=VMEM_LIMIT),
        name="experts",
    )(blk_exp, n_used, next_exp, slot, blk_rows, xb_tiles, wgu, bgu, wd, bd)


def _combine_kernel(y0_ref, y1_ref, y2_ref, y3_ref, gate_ref, x2_ref, fw_ref, out_ref):
    gates = gate_ref[...]
    lane = lax.broadcasted_iota(I32, gates.shape, 1)
    x3 = x2_ref[...]
    for kk, y_ref in enumerate((y0_ref, y1_ref, y2_ref, y3_ref)):
        gk = jnp.sum(jnp.where(lane == kk, gates, 0.0), axis=-1, keepdims=True)
        x3 = x3 + gk * _load_token_tiles(y_ref, x3.shape[0])
    out_ref[...] = x3 * lax.rsqrt(jnp.mean(x3 * x3, axis=-1, keepdims=True) + NORM_EPS) * fw_ref[...]


def _combine_into_kernel(y0_ref, y1_ref, y2_ref, y3_ref, gate_ref, x2_ref, fw_ref, prev_ref, out_ref):
    del prev_ref
    _combine_kernel(y0_ref, y1_ref, y2_ref, y3_ref, gate_ref, x2_ref, fw_ref, out_ref)


def _combine(y4, gates, x2, fw, out_prev, chunk, *, tile):
    t, d = x2.shape
    tc = y4.shape[0] // (TOP_K * (d // LANES))
    steps = tc // tile
    first = chunk * steps
    choice = lambda kk: pl.BlockSpec((tile * (d // LANES), LANES), lambda i: (kk * steps + i, 0))
    in_specs = ([choice(kk) for kk in range(TOP_K)]
                + [pl.BlockSpec((tile, LANES), lambda i: (first + i, 0)),
                   pl.BlockSpec((tile, d), lambda i: (first + i, 0)),
                   pl.BlockSpec((1, d), lambda i: (0, 0))])
    args = [y4, y4, y4, y4, gates, x2, fw]
    if out_prev is not None:
        in_specs.append(pl.BlockSpec(memory_space=pl.ANY))
        args.append(out_prev)
    return pl.pallas_call(
        _combine_kernel if out_prev is None else _combine_into_kernel,
        grid=(steps,),
        in_specs=in_specs,
        out_specs=pl.BlockSpec((tile, d), lambda i: (first + i, 0)),
        out_shape=jax.ShapeDtypeStruct((t, d), F32),
        input_output_aliases={} if out_prev is None else {len(args) - 1: 0},
        compiler_params=pltpu.CompilerParams(
            dimension_semantics=("arbitrary",), vmem_limit_bytes=VMEM_LIMIT),
        name="combine",
    )(*args)


def _pad_lanes(a, offset=0, fill=0.0):
    out = jnp.full((1, LANES), fill, a.dtype)
    return out.at[0, offset:offset + a.shape[0]].set(a)


def _layer(x, attn_norm_w, w_in, gdn_conv_w, gdn_a_log, gdn_dt_bias, gdn_norm_w,
           cf_dw_w, cf_dw_b, cf_ln_w, cf_ln_b, w_out, ffn_norm_w, w_router, b_router,
           w_gate_up, b_gate_up, w_down, b_down, final_norm_w):
    b, s, d = x.shape
    t = b * s
    assert d == SUBLANES * LANES, "the token-tile layout needs one (8, 128) tile per token row"
    qk = GDN_HEADS * HEAD_DIM
    cfc = cf_dw_w.shape[1]
    off_b = 4 * qk
    off_cf = off_b + 2 * GDN_HEADS

    w_in_b = w_in.astype(BF16)
    wm = w_in_b[:, :off_b]
    wba = jnp.pad(w_in_b[:, off_b:off_cf], ((0, 0), (0, LANES - 2 * GDN_HEADS)))
    wcf = w_in_b[:, off_cf:]
    alog = _pad_lanes(gdn_a_log, GDN_HEADS)
    dtb = _pad_lanes(gdn_dt_bias, GDN_HEADS)

    q, k, v, z, bg, bgt, u = _inproj(
        x, attn_norm_w[None, :], wm, wba, wcf, gdn_conv_w, alog, dtb,
        cf_dw_w, cf_dw_b[None, :], cf_ln_w[None, :], cf_ln_b[None, :], tile=min(TOKEN_TILE, s))
    o = _gdn(q, k, v, bg, bgt, tile=min(TOKEN_TILE, s))

    w_out_b = w_out.astype(BF16)
    wr = jnp.pad(w_router, ((0, 0), (0, LANES - N_EXPERTS)))
    br = _pad_lanes(b_router)
    x2, xn, route, gates, counts = _outproj_router(
        o.reshape(t, qk), z.reshape(t, qk), u.reshape(t, cfc), x.reshape(t, d),
        gdn_norm_w[None, :], w_out_b[:qk], w_out_b[qk:],
        ffn_norm_w[None, :], wr, br, tile=min(TOKEN_TILE, t))

    cnt = counts[0, :N_EXPERTS]
    nblk = (cnt + EXPERT_BLOCK - 1) // EXPERT_BLOCK
    blk_end = jnp.cumsum(nblk)
    pstart = (blk_end - nblk) * EXPERT_BLOCK
    n_blocks = (t * TOP_K) // EXPERT_BLOCK + N_EXPERTS
    blk_ids = jnp.arange(n_blocks, dtype=I32)
    blk_exp = jnp.minimum(
        jnp.sum((blk_end[None, :] <= blk_ids[:, None]).astype(I32), axis=1), N_EXPERTS - 1)
    n_used = blk_end[-1:].astype(I32)
    experts = jnp.arange(N_EXPERTS, dtype=I32)
    present = nblk > 0
    later = jnp.where((experts[None, :] > experts[:, None]) & present[None, :], experts[None, :], N_EXPERTS)
    next_of = jnp.min(later, axis=1)
    next_of = jnp.where(next_of < N_EXPERTS, next_of, -1)
    slot_of = (jnp.cumsum(present.astype(I32)) - 1) % 2
    of_block = (blk_exp[:, None] == experts[None, :]).astype(I32)
    next_exp = jnp.sum(of_block * next_of[None, :], axis=1).astype(I32)
    slot = jnp.sum(of_block * slot_of[None, :], axis=1).astype(I32)
    row_end = jnp.sum(of_block * (pstart + cnt)[None, :], axis=1)
    blk_rows = jnp.clip(row_end - blk_ids * EXPERT_BLOCK, 0, EXPERT_BLOCK).astype(I32)

    dest = _dest(route, _pad_lanes(pstart.astype(I32)), tile=min(DEST_TILE, t))[:TOP_K]
    n_rows = n_blocks * EXPERT_BLOCK
    chunks = d // LANES
    windows = t // ROW_WINDOW
    dest_win = dest.reshape(TOP_K, windows, ROW_WINDOW).transpose(1, 0, 2).reshape(windows, TOP_K * ROW_WINDOW)
    xb = _dispatch_rows(xn.reshape(t, chunks, LANES), dest_win, n_rows)
    yb = _experts(blk_exp, n_used, next_exp, slot, blk_rows, xb.reshape(n_rows * chunks, LANES), w_gate_up, b_gate_up[:, None, :],
                  w_down, b_down[:, None, :]).reshape(n_rows, chunks, LANES)

    tc = t // COMBINE_CHUNKS
    out = None
    for c in range(COMBINE_CHUNKS):
        src = dest[:, c * tc:(c + 1) * tc].reshape(TOP_K * tc // ROW_WINDOW, ROW_WINDOW)
        src_win = jnp.pad(src, ((0, 0), (0, LANES - ROW_WINDOW)))
        y4 = _collect_rows(yb, src_win).reshape(TOP_K * tc * chunks, LANES)
        out = _combine(y4, gates, x2, final_norm_w[None, :], out, c, tile=min(EXPERT_BLOCK, tc))
    return out.reshape(b, s, d)


def kernel(x, attn_norm_w, w_in, gdn_conv_w, gdn_a_log, gdn_dt_bias, gdn_norm_w, cf_dw_w, cf_dw_b,
           cf_ln_w, cf_ln_b, w_out, ffn_norm_w, w_router, b_router, w_gate_up, b_gate_up, w_down,
           b_down, final_norm_w):
    depth = w_in.shape[0]
    assert depth == 1, "the fused final norm assumes a single trunk layer"
    return _layer(x, attn_norm_w[0], w_in[0], gdn_conv_w[0], gdn_a_log[0], gdn_dt_bias[0],
                  gdn_norm_w[0], cf_dw_w[0], cf_dw_b[0], cf_ln_w[0], cf_ln_b[0], w_out[0],
                  ffn_norm_w[0], w_router[0], b_router[0], w_gate_up[0], b_gate_up[0], w_down[0],
                  b_down[0], final_norm_w)
```

```python
import functools

import jax
import jax.numpy as jnp
from jax import lax
from jax.experimental import pallas as pl
from jax.experimental.pallas import tpu as pltpu
from jax.experimental.pallas import tpu_sc as plsc

F32 = jnp.float32
BF16 = jnp.bfloat16
I32 = jnp.int32

NORM_EPS = 1e-6
LANES = 128
SUBLANES = 8
GDN_HEADS = 4
HEAD_DIM = 128
GDN_CHUNK = 64
GDN_CONV = 4
CF_KERNEL = 31
N_EXPERTS = 32
TOP_K = 4
SWIGLU_LIMIT = 7.0
SWIGLU_ALPHA = 1.702

QKV_HALO = 8
CF_HALO = 32
CF_ROWS = 256
EXPERT_BLOCK = 512
TOKEN_TILE = 512
DEST_TILE = 2048
VMEM_LIMIT = 56 * 1024 * 1024


def _silu(x):
    return x * jax.nn.sigmoid(x)


def _dot(a, b):
    return jnp.dot(a, b, preferred_element_type=F32)


def _store_token_tiles(ref, x):
    rows, d = x.shape
    chunks = d // LANES
    for c in range(chunks):
        ref[pl.ds(c, rows, stride=chunks), :] = x[:, c * LANES:(c + 1) * LANES]


def _load_token_tiles(ref, rows):
    chunks = ref.shape[0] // rows
    return jnp.concatenate([ref[pl.ds(c, rows, stride=chunks), :] for c in range(chunks)], axis=1)


def _inproj_kernel(x_ref, nw_ref, wm_ref, wba_ref, wcf_ref, cw_ref, alog_ref, dtb_ref,
                   dww_ref, dwb_ref, lnw_ref, lnb_ref,
                   q_ref, k_ref, v_ref, z_ref, bg_ref, bgt_ref, u_ref,
                   qkv_buf, cf_buf, cf_shift):
    tt = x_ref.shape[1]
    qk = GDN_HEADS * HEAD_DIM
    cfc = u_ref.shape[2]

    @pl.when(pl.program_id(1) == 0)
    def _():
        qkv_buf[0:QKV_HALO, :] = jnp.zeros((QKV_HALO, qkv_buf.shape[1]), F32)
        cf_buf[0:CF_HALO, :] = jnp.zeros((CF_HALO, cf_buf.shape[1]), F32)

    x = x_ref[0]
    h = x * lax.rsqrt(jnp.mean(x * x, axis=-1, keepdims=True) + NORM_EPS) * nw_ref[...]
    h = h.astype(BF16)
    pm = _dot(h, wm_ref[...])
    pba = _dot(h, wba_ref[...])
    pcf = _dot(h, wcf_ref[...])

    z_ref[0] = pm[:, 3 * qk:]

    qkv_buf[QKV_HALO:QKV_HALO + tt, :] = pm[:, :3 * qk]
    qkv_rows = qkv_buf[0:QKV_HALO + tt, :]
    acc = None
    for j in range(GDN_CONV):
        s = GDN_CONV - 1 - j
        shifted = qkv_rows if s == 0 else pltpu.roll(qkv_rows, s, 0)
        term = cw_ref[j:j + 1, :] * shifted[QKV_HALO:, :]
        acc = term if acc is None else acc + term
    qkv_buf[0:QKV_HALO, :] = qkv_buf[tt:tt + QKV_HALO, :]
    qkv = _silu(acc)
    for hd in range(GDN_HEADS):
        for base, ref in ((0, q_ref), (qk, k_ref)):
            t = qkv[:, base + hd * HEAD_DIM: base + (hd + 1) * HEAD_DIM]
            t = t * lax.rsqrt(jnp.sum(t * t, axis=-1, keepdims=True) + NORM_EPS)
            ref[0, :, hd * HEAD_DIM:(hd + 1) * HEAD_DIM] = t
    v_ref[0] = qkv[:, 2 * qk:]

    lane = lax.broadcasted_iota(I32, (tt, LANES), 1)
    row = lax.broadcasted_iota(I32, (tt, LANES), 0)
    beta = jax.nn.sigmoid(pba)
    sp_in = pba + dtb_ref[...]
    softplus = jnp.maximum(sp_in, 0.0) + jnp.log(1.0 + jnp.exp(-jnp.abs(sp_in)))
    g = -jnp.exp(alog_ref[...]) * softplus
    g = jnp.where((lane >= GDN_HEADS) & (lane < 2 * GDN_HEADS), g, 0.0)
    pos = row % GDN_CHUNK
    shift = 1
    while shift < GDN_CHUNK:
        g = g + jnp.where(pos >= shift, pltpu.roll(g, shift, 0), 0.0)
        shift *= 2
    bg = jnp.where(lane < GDN_HEADS, beta, g)
    bg_ref[0] = bg
    bgt_ref[0] = jnp.transpose(bg)[0:SUBLANES, :]

    glu = pcf[:, :cfc] * jax.nn.sigmoid(pcf[:, cfc:])
    cf_buf[CF_HALO:CF_HALO + tt, :] = glu
    lo = SUBLANES
    span = tt + CF_HALO - lo
    cf_rows = cf_buf[...]
    for r in range(1, SUBLANES):
        cf_shift[r - 1, lo:lo + span, :] = pltpu.roll(cf_rows, r, 0)[lo:lo + span, :]
    rows = CF_ROWS
    for r0 in range(0, tt, rows):
        acc = None
        for j in range(CF_KERNEL):
            a, r = divmod(CF_KERNEL - 1 - j, SUBLANES)
            start = CF_HALO + r0 - a * SUBLANES
            src = cf_buf[start:start + rows, :] if r == 0 else cf_shift[r - 1, start:start + rows, :]
            term = dww_ref[j:j + 1, :] * src
            acc = term if acc is None else acc + term
        c = acc + dwb_ref[...]
        mu = jnp.mean(c, axis=-1, keepdims=True)
        cc = c - mu
        y = cc * lax.rsqrt(jnp.mean(cc * cc, axis=-1, keepdims=True) + NORM_EPS)
        u_ref[0, r0:r0 + rows, :] = _silu(y * lnw_ref[...] + lnb_ref[...])
    cf_buf[0:CF_HALO, :] = cf_buf[tt:tt + CF_HALO, :]


def _inproj(x, nw, wm, wba, wcf, cw, alog, dtb, dww, dwb, lnw, lnb, *, tile):
    b, s, d = x.shape
    qk = GDN_HEADS * HEAD_DIM
    cfc = dww.shape[1]
    grid = (b, s // tile)
    full = lambda a: pl.BlockSpec(a.shape, lambda i, j: (0,) * a.ndim)
    tok = lambda w: pl.BlockSpec((1, tile, w), lambda i, j: (i, j, 0))
    out_shape = (
        jax.ShapeDtypeStruct((b, s, qk), F32), jax.ShapeDtypeStruct((b, s, qk), F32),
        jax.ShapeDtypeStruct((b, s, qk), F32), jax.ShapeDtypeStruct((b, s, qk), F32),
        jax.ShapeDtypeStruct((b, s, LANES), F32), jax.ShapeDtypeStruct((b, SUBLANES, s), F32),
        jax.ShapeDtypeStruct((b, s, cfc), F32))
    return pl.pallas_call(
        _inproj_kernel,
        grid=grid,
        in_specs=[tok(d)] + [full(a) for a in (nw, wm, wba, wcf, cw, alog, dtb, dww, dwb, lnw, lnb)],
        out_specs=(tok(qk), tok(qk), tok(qk), tok(qk), tok(LANES),
                   pl.BlockSpec((1, SUBLANES, tile), lambda i, j: (i, 0, j)), tok(cfc)),
        out_shape=out_shape,
        scratch_shapes=[pltpu.VMEM((QKV_HALO + tile, 3 * qk), F32),
                        pltpu.VMEM((CF_HALO + tile, cfc), F32),
                        pltpu.VMEM((SUBLANES - 1, CF_HALO + tile, cfc), F32)],
        compiler_params=pltpu.CompilerParams(
            dimension_semantics=("arbitrary", "arbitrary"), vmem_limit_bytes=VMEM_LIMIT),
        name="inproj",
    )(x, nw, wm, wba, wcf, cw, alog, dtb, dww, dwb, lnw, lnb)


def _bmm(a, b):
    return jnp.einsum("bmk,bkn->bmn", a, b, preferred_element_type=F32)


def _bmm_nt(a, b):
    return jnp.einsum("bmk,bnk->bmn", a, b, preferred_element_type=F32)


def _bmm_tn(a, b):
    return jnp.einsum("bkm,bkn->bmn", a, b, preferred_element_type=F32)


def _unit_lower_inverse(a):
    c = a.shape[-1]
    ii = lax.broadcasted_iota(I32, (c, c), 0)
    jj = lax.broadcasted_iota(I32, (c, c), 1)
    eye = (ii == jj).astype(F32)
    same16 = (ii // 16) == (jj // 16)
    same32 = (ii // 32) == (jj // 32)
    x = jnp.where(same16, -a, 0.0)
    t = eye + x
    xp = x
    for _ in range(3):
        xp_b = xp.astype(BF16)
        xp = _bmm(xp_b, xp_b)
        t = t + _bmm(t.astype(BF16), xp.astype(BF16))
    for off in (jnp.where(same32 & ~same16, a, 0.0), jnp.where(~same32, a, 0.0)):
        tb = t.astype(BF16)
        t = t - _bmm(tb, _bmm(off.astype(BF16), tb).astype(BF16))
    return t


def _gdn_kernel(q_ref, k_ref, v_ref, bg_ref, bgt_ref, o_ref, state, s_all):
    lt = q_ref.shape[1]
    c = GDN_CHUNK
    nh = GDN_HEADS
    nc = lt // c

    @pl.when(pl.program_id(1) == 0)
    def _():
        state[...] = jnp.zeros(state.shape, F32)

    def stack(fn):
        return jnp.stack([fn(slice(n * c, (n + 1) * c), h) for n in range(nc) for h in range(nh)])

    head = lambda h: slice(h * HEAD_DIM, (h + 1) * HEAD_DIM)
    q = stack(lambda r, h: q_ref[0, r, head(h)]) * (HEAD_DIM ** -0.5)
    k = stack(lambda r, h: k_ref[0, r, head(h)])
    v = stack(lambda r, h: v_ref[0, r, head(h)])
    beta = stack(lambda r, h: bg_ref[0, r, h:h + 1])
    gcol = stack(lambda r, h: bg_ref[0, r, nh + h:nh + h + 1])
    grow = stack(lambda r, h: bgt_ref[0, nh + h:nh + h + 1, r])

    ii = lax.broadcasted_iota(I32, (c, c), 0)
    jj = lax.broadcasted_iota(I32, (c, c), 1)
    glast = gcol[:, c - 1:c, :]
    eg = jnp.exp(gcol)
    decay = jnp.where(ii >= jj, jnp.exp(jnp.minimum(gcol - grow, 0.0)), 0.0)
    kb = k * beta
    k_b = k.astype(BF16)
    a = jnp.where(ii > jj, _bmm_nt(kb.astype(BF16), k_b) * decay, 0.0)
    t = _unit_lower_inverse(a)
    rhs = jnp.concatenate([v * beta, kb * eg], axis=-1).astype(BF16)
    sol = _bmm(t.astype(BF16), rhs)
    u_val = sol[..., :HEAD_DIM]
    w_key = sol[..., HEAD_DIM:]
    intra = _bmm_nt(q.astype(BF16), k_b) * decay
    k_tail = (k * jnp.exp(glast - gcol)).astype(BF16)
    upd = _bmm_tn(k_tail, sol.astype(BF16))
    b_mat = upd[..., :HEAD_DIM]
    p_mat = upd[..., HEAD_DIM:].astype(BF16)
    g_tot = jnp.exp(glast)

    s = state[...]
    for n in range(nc):
        grp = slice(n * nh, (n + 1) * nh)
        s_b = s.astype(BF16)
        s_all[grp] = s_b
        s = s * g_tot[grp] - _bmm(p_mat[grp], s_b) + b_mat[grp]
    state[...] = s

    wq = jnp.concatenate([w_key, q * eg], axis=1).astype(BF16)
    ws_qs = _bmm(wq, s_all[...])
    v_new = u_val - ws_qs[:, :c]
    o = ws_qs[:, c:] + _bmm(intra.astype(BF16), v_new.astype(BF16))
    for n in range(nc):
        for h in range(nh):
            o_ref[0, n * c:(n + 1) * c, head(h)] = o[n * nh + h]


def _gdn(q, k, v, bg, bgt, *, tile):
    b, s, qk = q.shape
    grid = (b, s // tile)
    tok = lambda w: pl.BlockSpec((1, tile, w), lambda i, j: (i, j, 0))
    n_prob = (tile // GDN_CHUNK) * GDN_HEADS
    return pl.pallas_call(
        _gdn_kernel,
        grid=grid,
        in_specs=[tok(qk), tok(qk), tok(qk), tok(LANES),
                  pl.BlockSpec((1, SUBLANES, tile), lambda i, j: (i, 0, j))],
        out_specs=tok(qk),
        out_shape=jax.ShapeDtypeStruct((b, s, qk), F32),
        scratch_shapes=[pltpu.VMEM((GDN_HEADS, HEAD_DIM, HEAD_DIM), F32),
                        pltpu.VMEM((n_prob, HEAD_DIM, HEAD_DIM), BF16)],
        compiler_params=pltpu.CompilerParams(
            dimension_semantics=("arbitrary", "arbitrary"), vmem_limit_bytes=VMEM_LIMIT),
        name="gdn",
    )(q, k, v, bg, bgt)


def _split_bf16(x):
    hi = x.astype(BF16)
    lo = (x - hi.astype(F32)).astype(BF16)
    return hi, lo


def _outproj_router_kernel(o_ref, z_ref, u_ref, x_ref, gnw_ref, wa_ref, wb_ref, fnw_ref,
                           wr_ref, br_ref,
                           x2_ref, xn_ref, route_ref, gate_ref, counts_ref, carry):
    tt = x_ref.shape[0]

    @pl.when(pl.program_id(0) == 0)
    def _():
        carry[...] = jnp.zeros(carry.shape, F32)

    parts = []
    for hd in range(GDN_HEADS):
        sl = slice(hd * HEAD_DIM, (hd + 1) * HEAD_DIM)
        oh = o_ref[:, sl]
        y = oh * lax.rsqrt(jnp.mean(oh * oh, axis=-1, keepdims=True) + NORM_EPS) * gnw_ref[...]
        parts.append((y * _silu(z_ref[:, sl])).astype(BF16))
    out_a = jnp.concatenate(parts, axis=-1)
    x2 = x_ref[...] + _dot(out_a, wa_ref[...]) + _dot(u_ref[...].astype(BF16), wb_ref[...])
    x2_ref[...] = x2

    xn = x2 * lax.rsqrt(jnp.mean(x2 * x2, axis=-1, keepdims=True) + NORM_EPS) * fnw_ref[...]
    _store_token_tiles(xn_ref, xn)

    xh, xl = _split_bf16(xn)
    wh, wl = _split_bf16(wr_ref[...])
    logits = _dot(xh, wh) + _dot(xh, wl) + _dot(xl, wh) + br_ref[...]

    lane = lax.broadcasted_iota(I32, (tt, LANES), 1)
    lane_f = lane.astype(F32)
    neg = jnp.float32(-jnp.inf)
    work = jnp.where(lane < N_EXPERTS, logits, neg)
    vals, idxs = [], []
    onehot = jnp.zeros((tt, LANES), F32)
    for _ in range(TOP_K):
        m = jnp.max(work, axis=-1, keepdims=True)
        idx = jnp.min(jnp.where(work == m, lane_f, float(LANES)), axis=-1, keepdims=True).astype(I32)
        sel = lane == idx
        vals.append(m)
        idxs.append(idx)
        onehot = onehot + sel.astype(F32)
        work = jnp.where(sel, neg, work)
    exps = [jnp.exp(v - vals[0]) for v in vals]
    denom = exps[0] + exps[1] + exps[2] + exps[3]

    ri = lax.broadcasted_iota(I32, (tt, tt), 0)
    ci = lax.broadcasted_iota(I32, (tt, tt), 1)
    strict = (ri > ci).astype(BF16)
    base = carry[...] + _dot(strict, onehot.astype(BF16))
    route = jnp.zeros((tt, LANES), I32)
    gates = jnp.zeros((tt, LANES), F32)
    for kk in range(TOP_K):
        rank = jnp.sum(jnp.where(lane == idxs[kk], base, 0.0), axis=-1, keepdims=True)
        route = jnp.where(lane == kk, idxs[kk], route)
        route = jnp.where(lane == kk + TOP_K, rank.astype(I32), route)
        gates = jnp.where(lane == kk, exps[kk] / denom, gates)
    route_ref[...] = route
    gate_ref[...] = gates
    new_carry = carry[...] + jnp.sum(onehot, axis=0, keepdims=True)
    carry[...] = new_carry
    counts_ref[...] = new_carry.astype(I32)


def _outproj_router(o, z, u, x, gnw, wa, wb, fnw, wr, br, *, tile):
    t, d = x.shape
    grid = (t // tile,)
    full = lambda a: pl.BlockSpec(a.shape, lambda i: (0,) * a.ndim)
    tok = lambda w: pl.BlockSpec((tile, w), lambda i: (i, 0))
    return pl.pallas_call(
        _outproj_router_kernel,
        grid=grid,
        in_specs=[tok(o.shape[1]), tok(z.shape[1]), tok(u.shape[1]), tok(d)]
                 + [full(a) for a in (gnw, wa, wb, fnw, wr, br)],
        out_specs=(tok(d), pl.BlockSpec((tile * (d // LANES), LANES), lambda i: (i, 0)),
                   tok(LANES), tok(LANES), pl.BlockSpec((1, LANES), lambda i: (0, 0))),
        out_shape=(jax.ShapeDtypeStruct((t, d), F32), jax.ShapeDtypeStruct((t * (d // LANES), LANES), F32),
                   jax.ShapeDtypeStruct((t, LANES), I32), jax.ShapeDtypeStruct((t, LANES), F32),
                   jax.ShapeDtypeStruct((1, LANES), I32)),
        scratch_shapes=[pltpu.VMEM((1, LANES), F32)],
        compiler_params=pltpu.CompilerParams(
            dimension_semantics=("arbitrary",), vmem_limit_bytes=VMEM_LIMIT),
        name="outproj_router",
    )(o, z, u, x, gnw, wa, wb, fnw, wr, br)


def _dest_kernel(route_ref, pstart_ref, dest_ref):
    route = route_ref[...].astype(F32)
    tt = route.shape[0]
    lane = lax.broadcasted_iota(I32, (tt, LANES), 1)
    pstart = pstart_ref[...].astype(F32)
    dest = jnp.zeros((tt, LANES), F32)
    for kk in range(TOP_K):
        idx = jnp.sum(jnp.where(lane == kk, route, 0.0), axis=-1, keepdims=True)
        rank = jnp.sum(jnp.where(lane == kk + TOP_K, route, 0.0), axis=-1, keepdims=True)
        start = jnp.sum(jnp.where(lane == idx.astype(I32), pstart, 0.0), axis=-1, keepdims=True)
        dest = jnp.where(lane == kk, start + rank, dest)
    dest_ref[...] = jnp.transpose(dest)[0:SUBLANES, :].astype(I32)


def _dest(route, pstart, *, tile):
    t = route.shape[0]
    return pl.pallas_call(
        _dest_kernel,
        grid=(t // tile,),
        in_specs=[pl.BlockSpec((tile, LANES), lambda i: (i, 0)),
                  pl.BlockSpec((1, LANES), lambda i: (0, 0))],
        out_specs=pl.BlockSpec((SUBLANES, tile), lambda i: (0, i)),
        out_shape=jax.ShapeDtypeStruct((SUBLANES, t), I32),
        compiler_params=pltpu.CompilerParams(dimension_semantics=("arbitrary",)),
        name="dest_rows",
    )(route, pstart)


ROW_WINDOW = 32
COMBINE_CHUNKS = 8


def _sc_mesh():
    return plsc.VectorSubcoreMesh(core_axis_name="core", subcore_axis_name="subcore")


def _dispatch_rows(xn_tiles, dest_win, n_rows):
    t = xn_tiles.shape[0]

    @functools.partial(
        pl.kernel, mesh=_sc_mesh(), scratch_types=[],
        out_type=jax.ShapeDtypeStruct((n_rows,) + xn_tiles.shape[1:], xn_tiles.dtype))
    def dispatch(x_hbm, idx_hbm, o_hbm):
        def body(x_vmem, idx_vmem):
            for kk in range(TOP_K):
                pltpu.sync_copy(x_vmem, o_hbm.at[idx_vmem.at[0, pl.ds(kk * ROW_WINDOW, ROW_WINDOW)]])

        pltpu.emit_pipeline(
            body,
            grid=(t // ROW_WINDOW,),
            in_specs=[pl.BlockSpec((ROW_WINDOW,) + xn_tiles.shape[1:], lambda i: (i, 0, 0)),
                      pl.BlockSpec((1, TOP_K * ROW_WINDOW), lambda i: (i, 0))],
            out_specs=[],
            core_axis_name=("core", "subcore"),
            dimension_semantics=(pltpu.PARALLEL,),
        )(x_hbm, idx_hbm)

    return dispatch(xn_tiles, dest_win)


def _collect_rows(yb_tiles, src_win):
    n_pairs = src_win.shape[0] * ROW_WINDOW

    @functools.partial(
        pl.kernel, mesh=_sc_mesh(), scratch_types=[],
        out_type=jax.ShapeDtypeStruct((n_pairs,) + yb_tiles.shape[1:], yb_tiles.dtype))
    def collect(y_hbm, idx_hbm, o_hbm):
        def body(idx_vmem, o_vmem):
            pltpu.sync_copy(y_hbm.at[idx_vmem.at[0, pl.ds(0, ROW_WINDOW)]], o_vmem)

        pltpu.emit_pipeline(
            body,
            grid=(n_pairs // ROW_WINDOW,),
            in_specs=[pl.BlockSpec((1, LANES), lambda i: (i, 0))],
            out_specs=[pl.BlockSpec((ROW_WINDOW,) + yb_tiles.shape[1:], lambda i: (i, 0, 0))],
            core_axis_name=("core", "subcore"),
            dimension_semantics=(pltpu.PARALLEL,),
        )(idx_hbm, o_hbm)

    return collect(yb_tiles, src_win)


def _expert_kernel(blk_exp_ref, n_used_ref, next_exp_ref, slot_ref, blk_rows_ref,
                   xb_ref, wgu_hbm, bgu_ref, wd_hbm, bd_ref, yb_ref,
                   wgu_f, wd_f, wgu_b, wd_b, wsem):
    i = pl.program_id(0)
    n_used = n_used_ref[0]
    bm = xb_ref.shape[0] // SUBLANES
    dff = wd_f.shape[1]

    def weight_copies(e, s):
        return (pltpu.make_async_copy(wgu_hbm.at[e], wgu_f.at[s], wsem.at[0, s]),
                pltpu.make_async_copy(wd_hbm.at[e], wd_f.at[s], wsem.at[1, s]))

    prev = blk_exp_ref[jnp.maximum(i - 1, 0)]
    first_of_expert = (i == 0) | (blk_exp_ref[i] != prev)

    @pl.when(first_of_expert & (i < n_used))
    def _():
        e = blk_exp_ref[i]
        s = slot_ref[i]

        @pl.when(i == 0)
        def _():
            for cp in weight_copies(e, s):
                cp.start()

        for cp in weight_copies(e, s):
            cp.wait()
        wgu_b[...] = wgu_f[s].astype(BF16)
        wd_b[...] = wd_f[s].astype(BF16)

        nxt = next_exp_ref[i]

        @pl.when(nxt >= 0)
        def _():
            for cp in weight_copies(nxt, 1 - s):
                cp.start()

    def swiglu_rows(rows):
        x_rows = xb_ref.at[pl.ds(0, rows * SUBLANES)]
        y_rows = yb_ref.at[pl.ds(0, rows * SUBLANES)]
        xb = _load_token_tiles(x_rows, rows).astype(BF16)
        hid = _dot(xb, wgu_b[...]) + bgu_ref[0]
        gate = jnp.minimum(hid[:, :dff], SWIGLU_LIMIT)
        up = jnp.clip(hid[:, dff:], -SWIGLU_LIMIT, SWIGLU_LIMIT)
        glu = gate * jax.nn.sigmoid(SWIGLU_ALPHA * gate)
        act = ((up + 1.0) * glu).astype(BF16)
        _store_token_tiles(y_rows, _dot(act, wd_b[...]) + bd_ref[0])

    used = i < n_used
    half_full = blk_rows_ref[i] <= bm // 2
    pl.when(used & jnp.logical_not(half_full))(functools.partial(swiglu_rows, bm))
    pl.when(used & half_full)(functools.partial(swiglu_rows, bm // 2))


def _experts(blk_exp, n_used, next_exp, slot, blk_rows, xb_tiles, wgu, bgu, wd, bd):
    d = wgu.shape[1]
    chunks = d // LANES
    n_blocks = xb_tiles.shape[0] // (EXPERT_BLOCK * chunks)
    two_f = wgu.shape[2]
    dff = wd.shape[1]
    blk = lambda i, be, nu, ne, sl, br: (jnp.minimum(i, nu[0] - 1), 0)
    exp3 = lambda i, be, nu, ne, sl, br: (be[jnp.minimum(i, nu[0] - 1)], 0, 0)
    grid_spec = pltpu.PrefetchScalarGridSpec(
        num_scalar_prefetch=5,
        grid=(n_blocks,),
        in_specs=[pl.BlockSpec((EXPERT_BLOCK * chunks, LANES), blk),
                  pl.BlockSpec(memory_space=pl.ANY),
                  pl.BlockSpec((1, 1, two_f), exp3),
                  pl.BlockSpec(memory_space=pl.ANY),
                  pl.BlockSpec((1, 1, d), exp3)],
        out_specs=pl.BlockSpec((EXPERT_BLOCK * chunks, LANES), blk),
        scratch_shapes=[pltpu.VMEM((2, d, two_f), F32), pltpu.VMEM((2, dff, d), F32),
                        pltpu.VMEM((d, two_f), BF16), pltpu.VMEM((dff, d), BF16),
                        pltpu.SemaphoreType.DMA((2, 2))])
    return pl.pallas_call(
        _expert_kernel,
        grid_spec=grid_spec,
        out_shape=jax.ShapeDtypeStruct(xb_tiles.shape, F32),
        compiler_params=pltpu.CompilerParams(
            dimension_semantics=("arbitrary",), vmem_limit_bytes=VMEM_LIMIT),
        name="experts",
    )(blk_exp, n_used, next_exp, slot, blk_rows, xb_tiles, wgu, bgu, wd, bd)


def _combine_kernel(y0_ref, y1_ref, y2_ref, y3_ref, gate_ref, x2_ref, fw_ref, out_ref):
    gates = gate_ref[...]
    lane = lax.broadcasted_iota(I32, gates.shape, 1)
    x3 = x2_ref[...]
    for kk, y_ref in enumerate((y0_ref, y1_ref, y2_ref, y3_ref)):
        gk = jnp.sum(jnp.where(lane == kk, gates, 0.0), axis=-1, keepdims=True)
        x3 = x3 + gk * _load_token_tiles(y_ref, x3.shape[0])
    out_ref[...] = x3 * lax.rsqrt(jnp.mean(x3 * x3, axis=-1, keepdims=True) + NORM_EPS) * fw_ref[...]


def _combine_into_kernel(y0_ref, y1_ref, y2_ref, y3_ref, gate_ref, x2_ref, fw_ref, prev_ref, out_ref):
    del prev_ref
    _combine_kernel(y0_ref, y1_ref, y2_ref, y3_ref, gate_ref, x2_ref, fw_ref, out_ref)


def _combine(y4, gates, x2, fw, out_prev, chunk, *, tile):
    t, d = x2.shape
    tc = y4.shape[0] // (TOP_K * (d // LANES))
    steps = tc // tile
    first = chunk * steps
    choice = lambda kk: pl.BlockSpec((tile * (d // LANES), LANES), lambda i: (kk * steps + i, 0))
    in_specs = ([choice(kk) for kk in range(TOP_K)]
                + [pl.BlockSpec((tile, LANES), lambda i: (first + i, 0)),
                   pl.BlockSpec((tile, d), lambda i: (first + i, 0)),
                   pl.BlockSpec((1, d), lambda i: (0, 0))])
    args = [y4, y4, y4, y4, gates, x2, fw]
    if out_prev is not None:
        in_specs.append(pl.BlockSpec(memory_space=pl.ANY))
        args.append(out_prev)
    return pl.pallas_call(
        _combine_kernel if out_prev is None else _combine_into_kernel,
        grid=(steps,),
        in_specs=in_specs,
        out_specs=pl.BlockSpec((tile, d), lambda i: (first + i, 0)),
        out_shape=jax.ShapeDtypeStruct((t, d), F32),
        input_output_aliases={} if out_prev is None else {len(args) - 1: 0},
        compiler_params=pltpu.CompilerParams(
            dimension_semantics=("arbitrary",), vmem_limit_bytes=VMEM_LIMIT),
        name="combine",
    )(*args)


def _pad_lanes(a, offset=0, fill=0.0):
    out = jnp.full((1, LANES), fill, a.dtype)
    return out.at[0, offset:offset + a.shape[0]].set(a)


def _layer(x, attn_norm_w, w_in, gdn_conv_w, gdn_a_log, gdn_dt_bias, gdn_norm_w,
           cf_dw_w, cf_dw_b, cf_ln_w, cf_ln_b, w_out, ffn_norm_w, w_router, b_router,
           w_gate_up, b_gate_up, w_down, b_down, final_norm_w):
    b, s, d = x.shape
    t = b * s
    assert d == SUBLANES * LANES, "the token-tile layout needs one (8, 128) tile per token row"
    qk = GDN_HEADS * HEAD_DIM
    cfc = cf_dw_w.shape[1]
    off_b = 4 * qk
    off_cf = off_b + 2 * GDN_HEADS

    w_in_b = w_in.astype(BF16)
    wm = w_in_b[:, :off_b]
    wba = jnp.pad(w_in_b[:, off_b:off_cf], ((0, 0), (0, LANES - 2 * GDN_HEADS)))
    wcf = w_in_b[:, off_cf:]
    alog = _pad_lanes(gdn_a_log, GDN_HEADS)
    dtb = _pad_lanes(gdn_dt_bias, GDN_HEADS)

    q, k, v, z, bg, bgt, u = _inproj(
        x, attn_norm_w[None, :], wm, wba, wcf, gdn_conv_w, alog, dtb,
        cf_dw_w, cf_dw_b[None, :], cf_ln_w[None, :], cf_ln_b[None, :], tile=min(TOKEN_TILE, s))
    o = _gdn(q, k, v, bg, bgt, tile=min(TOKEN_TILE, s))

    w_out_b = w_out.astype(BF16)
    wr = jnp.pad(w_router, ((0, 0), (0, LANES - N_EXPERTS)))
    br = _pad_lanes(b_router)
    x2, xn, route, gates, counts = _outproj_router(
        o.reshape(t, qk), z.reshape(t, qk), u.reshape(t, cfc), x.reshape(t, d),
        gdn_norm_w[None, :], w_out_b[:qk], w_out_b[qk:],
        ffn_norm_w[None, :], wr, br, tile=min(TOKEN_TILE, t))

    cnt = counts[0, :N_EXPERTS]
    nblk = (cnt + EXPERT_BLOCK - 1) // EXPERT_BLOCK
    blk_end = jnp.cumsum(nblk)
    pstart = (blk_end - nblk) * EXPERT_BLOCK
    n_blocks = (t * TOP_K) // EXPERT_BLOCK + N_EXPERTS
    blk_ids = jnp.arange(n_blocks, dtype=I32)
    blk_exp = jnp.minimum(
        jnp.sum((blk_end[None, :] <= blk_ids[:, None]).astype(I32), axis=1), N_EXPERTS - 1)
    n_used = blk_end[-1:].astype(I32)
    experts = jnp.arange(N_EXPERTS, dtype=I32)
    present = nblk > 0
    later = jnp.where((experts[None, :] > experts[:, None]) & present[None, :], experts[None, :], N_EXPERTS)
    next_of = jnp.min(later, axis=1)
    next_of = jnp.where(next_of < N_EXPERTS, next_of, -1)
    slot_of = (jnp.cumsum(present.astype(I32)) - 1) % 2
    of_block = (blk_exp[:, None] == experts[None, :]).astype(I32)
    next_exp = jnp.sum(of_block * next_of[None, :], axis=1).astype(I32)
    slot = jnp.sum(of_block * slot_of[None, :], axis=1).astype(I32)
    row_end = jnp.sum(of_block * (pstart + cnt)[None, :], axis=1)
    blk_rows = jnp.clip(row_end - blk_ids * EXPERT_BLOCK, 0, EXPERT_BLOCK).astype(I32)

    dest = _dest(route, _pad_lanes(pstart.astype(I32)), tile=min(DEST_TILE, t))[:TOP_K]
    n_rows = n_blocks * EXPERT_BLOCK
    chunks = d // LANES
    windows = t // ROW_WINDOW
    dest_win = dest.reshape(TOP_K, windows, ROW_WINDOW).transpose(1, 0, 2).reshape(windows, TOP_K * ROW_WINDOW)
    xb = _dispatch_rows(xn.reshape(t, chunks, LANES), dest_win, n_rows)
    yb = _experts(blk_exp, n_used, next_exp, slot, blk_rows, xb.reshape(n_rows * chunks, LANES), w_gate_up, b_gate_up[:, None, :],
                  w_down, b_down[:, None, :]).reshape(n_rows, chunks, LANES)

    tc = t // COMBINE_CHUNKS
    out = None
    for c in range(COMBINE_CHUNKS):
        src = dest[:, c * tc:(c + 1) * tc].reshape(TOP_K * tc // ROW_WINDOW, ROW_WINDOW)
        src_win = jnp.pad(src, ((0, 0), (0, LANES - ROW_WINDOW)))
        y4 = _collect_rows(yb, src_win).reshape(TOP_K * tc * chunks, LANES)
        out = _combine(y4, gates, x2, final_norm_w[None, :], out, c, tile=min(EXPERT_BLOCK, tc))
    return out.reshape(b, s, d)


def kernel(x, attn_norm_w, w_in, gdn_conv_w, gdn_a_log, gdn_dt_bias, gdn_norm_w, cf_dw_w, cf_dw_b,
           cf_ln_w, cf_ln_b, w_out, ffn_norm_w, w_router, b_router, w_gate_up, b_gate_up, w_down,
           b_down, final_norm_w):
    depth = w_in.shape[0]
    assert depth == 1, "the fused final norm assumes a single trunk layer"
    return _layer(x, attn_norm_w[0], w_in[0], gdn_conv_w[0], gdn_a_log[0], gdn_dt_bias[0],
                  gdn_norm_w[0], cf_dw_w[0], cf_dw_b[0], cf_ln_w[0], cf_ln_b[0], w_out[0],
                  ffn_norm_w[0], w_router[0], b_router[0], w_gate_up[0], b_gate_up[0], w_down[0],
                  b_down[0], final_norm_w)
```

```python
import functools

import jax
import jax.numpy as jnp
from jax import lax
from jax.experimental import pallas as pl
from jax.experimental.pallas import tpu as pltpu
from jax.experimental.pallas import tpu_sc as plsc

F32 = jnp.float32
BF16 = jnp.bfloat16
I32 = jnp.int32

NORM_EPS = 1e-6
LANES = 128
SUBLANES = 8
GDN_HEADS = 4
HEAD_DIM = 128
GDN_CHUNK = 64
GDN_CONV = 4
CF_KERNEL = 31
N_EXPERTS = 32
TOP_K = 4
SWIGLU_LIMIT = 7.0
SWIGLU_ALPHA = 1.702

QKV_HALO = 8
CF_HALO = 32
CF_ROWS = 256
EXPERT_BLOCK = 512
TOKEN_TILE = 512
DEST_TILE = 2048
VMEM_LIMIT = 56 * 1024 * 1024


def _silu(x):
    return x * jax.nn.sigmoid(x)


def _dot(a, b):
    return jnp.dot(a, b, preferred_element_type=F32)


def _store_token_tiles(ref, x):
    rows, d = x.shape
    chunks = d // LANES
    for c in range(chunks):
        ref[pl.ds(c, rows, stride=chunks), :] = x[:, c * LANES:(c + 1) * LANES]


def _load_token_tiles(ref, rows):
    chunks = ref.shape[0] // rows
    return jnp.concatenate([ref[pl.ds(c, rows, stride=chunks), :] for c in range(chunks)], axis=1)


def _inproj_kernel(x_ref, nw_ref, wm_ref, wba_ref, wcf_ref, cw_ref, alog_ref, dtb_ref,
                   dww_ref, dwb_ref, lnw_ref, lnb_ref,
                   q_ref, k_ref, v_ref, z_ref, bg_ref, bgt_ref, u_ref,
                   qkv_buf, cf_buf, cf_shift):
    tt = x_ref.shape[1]
    qk = GDN_HEADS * HEAD_DIM
    cfc = u_ref.shape[2]

    @pl.when(pl.program_id(1) == 0)
    def _():
        qkv_buf[0:QKV_HALO, :] = jnp.zeros((QKV_HALO, qkv_buf.shape[1]), F32)
        cf_buf[0:CF_HALO, :] = jnp.zeros((CF_HALO, cf_buf.shape[1]), F32)

    x = x_ref[0]
    h = x * lax.rsqrt(jnp.mean(x * x, axis=-1, keepdims=True) + NORM_EPS) * nw_ref[...]
    h = h.astype(BF16)
    pm = _dot(h, wm_ref[...])
    pba = _dot(h, wba_ref[...])
    pcf = _dot(h, wcf_ref[...])

    z_ref[0] = pm[:, 3 * qk:]

    qkv_buf[QKV_HALO:QKV_HALO + tt, :] = pm[:, :3 * qk]
    qkv_rows = qkv_buf[0:QKV_HALO + tt, :]
    acc = None
    for j in range(GDN_CONV):
        s = GDN_CONV - 1 - j
        shifted = qkv_rows if s == 0 else pltpu.roll(qkv_rows, s, 0)
        term = cw_ref[j:j + 1, :] * shifted[QKV_HALO:, :]
        acc = term if acc is None else acc + term
    qkv_buf[0:QKV_HALO, :] = qkv_buf[tt:tt + QKV_HALO, :]
    qkv = _silu(acc)
    for hd in range(GDN_HEADS):
        for base, ref in ((0, q_ref), (qk, k_ref)):
            t = qkv[:, base + hd * HEAD_DIM: base + (hd + 1) * HEAD_DIM]
            t = t * lax.rsqrt(jnp.sum(t * t, axis=-1, keepdims=True) + NORM_EPS)
            ref[0, :, hd * HEAD_DIM:(hd + 1) * HEAD_DIM] = t
    v_ref[0] = qkv[:, 2 * qk:]

    lane = lax.broadcasted_iota(I32, (tt, LANES), 1)
    row = lax.broadcasted_iota(I32, (tt, LANES), 0)
    beta = jax.nn.sigmoid(pba)
    sp_in = pba + dtb_ref[...]
    softplus = jnp.maximum(sp_in, 0.0) + jnp.log(1.0 + jnp.exp(-jnp.abs(sp_in)))
    g = -jnp.exp(alog_ref[...]) * softplus
    g = jnp.where((lane >= GDN_HEADS) & (lane < 2 * GDN_HEADS), g, 0.0)
    pos = row % GDN_CHUNK
    shift = 1
    while shift < GDN_CHUNK:
        g = g + jnp.where(pos >= shift, pltpu.roll(g, shift, 0), 0.0)
        shift *= 2
    bg = jnp.where(lane < GDN_HEADS, beta, g)
    bg_ref[0] = bg
    bgt_ref[0] = jnp.transpose(bg)[0:SUBLANES, :]

    glu = pcf[:, :cfc] * jax.nn.sigmoid(pcf[:, cfc:])
    cf_buf[CF_HALO:CF_HALO + tt, :] = glu
    lo = SUBLANES
    span = tt + CF_HALO - lo
    cf_rows = cf_buf[...]
    for r in range(1, SUBLANES):
        cf_shift[r - 1, lo:lo + span, :] = pltpu.roll(cf_rows, r, 0)[lo:lo + span, :]
    rows = CF_ROWS
    for r0 in range(0, tt, rows):
        acc = None
        for j in range(CF_KERNEL):
            a, r = divmod(CF_KERNEL - 1 - j, SUBLANES)
            start = CF_HALO + r0 - a * SUBLANES
            src = cf_buf[start:start + rows, :] if r == 0 else cf_shift[r - 1, start:start + rows, :]
            term = dww_ref[j:j + 1, :] * src
            acc = term if acc is None else acc + term
        c = acc + dwb_ref[...]
        mu = jnp.mean(c, axis=-1, keepdims=True)
        cc = c - mu
        y = cc * lax.rsqrt(jnp.mean(cc * cc, axis=-1, keepdims=True) + NORM_EPS)
        u_ref[0, r0:r0 + rows, :] = _silu(y * lnw_ref[...] + lnb_ref[...])
    cf_buf[0:CF_HALO, :] = cf_buf[tt:tt + CF_HALO, :]


def _inproj(x, nw, wm, wba, wcf, cw, alog, dtb, dww, dwb, lnw, lnb, *, tile):
    b, s, d = x.shape
    qk = GDN_HEADS * HEAD_DIM
    cfc = dww.shape[1]
    grid = (b, s // tile)
    full = lambda a: pl.BlockSpec(a.shape, lambda i, j: (0,) * a.ndim)
    tok = lambda w: pl.BlockSpec((1, tile, w), lambda i, j: (i, j, 0))
    out_shape = (
        jax.ShapeDtypeStruct((b, s, qk), F32), jax.ShapeDtypeStruct((b, s, qk), F32),
        jax.ShapeDtypeStruct((b, s, qk), F32), jax.ShapeDtypeStruct((b, s, qk), F32),
        jax.ShapeDtypeStruct((b, s, LANES), F32), jax.ShapeDtypeStruct((b, SUBLANES, s), F32),
        jax.ShapeDtypeStruct((b, s, cfc), F32))
    return pl.pallas_call(
        _inproj_kernel,
        grid=grid,
        in_specs=[tok(d)] + [full(a) for a in (nw, wm, wba, wcf, cw, alog, dtb, dww, dwb, lnw, lnb)],
        out_specs=(tok(qk), tok(qk), tok(qk), tok(qk), tok(LANES),
                   pl.BlockSpec((1, SUBLANES, tile), lambda i, j: (i, 0, j)), tok(cfc)),
        out_shape=out_shape,
        scratch_shapes=[pltpu.VMEM((QKV_HALO + tile, 3 * qk), F32),
                        pltpu.VMEM((CF_HALO + tile, cfc), F32),
                        pltpu.VMEM((SUBLANES - 1, CF_HALO + tile, cfc), F32)],
        compiler_params=pltpu.CompilerParams(
            dimension_semantics=("arbitrary", "arbitrary"), vmem_limit_bytes=VMEM_LIMIT),
        name="inproj",
    )(x, nw, wm, wba, wcf, cw, alog, dtb, dww, dwb, lnw, lnb)


def _bmm(a, b):
    return jnp.einsum("bmk,bkn->bmn", a, b, preferred_element_type=F32)


def _bmm_nt(a, b):
    return jnp.einsum("bmk,bnk->bmn", a, b, preferred_element_type=F32)


def _bmm_tn(a, b):
    return jnp.einsum("bkm,bkn->bmn", a, b, preferred_element_type=F32)


def _unit_lower_inverse(a):
    c = a.shape[-1]
    ii = lax.broadcasted_iota(I32, (c, c), 0)
    jj = lax.broadcasted_iota(I32, (c, c), 1)
    eye = (ii == jj).astype(F32)
    same16 = (ii // 16) == (jj // 16)
    same32 = (ii // 32) == (jj // 32)
    x = jnp.where(same16, -a, 0.0)
    t = eye + x
    xp = x
    for _ in range(3):
        xp_b = xp.astype(BF16)
        xp = _bmm(xp_b, xp_b)
        t = t + _bmm(t.astype(BF16), xp.astype(BF16))
    for off in (jnp.where(same32 & ~same16, a, 0.0), jnp.where(~same32, a, 0.0)):
        tb = t.astype(BF16)
        t = t - _bmm(tb, _bmm(off.astype(BF16), tb).astype(BF16))
    return t


def _gdn_kernel(q_ref, k_ref, v_ref, bg_ref, bgt_ref, o_ref, state, s_all):
    lt = q_ref.shape[1]
    c = GDN_CHUNK
    nh = GDN_HEADS
    nc = lt // c

    @pl.when(pl.program_id(1) == 0)
    def _():
        state[...] = jnp.zeros(state.shape, F32)

    def stack(fn):
        return jnp.stack([fn(slice(n * c, (n + 1) * c), h) for n in range(nc) for h in range(nh)])

    head = lambda h: slice(h * HEAD_DIM, (h + 1) * HEAD_DIM)
    q = stack(lambda r, h: q_ref[0, r, head(h)]) * (HEAD_DIM ** -0.5)
    k = stack(lambda r, h: k_ref[0, r, head(h)])
    v = stack(lambda r, h: v_ref[0, r, head(h)])
    beta = stack(lambda r, h: bg_ref[0, r, h:h + 1])
    gcol = stack(lambda r, h: bg_ref[0, r, nh + h:nh + h + 1])
    grow = stack(lambda r, h: bgt_ref[0, nh + h:nh + h + 1, r])

    ii = lax.broadcasted_iota(I32, (c, c), 0)
    jj = lax.broadcasted_iota(I32, (c, c), 1)
    glast = gcol[:, c - 1:c, :]
    eg = jnp.exp(gcol)
    decay = jnp.where(ii >= jj, jnp.exp(jnp.minimum(gcol - grow, 0.0)), 0.0)
    kb = k * beta
    k_b = k.astype(BF16)
    a = jnp.where(ii > jj, _bmm_nt(kb.astype(BF16), k_b) * decay, 0.0)
    t = _unit_lower_inverse(a)
    rhs = jnp.concatenate([v * beta, kb * eg], axis=-1).astype(BF16)
    sol = _bmm(t.astype(BF16), rhs)
    u_val = sol[..., :HEAD_DIM]
    w_key = sol[..., HEAD_DIM:]
    intra = _bmm_nt(q.astype(BF16), k_b) * decay
    k_tail = (k * jnp.exp(glast - gcol)).astype(BF16)
    upd = _bmm_tn(k_tail, sol.astype(BF16))
    b_mat = upd[..., :HEAD_DIM]
    p_mat = upd[..., HEAD_DIM:].astype(BF16)
    g_tot = jnp.exp(glast)

    s = state[...]
    for n in range(nc):
        grp = slice(n * nh, (n + 1) * nh)
        s_b = s.astype(BF16)
        s_all[grp] = s_b
        s = s * g_tot[grp] - _bmm(p_mat[grp], s_b) + b_mat[grp]
    state[...] = s

    wq = jnp.concatenate([w_key, q * eg], axis=1).astype(BF16)
    ws_qs = _bmm(wq, s_all[...])
    v_new = u_val - ws_qs[:, :c]
    o = ws_qs[:, c:] + _bmm(intra.astype(BF16), v_new.astype(BF16))
    for n in range(nc):
        for h in range(nh):
            o_ref[0, n * c:(n + 1) * c, head(h)] = o[n * nh + h]


def _gdn(q, k, v, bg, bgt, *, tile):
    b, s, qk = q.shape
    grid = (b, s // tile)
    tok = lambda w: pl.BlockSpec((1, tile, w), lambda i, j: (i, j, 0))
    n_prob = (tile // GDN_CHUNK) * GDN_HEADS
    return pl.pallas_call(
        _gdn_kernel,
        grid=grid,
        in_specs=[tok(qk), tok(qk), tok(qk), tok(LANES),
                  pl.BlockSpec((1, SUBLANES, tile), lambda i, j: (i, 0, j))],
        out_specs=tok(qk),
        out_shape=jax.ShapeDtypeStruct((b, s, qk), F32),
        scratch_shapes=[pltpu.VMEM((GDN_HEADS, HEAD_DIM, HEAD_DIM), F32),
                        pltpu.VMEM((n_prob, HEAD_DIM, HEAD_DIM), BF16)],
        compiler_params=pltpu.CompilerParams(
            dimension_semantics=("arbitrary", "arbitrary"), vmem_limit_bytes=VMEM_LIMIT),
        name="gdn",
    )(q, k, v, bg, bgt)


def _split_bf16(x):
    hi = x.astype(BF16)
    lo = (x - hi.astype(F32)).astype(BF16)
    return hi, lo


def _outproj_router_kernel(o_ref, z_ref, u_ref, x_ref, gnw_ref, wa_ref, wb_ref, fnw_ref,
                           wr_ref, br_ref,
                           x2_ref, xn_ref, route_ref, gate_ref, counts_ref, carry):
    tt = x_ref.shape[0]

    @pl.when(pl.program_id(0) == 0)
    def _():
        carry[...] = jnp.zeros(carry.shape, F32)

    parts = []
    for hd in range(GDN_HEADS):
        sl = slice(hd * HEAD_DIM, (hd + 1) * HEAD_DIM)
        oh = o_ref[:, sl]
        y = oh * lax.rsqrt(jnp.mean(oh * oh, axis=-1, keepdims=True) + NORM_EPS) * gnw_ref[...]
        parts.append((y * _silu(z_ref[:, sl])).astype(BF16))
    out_a = jnp.concatenate(parts, axis=-1)
    x2 = x_ref[...] + _dot(out_a, wa_ref[...]) + _dot(u_ref[...].astype(BF16), wb_ref[...])
    x2_ref[...] = x2

    xn = x2 * lax.rsqrt(jnp.mean(x2 * x2, axis=-1, keepdims=True) + NORM_EPS) * fnw_ref[...]
    _store_token_tiles(xn_ref, xn)

    xh, xl = _split_bf16(xn)
    wh, wl = _split_bf16(wr_ref[...])
    logits = _dot(xh, wh) + _dot(xh, wl) + _dot(xl, wh) + br_ref[...]

    lane = lax.broadcasted_iota(I32, (tt, LANES), 1)
    lane_f = lane.astype(F32)
    neg = jnp.float32(-jnp.inf)
    work = jnp.where(lane < N_EXPERTS, logits, neg)
    vals, idxs = [], []
    onehot = jnp.zeros((tt, LANES), F32)
    for _ in range(TOP_K):
        m = jnp.max(work, axis=-1, keepdims=True)
        idx = jnp.min(jnp.where(work == m, lane_f, float(LANES)), axis=-1, keepdims=True).astype(I32)
        sel = lane == idx
        vals.append(m)
        idxs.append(idx)
        onehot = onehot + sel.astype(F32)
        work = jnp.where(sel, neg, work)
    exps = [jnp.exp(v - vals[0]) for v in vals]
    denom = exps[0] + exps[1] + exps[2] + exps[3]

    ri = lax.broadcasted_iota(I32, (tt, tt), 0)
    ci = lax.broadcasted_iota(I32, (tt, tt), 1)
    strict = (ri > ci).astype(BF16)
    base = carry[...] + _dot(strict, onehot.astype(BF16))
    route = jnp.zeros((tt, LANES), I32)
    gates = jnp.zeros((tt, LANES), F32)
    for kk in range(TOP_K):
        rank = jnp.sum(jnp.where(lane == idxs[kk], base, 0.0), axis=-1, keepdims=True)
        route = jnp.where(lane == kk, idxs[kk], route)
        route = jnp.where(lane == kk + TOP_K, rank.astype(I32), route)
        gates = jnp.where(lane == kk, exps[kk] / denom, gates)
    route_ref[...] = route
    gate_ref[...] = gates
    new_carry = carry[...] + jnp.sum(onehot, axis=0, keepdims=True)
    carry[...] = new_carry
    counts_ref[...] = new_carry.astype(I32)


def _outproj_router(o, z, u, x, gnw, wa, wb, fnw, wr, br, *, tile):
    t, d = x.shape
    grid = (t // tile,)
    full = lambda a: pl.BlockSpec(a.shape, lambda i: (0,) * a.ndim)
    tok = lambda w: pl.BlockSpec((tile, w), lambda i: (i, 0))
    return pl.pallas_call(
        _outproj_router_kernel,
        grid=grid,
        in_specs=[tok(o.shape[1]), tok(z.shape[1]), tok(u.shape[1]), tok(d)]
                 + [full(a) for a in (gnw, wa, wb, fnw, wr, br)],
        out_specs=(tok(d), pl.BlockSpec((tile * (d // LANES), LANES), lambda i: (i, 0)),
                   tok(LANES), tok(LANES), pl.BlockSpec((1, LANES), lambda i: (0, 0))),
        out_shape=(jax.ShapeDtypeStruct((t, d), F32), jax.ShapeDtypeStruct((t * (d // LANES), LANES), F32),
                   jax.ShapeDtypeStruct((t, LANES), I32), jax.ShapeDtypeStruct((t, LANES), F32),
                   jax.ShapeDtypeStruct((1, LANES), I32)),
        scratch_shapes=[pltpu.VMEM((1, LANES), F32)],
        compiler_params=pltpu.CompilerParams(
            dimension_semantics=("arbitrary",), vmem_limit_bytes=VMEM_LIMIT),
        name="outproj_router",
    )(o, z, u, x, gnw, wa, wb, fnw, wr, br)


def _dest_kernel(route_ref, pstart_ref, dest_ref):
    route = route_ref[...]
    tt = route.shape[0]
    lane = lax.broadcasted_iota(I32, (tt, LANES), 1)
    expert = jnp.where(lane < TOP_K, route, 0)
    start = jnp.take_along_axis(jnp.broadcast_to(pstart_ref[...], (tt, LANES)), expert, axis=1)
    rank = pltpu.roll(route, LANES - TOP_K, 1)
    dest = jnp.where(lane < TOP_K, start + rank, 0)
    dest_ref[...] = jnp.transpose(dest)[0:SUBLANES, :]


def _dest(route, pstart, *, tile):
    t = route.shape[0]
    return pl.pallas_call(
        _dest_kernel,
        grid=(t // tile,),
        in_specs=[pl.BlockSpec((tile, LANES), lambda i: (i, 0)),
                  pl.BlockSpec((1, LANES), lambda i: (0, 0))],
        out_specs=pl.BlockSpec((SUBLANES, tile), lambda i: (0, i)),
        out_shape=jax.ShapeDtypeStruct((SUBLANES, t), I32),
        compiler_params=pltpu.CompilerParams(dimension_semantics=("arbitrary",)),
        name="dest_rows",
    )(route, pstart)


ROW_WINDOW = 32
COMBINE_CHUNKS = 8


def _sc_mesh():
    return plsc.VectorSubcoreMesh(core_axis_name="core", subcore_axis_name="subcore")


def _dispatch_rows(xn_tiles, dest_win, n_rows):
    t = xn_tiles.shape[0]

    @functools.partial(
        pl.kernel, mesh=_sc_mesh(), scratch_types=[],
        out_type=jax.ShapeDtypeStruct((n_rows,) + xn_tiles.shape[1:], xn_tiles.dtype))
    def dispatch(x_hbm, idx_hbm, o_hbm):
        def body(x_vmem, idx_vmem):
            for kk in range(TOP_K):
                pltpu.sync_copy(x_vmem, o_hbm.at[idx_vmem.at[0, pl.ds(kk * ROW_WINDOW, ROW_WINDOW)]])

        pltpu.emit_pipeline(
            body,
            grid=(t // ROW_WINDOW,),
            in_specs=[pl.BlockSpec((ROW_WINDOW,) + xn_tiles.shape[1:], lambda i: (i, 0, 0)),
                      pl.BlockSpec((1, TOP_K * ROW_WINDOW), lambda i: (i, 0))],
            out_specs=[],
            core_axis_name=("core", "subcore"),
            dimension_semantics=(pltpu.PARALLEL,),
        )(x_hbm, idx_hbm)

    return dispatch(xn_tiles, dest_win)


def _collect_rows(yb_tiles, src_win):
    n_pairs = src_win.shape[0] * ROW_WINDOW

    @functools.partial(
        pl.kernel, mesh=_sc_mesh(), scratch_types=[],
        out_type=jax.ShapeDtypeStruct((n_pairs,) + yb_tiles.shape[1:], yb_tiles.dtype))
    def collect(y_hbm, idx_hbm, o_hbm):
        def body(idx_vmem, o_vmem):
            pltpu.sync_copy(y_hbm.at[idx_vmem.at[0, pl.ds(0, ROW_WINDOW)]], o_vmem)

        pltpu.emit_pipeline(
            body,
            grid=(n_pairs // ROW_WINDOW,),
            in_specs=[pl.BlockSpec((1, LANES), lambda i: (i, 0))],
            out_specs=[pl.BlockSpec((ROW_WINDOW,) + yb_tiles.shape[1:], lambda i: (i, 0, 0))],
            core_axis_name=("core", "subcore"),
            dimension_semantics=(pltpu.PARALLEL,),
        )(idx_hbm, o_hbm)

    return collect(yb_tiles, src_win)


def _expert_kernel(blk_exp_ref, n_used_ref, next_exp_ref, slot_ref, blk_rows_ref,
                   xb_ref, wgu_hbm, bgu_ref, wd_hbm, bd_ref, yb_ref,
                   wgu_f, wd_f, wgu_b, wd_b, wsem):
    i = pl.program_id(0)
    n_used = n_used_ref[0]
    bm = xb_ref.shape[0] // SUBLANES
    dff = wd_f.shape[1]

    def weight_copies(e, s):
        return (pltpu.make_async_copy(wgu_hbm.at[e], wgu_f.at[s], wsem.at[0, s]),
                pltpu.make_async_copy(wd_hbm.at[e], wd_f.at[s], wsem.at[1, s]))

    prev = blk_exp_ref[jnp.maximum(i - 1, 0)]
    first_of_expert = (i == 0) | (blk_exp_ref[i] != prev)

    @pl.when(first_of_expert & (i < n_used))
    def _():
        e = blk_exp_ref[i]
        s = slot_ref[i]

        @pl.when(i == 0)
        def _():
            for cp in weight_copies(e, s):
                cp.start()

        for cp in weight_copies(e, s):
            cp.wait()
        wgu_b[...] = wgu_f[s].astype(BF16)
        wd_b[...] = wd_f[s].astype(BF16)

        nxt = next_exp_ref[i]

        @pl.when(nxt >= 0)
        def _():
            for cp in weight_copies(nxt, 1 - s):
                cp.start()

    def swiglu_rows(rows):
        x_rows = xb_ref.at[pl.ds(0, rows * SUBLANES)]
        y_rows = yb_ref.at[pl.ds(0, rows * SUBLANES)]
        xb = _load_token_tiles(x_rows, rows).astype(BF16)
        hid = _dot(xb, wgu_b[...]) + bgu_ref[0]
        gate = jnp.minimum(hid[:, :dff], SWIGLU_LIMIT)
        up = jnp.clip(hid[:, dff:], -SWIGLU_LIMIT, SWIGLU_LIMIT)
        glu = gate * jax.nn.sigmoid(SWIGLU_ALPHA * gate)
        act = ((up + 1.0) * glu).astype(BF16)
        _store_token_tiles(y_rows, _dot(act, wd_b[...]) + bd_ref[0])

    used = i < n_used
    half_full = blk_rows_ref[i] <= bm // 2
    pl.when(used & jnp.logical_not(half_full))(functools.partial(swiglu_rows, bm))
    pl.when(used & half_full)(functools.partial(swiglu_rows, bm // 2))


def _experts(blk_exp, n_used, next_exp, slot, blk_rows, xb_tiles, wgu, bgu, wd, bd):
    d = wgu.shape[1]
    chunks = d // LANES
    n_blocks = xb_tiles.shape[0] // (EXPERT_BLOCK * chunks)
    two_f = wgu.shape[2]
    dff = wd.shape[1]
    blk = lambda i, be, nu, ne, sl, br: (jnp.minimum(i, nu[0] - 1), 0)
    exp3 = lambda i, be, nu, ne, sl, br: (be[jnp.minimum(i, nu[0] - 1)], 0, 0)
    grid_spec = pltpu.PrefetchScalarGridSpec(
        num_scalar_prefetch=5,
        grid=(n_blocks,),
        in_specs=[pl.BlockSpec((EXPERT_BLOCK * chunks, LANES), blk),
                  pl.BlockSpec(memory_space=pl.ANY),
                  pl.BlockSpec((1, 1, two_f), exp3),
                  pl.BlockSpec(memory_space=pl.ANY),
                  pl.BlockSpec((1, 1, d), exp3)],
        out_specs=pl.BlockSpec((EXPERT_BLOCK * chunks, LANES), blk),
        scratch_shapes=[pltpu.VMEM((2, d, two_f), F32), pltpu.VMEM((2, dff, d), F32),
                        pltpu.VMEM((d, two_f), BF16), pltpu.VMEM((dff, d), BF16),
                        pltpu.SemaphoreType.DMA((2, 2))])
    return pl.pallas_call(
        _expert_kernel,
        grid_spec=grid_spec,
        out_shape=jax.ShapeDtypeStruct(xb_tiles.shape, F32),
        compiler_params=pltpu.CompilerParams(
            dimension_semantics=("arbitrary",), vmem_limit_bytes=VMEM_LIMIT),
        name="experts",
    )(blk_exp, n_used, next_exp, slot, blk_rows, xb_tiles, wgu, bgu, wd, bd)


def _combine_kernel(y0_ref, y1_ref, y2_ref, y3_ref, gate_ref, x2_ref, fw_ref, out_ref):
    gates = gate_ref[...]
    lane = lax.broadcasted_iota(I32, gates.shape, 1)
    x3 = x2_ref[...]
    for kk, y_ref in enumerate((y0_ref, y1_ref, y2_ref, y3_ref)):
        gk = jnp.sum(jnp.where(lane == kk, gates, 0.0), axis=-1, keepdims=True)
        x3 = x3 + gk * _load_token_tiles(y_ref, x3.shape[0])
    out_ref[...] = x3 * lax.rsqrt(jnp.mean(x3 * x3, axis=-1, keepdims=True) + NORM_EPS) * fw_ref[...]


def _combine_into_kernel(y0_ref, y1_ref, y2_ref, y3_ref, gate_ref, x2_ref, fw_ref, prev_ref, out_ref):
    del prev_ref
    _combine_kernel(y0_ref, y1_ref, y2_ref, y3_ref, gate_ref, x2_ref, fw_ref, out_ref)


def _combine(y4, gates, x2, fw, out_prev, chunk, *, tile):
    t, d = x2.shape
    tc = y4.shape[0] // (TOP_K * (d // LANES))
    steps = tc // tile
    first = chunk * steps
    choice = lambda kk: pl.BlockSpec((tile * (d // LANES), LANES), lambda i: (kk * steps + i, 0))
    in_specs = ([choice(kk) for kk in range(TOP_K)]
                + [pl.BlockSpec((tile, LANES), lambda i: (first + i, 0)),
                   pl.BlockSpec((tile, d), lambda i: (first + i, 0)),
                   pl.BlockSpec((1, d), lambda i: (0, 0))])
    args = [y4, y4, y4, y4, gates, x2, fw]
    if out_prev is not None:
        in_specs.append(pl.BlockSpec(memory_space=pl.ANY))
        args.append(out_prev)
    return pl.pallas_call(
        _combine_kernel if out_prev is None else _combine_into_kernel,
        grid=(steps,),
        in_specs=in_specs,
        out_specs=pl.BlockSpec((tile, d), lambda i: (first + i, 0)),
        out_shape=jax.ShapeDtypeStruct((t, d), F32),
        input_output_aliases={} if out_prev is None else {len(args) - 1: 0},
        compiler_params=pltpu.CompilerParams(
            dimension_semantics=("arbitrary",), vmem_limit_bytes=VMEM_LIMIT),
        name="combine",
    )(*args)


def _pad_lanes(a, offset=0, fill=0.0):
    out = jnp.full((1, LANES), fill, a.dtype)
    return out.at[0, offset:offset + a.shape[0]].set(a)


def _layer(x, attn_norm_w, w_in, gdn_conv_w, gdn_a_log, gdn_dt_bias, gdn_norm_w,
           cf_dw_w, cf_dw_b, cf_ln_w, cf_ln_b, w_out, ffn_norm_w, w_router, b_router,
           w_gate_up, b_gate_up, w_down, b_down, final_norm_w):
    b, s, d = x.shape
    t = b * s
    assert d == SUBLANES * LANES, "the token-tile layout needs one (8, 128) tile per token row"
    qk = GDN_HEADS * HEAD_DIM
    cfc = cf_dw_w.shape[1]
    off_b = 4 * qk
    off_cf = off_b + 2 * GDN_HEADS

    w_in_b = w_in.astype(BF16)
    wm = w_in_b[:, :off_b]
    wba = jnp.pad(w_in_b[:, off_b:off_cf], ((0, 0), (0, LANES - 2 * GDN_HEADS)))
    wcf = w_in_b[:, off_cf:]
    alog = _pad_lanes(gdn_a_log, GDN_HEADS)
    dtb = _pad_lanes(gdn_dt_bias, GDN_HEADS)

    q, k, v, z, bg, bgt, u = _inproj(
        x, attn_norm_w[None, :], wm, wba, wcf, gdn_conv_w, alog, dtb,
        cf_dw_w, cf_dw_b[None, :], cf_ln_w[None, :], cf_ln_b[None, :], tile=min(TOKEN_TILE, s))
    o = _gdn(q, k, v, bg, bgt, tile=min(TOKEN_TILE, s))

    w_out_b = w_out.astype(BF16)
    wr = jnp.pad(w_router, ((0, 0), (0, LANES - N_EXPERTS)))
    br = _pad_lanes(b_router)
    x2, xn, route, gates, counts = _outproj_router(
        o.reshape(t, qk), z.reshape(t, qk), u.reshape(t, cfc), x.reshape(t, d),
        gdn_norm_w[None, :], w_out_b[:qk], w_out_b[qk:],
        ffn_norm_w[None, :], wr, br, tile=min(TOKEN_TILE, t))

    cnt = counts[0, :N_EXPERTS]
    nblk = (cnt + EXPERT_BLOCK - 1) // EXPERT_BLOCK
    blk_end = jnp.cumsum(nblk)
    pstart = (blk_end - nblk) * EXPERT_BLOCK
    n_blocks = (t * TOP_K) // EXPERT_BLOCK + N_EXPERTS
    blk_ids = jnp.arange(n_blocks, dtype=I32)
    blk_exp = jnp.minimum(
        jnp.sum((blk_end[None, :] <= blk_ids[:, None]).astype(I32), axis=1), N_EXPERTS - 1)
    n_used = blk_end[-1:].astype(I32)
    experts = jnp.arange(N_EXPERTS, dtype=I32)
    present = nblk > 0
    later = jnp.where((experts[None, :] > experts[:, None]) & present[None, :], experts[None, :], N_EXPERTS)
    next_of = jnp.min(later, axis=1)
    next_of = jnp.where(next_of < N_EXPERTS, next_of, -1)
    slot_of = (jnp.cumsum(present.astype(I32)) - 1) % 2
    of_block = (blk_exp[:, None] == experts[None, :]).astype(I32)
    next_exp = jnp.sum(of_block * next_of[None, :], axis=1).astype(I32)
    slot = jnp.sum(of_block * slot_of[None, :], axis=1).astype(I32)
    row_end = jnp.sum(of_block * (pstart + cnt)[None, :], axis=1)
    blk_rows = jnp.clip(row_end - blk_ids * EXPERT_BLOCK, 0, EXPERT_BLOCK).astype(I32)

    dest = _dest(route, _pad_lanes(pstart.astype(I32)), tile=min(DEST_TILE, t))[:TOP_K]
    n_rows = n_blocks * EXPERT_BLOCK
    chunks = d // LANES
    windows = t // ROW_WINDOW
    dest_win = dest.reshape(TOP_K, windows, ROW_WINDOW).transpose(1, 0, 2).reshape(windows, TOP_K * ROW_WINDOW)
    xb = _dispatch_rows(xn.reshape(t, chunks, LANES), dest_win, n_rows)
    yb = _experts(blk_exp, n_used, next_exp, slot, blk_rows, xb.reshape(n_rows * chunks, LANES), w_gate_up, b_gate_up[:, None, :],
                  w_down, b_down[:, None, :]).reshape(n_rows, chunks, LANES)

    tc = t // COMBINE_CHUNKS
    out = None
    for c in range(COMBINE_CHUNKS):
        src = dest[:, c * tc:(c + 1) * tc].reshape(TOP_K * tc // ROW_WINDOW, ROW_WINDOW)
        src_win = jnp.pad(src, ((0, 0), (0, LANES - ROW_WINDOW)))
        y4 = _collect_rows(yb, src_win).reshape(TOP_K * tc * chunks, LANES)
        out = _combine(y4, gates, x2, final_norm_w[None, :], out, c, tile=min(EXPERT_BLOCK, tc))
    return out.reshape(b, s, d)


def kernel(x, attn_norm_w, w_in, gdn_conv_w, gdn_a_log, gdn_dt_bias, gdn_norm_w, cf_dw_w, cf_dw_b,
           cf_ln_w, cf_ln_b, w_out, ffn_norm_w, w_router, b_router, w_gate_up, b_gate_up, w_down,
           b_down, final_norm_w):
    depth = w_in.shape[0]
    assert depth == 1, "the fused final norm assumes a single trunk layer"
    return _layer(x, attn_norm_w[0], w_in[0], gdn_conv_w[0], gdn_a_log[0], gdn_dt_bias[0],
                  gdn_norm_w[0], cf_dw_w[0], cf_dw_b[0], cf_ln_w[0], cf_ln_b[0], w_out[0],
                  ffn_norm_w[0], w_router[0], b_router[0], w_gate_up[0], b_gate_up[0], w_down[0],
                  b_down[0], final_norm_w)
```

```python
import functools

import jax
import jax.numpy as jnp
from jax import lax
from jax.experimental import pallas as pl
from jax.experimental.pallas import tpu as pltpu
from jax.experimental.pallas import tpu_sc as plsc

F32 = jnp.float32
BF16 = jnp.bfloat16
I32 = jnp.int32

NORM_EPS = 1e-6
LANES = 128
SUBLANES = 8
GDN_HEADS = 4
HEAD_DIM = 128
GDN_CHUNK = 64
GDN_CONV = 4
CF_KERNEL = 31
N_EXPERTS = 32
TOP_K = 4
SWIGLU_LIMIT = 7.0
SWIGLU_ALPHA = 1.702

QKV_HALO = 8
CF_HALO = 32
CF_ROWS = 256
EXPERT_BLOCK = 512
TOKEN_TILE = 512
DEST_TILE = 2048
VMEM_LIMIT = 56 * 1024 * 1024


def _silu(x):
    return x * jax.nn.sigmoid(x)


def _dot(a, b):
    return jnp.dot(a, b, preferred_element_type=F32)


def _store_token_tiles(ref, x):
    rows, d = x.shape
    chunks = d // LANES
    for c in range(chunks):
        ref[pl.ds(c, rows, stride=chunks), :] = x[:, c * LANES:(c + 1) * LANES]


def _load_token_tiles(ref, rows):
    chunks = ref.shape[0] // rows
    return jnp.concatenate([ref[pl.ds(c, rows, stride=chunks), :] for c in range(chunks)], axis=1)


def _inproj_kernel(x_ref, nw_ref, wm_ref, wba_ref, wcf_ref, cw_ref, alog_ref, dtb_ref,
                   dww_ref, dwb_ref, lnw_ref, lnb_ref,
                   q_ref, k_ref, v_ref, z_ref, bg_ref, bgt_ref, u_ref,
                   qkv_buf, cf_buf, cf_shift):
    tt = x_ref.shape[1]
    qk = GDN_HEADS * HEAD_DIM
    cfc = u_ref.shape[2]

    @pl.when(pl.program_id(1) == 0)
    def _():
        qkv_buf[0:QKV_HALO, :] = jnp.zeros((QKV_HALO, qkv_buf.shape[1]), F32)
        cf_buf[0:CF_HALO, :] = jnp.zeros((CF_HALO, cf_buf.shape[1]), F32)

    x = x_ref[0]
    h = x * lax.rsqrt(jnp.mean(x * x, axis=-1, keepdims=True) + NORM_EPS) * nw_ref[...]
    h = h.astype(BF16)
    pm = _dot(h, wm_ref[...])
    pba = _dot(h, wba_ref[...])
    pcf = _dot(h, wcf_ref[...])

    z_ref[0] = pm[:, 3 * qk:]

    qkv_buf[QKV_HALO:QKV_HALO + tt, :] = pm[:, :3 * qk]
    qkv_rows = qkv_buf[0:QKV_HALO + tt, :]
    acc = None
    for j in range(GDN_CONV):
        s = GDN_CONV - 1 - j
        shifted = qkv_rows if s == 0 else pltpu.roll(qkv_rows, s, 0)
        term = cw_ref[j:j + 1, :] * shifted[QKV_HALO:, :]
        acc = term if acc is None else acc + term
    qkv_buf[0:QKV_HALO, :] = qkv_buf[tt:tt + QKV_HALO, :]
    qkv = _silu(acc)
    for hd in range(GDN_HEADS):
        for base, ref in ((0, q_ref), (qk, k_ref)):
            t = qkv[:, base + hd * HEAD_DIM: base + (hd + 1) * HEAD_DIM]
            t = t * lax.rsqrt(jnp.sum(t * t, axis=-1, keepdims=True) + NORM_EPS)
            ref[0, :, hd * HEAD_DIM:(hd + 1) * HEAD_DIM] = t
    v_ref[0] = qkv[:, 2 * qk:]

    lane = lax.broadcasted_iota(I32, (tt, LANES), 1)
    row = lax.broadcasted_iota(I32, (tt, LANES), 0)
    beta = jax.nn.sigmoid(pba)
    sp_in = pba + dtb_ref[...]
    softplus = jnp.maximum(sp_in, 0.0) + jnp.log(1.0 + jnp.exp(-jnp.abs(sp_in)))
    g = -jnp.exp(alog_ref[...]) * softplus
    g = jnp.where((lane >= GDN_HEADS) & (lane < 2 * GDN_HEADS), g, 0.0)
    pos = row % GDN_CHUNK
    shift = 1
    while shift < GDN_CHUNK:
        g = g + jnp.where(pos >= shift, pltpu.roll(g, shift, 0), 0.0)
        shift *= 2
    bg = jnp.where(lane < GDN_HEADS, beta, g)
    bg_ref[0] = bg
    bgt_ref[0] = jnp.transpose(bg)[0:SUBLANES, :]

    glu = pcf[:, :cfc] * jax.nn.sigmoid(pcf[:, cfc:])
    cf_buf[CF_HALO:CF_HALO + tt, :] = glu
    lo = SUBLANES
    span = tt + CF_HALO - lo
    cf_rows = cf_buf[...]
    for r in range(1, SUBLANES):
        cf_shift[r - 1, lo:lo + span, :] = pltpu.roll(cf_rows, r, 0)[lo:lo + span, :]
    rows = CF_ROWS
    for r0 in range(0, tt, rows):
        acc = None
        for j in range(CF_KERNEL):
            a, r = divmod(CF_KERNEL - 1 - j, SUBLANES)
            start = CF_HALO + r0 - a * SUBLANES
            src = cf_buf[start:start + rows, :] if r == 0 else cf_shift[r - 1, start:start + rows, :]
            term = dww_ref[j:j + 1, :] * src
            acc = term if acc is None else acc + term
        c = acc + dwb_ref[...]
        mu = jnp.mean(c, axis=-1, keepdims=True)
        cc = c - mu
        y = cc * lax.rsqrt(jnp.mean(cc * cc, axis=-1, keepdims=True) + NORM_EPS)
        u_ref[0, r0:r0 + rows, :] = _silu(y * lnw_ref[...] + lnb_ref[...])
    cf_buf[0:CF_HALO, :] = cf_buf[tt:tt + CF_HALO, :]


def _inproj(x, nw, wm, wba, wcf, cw, alog, dtb, dww, dwb, lnw, lnb, *, tile):
    b, s, d = x.shape
    qk = GDN_HEADS * HEAD_DIM
    cfc = dww.shape[1]
    grid = (b, s // tile)
    full = lambda a: pl.BlockSpec(a.shape, lambda i, j: (0,) * a.ndim)
    tok = lambda w: pl.BlockSpec((1, tile, w), lambda i, j: (i, j, 0))
    out_shape = (
        jax.ShapeDtypeStruct((b, s, qk), F32), jax.ShapeDtypeStruct((b, s, qk), F32),
        jax.ShapeDtypeStruct((b, s, qk), F32), jax.ShapeDtypeStruct((b, s, qk), F32),
        jax.ShapeDtypeStruct((b, s, LANES), F32), jax.ShapeDtypeStruct((b, SUBLANES, s), F32),
        jax.ShapeDtypeStruct((b, s, cfc), F32))
    return pl.pallas_call(
        _inproj_kernel,
        grid=grid,
        in_specs=[tok(d)] + [full(a) for a in (nw, wm, wba, wcf, cw, alog, dtb, dww, dwb, lnw, lnb)],
        out_specs=(tok(qk), tok(qk), tok(qk), tok(qk), tok(LANES),
                   pl.BlockSpec((1, SUBLANES, tile), lambda i, j: (i, 0, j)), tok(cfc)),
        out_shape=out_shape,
        scratch_shapes=[pltpu.VMEM((QKV_HALO + tile, 3 * qk), F32),
                        pltpu.VMEM((CF_HALO + tile, cfc), F32),
                        pltpu.VMEM((SUBLANES - 1, CF_HALO + tile, cfc), F32)],
        compiler_params=pltpu.CompilerParams(
            dimension_semantics=("arbitrary", "arbitrary"), vmem_limit_bytes=VMEM_LIMIT),
        name="inproj",
    )(x, nw, wm, wba, wcf, cw, alog, dtb, dww, dwb, lnw, lnb)


def _bmm(a, b):
    return jnp.einsum("bmk,bkn->bmn", a, b, preferred_element_type=F32)


def _bmm_nt(a, b):
    return jnp.einsum("bmk,bnk->bmn", a, b, preferred_element_type=F32)


def _bmm_tn(a, b):
    return jnp.einsum("bkm,bkn->bmn", a, b, preferred_element_type=F32)


def _unit_lower_inverse(a):
    c = a.shape[-1]
    ii = lax.broadcasted_iota(I32, (c, c), 0)
    jj = lax.broadcasted_iota(I32, (c, c), 1)
    eye = (ii == jj).astype(F32)
    same16 = (ii // 16) == (jj // 16)
    same32 = (ii // 32) == (jj // 32)
    x = jnp.where(same16, -a, 0.0)
    t = eye + x
    xp = x
    for _ in range(3):
        xp_b = xp.astype(BF16)
        xp = _bmm(xp_b, xp_b)
        t = t + _bmm(t.astype(BF16), xp.astype(BF16))
    for off in (jnp.where(same32 & ~same16, a, 0.0), jnp.where(~same32, a, 0.0)):
        tb = t.astype(BF16)
        t = t - _bmm(tb, _bmm(off.astype(BF16), tb).astype(BF16))
    return t


def _gdn_kernel(q_ref, k_ref, v_ref, bg_ref, bgt_ref, o_ref, state, s_all):
    lt = q_ref.shape[1]
    c = GDN_CHUNK
    nh = GDN_HEADS
    nc = lt // c

    @pl.when(pl.program_id(1) == 0)
    def _():
        state[...] = jnp.zeros(state.shape, F32)

    def stack(fn):
        return jnp.stack([fn(slice(n * c, (n + 1) * c), h) for n in range(nc) for h in range(nh)])

    head = lambda h: slice(h * HEAD_DIM, (h + 1) * HEAD_DIM)
    q = stack(lambda r, h: q_ref[0, r, head(h)]) * (HEAD_DIM ** -0.5)
    k = stack(lambda r, h: k_ref[0, r, head(h)])
    v = stack(lambda r, h: v_ref[0, r, head(h)])
    beta = stack(lambda r, h: bg_ref[0, r, h:h + 1])
    gcol = stack(lambda r, h: bg_ref[0, r, nh + h:nh + h + 1])
    grow = stack(lambda r, h: bgt_ref[0, nh + h:nh + h + 1, r])

    ii = lax.broadcasted_iota(I32, (c, c), 0)
    jj = lax.broadcasted_iota(I32, (c, c), 1)
    glast = gcol[:, c - 1:c, :]
    eg = jnp.exp(gcol)
    decay = jnp.where(ii >= jj, jnp.exp(jnp.minimum(gcol - grow, 0.0)), 0.0)
    kb = k * beta
    k_b = k.astype(BF16)
    a = jnp.where(ii > jj, _bmm_nt(kb.astype(BF16), k_b) * decay, 0.0)
    t = _unit_lower_inverse(a)
    rhs = jnp.concatenate([v * beta, kb * eg], axis=-1).astype(BF16)
    sol = _bmm(t.astype(BF16), rhs)
    u_val = sol[..., :HEAD_DIM]
    w_key = sol[..., HEAD_DIM:]
    intra = _bmm_nt(q.astype(BF16), k_b) * decay
    k_tail = (k * jnp.exp(glast - gcol)).astype(BF16)
    upd = _bmm_tn(k_tail, sol.astype(BF16))
    b_mat = upd[..., :HEAD_DIM]
    p_mat = upd[..., HEAD_DIM:].astype(BF16)
    g_tot = jnp.exp(glast)

    s = state[...]
    for n in range(nc):
        grp = slice(n * nh, (n + 1) * nh)
        s_b = s.astype(BF16)
        s_all[grp] = s_b
        s = s * g_tot[grp] - _bmm(p_mat[grp], s_b) + b_mat[grp]
    state[...] = s

    wq = jnp.concatenate([w_key, q * eg], axis=1).astype(BF16)
    ws_qs = _bmm(wq, s_all[...])
    v_new = u_val - ws_qs[:, :c]
    o = ws_qs[:, c:] + _bmm(intra.astype(BF16), v_new.astype(BF16))
    for n in range(nc):
        for h in range(nh):
            o_ref[0, n * c:(n + 1) * c, head(h)] = o[n * nh + h]


def _gdn(q, k, v, bg, bgt, *, tile):
    b, s, qk = q.shape
    grid = (b, s // tile)
    tok = lambda w: pl.BlockSpec((1, tile, w), lambda i, j: (i, j, 0))
    n_prob = (tile // GDN_CHUNK) * GDN_HEADS
    return pl.pallas_call(
        _gdn_kernel,
        grid=grid,
        in_specs=[tok(qk), tok(qk), tok(qk), tok(LANES),
                  pl.BlockSpec((1, SUBLANES, tile), lambda i, j: (i, 0, j))],
        out_specs=tok(qk),
        out_shape=jax.ShapeDtypeStruct((b, s, qk), F32),
        scratch_shapes=[pltpu.VMEM((GDN_HEADS, HEAD_DIM, HEAD_DIM), F32),
                        pltpu.VMEM((n_prob, HEAD_DIM, HEAD_DIM), BF16)],
        compiler_params=pltpu.CompilerParams(
            dimension_semantics=("arbitrary", "arbitrary"), vmem_limit_bytes=VMEM_LIMIT),
        name="gdn",
    )(q, k, v, bg, bgt)


def _split_bf16(x):
    hi = x.astype(BF16)
    lo = (x - hi.astype(F32)).astype(BF16)
    return hi, lo


def _outproj_router_kernel(o_ref, z_ref, u_ref, x_ref, gnw_ref, wa_ref, wb_ref, fnw_ref,
                           wr_ref, br_ref,
                           x2_ref, xn_ref, route_ref, gate_ref, counts_ref, carry):
    tt = x_ref.shape[0]

    @pl.when(pl.program_id(0) == 0)
    def _():
        carry[...] = jnp.zeros(carry.shape, F32)

    parts = []
    for hd in range(GDN_HEADS):
        sl = slice(hd * HEAD_DIM, (hd + 1) * HEAD_DIM)
        oh = o_ref[:, sl]
        y = oh * lax.rsqrt(jnp.mean(oh * oh, axis=-1, keepdims=True) + NORM_EPS) * gnw_ref[...]
        parts.append((y * _silu(z_ref[:, sl])).astype(BF16))
    out_a = jnp.concatenate(parts, axis=-1)
    x2 = x_ref[...] + _dot(out_a, wa_ref[...]) + _dot(u_ref[...].astype(BF16), wb_ref[...])
    x2_ref[...] = x2

    xn = x2 * lax.rsqrt(jnp.mean(x2 * x2, axis=-1, keepdims=True) + NORM_EPS) * fnw_ref[...]
    _store_token_tiles(xn_ref, xn)

    xh, xl = _split_bf16(xn)
    wh, wl = _split_bf16(wr_ref[...])
    logits = _dot(xh, wh) + _dot(xh, wl) + _dot(xl, wh) + br_ref[...]

    lane = lax.broadcasted_iota(I32, (tt, LANES), 1)
    lane_f = lane.astype(F32)
    neg = jnp.float32(-jnp.inf)
    work = jnp.where(lane < N_EXPERTS, logits, neg)
    vals, idxs = [], []
    onehot = jnp.zeros((tt, LANES), F32)
    for _ in range(TOP_K):
        m = jnp.max(work, axis=-1, keepdims=True)
        idx = jnp.min(jnp.where(work == m, lane_f, float(LANES)), axis=-1, keepdims=True).astype(I32)
        sel = lane == idx
        vals.append(m)
        idxs.append(idx)
        onehot = onehot + sel.astype(F32)
        work = jnp.where(sel, neg, work)
    exps = [jnp.exp(v - vals[0]) for v in vals]
    denom = exps[0] + exps[1] + exps[2] + exps[3]

    ri = lax.broadcasted_iota(I32, (tt, tt), 0)
    ci = lax.broadcasted_iota(I32, (tt, tt), 1)
    strict = (ri > ci).astype(BF16)
    base = carry[...] + _dot(strict, onehot.astype(BF16))
    route = jnp.zeros((tt, LANES), I32)
    gates = jnp.zeros((tt, LANES), F32)
    for kk in range(TOP_K):
        rank = jnp.sum(jnp.where(lane == idxs[kk], base, 0.0), axis=-1, keepdims=True)
        route = jnp.where(lane == kk, idxs[kk], route)
        route = jnp.where(lane == kk + TOP_K, rank.astype(I32), route)
        gates = jnp.where(lane == kk, exps[kk] / denom, gates)
    route_ref[...] = route
    gate_ref[...] = gates
    new_carry = carry[...] + jnp.sum(onehot, axis=0, keepdims=True)
    carry[...] = new_carry
    counts_ref[...] = new_carry.astype(I32)


def _outproj_router(o, z, u, x, gnw, wa, wb, fnw, wr, br, *, tile):
    t, d = x.shape
    grid = (t // tile,)
    full = lambda a: pl.BlockSpec(a.shape, lambda i: (0,) * a.ndim)
    tok = lambda w: pl.BlockSpec((tile, w), lambda i: (i, 0))
    return pl.pallas_call(
        _outproj_router_kernel,
        grid=grid,
        in_specs=[tok(o.shape[1]), tok(z.shape[1]), tok(u.shape[1]), tok(d)]
                 + [full(a) for a in (gnw, wa, wb, fnw, wr, br)],
        out_specs=(tok(d), pl.BlockSpec((tile * (d // LANES), LANES), lambda i: (i, 0)),
                   tok(LANES), tok(LANES), pl.BlockSpec((1, LANES), lambda i: (0, 0))),
        out_shape=(jax.ShapeDtypeStruct((t, d), F32), jax.ShapeDtypeStruct((t * (d // LANES), LANES), F32),
                   jax.ShapeDtypeStruct((t, LANES), I32), jax.ShapeDtypeStruct((t, LANES), F32),
                   jax.ShapeDtypeStruct((1, LANES), I32)),
        scratch_shapes=[pltpu.VMEM((1, LANES), F32)],
        compiler_params=pltpu.CompilerParams(
            dimension_semantics=("arbitrary",), vmem_limit_bytes=VMEM_LIMIT),
        name="outproj_router",
    )(o, z, u, x, gnw, wa, wb, fnw, wr, br)


def _dest_kernel(route_ref, pstart_ref, dest_ref):
    route = route_ref[...]
    tt = route.shape[0]
    lane = lax.broadcasted_iota(I32, (tt, LANES), 1)
    expert = jnp.where(lane < TOP_K, route, 0)
    start = jnp.take_along_axis(jnp.broadcast_to(pstart_ref[...], (tt, LANES)), expert, axis=1)
    rank = pltpu.roll(route, LANES - TOP_K, 1)
    dest = jnp.where(lane < TOP_K, start + rank, 0)
    dest_ref[...] = jnp.transpose(dest)[0:SUBLANES, :]


def _dest(route, pstart, *, tile):
    t = route.shape[0]
    return pl.pallas_call(
        _dest_kernel,
        grid=(t // tile,),
        in_specs=[pl.BlockSpec((tile, LANES), lambda i: (i, 0)),
                  pl.BlockSpec((1, LANES), lambda i: (0, 0))],
        out_specs=pl.BlockSpec((SUBLANES, tile), lambda i: (0, i)),
        out_shape=jax.ShapeDtypeStruct((SUBLANES, t), I32),
        compiler_params=pltpu.CompilerParams(dimension_semantics=("arbitrary",)),
        name="dest_rows",
    )(route, pstart)


ROW_WINDOW = 32
COMBINE_CHUNKS = 8


def _sc_mesh():
    return plsc.VectorSubcoreMesh(core_axis_name="core", subcore_axis_name="subcore")


def _dispatch_rows(xn_tiles, dest_win, n_rows):
    t = xn_tiles.shape[0]

    @functools.partial(
        pl.kernel, mesh=_sc_mesh(), scratch_types=[],
        out_type=jax.ShapeDtypeStruct((n_rows,) + xn_tiles.shape[1:], xn_tiles.dtype))
    def dispatch(x_hbm, idx_hbm, o_hbm):
        def body(x_vmem, idx_vmem):
            for kk in range(TOP_K):
                pltpu.sync_copy(x_vmem, o_hbm.at[idx_vmem.at[0, pl.ds(kk * ROW_WINDOW, ROW_WINDOW)]])

        pltpu.emit_pipeline(
            body,
            grid=(t // ROW_WINDOW,),
            in_specs=[pl.BlockSpec((ROW_WINDOW,) + xn_tiles.shape[1:], lambda i: (i, 0, 0)),
                      pl.BlockSpec((1, TOP_K * ROW_WINDOW), lambda i: (i, 0))],
            out_specs=[],
            core_axis_name=("core", "subcore"),
            dimension_semantics=(pltpu.PARALLEL,),
        )(x_hbm, idx_hbm)

    return dispatch(xn_tiles, dest_win)


def _collect_rows(yb_tiles, src_win):
    n_pairs = src_win.shape[0] * ROW_WINDOW

    @functools.partial(
        pl.kernel, mesh=_sc_mesh(), scratch_types=[],
        out_type=jax.ShapeDtypeStruct((n_pairs,) + yb_tiles.shape[1:], yb_tiles.dtype))
    def collect(y_hbm, idx_hbm, o_hbm):
        def body(idx_vmem, o_vmem):
            pltpu.sync_copy(y_hbm.at[idx_vmem.at[0, pl.ds(0, ROW_WINDOW)]], o_vmem)

        pltpu.emit_pipeline(
            body,
            grid=(n_pairs // ROW_WINDOW,),
            in_specs=[pl.BlockSpec((1, LANES), lambda i: (i, 0))],
            out_specs=[pl.BlockSpec((ROW_WINDOW,) + yb_tiles.shape[1:], lambda i: (i, 0, 0))],
            core_axis_name=("core", "subcore"),
            dimension_semantics=(pltpu.PARALLEL,),
        )(idx_hbm, o_hbm)

    return collect(yb_tiles, src_win)


def _expert_kernel(blk_exp_ref, n_used_ref, next_exp_ref, slot_ref, blk_rows_ref,
                   xb_ref, wgu_hbm, bgu_ref, wd_hbm, bd_ref, yb_ref,
                   wgu_f, wd_f, wgu_b, wd_b, wsem):
    i = pl.program_id(0)
    n_used = n_used_ref[0]
    bm = xb_ref.shape[0] // SUBLANES
    dff = wd_f.shape[1]

    def weight_copies(e, s):
        return (pltpu.make_async_copy(wgu_hbm.at[e], wgu_f.at[s], wsem.at[0, s]),
                pltpu.make_async_copy(wd_hbm.at[e], wd_f.at[s], wsem.at[1, s]))

    prev = blk_exp_ref[jnp.maximum(i - 1, 0)]
    first_of_expert = (i == 0) | (blk_exp_ref[i] != prev)

    @pl.when(first_of_expert & (i < n_used))
    def _():
        e = blk_exp_ref[i]
        s = slot_ref[i]

        @pl.when(i == 0)
        def _():
            for cp in weight_copies(e, s):
                cp.start()

        for cp in weight_copies(e, s):
            cp.wait()
        wgu_b[...] = wgu_f[s].astype(BF16)
        wd_b[...] = wd_f[s].astype(BF16)

        nxt = next_exp_ref[i]

        @pl.when(nxt >= 0)
        def _():
            for cp in weight_copies(nxt, 1 - s):
                cp.start()

    def swiglu_rows(rows):
        x_rows = xb_ref.at[pl.ds(0, rows * SUBLANES)]
        y_rows = yb_ref.at[pl.ds(0, rows * SUBLANES)]
        xb = _load_token_tiles(x_rows, rows).astype(BF16)
        hid = _dot(xb, wgu_b[...]) + bgu_ref[0]
        gate = jnp.minimum(hid[:, :dff], SWIGLU_LIMIT)
        up = jnp.clip(hid[:, dff:], -SWIGLU_LIMIT, SWIGLU_LIMIT)
        glu = gate * jax.nn.sigmoid(SWIGLU_ALPHA * gate)
        act = ((up + 1.0) * glu).astype(BF16)
        _store_token_tiles(y_rows, _dot(act, wd_b[...]) + bd_ref[0])

    used = i < n_used
    half_full = blk_rows_ref[i] <= bm // 2
    pl.when(used & jnp.logical_not(half_full))(functools.partial(swiglu_rows, bm))
    pl.when(used & half_full)(functools.partial(swiglu_rows, bm // 2))


def _experts(blk_exp, n_used, next_exp, slot, blk_rows, xb_tiles, wgu, bgu, wd, bd):
    d = wgu.shape[1]
    chunks = d // LANES
    n_blocks = xb_tiles.shape[0] // (EXPERT_BLOCK * chunks)
    two_f = wgu.shape[2]
    dff = wd.shape[1]
    blk = lambda i, be, nu, ne, sl, br: (jnp.minimum(i, nu[0] - 1), 0)
    exp3 = lambda i, be, nu, ne, sl, br: (be[jnp.minimum(i, nu[0] - 1)], 0, 0)
    grid_spec = pltpu.PrefetchScalarGridSpec(
        num_scalar_prefetch=5,
        grid=(n_blocks,),
        in_specs=[pl.BlockSpec((EXPERT_BLOCK * chunks, LANES), blk),
                  pl.BlockSpec(memory_space=pl.ANY),
                  pl.BlockSpec((1, 1, two_f), exp3),
                  pl.BlockSpec(memory_space=pl.ANY),
                  pl.BlockSpec((1, 1, d), exp3)],
        out_specs=pl.BlockSpec((EXPERT_BLOCK * chunks, LANES), blk),
        scratch_shapes=[pltpu.VMEM((2, d, two_f), F32), pltpu.VMEM((2, dff, d), F32),
                        pltpu.VMEM((d, two_f), BF16), pltpu.VMEM((dff, d), BF16),
                        pltpu.SemaphoreType.DMA((2, 2))])
    return pl.pallas_call(
        _expert_kernel,
        grid_spec=grid_spec,
        out_shape=jax.ShapeDtypeStruct(xb_tiles.shape, F32),
        compiler_params=pltpu.CompilerParams(
            dimension_semantics=("arbitrary",), vmem_limit_bytes=VMEM_LIMIT),
        name="experts",
    )(blk_exp, n_used, next_exp, slot, blk_rows, xb_tiles, wgu, bgu, wd, bd)


def _combine_kernel(y0_ref, y1_ref, y2_ref, y3_ref, gate_ref, x2_ref, fw_ref, out_ref):
    gates = gate_ref[...]
    lane = lax.broadcasted_iota(I32, gates.shape, 1)
    x3 = x2_ref[...]
    for kk, y_ref in enumerate((y0_ref, y1_ref, y2_ref, y3_ref)):
        gk = jnp.sum(jnp.where(lane == kk, gates, 0.0), axis=-1, keepdims=True)
        x3 = x3 + gk * _load_token_tiles(y_ref, x3.shape[0])
    out_ref[...] = x3 * lax.rsqrt(jnp.mean(x3 * x3, axis=-1, keepdims=True) + NORM_EPS) * fw_ref[...]


def _combine_into_kernel(y0_ref, y1_ref, y2_ref, y3_ref, gate_ref, x2_ref, fw_ref, prev_ref, out_ref):
    del prev_ref
    _combine_kernel(y0_ref, y1_ref, y2_ref, y3_ref, gate_ref, x2_ref, fw_ref, out_ref)


def _combine(y4, gates, x2, fw, out_prev, chunk, *, tile):
    t, d = x2.shape
    tc = y4.shape[0] // (TOP_K * (d // LANES))
    steps = tc // tile
    first = chunk * steps
    choice = lambda kk: pl.BlockSpec((tile * (d // LANES), LANES), lambda i: (kk * steps + i, 0))
    in_specs = ([choice(kk) for kk in range(TOP_K)]
                + [pl.BlockSpec((tile, LANES), lambda i: (first + i, 0)),
                   pl.BlockSpec((tile, d), lambda i: (first + i, 0)),
                   pl.BlockSpec((1, d), lambda i: (0, 0))])
    args = [y4, y4, y4, y4, gates, x2, fw]
    if out_prev is not None:
        in_specs.append(pl.BlockSpec(memory_space=pl.ANY))
        args.append(out_prev)
    return pl.pallas_call(
        _combine_kernel if out_prev is None else _combine_into_kernel,
        grid=(steps,),
        in_specs=in_specs,
        out_specs=pl.BlockSpec((tile, d), lambda i: (first + i, 0)),
        out_shape=jax.ShapeDtypeStruct((t, d), F32),
        input_output_aliases={} if out_prev is None else {len(args) - 1: 0},
        compiler_params=pltpu.CompilerParams(
            dimension_semantics=("arbitrary",), vmem_limit_bytes=VMEM_LIMIT),
        name="combine",
    )(*args)


def _pad_lanes(a, offset=0, fill=0.0):
    out = jnp.full((1, LANES), fill, a.dtype)
    return out.at[0, offset:offset + a.shape[0]].set(a)


def _layer(x, attn_norm_w, w_in, gdn_conv_w, gdn_a_log, gdn_dt_bias, gdn_norm_w,
           cf_dw_w, cf_dw_b, cf_ln_w, cf_ln_b, w_out, ffn_norm_w, w_router, b_router,
           w_gate_up, b_gate_up, w_down, b_down, final_norm_w):
    b, s, d = x.shape
    t = b * s
    assert d == SUBLANES * LANES, "the token-tile layout needs one (8, 128) tile per token row"
    qk = GDN_HEADS * HEAD_DIM
    cfc = cf_dw_w.shape[1]
    off_b = 4 * qk
    off_cf = off_b + 2 * GDN_HEADS

    w_in_b = w_in.astype(BF16)
    wm = w_in_b[:, :off_b]
    wba = jnp.pad(w_in_b[:, off_b:off_cf], ((0, 0), (0, LANES - 2 * GDN_HEADS)))
    wcf = w_in_b[:, off_cf:]
    alog = _pad_lanes(gdn_a_log, GDN_HEADS)
    dtb = _pad_lanes(gdn_dt_bias, GDN_HEADS)

    q, k, v, z, bg, bgt, u = _inproj(
        x, attn_norm_w[None, :], wm, wba, wcf, gdn_conv_w, alog, dtb,
        cf_dw_w, cf_dw_b[None, :], cf_ln_w[None, :], cf_ln_b[None, :], tile=min(2 * TOKEN_TILE, s))
    o = _gdn(q, k, v, bg, bgt, tile=min(TOKEN_TILE, s))

    w_out_b = w_out.astype(BF16)
    wr = jnp.pad(w_router, ((0, 0), (0, LANES - N_EXPERTS)))
    br = _pad_lanes(b_router)
    x2, xn, route, gates, counts = _outproj_router(
        o.reshape(t, qk), z.reshape(t, qk), u.reshape(t, cfc), x.reshape(t, d),
        gdn_norm_w[None, :], w_out_b[:qk], w_out_b[qk:],
        ffn_norm_w[None, :], wr, br, tile=min(TOKEN_TILE, t))

    cnt = counts[0, :N_EXPERTS]
    nblk = (cnt + EXPERT_BLOCK - 1) // EXPERT_BLOCK
    blk_end = jnp.cumsum(nblk)
    pstart = (blk_end - nblk) * EXPERT_BLOCK
    n_blocks = (t * TOP_K) // EXPERT_BLOCK + N_EXPERTS
    blk_ids = jnp.arange(n_blocks, dtype=I32)
    blk_exp = jnp.minimum(
        jnp.sum((blk_end[None, :] <= blk_ids[:, None]).astype(I32), axis=1), N_EXPERTS - 1)
    n_used = blk_end[-1:].astype(I32)
    experts = jnp.arange(N_EXPERTS, dtype=I32)
    present = nblk > 0
    later = jnp.where((experts[None, :] > experts[:, None]) & present[None, :], experts[None, :], N_EXPERTS)
    next_of = jnp.min(later, axis=1)
    next_of = jnp.where(next_of < N_EXPERTS, next_of, -1)
    slot_of = (jnp.cumsum(present.astype(I32)) - 1) % 2
    of_block = (blk_exp[:, None] == experts[None, :]).astype(I32)
    next_exp = jnp.sum(of_block * next_of[None, :], axis=1).astype(I32)
    slot = jnp.sum(of_block * slot_of[None, :], axis=1).astype(I32)
    row_end = jnp.sum(of_block * (pstart + cnt)[None, :], axis=1)
    blk_rows = jnp.clip(row_end - blk_ids * EXPERT_BLOCK, 0, EXPERT_BLOCK).astype(I32)

    dest = _dest(route, _pad_lanes(pstart.astype(I32)), tile=min(DEST_TILE, t))[:TOP_K]
    n_rows = n_blocks * EXPERT_BLOCK
    chunks = d // LANES
    windows = t // ROW_WINDOW
    dest_win = dest.reshape(TOP_K, windows, ROW_WINDOW).transpose(1, 0, 2).reshape(windows, TOP_K * ROW_WINDOW)
    xb = _dispatch_rows(xn.reshape(t, chunks, LANES), dest_win, n_rows)
    yb = _experts(blk_exp, n_used, next_exp, slot, blk_rows, xb.reshape(n_rows * chunks, LANES), w_gate_up, b_gate_up[:, None, :],
                  w_down, b_down[:, None, :]).reshape(n_rows, chunks, LANES)

    tc = t // COMBINE_CHUNKS
    out = None
    for c in range(COMBINE_CHUNKS):
        src = dest[:, c * tc:(c + 1) * tc].reshape(TOP_K * tc // ROW_WINDOW, ROW_WINDOW)
        src_win = jnp.pad(src, ((0, 0), (0, LANES - ROW_WINDOW)))
        y4 = _collect_rows(yb, src_win).reshape(TOP_K * tc * chunks, LANES)
        out = _combine(y4, gates, x2, final_norm_w[None, :], out, c, tile=min(EXPERT_BLOCK, tc))
    return out.reshape(b, s, d)


def kernel(x, attn_norm_w, w_in, gdn_conv_w, gdn_a_log, gdn_dt_bias, gdn_norm_w, cf_dw_w, cf_dw_b,
           cf_ln_w, cf_ln_b, w_out, ffn_norm_w, w_router, b_router, w_gate_up, b_gate_up, w_down,
           b_down, final_norm_w):
    depth = w_in.shape[0]
    assert depth == 1, "the fused final norm assumes a single trunk layer"
    return _layer(x, attn_norm_w[0], w_in[0], gdn_conv_w[0], gdn_a_log[0], gdn_dt_bias[0],
                  gdn_norm_w[0], cf_dw_w[0], cf_dw_b[0], cf_ln_w[0], cf_ln_b[0], w_out[0],
                  ffn_norm_w[0], w_router[0], b_router[0], w_gate_up[0], b_gate_up[0], w_down[0],
                  b_down[0], final_norm_w)
```

```python
import functools

import jax
import jax.numpy as jnp
from jax import lax
from jax.experimental import pallas as pl
from jax.experimental.pallas import tpu as pltpu
from jax.experimental.pallas import tpu_sc as plsc

F32 = jnp.float32
BF16 = jnp.bfloat16
I32 = jnp.int32

NORM_EPS = 1e-6
LANES = 128
SUBLANES = 8
GDN_HEADS = 4
HEAD_DIM = 128
GDN_CHUNK = 64
GDN_CONV = 4
CF_KERNEL = 31
N_EXPERTS = 32
TOP_K = 4
SWIGLU_LIMIT = 7.0
SWIGLU_ALPHA = 1.702

QKV_HALO = 8
CF_HALO = 32
CF_ROWS = 256
EXPERT_BLOCK = 512
PROJ_TILE = 1024
GDN_TILE = 512
DEST_TILE = 2048
VMEM_LIMIT = 56 * 1024 * 1024


def _silu(x):
    return x * jax.nn.sigmoid(x)


def _dot(a, b):
    return jnp.dot(a, b, preferred_element_type=F32)


def _store_token_tiles(ref, x):
    rows, d = x.shape
    chunks = d // LANES
    for c in range(chunks):
        ref[pl.ds(c, rows, stride=chunks), :] = x[:, c * LANES:(c + 1) * LANES]


def _load_token_tiles(ref, rows):
    chunks = ref.shape[0] // rows
    return jnp.concatenate([ref[pl.ds(c, rows, stride=chunks), :] for c in range(chunks)], axis=1)


def _inproj_kernel(x_ref, nw_ref, wm_ref, wba_ref, wcf_ref, cw_ref, alog_ref, dtb_ref,
                   dww_ref, dwb_ref, lnw_ref, lnb_ref,
                   q_ref, k_ref, v_ref, z_ref, bg_ref, bgt_ref, u_ref,
                   qkv_buf, cf_buf, cf_shift):
    tt = x_ref.shape[1]
    qk = GDN_HEADS * HEAD_DIM
    cfc = u_ref.shape[2]

    @pl.when(pl.program_id(1) == 0)
    def _():
        qkv_buf[0:QKV_HALO, :] = jnp.zeros((QKV_HALO, qkv_buf.shape[1]), F32)
        cf_buf[0:CF_HALO, :] = jnp.zeros((CF_HALO, cf_buf.shape[1]), F32)

    x = x_ref[0]
    h = x * lax.rsqrt(jnp.mean(x * x, axis=-1, keepdims=True) + NORM_EPS) * nw_ref[...]
    h = h.astype(BF16)
    pm = _dot(h, wm_ref[...])
    pba = _dot(h, wba_ref[...])
    pcf = _dot(h, wcf_ref[...])

    z_ref[0] = pm[:, 3 * qk:]

    qkv_buf[QKV_HALO:QKV_HALO + tt, :] = pm[:, :3 * qk]
    qkv_rows = qkv_buf[0:QKV_HALO + tt, :]
    acc = None
    for j in range(GDN_CONV):
        s = GDN_CONV - 1 - j
        shifted = qkv_rows if s == 0 else pltpu.roll(qkv_rows, s, 0)
        term = cw_ref[j:j + 1, :] * shifted[QKV_HALO:, :]
        acc = term if acc is None else acc + term
    qkv_buf[0:QKV_HALO, :] = qkv_buf[tt:tt + QKV_HALO, :]
    qkv = _silu(acc)
    for hd in range(GDN_HEADS):
        for base, ref in ((0, q_ref), (qk, k_ref)):
            t = qkv[:, base + hd * HEAD_DIM: base + (hd + 1) * HEAD_DIM]
            t = t * lax.rsqrt(jnp.sum(t * t, axis=-1, keepdims=True) + NORM_EPS)
            ref[0, :, hd * HEAD_DIM:(hd + 1) * HEAD_DIM] = t
    v_ref[0] = qkv[:, 2 * qk:]

    lane = lax.broadcasted_iota(I32, (tt, LANES), 1)
    row = lax.broadcasted_iota(I32, (tt, LANES), 0)
    beta = jax.nn.sigmoid(pba)
    sp_in = pba + dtb_ref[...]
    softplus = jnp.maximum(sp_in, 0.0) + jnp.log(1.0 + jnp.exp(-jnp.abs(sp_in)))
    g = -jnp.exp(alog_ref[...]) * softplus
    g = jnp.where((lane >= GDN_HEADS) & (lane < 2 * GDN_HEADS), g, 0.0)
    pos = row % GDN_CHUNK
    shift = 1
    while shift < GDN_CHUNK:
        g = g + jnp.where(pos >= shift, pltpu.roll(g, shift, 0), 0.0)
        shift *= 2
    bg = jnp.where(lane < GDN_HEADS, beta, g)
    bg_ref[0] = bg
    bgt_ref[0] = jnp.transpose(bg)[0:SUBLANES, :]

    glu = pcf[:, :cfc] * jax.nn.sigmoid(pcf[:, cfc:])
    cf_buf[CF_HALO:CF_HALO + tt, :] = glu
    lo = SUBLANES
    span = tt + CF_HALO - lo
    cf_rows = cf_buf[...]
    for r in range(1, SUBLANES):
        cf_shift[r - 1, lo:lo + span, :] = pltpu.roll(cf_rows, r, 0)[lo:lo + span, :]
    rows = CF_ROWS
    for r0 in range(0, tt, rows):
        acc = None
        for j in range(CF_KERNEL):
            a, r = divmod(CF_KERNEL - 1 - j, SUBLANES)
            start = CF_HALO + r0 - a * SUBLANES
            src = cf_buf[start:start + rows, :] if r == 0 else cf_shift[r - 1, start:start + rows, :]
            term = dww_ref[j:j + 1, :] * src
            acc = term if acc is None else acc + term
        c = acc + dwb_ref[...]
        mu = jnp.mean(c, axis=-1, keepdims=True)
        cc = c - mu
        y = cc * lax.rsqrt(jnp.mean(cc * cc, axis=-1, keepdims=True) + NORM_EPS)
        u_ref[0, r0:r0 + rows, :] = _silu(y * lnw_ref[...] + lnb_ref[...])
    cf_buf[0:CF_HALO, :] = cf_buf[tt:tt + CF_HALO, :]


def _inproj(x, nw, wm, wba, wcf, cw, alog, dtb, dww, dwb, lnw, lnb, *, tile):
    b, s, d = x.shape
    qk = GDN_HEADS * HEAD_DIM
    cfc = dww.shape[1]
    grid = (b, s // tile)
    full = lambda a: pl.BlockSpec(a.shape, lambda i, j: (0,) * a.ndim)
    tok = lambda w: pl.BlockSpec((1, tile, w), lambda i, j: (i, j, 0))
    out_shape = (
        jax.ShapeDtypeStruct((b, s, qk), F32), jax.ShapeDtypeStruct((b, s, qk), F32),
        jax.ShapeDtypeStruct((b, s, qk), F32), jax.ShapeDtypeStruct((b, s, qk), F32),
        jax.ShapeDtypeStruct((b, s, LANES), F32), jax.ShapeDtypeStruct((b, SUBLANES, s), F32),
        jax.ShapeDtypeStruct((b, s, cfc), F32))
    return pl.pallas_call(
        _inproj_kernel,
        grid=grid,
        in_specs=[tok(d)] + [full(a) for a in (nw, wm, wba, wcf, cw, alog, dtb, dww, dwb, lnw, lnb)],
        out_specs=(tok(qk), tok(qk), tok(qk), tok(qk), tok(LANES),
                   pl.BlockSpec((1, SUBLANES, tile), lambda i, j: (i, 0, j)), tok(cfc)),
        out_shape=out_shape,
        scratch_shapes=[pltpu.VMEM((QKV_HALO + tile, 3 * qk), F32),
                        pltpu.VMEM((CF_HALO + tile, cfc), F32),
                        pltpu.VMEM((SUBLANES - 1, CF_HALO + tile, cfc), F32)],
        compiler_params=pltpu.CompilerParams(
            dimension_semantics=("arbitrary", "arbitrary"), vmem_limit_bytes=VMEM_LIMIT),
        name="inproj",
    )(x, nw, wm, wba, wcf, cw, alog, dtb, dww, dwb, lnw, lnb)


def _bmm(a, b):
    return jnp.einsum("bmk,bkn->bmn", a, b, preferred_element_type=F32)


def _bmm_nt(a, b):
    return jnp.einsum("bmk,bnk->bmn", a, b, preferred_element_type=F32)


def _bmm_tn(a, b):
    return jnp.einsum("bkm,bkn->bmn", a, b, preferred_element_type=F32)


def _unit_lower_inverse(a):
    c = a.shape[-1]
    ii = lax.broadcasted_iota(I32, (c, c), 0)
    jj = lax.broadcasted_iota(I32, (c, c), 1)
    eye = (ii == jj).astype(F32)
    same16 = (ii // 16) == (jj // 16)
    same32 = (ii // 32) == (jj // 32)
    x = jnp.where(same16, -a, 0.0)
    t = eye + x
    xp = x
    for _ in range(3):
        xp_b = xp.astype(BF16)
        xp = _bmm(xp_b, xp_b)
        t = t + _bmm(t.astype(BF16), xp.astype(BF16))
    for off in (jnp.where(same32 & ~same16, a, 0.0), jnp.where(~same32, a, 0.0)):
        tb = t.astype(BF16)
        t = t - _bmm(tb, _bmm(off.astype(BF16), tb).astype(BF16))
    return t


def _gdn_kernel(q_ref, k_ref, v_ref, bg_ref, bgt_ref, o_ref, state, s_all):
    lt = q_ref.shape[1]
    c = GDN_CHUNK
    nh = GDN_HEADS
    nc = lt // c

    @pl.when(pl.program_id(1) == 0)
    def _():
        state[...] = jnp.zeros(state.shape, F32)

    def stack(fn):
        return jnp.stack([fn(slice(n * c, (n + 1) * c), h) for n in range(nc) for h in range(nh)])

    head = lambda h: slice(h * HEAD_DIM, (h + 1) * HEAD_DIM)
    q = stack(lambda r, h: q_ref[0, r, head(h)]) * (HEAD_DIM ** -0.5)
    k = stack(lambda r, h: k_ref[0, r, head(h)])
    v = stack(lambda r, h: v_ref[0, r, head(h)])
    beta = stack(lambda r, h: bg_ref[0, r, h:h + 1])
    gcol = stack(lambda r, h: bg_ref[0, r, nh + h:nh + h + 1])
    grow = stack(lambda r, h: bgt_ref[0, nh + h:nh + h + 1, r])

    ii = lax.broadcasted_iota(I32, (c, c), 0)
    jj = lax.broadcasted_iota(I32, (c, c), 1)
    glast = gcol[:, c - 1:c, :]
    eg = jnp.exp(gcol)
    decay = jnp.where(ii >= jj, jnp.exp(jnp.minimum(gcol - grow, 0.0)), 0.0)
    kb = k * beta
    k_b = k.astype(BF16)
    a = jnp.where(ii > jj, _bmm_nt(kb.astype(BF16), k_b) * decay, 0.0)
    t = _unit_lower_inverse(a)
    rhs = jnp.concatenate([v * beta, kb * eg], axis=-1).astype(BF16)
    sol = _bmm(t.astype(BF16), rhs)
    u_val = sol[..., :HEAD_DIM]
    w_key = sol[..., HEAD_DIM:]
    intra = _bmm_nt(q.astype(BF16), k_b) * decay
    k_tail = (k * jnp.exp(glast - gcol)).astype(BF16)
    upd = _bmm_tn(k_tail, sol.astype(BF16))
    b_mat = upd[..., :HEAD_DIM]
    p_mat = upd[..., HEAD_DIM:].astype(BF16)
    g_tot = jnp.exp(glast)

    s = state[...]
    for n in range(nc):
        grp = slice(n * nh, (n + 1) * nh)
        s_b = s.astype(BF16)
        s_all[grp] = s_b
        s = s * g_tot[grp] - _bmm(p_mat[grp], s_b) + b_mat[grp]
    state[...] = s

    wq = jnp.concatenate([w_key, q * eg], axis=1).astype(BF16)
    ws_qs = _bmm(wq, s_all[...])
    v_new = u_val - ws_qs[:, :c]
    o = ws_qs[:, c:] + _bmm(intra.astype(BF16), v_new.astype(BF16))
    for n in range(nc):
        for h in range(nh):
            o_ref[0, n * c:(n + 1) * c, head(h)] = o[n * nh + h]


def _gdn(q, k, v, bg, bgt, *, tile):
    b, s, qk = q.shape
    grid = (b, s // tile)
    tok = lambda w: pl.BlockSpec((1, tile, w), lambda i, j: (i, j, 0))
    n_prob = (tile // GDN_CHUNK) * GDN_HEADS
    return pl.pallas_call(
        _gdn_kernel,
        grid=grid,
        in_specs=[tok(qk), tok(qk), tok(qk), tok(LANES),
                  pl.BlockSpec((1, SUBLANES, tile), lambda i, j: (i, 0, j))],
        out_specs=tok(qk),
        out_shape=jax.ShapeDtypeStruct((b, s, qk), F32),
        scratch_shapes=[pltpu.VMEM((GDN_HEADS, HEAD_DIM, HEAD_DIM), F32),
                        pltpu.VMEM((n_prob, HEAD_DIM, HEAD_DIM), BF16)],
        compiler_params=pltpu.CompilerParams(
            dimension_semantics=("arbitrary", "arbitrary"), vmem_limit_bytes=VMEM_LIMIT),
        name="gdn",
    )(q, k, v, bg, bgt)


def _split_bf16(x):
    hi = x.astype(BF16)
    lo = (x - hi.astype(F32)).astype(BF16)
    return hi, lo


def _outproj_router_kernel(o_ref, z_ref, u_ref, x_ref, gnw_ref, wa_ref, wb_ref, fnw_ref,
                           wr_ref, br_ref,
                           x2_ref, xn_ref, route_ref, gate_ref, counts_ref, carry):
    tt = x_ref.shape[0]

    @pl.when(pl.program_id(0) == 0)
    def _():
        carry[...] = jnp.zeros(carry.shape, F32)

    parts = []
    for hd in range(GDN_HEADS):
        sl = slice(hd * HEAD_DIM, (hd + 1) * HEAD_DIM)
        oh = o_ref[:, sl]
        y = oh * lax.rsqrt(jnp.mean(oh * oh, axis=-1, keepdims=True) + NORM_EPS) * gnw_ref[...]
        parts.append((y * _silu(z_ref[:, sl])).astype(BF16))
    out_a = jnp.concatenate(parts, axis=-1)
    x2 = x_ref[...] + _dot(out_a, wa_ref[...]) + _dot(u_ref[...].astype(BF16), wb_ref[...])
    x2_ref[...] = x2

    xn = x2 * lax.rsqrt(jnp.mean(x2 * x2, axis=-1, keepdims=True) + NORM_EPS) * fnw_ref[...]
    _store_token_tiles(xn_ref, xn)

    xh, xl = _split_bf16(xn)
    wh, wl = _split_bf16(wr_ref[...])
    logits = _dot(xh, wh) + _dot(xh, wl) + _dot(xl, wh) + br_ref[...]

    lane = lax.broadcasted_iota(I32, (tt, LANES), 1)
    lane_f = lane.astype(F32)
    neg = jnp.float32(-jnp.inf)
    work = jnp.where(lane < N_EXPERTS, logits, neg)
    vals, idxs = [], []
    onehot = jnp.zeros((tt, LANES), F32)
    for _ in range(TOP_K):
        m = jnp.max(work, axis=-1, keepdims=True)
        idx = jnp.min(jnp.where(work == m, lane_f, float(LANES)), axis=-1, keepdims=True).astype(I32)
        sel = lane == idx
        vals.append(m)
        idxs.append(idx)
        onehot = onehot + sel.astype(F32)
        work = jnp.where(sel, neg, work)
    exps = [jnp.exp(v - vals[0]) for v in vals]
    denom = exps[0] + exps[1] + exps[2] + exps[3]

    ri = lax.broadcasted_iota(I32, (tt, tt), 0)
    ci = lax.broadcasted_iota(I32, (tt, tt), 1)
    strict = (ri > ci).astype(BF16)
    base = carry[...] + _dot(strict, onehot.astype(BF16))
    route = jnp.zeros((tt, LANES), I32)
    gates = jnp.zeros((tt, LANES), F32)
    for kk in range(TOP_K):
        rank = jnp.sum(jnp.where(lane == idxs[kk], base, 0.0), axis=-1, keepdims=True)
        route = jnp.where(lane == kk, idxs[kk], route)
        route = jnp.where(lane == kk + TOP_K, rank.astype(I32), route)
        gates = jnp.where(lane == kk, exps[kk] / denom, gates)
    route_ref[...] = route
    gate_ref[...] = gates
    new_carry = carry[...] + jnp.sum(onehot, axis=0, keepdims=True)
    carry[...] = new_carry
    counts_ref[...] = new_carry.astype(I32)


def _outproj_router(o, z, u, x, gnw, wa, wb, fnw, wr, br, *, tile):
    t, d = x.shape
    grid = (t // tile,)
    full = lambda a: pl.BlockSpec(a.shape, lambda i: (0,) * a.ndim)
    tok = lambda w: pl.BlockSpec((tile, w), lambda i: (i, 0))
    return pl.pallas_call(
        _outproj_router_kernel,
        grid=grid,
        in_specs=[tok(o.shape[1]), tok(z.shape[1]), tok(u.shape[1]), tok(d)]
                 + [full(a) for a in (gnw, wa, wb, fnw, wr, br)],
        out_specs=(tok(d), pl.BlockSpec((tile * (d // LANES), LANES), lambda i: (i, 0)),
                   tok(LANES), tok(LANES), pl.BlockSpec((1, LANES), lambda i: (0, 0))),
        out_shape=(jax.ShapeDtypeStruct((t, d), F32), jax.ShapeDtypeStruct((t * (d // LANES), LANES), F32),
                   jax.ShapeDtypeStruct((t, LANES), I32), jax.ShapeDtypeStruct((t, LANES), F32),
                   jax.ShapeDtypeStruct((1, LANES), I32)),
        scratch_shapes=[pltpu.VMEM((1, LANES), F32)],
        compiler_params=pltpu.CompilerParams(
            dimension_semantics=("arbitrary",), vmem_limit_bytes=VMEM_LIMIT),
        name="outproj_router",
    )(o, z, u, x, gnw, wa, wb, fnw, wr, br)


def _dest_kernel(route_ref, pstart_ref, dest_ref):
    route = route_ref[...]
    tt = route.shape[0]
    lane = lax.broadcasted_iota(I32, (tt, LANES), 1)
    expert = jnp.where(lane < TOP_K, route, 0)
    start = jnp.take_along_axis(jnp.broadcast_to(pstart_ref[...], (tt, LANES)), expert, axis=1)
    rank = pltpu.roll(route, LANES - TOP_K, 1)
    dest = jnp.where(lane < TOP_K, start + rank, 0)
    dest_ref[...] = jnp.transpose(dest)[0:SUBLANES, :]


def _dest(route, pstart, *, tile):
    t = route.shape[0]
    return pl.pallas_call(
        _dest_kernel,
        grid=(t // tile,),
        in_specs=[pl.BlockSpec((tile, LANES), lambda i: (i, 0)),
                  pl.BlockSpec((1, LANES), lambda i: (0, 0))],
        out_specs=pl.BlockSpec((SUBLANES, tile), lambda i: (0, i)),
        out_shape=jax.ShapeDtypeStruct((SUBLANES, t), I32),
        compiler_params=pltpu.CompilerParams(dimension_semantics=("arbitrary",)),
        name="dest_rows",
    )(route, pstart)


ROW_WINDOW = 32
COMBINE_CHUNKS = 8


def _sc_mesh():
    return plsc.VectorSubcoreMesh(core_axis_name="core", subcore_axis_name="subcore")


def _dispatch_rows(xn_tiles, dest_win, n_rows):
    t = xn_tiles.shape[0]

    @functools.partial(
        pl.kernel, mesh=_sc_mesh(), scratch_types=[],
        out_type=jax.ShapeDtypeStruct((n_rows,) + xn_tiles.shape[1:], xn_tiles.dtype))
    def dispatch(x_hbm, idx_hbm, o_hbm):
        def body(x_vmem, idx_vmem):
            for kk in range(TOP_K):
                pltpu.sync_copy(x_vmem, o_hbm.at[idx_vmem.at[0, pl.ds(kk * ROW_WINDOW, ROW_WINDOW)]])

        pltpu.emit_pipeline(
            body,
            grid=(t // ROW_WINDOW,),
            in_specs=[pl.BlockSpec((ROW_WINDOW,) + xn_tiles.shape[1:], lambda i: (i, 0, 0)),
                      pl.BlockSpec((1, TOP_K * ROW_WINDOW), lambda i: (i, 0))],
            out_specs=[],
            core_axis_name=("core", "subcore"),
            dimension_semantics=(pltpu.PARALLEL,),
        )(x_hbm, idx_hbm)

    return dispatch(xn_tiles, dest_win)


def _collect_rows(yb_tiles, src_win):
    n_pairs = src_win.shape[0] * ROW_WINDOW

    @functools.partial(
        pl.kernel, mesh=_sc_mesh(), scratch_types=[],
        out_type=jax.ShapeDtypeStruct((n_pairs,) + yb_tiles.shape[1:], yb_tiles.dtype))
    def collect(y_hbm, idx_hbm, o_hbm):
        def body(idx_vmem, o_vmem):
            pltpu.sync_copy(y_hbm.at[idx_vmem.at[0, pl.ds(0, ROW_WINDOW)]], o_vmem)

        pltpu.emit_pipeline(
            body,
            grid=(n_pairs // ROW_WINDOW,),
            in_specs=[pl.BlockSpec((1, LANES), lambda i: (i, 0))],
            out_specs=[pl.BlockSpec((ROW_WINDOW,) + yb_tiles.shape[1:], lambda i: (i, 0, 0))],
            core_axis_name=("core", "subcore"),
            dimension_semantics=(pltpu.PARALLEL,),
        )(idx_hbm, o_hbm)

    return collect(yb_tiles, src_win)


def _expert_kernel(blk_exp_ref, n_used_ref, next_exp_ref, slot_ref, blk_rows_ref,
                   xb_ref, wgu_hbm, bgu_ref, wd_hbm, bd_ref, yb_ref,
                   wgu_f, wd_f, wgu_b, wd_b, wsem):
    i = pl.program_id(0)
    n_used = n_used_ref[0]
    bm = xb_ref.shape[0] // SUBLANES
    dff = wd_f.shape[1]

    def weight_copies(e, s):
        return (pltpu.make_async_copy(wgu_hbm.at[e], wgu_f.at[s], wsem.at[0, s]),
                pltpu.make_async_copy(wd_hbm.at[e], wd_f.at[s], wsem.at[1, s]))

    prev = blk_exp_ref[jnp.maximum(i - 1, 0)]
    first_of_expert = (i == 0) | (blk_exp_ref[i] != prev)

    @pl.when(first_of_expert & (i < n_used))
    def _():
        e = blk_exp_ref[i]
        s = slot_ref[i]

        @pl.when(i == 0)
        def _():
            for cp in weight_copies(e, s):
                cp.start()

        for cp in weight_copies(e, s):
            cp.wait()
        wgu_b[...] = wgu_f[s].astype(BF16)
        wd_b[...] = wd_f[s].astype(BF16)

        nxt = next_exp_ref[i]

        @pl.when(nxt >= 0)
        def _():
            for cp in weight_copies(nxt, 1 - s):
                cp.start()

    def swiglu_rows(rows):
        x_rows = xb_ref.at[pl.ds(0, rows * SUBLANES)]
        y_rows = yb_ref.at[pl.ds(0, rows * SUBLANES)]
        xb = _load_token_tiles(x_rows, rows).astype(BF16)
        hid = _dot(xb, wgu_b[...]) + bgu_ref[0]
        gate = jnp.minimum(hid[:, :dff], SWIGLU_LIMIT)
        up = jnp.clip(hid[:, dff:], -SWIGLU_LIMIT, SWIGLU_LIMIT)
        glu = gate * jax.nn.sigmoid(SWIGLU_ALPHA * gate)
        act = ((up + 1.0) * glu).astype(BF16)
        _store_token_tiles(y_rows, _dot(act, wd_b[...]) + bd_ref[0])

    used = i < n_used
    half_full = blk_rows_ref[i] <= bm // 2
    pl.when(used & jnp.logical_not(half_full))(functools.partial(swiglu_rows, bm))
    pl.when(used & half_full)(functools.partial(swiglu_rows, bm // 2))


def _experts(blk_exp, n_used, next_exp, slot, blk_rows, xb_tiles, wgu, bgu, wd, bd):
    d = wgu.shape[1]
    chunks = d // LANES
    n_blocks = xb_tiles.shape[0] // (EXPERT_BLOCK * chunks)
    two_f = wgu.shape[2]
    dff = wd.shape[1]
    blk = lambda i, be, nu, ne, sl, br: (jnp.minimum(i, nu[0] - 1), 0)
    exp3 = lambda i, be, nu, ne, sl, br: (be[jnp.minimum(i, nu[0] - 1)], 0, 0)
    grid_spec = pltpu.PrefetchScalarGridSpec(
        num_scalar_prefetch=5,
        grid=(n_blocks,),
        in_specs=[pl.BlockSpec((EXPERT_BLOCK * chunks, LANES), blk),
                  pl.BlockSpec(memory_space=pl.ANY),
                  pl.BlockSpec((1, 1, two_f), exp3),
                  pl.BlockSpec(memory_space=pl.ANY),
                  pl.BlockSpec((1, 1, d), exp3)],
        out_specs=pl.BlockSpec((EXPERT_BLOCK * chunks, LANES), blk),
        scratch_shapes=[pltpu.VMEM((2, d, two_f), F32), pltpu.VMEM((2, dff, d), F32),
                        pltpu.VMEM((d, two_f), BF16), pltpu.VMEM((dff, d), BF16),
                        pltpu.SemaphoreType.DMA((2, 2))])
    return pl.pallas_call(
        _expert_kernel,
        grid_spec=grid_spec,
        out_shape=jax.ShapeDtypeStruct(xb_tiles.shape, F32),
        compiler_params=pltpu.CompilerParams(
            dimension_semantics=("arbitrary",), vmem_limit_bytes=VMEM_LIMIT),
        name="experts",
    )(blk_exp, n_used, next_exp, slot, blk_rows, xb_tiles, wgu, bgu, wd, bd)


def _combine_kernel(y0_ref, y1_ref, y2_ref, y3_ref, gate_ref, x2_ref, fw_ref, out_ref):
    gates = gate_ref[...]
    lane = lax.broadcasted_iota(I32, gates.shape, 1)
    x3 = x2_ref[...]
    for kk, y_ref in enumerate((y0_ref, y1_ref, y2_ref, y3_ref)):
        gk = jnp.sum(jnp.where(lane == kk, gates, 0.0), axis=-1, keepdims=True)
        x3 = x3 + gk * _load_token_tiles(y_ref, x3.shape[0])
    out_ref[...] = x3 * lax.rsqrt(jnp.mean(x3 * x3, axis=-1, keepdims=True) + NORM_EPS) * fw_ref[...]


def _combine_into_kernel(y0_ref, y1_ref, y2_ref, y3_ref, gate_ref, x2_ref, fw_ref, prev_ref, out_ref):
    del prev_ref
    _combine_kernel(y0_ref, y1_ref, y2_ref, y3_ref, gate_ref, x2_ref, fw_ref, out_ref)


def _combine(y4, gates, x2, fw, out_prev, chunk, *, tile):
    t, d = x2.shape
    tc = y4.shape[0] // (TOP_K * (d // LANES))
    steps = tc // tile
    first = chunk * steps
    choice = lambda kk: pl.BlockSpec((tile * (d // LANES), LANES), lambda i: (kk * steps + i, 0))
    in_specs = ([choice(kk) for kk in range(TOP_K)]
                + [pl.BlockSpec((tile, LANES), lambda i: (first + i, 0)),
                   pl.BlockSpec((tile, d), lambda i: (first + i, 0)),
                   pl.BlockSpec((1, d), lambda i: (0, 0))])
    args = [y4, y4, y4, y4, gates, x2, fw]
    if out_prev is not None:
        in_specs.append(pl.BlockSpec(memory_space=pl.ANY))
        args.append(out_prev)
    return pl.pallas_call(
        _combine_kernel if out_prev is None else _combine_into_kernel,
        grid=(steps,),
        in_specs=in_specs,
        out_specs=pl.BlockSpec((tile, d), lambda i: (first + i, 0)),
        out_shape=jax.ShapeDtypeStruct((t, d), F32),
        input_output_aliases={} if out_prev is None else {len(args) - 1: 0},
        compiler_params=pltpu.CompilerParams(
            dimension_semantics=("arbitrary",), vmem_limit_bytes=VMEM_LIMIT),
        name="combine",
    )(*args)


def _pad_lanes(a, offset=0, fill=0.0):
    out = jnp.full((1, LANES), fill, a.dtype)
    return out.at[0, offset:offset + a.shape[0]].set(a)


def _layer(x, attn_norm_w, w_in, gdn_conv_w, gdn_a_log, gdn_dt_bias, gdn_norm_w,
           cf_dw_w, cf_dw_b, cf_ln_w, cf_ln_b, w_out, ffn_norm_w, w_router, b_router,
           w_gate_up, b_gate_up, w_down, b_down, final_norm_w):
    b, s, d = x.shape
    t = b * s
    assert d == SUBLANES * LANES, "the token-tile layout needs one (8, 128) tile per token row"
    qk = GDN_HEADS * HEAD_DIM
    cfc = cf_dw_w.shape[1]
    off_b = 4 * qk
    off_cf = off_b + 2 * GDN_HEADS

    w_in_b = w_in.astype(BF16)
    wm = w_in_b[:, :off_b]
    wba = jnp.pad(w_in_b[:, off_b:off_cf], ((0, 0), (0, LANES - 2 * GDN_HEADS)))
    wcf = w_in_b[:, off_cf:]
    alog = _pad_lanes(gdn_a_log, GDN_HEADS)
    dtb = _pad_lanes(gdn_dt_bias, GDN_HEADS)

    q, k, v, z, bg, bgt, u = _inproj(
        x, attn_norm_w[None, :], wm, wba, wcf, gdn_conv_w, alog, dtb,
        cf_dw_w, cf_dw_b[None, :], cf_ln_w[None, :], cf_ln_b[None, :], tile=min(PROJ_TILE, s))
    o = _gdn(q, k, v, bg, bgt, tile=min(GDN_TILE, s))

    w_out_b = w_out.astype(BF16)
    wr = jnp.pad(w_router, ((0, 0), (0, LANES - N_EXPERTS)))
    br = _pad_lanes(b_router)
    x2, xn, route, gates, counts = _outproj_router(
        o.reshape(t, qk), z.reshape(t, qk), u.reshape(t, cfc), x.reshape(t, d),
        gdn_norm_w[None, :], w_out_b[:qk], w_out_b[qk:],
        ffn_norm_w[None, :], wr, br, tile=min(PROJ_TILE, t))

    cnt = counts[0, :N_EXPERTS]
    nblk = (cnt + EXPERT_BLOCK - 1) // EXPERT_BLOCK
    blk_end = jnp.cumsum(nblk)
    pstart = (blk_end - nblk) * EXPERT_BLOCK
    n_blocks = (t * TOP_K) // EXPERT_BLOCK + N_EXPERTS
    blk_ids = jnp.arange(n_blocks, dtype=I32)
    blk_exp = jnp.minimum(
        jnp.sum((blk_end[None, :] <= blk_ids[:, None]).astype(I32), axis=1), N_EXPERTS - 1)
    n_used = blk_end[-1:].astype(I32)
    experts = jnp.arange(N_EXPERTS, dtype=I32)
    present = nblk > 0
    later = jnp.where((experts[None, :] > experts[:, None]) & present[None, :], experts[None, :], N_EXPERTS)
    next_of = jnp.min(later, axis=1)
    next_of = jnp.where(next_of < N_EXPERTS, next_of, -1)
    slot_of = (jnp.cumsum(present.astype(I32)) - 1) % 2
    of_block = (blk_exp[:, None] == experts[None, :]).astype(I32)
    next_exp = jnp.sum(of_block * next_of[None, :], axis=1).astype(I32)
    slot = jnp.sum(of_block * slot_of[None, :], axis=1).astype(I32)
    row_end = jnp.sum(of_block * (pstart + cnt)[None, :], axis=1)
    blk_rows = jnp.clip(row_end - blk_ids * EXPERT_BLOCK, 0, EXPERT_BLOCK).astype(I32)

    dest = _dest(route, _pad_lanes(pstart.astype(I32)), tile=min(DEST_TILE, t))[:TOP_K]
    n_rows = n_blocks * EXPERT_BLOCK
    chunks = d // LANES
    windows = t // ROW_WINDOW
    dest_win = dest.reshape(TOP_K, windows, ROW_WINDOW).transpose(1, 0, 2).reshape(windows, TOP_K * ROW_WINDOW)
    xb = _dispatch_rows(xn.reshape(t, chunks, LANES), dest_win, n_rows)
    yb = _experts(blk_exp, n_used, next_exp, slot, blk_rows, xb.reshape(n_rows * chunks, LANES), w_gate_up, b_gate_up[:, None, :],
                  w_down, b_down[:, None, :]).reshape(n_rows, chunks, LANES)

    tc = t // COMBINE_CHUNKS
    out = None
    for c in range(COMBINE_CHUNKS):
        src = dest[:, c * tc:(c + 1) * tc].reshape(TOP_K * tc // ROW_WINDOW, ROW_WINDOW)
        src_win = jnp.pad(src, ((0, 0), (0, LANES - ROW_WINDOW)))
        y4 = _collect_rows(yb, src_win).reshape(TOP_K * tc * chunks, LANES)
        out = _combine(y4, gates, x2, final_norm_w[None, :], out, c, tile=min(EXPERT_BLOCK, tc))
    return out.reshape(b, s, d)


def kernel(x, attn_norm_w, w_in, gdn_conv_w, gdn_a_log, gdn_dt_bias, gdn_norm_w, cf_dw_w, cf_dw_b,
           cf_ln_w, cf_ln_b, w_out, ffn_norm_w, w_router, b_router, w_gate_up, b_gate_up, w_down,
           b_down, final_norm_w):
    depth = w_in.shape[0]
    assert depth == 1, "the fused final norm assumes a single trunk layer"
    return _layer(x, attn_norm_w[0], w_in[0], gdn_conv_w[0], gdn_a_log[0], gdn_dt_bias[0],
                  gdn_norm_w[0], cf_dw_w[0], cf_dw_b[0], cf_ln_w[0], cf_ln_b[0], w_out[0],
                  ffn_norm_w[0], w_router[0], b_router[0], w_gate_up[0], b_gate_up[0], w_down[0],
                  b_down[0], final_norm_w)
```
